```python
import jax
import jax.numpy as jnp
from jax import lax
import numpy as np

D_MODEL = 1024
BATCH = 2
SEQ = 8192
DEPTH = 2

GRID_W = 64
CTX_LEN = 256
HEAD_DIM = 64
N_GROUPS = 4
BRANCH_WIDTH = N_GROUPS * HEAD_DIM
N_BRANCH = 4
CHUNK = 128
CONV_WIDTH = 31
WIN_ROWS = 8
WIN_COLS = 16
QB_COLS = 16
KB_COLS = QB_COLS + WIN_COLS
ROPE_BASE = 10000.0
N_EXPERTS = 16
EXPERT_FF = 1024
EC_CAPACITY = 2
EPS = 1e-6
NEG_INF = -1e30
A_END = 2 * BRANCH_WIDTH
B_END = A_END + BRANCH_WIDTH
C_END = B_END + 2 * BRANCH_WIDTH
Q_END = C_END + BRANCH_WIDTH
K_END = Q_END + BRANCH_WIDTH
V_END = K_END + BRANCH_WIDTH
IN_COLS = V_END + N_BRANCH * D_MODEL
SPLITS = [A_END, B_END, C_END, Q_END, K_END, V_END]

kernel_name = 'hybrid_diffusion_trunk'


def rmsnorm(x, g):
    x32 = x.astype(jnp.float32)
    y = x32 * lax.rsqrt(jnp.mean(x32 * x32, axis=-1, keepdims=True) + EPS)
    return (y * g).astype(x.dtype)


def layernorm(x, g, b):
    x32 = x.astype(jnp.float32)
    mu = jnp.mean(x32, axis=-1, keepdims=True)
    var = jnp.mean(jnp.square(x32 - mu), axis=-1, keepdims=True)
    return ((x32 - mu) * lax.rsqrt(var + EPS) * g + b).astype(x.dtype)


def heads(z):
    return z.reshape(*z.shape[:-1], N_GROUPS, HEAD_DIM)


def chunk_gmlp(z, ln_g, ln_b, w_s, b_s):
    z = jax.nn.gelu(z)
    u, v = jnp.split(z, 2, axis=-1)
    v = layernorm(v, ln_g, ln_b)
    bsz, n, _ = v.shape
    v = v.reshape(bsz, n // CHUNK, CHUNK, N_GROUPS, HEAD_DIM)
    mixed = jnp.einsum('bcpgd,gqp->bcqgd', v, w_s) + b_s.T[None, None, :, :, None]
    return u * mixed.reshape(bsz, n, BRANCH_WIDTH)


def fourier_mix(z):
    bsz, n, _ = z.shape
    zg = z.astype(jnp.float32).reshape(bsz, n, N_GROUPS, HEAD_DIM)
    f = jnp.fft.fft2(zg, axes=(1, 3), norm='ortho').real
    return f.reshape(bsz, n, BRANCH_WIDTH).astype(z.dtype)


def conformer_conv(z, conv_w, conv_b, ln_g, ln_b):
    a, gt = jnp.split(z, 2, axis=-1)
    y = a * jax.nn.sigmoid(gt)
    y = lax.conv_general_dilated(
        y, conv_w[:, None, :], window_strides=(1,),
        padding=[(CONV_WIDTH // 2, CONV_WIDTH // 2)],
        dimension_numbers=('NWC', 'WIO', 'NWC'),
        feature_group_count=BRANCH_WIDTH) + conv_b
    return jax.nn.silu(layernorm(y, ln_g, ln_b))


def axial_rope(x, rows, cols):
    half = HEAD_DIM // 2
    quarter = half // 2
    inv = ROPE_BASE ** (-jnp.arange(quarter, dtype=jnp.float32) / quarter)

    def rot(xp, pos):
        ang = pos.astype(jnp.float32)[:, None] * inv[None, :]
        cos = jnp.cos(ang)[None, :, None, :]
        sin = jnp.sin(ang)[None, :, None, :]
        x1, x2 = jnp.split(xp.astype(jnp.float32), 2, axis=-1)
        return jnp.concatenate([x1 * cos - x2 * sin, x1 * sin + x2 * cos], axis=-1)

    out = jnp.concatenate([rot(x[..., :half], rows), rot(x[..., half:], cols)], axis=-1)
    return out.astype(x.dtype)


def neighbourhood_attention(q, k, v, kc, vc, rpb):
    bsz, n, nh, dh = q.shape
    n_rows = n // GRID_W
    wr = min(WIN_ROWS, n_rows)
    n_cb = GRID_W // QB_COLS
    n_win = wr * KB_COLS
    t = jnp.arange(n)
    qs = q * (dh ** -0.5)
    q_rot = axial_rope(qs, t // GRID_W, t % GRID_W)
    k_rot = axial_rope(k, t // GRID_W, t % GRID_W)
    r = jnp.arange(n_rows)
    key_rows = jnp.clip(r - wr // 2, 0, n_rows - wr)[:, None] + jnp.arange(wr)[None, :]
    jb = jnp.arange(n_cb)
    key_cols = (jnp.clip(jb * QB_COLS - WIN_COLS // 2, 0, GRID_W - KB_COLS)[:, None]
                + jnp.arange(KB_COLS)[None, :])
    q_cols = jb[:, None] * QB_COLS + jnp.arange(QB_COLS)[None, :]
    c_start = jnp.clip(q_cols - WIN_COLS // 2, 0, GRID_W - WIN_COLS)
    kcol = key_cols[:, None, :]
    col_ok = (kcol >= c_start[..., None]) & (kcol < c_start[..., None] + WIN_COLS)
    d_row = key_rows - r[:, None] + (WIN_ROWS - 1)
    d_col = jnp.clip(kcol - q_cols[..., None], 1 - WIN_COLS, WIN_COLS - 1) + (WIN_COLS - 1)
    bias = rpb[:, d_row[:, None, None, :, None], d_col[None, :, :, None, :]].astype(jnp.float32)
    bias = jnp.where(col_ok[None, None, :, :, None, :], bias, NEG_INF)
    bias = bias.reshape(nh, n_rows, n_cb, QB_COLS, n_win)
    key_idx = (key_rows[:, None, :, None] * GRID_W + key_cols[None, :, None, :]).reshape(n_rows, n_cb, n_win)
    k_win = k_rot[:, key_idx]
    v_win = v[:, key_idx]
    q_rot = q_rot.reshape(bsz, n_rows, n_cb, QB_COLS, nh, dh)
    q_pl = qs.reshape(bsz, n_rows, n_cb, QB_COLS, nh, dh)
    s_win = jnp.einsum('brjqhd,brjkhd->bhrjqk', q_rot, k_win).astype(jnp.float32) + bias
    s_ctx = jnp.einsum('brjqhd,bchd->bhrjqc', q_pl, kc).astype(jnp.float32)
    p = jax.nn.softmax(jnp.concatenate([s_win, s_ctx], axis=-1), axis=-1).astype(v.dtype)
    out = (jnp.einsum('bhrjqk,brjkhd->brjqhd', p[..., :n_win], v_win)
           + jnp.einsum('bhrjqc,bchd->brjqhd', p[..., n_win:], vc))
    return out.reshape(bsz, n, nh * dh)


def context_attention(q, k, v):
    bsz, n, nh, dh = q.shape
    s = jnp.einsum('bqhd,bkhd->bhqk', q * (dh ** -0.5), k).astype(jnp.float32)
    p = jax.nn.softmax(s, axis=-1).astype(v.dtype)
    return jnp.einsum('bhqk,bkhd->bqhd', p, v).reshape(bsz, n, nh * dh)


def merge_branches(a, b, cc, d, zg, w_a, w_b, w_c, w_d, w_o):
    ga, gb, gc, gd = jnp.split(jax.nn.sigmoid(zg), N_BRANCH, axis=-1)
    m = ga * (a @ w_a) + gb * (b @ w_b) + gc * (cc @ w_c) + gd * (d @ w_d)
    return m @ w_o


def expert_choice_ffn(h, router, w_gate, w_up, w_down):
    bsz, n, d = h.shape
    cap = EC_CAPACITY * n // N_EXPERTS
    aff = jax.nn.softmax((h @ router).astype(jnp.float32), axis=-1)
    gate, idx = lax.top_k(jnp.swapaxes(aff, 1, 2), cap)
    xe = jax.vmap(lambda hb, ib: hb[ib])(h, idx)
    a = jnp.einsum('becd,edf->becf', xe, w_gate)
    u = jnp.einsum('becd,edf->becf', xe, w_up)
    y = jnp.einsum('becf,efd->becd', jax.nn.silu(a) * u, w_down) * gate[..., None].astype(h.dtype)
    return jax.vmap(lambda ib, yb: jnp.zeros((n, d), h.dtype).at[ib.reshape(-1)].add(yb.reshape(-1, d)))(idx, y)


def setup_inputs(seed: int = 0) -> dict:
    key = jax.random.key(seed)
    ks = jax.random.split(key, 28)
    L, D, W = DEPTH, D_MODEL, BRANCH_WIDTH

    def nrm(k, shape, s):
        return jax.random.normal(k, shape, jnp.float32) * s

    return {
        'x': nrm(ks[0], (BATCH, SEQ, D), 1.0),
        'c': nrm(ks[1], (BATCH, D), 1.0),
        'ctx': nrm(ks[2], (BATCH, CTX_LEN, D), 1.0),
        'c_ctx': nrm(ks[3], (D,), 1.0),
        'w_mod': nrm(ks[4], (L, D, 6 * D), 0.5 * D ** -0.5),
        'b_mod': nrm(ks[5], (L, 6 * D), 0.02),
        'norm1_g': 1.0 + nrm(ks[6], (L, D), 0.02),
        'norm2_g': 1.0 + nrm(ks[7], (L, D), 0.02),
        'w_in': nrm(ks[8], (L, D, IN_COLS), D ** -0.5),
        'sgu_ln_g': 1.0 + nrm(ks[9], (L, W), 0.02),
        'sgu_ln_b': nrm(ks[10], (L, W), 0.02),
        'w_spatial': nrm(ks[11], (L, N_GROUPS, CHUNK, CHUNK), CHUNK ** -0.5),
        'b_spatial': 1.0 + nrm(ks[12], (L, N_GROUPS, CHUNK), 0.02),
        'w_a_out': nrm(ks[13], (L, W, D), W ** -0.5),
        'w_b_out': nrm(ks[14], (L, W, D), W ** -0.5),
        'conv_w': nrm(ks[15], (L, CONV_WIDTH, W), CONV_WIDTH ** -0.5),
        'conv_b': nrm(ks[16], (L, W), 0.02),
        'conv_ln_g': 1.0 + nrm(ks[17], (L, W), 0.02),
        'conv_ln_b': nrm(ks[18], (L, W), 0.02),
        'w_c_out': nrm(ks[19], (L, W, D), W ** -0.5),
        'rpb': nrm(ks[20], (L, N_GROUPS, 2 * WIN_ROWS - 1, 2 * WIN_COLS - 1), 0.1),
        'w_d_out': nrm(ks[21], (L, W, D), W ** -0.5),
        'w_out': nrm(ks[22], (L, D, D), D ** -0.5),
        'w_router': nrm(ks[23], (L, D, N_EXPERTS), D ** -0.5),
        'w_gate_e': nrm(ks[24], (L, N_EXPERTS, D, EXPERT_FF), D ** -0.5),
        'w_up_e': nrm(ks[25], (L, N_EXPERTS, D, EXPERT_FF), D ** -0.5),
        'w_down_e': nrm(ks[26], (L, N_EXPERTS, EXPERT_FF, D), EXPERT_FF ** -0.5),
        'final_norm_g': 1.0 + nrm(ks[27], (D,), 0.02),
    }


def reference(x, c, ctx, c_ctx, w_mod, b_mod, norm1_g, norm2_g, w_in, sgu_ln_g, sgu_ln_b,
              w_spatial, b_spatial, w_a_out, w_b_out, conv_w, conv_b, conv_ln_g, conv_ln_b,
              w_c_out, rpb, w_d_out, w_out, w_router, w_gate_e, w_up_e, w_down_e, final_norm_g):
    xc = ctx
    silu_c = jax.nn.silu(c)
    silu_cc = jax.nn.silu(c_ctx)
    for l in range(DEPTH):
        last = l == DEPTH - 1
        mod = (silu_c @ w_mod[l] + b_mod[l])[:, None, :]
        sh1, sc1, gt1, sh2, sc2, gt2 = jnp.split(mod, 6, axis=-1)
        cmod = silu_cc @ w_mod[l] + b_mod[l]
        csh1, csc1, cgt1, csh2, csc2, cgt2 = jnp.split(cmod, 6, axis=-1)

        h = rmsnorm(x, norm1_g[l]) * (1 + sc1) + sh1
        hc = rmsnorm(xc, norm1_g[l]) * (1 + csc1) + csh1
        za, zb, zcv, zq, zk, zv, zg = jnp.split(h @ w_in[l], SPLITS, axis=-1)
        if last:
            zkc, zvc = jnp.split(hc @ w_in[l][:, Q_END:V_END], 2, axis=-1)
        else:
            cza, czb, czcv, czq, zkc, zvc, czg = jnp.split(hc @ w_in[l], SPLITS, axis=-1)
        kc, vc = heads(zkc), heads(zvc)
        d_lat = neighbourhood_attention(heads(zq), heads(zk), heads(zv), kc, vc, rpb[l])
        mix = merge_branches(
            chunk_gmlp(za, sgu_ln_g[l], sgu_ln_b[l], w_spatial[l], b_spatial[l]),
            fourier_mix(zb),
            conformer_conv(zcv, conv_w[l], conv_b[l], conv_ln_g[l], conv_ln_b[l]),
            d_lat, zg, w_a_out[l], w_b_out[l], w_c_out[l], w_d_out[l], w_out[l])
        x = x + gt1 * mix
        if not last:
            cmix = merge_branches(
                chunk_gmlp(cza, sgu_ln_g[l], sgu_ln_b[l], w_spatial[l], b_spatial[l]),
                fourier_mix(czb),
                conformer_conv(czcv, conv_w[l], conv_b[l], conv_ln_g[l], conv_ln_b[l]),
                context_attention(heads(czq), kc, vc),
                czg, w_a_out[l], w_b_out[l], w_c_out[l], w_d_out[l], w_out[l])
            xc = xc + cgt1 * cmix

        h2 = rmsnorm(x, norm2_g[l]) * (1 + sc2) + sh2
        x = x + gt2 * expert_choice_ffn(h2, w_router[l], w_gate_e[l], w_up_e[l], w_down_e[l])
        if not last:
            hc2 = rmsnorm(xc, norm2_g[l]) * (1 + csc2) + csh2
            xc = xc + cgt2 * expert_choice_ffn(hc2, w_router[l], w_gate_e[l], w_up_e[l], w_down_e[l])
    return rmsnorm(x, final_norm_g)
```

```python
import functools

import jax
import jax.numpy as jnp
from jax import lax
from jax.experimental import pallas as pl
from jax.experimental.pallas import tpu as pltpu

D_MODEL = 1024
DEPTH = 2
GRID_W = 64
HEAD_DIM = 64
N_GROUPS = 4
BRANCH_WIDTH = N_GROUPS * HEAD_DIM
N_BRANCH = 4
CHUNK = 128
CONV_WIDTH = 31
WIN_ROWS = 8
WIN_COLS = 16
QB_COLS = 16
KB_COLS = QB_COLS + WIN_COLS
ROPE_BASE = 10000.0
N_EXPERTS = 16
EXPERT_FF = 1024
EC_CAPACITY = 2
EPS = 1e-6
NEG_INF = -1e30
A_END = 2 * BRANCH_WIDTH
B_END = A_END + BRANCH_WIDTH
C_END = B_END + 2 * BRANCH_WIDTH
Q_END = C_END + BRANCH_WIDTH
K_END = Q_END + BRANCH_WIDTH
V_END = K_END + BRANCH_WIDTH
IN_COLS = V_END + N_BRANCH * D_MODEL
SPLITS = [A_END, B_END, C_END, Q_END, K_END, V_END]

VMEM_LIMIT_BYTES = 56 * 1024 * 1024
F32 = jnp.float32
BF16 = jnp.bfloat16


def _params(*sem):
    return pltpu.CompilerParams(dimension_semantics=sem, vmem_limit_bytes=VMEM_LIMIT_BYTES)


def _norm_inproj_kernel(x_ref, g_ref, sc_ref, sh_ref, w_ref, o_ref, *, col_chunk):
    x = x_ref[0]
    y = x * lax.rsqrt(jnp.mean(x * x, axis=-1, keepdims=True) + EPS) * g_ref[...]
    h = (y * (1.0 + sc_ref[0]) + sh_ref[0]).astype(BF16)
    n_cols = w_ref.shape[1]
    for c in range(n_cols // col_chunk):
        cols = slice(c * col_chunk, (c + 1) * col_chunk)
        o_ref[0, :, cols] = jnp.dot(h, w_ref[:, cols], preferred_element_type=F32).astype(o_ref.dtype)


def norm_inproj(x, g, sc, sh, w, *, tm, out_dtype=F32):
    bsz, n, d = x.shape
    cols = w.shape[1]
    col_chunk = 512 if cols % 512 == 0 else cols
    return pl.pallas_call(
        functools.partial(_norm_inproj_kernel, col_chunk=col_chunk),
        grid=(bsz, n // tm),
        in_specs=[
            pl.BlockSpec((1, tm, d), lambda b, i: (b, i, 0)),
            pl.BlockSpec((1, d), lambda b, i: (0, 0)),
            pl.BlockSpec((1, 1, d), lambda b, i: (b, 0, 0)),
            pl.BlockSpec((1, 1, d), lambda b, i: (b, 0, 0)),
            pl.BlockSpec((d, cols), lambda b, i: (0, 0)),
        ],
        out_specs=pl.BlockSpec((1, tm, cols), lambda b, i: (b, i, 0)),
        out_shape=jax.ShapeDtypeStruct((bsz, n, cols), out_dtype),
        compiler_params=_params("arbitrary", "arbitrary"),
        name="norm_inproj",
    )(x, g, sc, sh, w)


def _merge_kernel(a_ref, b_ref, c_ref, d_ref, ga_ref, gb_ref, gc_ref, gd_ref,
                  wa_ref, wb_ref, wc_ref, wd_ref, wo_ref, x_ref, gt_ref, o_ref):
    m = None
    for br, gz, w in ((a_ref, ga_ref, wa_ref), (b_ref, gb_ref, wb_ref),
                      (c_ref, gc_ref, wc_ref), (d_ref, gd_ref, wd_ref)):
        p = jnp.dot(br[0].astype(BF16), w[...], preferred_element_type=F32)
        t = jax.nn.sigmoid(gz[0].astype(F32)) * p
        m = t if m is None else m + t
    mix = jnp.dot(m.astype(BF16), wo_ref[...], preferred_element_type=F32)
    o_ref[0] = x_ref[0] + gt_ref[0] * mix


def merge_branches(a, b, cc, d, z, w_a, w_b, w_c, w_d, w_o, x, gt, *, tm):
    bsz, n, dm = x.shape
    w = a.shape[-1]
    g0 = V_END // dm
    br_spec = pl.BlockSpec((1, tm, w), lambda bi, i: (bi, i, 0))
    gate_specs = [pl.BlockSpec((1, tm, dm), functools.partial(lambda bi, i, k: (bi, i, g0 + k), k=k))
                  for k in range(N_BRANCH)]
    wbr_spec = pl.BlockSpec((w, dm), lambda bi, i: (0, 0))
    return pl.pallas_call(
        _merge_kernel,
        grid=(bsz, n // tm),
        in_specs=[br_spec] * 4 + gate_specs + [wbr_spec] * 4 + [
            pl.BlockSpec((dm, dm), lambda bi, i: (0, 0)),
            pl.BlockSpec((1, tm, dm), lambda bi, i: (bi, i, 0)),
            pl.BlockSpec((1, 1, dm), lambda bi, i: (bi, 0, 0)),
        ],
        out_specs=pl.BlockSpec((1, tm, dm), lambda bi, i: (bi, i, 0)),
        out_shape=jax.ShapeDtypeStruct((bsz, n, dm), F32),
        compiler_params=_params("arbitrary", "arbitrary"),
        name="merge_branches",
    )(a, b, cc, d, z, z, z, z, w_a, w_b, w_c, w_d, w_o, x, gt)


def _expert_ffn_kernel(xe_ref, gate_ref, wg_ref, wu_ref, wd_ref, o_ref, *, f_chunk):
    xe = xe_ref[0, 0].astype(BF16)
    ff = wg_ref.shape[2]
    acc = None
    for f in range(ff // f_chunk):
        cols = slice(f * f_chunk, (f + 1) * f_chunk)
        a = jnp.dot(xe, wg_ref[0, :, cols], preferred_element_type=F32)
        u = jnp.dot(xe, wu_ref[0, :, cols], preferred_element_type=F32)
        hm = (jax.nn.silu(a) * u).astype(BF16)
        t = jnp.dot(hm, wd_ref[0, cols, :], preferred_element_type=F32)
        acc = t if acc is None else acc + t
    o_ref[0, 0] = acc * gate_ref[0, 0]


def expert_ffn(xe, gate, wg, wu, wd):
    bsz, ne, cap, d = xe.shape
    ff = wg.shape[2]
    return pl.pallas_call(
        functools.partial(_expert_ffn_kernel, f_chunk=256),
        grid=(ne, bsz),
        in_specs=[
            pl.BlockSpec((1, 1, cap, d), lambda e, b: (b, e, 0, 0)),
            pl.BlockSpec((1, 1, cap, 1), lambda e, b: (b, e, 0, 0)),
            pl.BlockSpec((1, d, ff), lambda e, b: (e, 0, 0)),
            pl.BlockSpec((1, d, ff), lambda e, b: (e, 0, 0)),
            pl.BlockSpec((1, ff, d), lambda e, b: (e, 0, 0)),
        ],
        out_specs=pl.BlockSpec((1, 1, cap, d), lambda e, b: (b, e, 0, 0)),
        out_shape=jax.ShapeDtypeStruct((bsz, ne, cap, d), F32),
        compiler_params=_params("arbitrary", "arbitrary"),
        name="expert_ffn",
    )(xe, gate, wg, wu, wd)


def _layernorm(x, g, b):
    mu = jnp.mean(x, axis=-1, keepdims=True)
    var = jnp.mean(jnp.square(x - mu), axis=-1, keepdims=True)
    return (x - mu) * lax.rsqrt(var + EPS) * g + b


def _heads(z):
    return z.reshape(*z.shape[:-1], N_GROUPS, HEAD_DIM)


def _chunk_gmlp(z, ln_g, ln_b, w_s, b_s):
    z = jax.nn.gelu(z)
    u, v = jnp.split(z, 2, axis=-1)
    v = _layernorm(v, ln_g, ln_b)
    bsz, n, _ = v.shape
    v = v.reshape(bsz, n // CHUNK, CHUNK, N_GROUPS, HEAD_DIM)
    mixed = jnp.einsum('bcpgd,gqp->bcqgd', v, w_s) + b_s.T[None, None, :, :, None]
    return u * mixed.reshape(bsz, n, BRANCH_WIDTH)


def _fourier_mix(z):
    bsz, n, _ = z.shape
    zg = z.reshape(bsz, n, N_GROUPS, HEAD_DIM)
    f = jnp.fft.fft2(zg, axes=(1, 3), norm='ortho').real
    return f.reshape(bsz, n, BRANCH_WIDTH)


def _conformer_conv(z, conv_w, conv_b, ln_g, ln_b):
    a, gt = jnp.split(z, 2, axis=-1)
    y = a * jax.nn.sigmoid(gt)
    y = lax.conv_general_dilated(
        y, conv_w[:, None, :], window_strides=(1,),
        padding=[(CONV_WIDTH // 2, CONV_WIDTH // 2)],
        dimension_numbers=('NWC', 'WIO', 'NWC'),
        feature_group_count=BRANCH_WIDTH) + conv_b
    return jax.nn.silu(_layernorm(y, ln_g, ln_b))


def _axial_rope(x, rows, cols):
    half = HEAD_DIM // 2
    quarter = half // 2
    inv = ROPE_BASE ** (-jnp.arange(quarter, dtype=F32) / quarter)

    def rot(xp, pos):
        ang = pos.astype(F32)[:, None] * inv[None, :]
        cos = jnp.cos(ang)[None, :, None, :]
        sin = jnp.sin(ang)[None, :, None, :]
        x1, x2 = jnp.split(xp, 2, axis=-1)
        return jnp.concatenate([x1 * cos - x2 * sin, x1 * sin + x2 * cos], axis=-1)

    return jnp.concatenate([rot(x[..., :half], rows), rot(x[..., half:], cols)], axis=-1)


def _neighbourhood_attention(q, k, v, kc, vc, rpb):
    bsz, n, nh, dh = q.shape
    n_rows = n // GRID_W
    wr = min(WIN_ROWS, n_rows)
    n_cb = GRID_W // QB_COLS
    n_win = wr * KB_COLS
    t = jnp.arange(n)
    qs = q * (dh ** -0.5)
    q_rot = _axial_rope(qs, t // GRID_W, t % GRID_W)
    k_rot = _axial_rope(k, t // GRID_W, t % GRID_W)
    r = jnp.arange(n_rows)
    key_rows = jnp.clip(r - wr // 2, 0, n_rows - wr)[:, None] + jnp.arange(wr)[None, :]
    jb = jnp.arange(n_cb)
    key_cols = (jnp.clip(jb * QB_COLS - WIN_COLS // 2, 0, GRID_W - KB_COLS)[:, None]
                + jnp.arange(KB_COLS)[None, :])
    q_cols = jb[:, None] * QB_COLS + jnp.arange(QB_COLS)[None, :]
    c_start = jnp.clip(q_cols - WIN_COLS // 2, 0, GRID_W - WIN_COLS)
    kcol = key_cols[:, None, :]
    col_ok = (kcol >= c_start[..., None]) & (kcol < c_start[..., None] + WIN_COLS)
    d_row = key_rows - r[:, None] + (WIN_ROWS - 1)
    d_col = jnp.clip(kcol - q_cols[..., None], 1 - WIN_COLS, WIN_COLS - 1) + (WIN_COLS - 1)
    bias = rpb[:, d_row[:, None, None, :, None], d_col[None, :, :, None, :]].astype(F32)
    bias = jnp.where(col_ok[None, None, :, :, None, :], bias, NEG_INF)
    bias = bias.reshape(nh, n_rows, n_cb, QB_COLS, n_win)
    key_idx = (key_rows[:, None, :, None] * GRID_W + key_cols[None, :, None, :]).reshape(n_rows, n_cb, n_win)
    k_win = k_rot[:, key_idx]
    v_win = v[:, key_idx]
    q_rot = q_rot.reshape(bsz, n_rows, n_cb, QB_COLS, nh, dh)
    q_pl = qs.reshape(bsz, n_rows, n_cb, QB_COLS, nh, dh)
    s_win = jnp.einsum('brjqhd,brjkhd->bhrjqk', q_rot, k_win) + bias
    s_ctx = jnp.einsum('brjqhd,bchd->bhrjqc', q_pl, kc)
    p = jax.nn.softmax(jnp.concatenate([s_win, s_ctx], axis=-1), axis=-1)
    out = (jnp.einsum('bhrjqk,brjkhd->brjqhd', p[..., :n_win], v_win)
           + jnp.einsum('bhrjqc,bchd->brjqhd', p[..., n_win:], vc))
    return out.reshape(bsz, n, nh * dh)


def _context_attention(q, k, v):
    bsz, n, nh, dh = q.shape
    s = jnp.einsum('bqhd,bkhd->bhqk', q * (dh ** -0.5), k)
    p = jax.nn.softmax(s, axis=-1)
    return jnp.einsum('bhqk,bkhd->bqhd', p, v).reshape(bsz, n, nh * dh)


def _expert_choice_ffn(h, router, wg, wu, wd):
    bsz, n, d = h.shape
    cap = EC_CAPACITY * n // N_EXPERTS
    logits = jnp.einsum('bnd,de->bne', h, router, precision=lax.Precision.HIGHEST)
    aff = jax.nn.softmax(logits, axis=-1)
    gate, idx = lax.top_k(jnp.swapaxes(aff, 1, 2), cap)
    xe = jax.vmap(lambda hb, ib: hb[ib])(h, idx)
    y = expert_ffn(xe, gate[..., None], wg, wu, wd)
    return jax.vmap(lambda ib, yb: jnp.zeros((n, d), h.dtype).at[ib.reshape(-1)].add(yb.reshape(-1, d)))(idx, y)


def _mod_norm(x, g, sc, sh):
    y = x * lax.rsqrt(jnp.mean(x * x, axis=-1, keepdims=True) + EPS) * g
    return y * (1 + sc) + sh


def kernel(x, c, ctx, c_ctx, w_mod, b_mod, norm1_g, norm2_g, w_in, sgu_ln_g, sgu_ln_b, w_spatial, b_spatial,
           w_a_out, w_b_out, conv_w, conv_b, conv_ln_g, conv_ln_b, w_c_out, rpb, w_d_out, w_out, w_router,
           w_gate_e, w_up_e, w_down_e, final_norm_g):
    bsz = x.shape[0]
    xc = ctx
    silu_c = jax.nn.silu(c)
    silu_cc = jax.nn.silu(c_ctx)
    for l in range(DEPTH):
        last = l == DEPTH - 1
        hp = lax.Precision.HIGHEST
        mod = (jnp.dot(silu_c, w_mod[l], precision=hp) + b_mod[l])[:, None, :]
        sh1, sc1, gt1, sh2, sc2, gt2 = jnp.split(mod, 6, axis=-1)
        cmod = jnp.broadcast_to((jnp.dot(silu_cc, w_mod[l], precision=hp) + b_mod[l])[None, None, :],
                                (bsz, 1, 6 * D_MODEL))
        csh1, csc1, cgt1, csh2, csc2, cgt2 = jnp.split(cmod, 6, axis=-1)

        w_in_b = w_in[l].astype(BF16)
        wa, wb, wc, wd_, wo = (w.astype(BF16) for w in (w_a_out[l], w_b_out[l], w_c_out[l], w_d_out[l], w_out[l]))
        wg, wu, wdn = w_gate_e[l].astype(BF16), w_up_e[l].astype(BF16), w_down_e[l].astype(BF16)
        g1 = norm1_g[l][None, :]

        z = norm_inproj(x, g1, sc1, sh1, w_in_b, tm=256)
        za, zb, zcv, zq, zk, zv, _ = jnp.split(z, SPLITS, axis=-1)
        if last:
            zc = norm_inproj(xc, g1, csc1, csh1, w_in_b[:, Q_END:V_END], tm=CTX_TM)
            zkc, zvc = jnp.split(zc, 2, axis=-1)
        else:
            zc = norm_inproj(xc, g1, csc1, csh1, w_in_b, tm=CTX_TM)
            cza, czb, czcv, czq, zkc, zvc, _ = jnp.split(zc, SPLITS, axis=-1)
        kc, vc = _heads(zkc), _heads(zvc)
        d_lat = _neighbourhood_attention(_heads(zq), _heads(zk), _heads(zv), kc, vc, rpb[l])
        a_lat = _chunk_gmlp(za, sgu_ln_g[l], sgu_ln_b[l], w_spatial[l], b_spatial[l])
        b_lat = _fourier_mix(zb)
        c_lat = _conformer_conv(zcv, conv_w[l], conv_b[l], conv_ln_g[l], conv_ln_b[l])
        x = merge_branches(a_lat, b_lat, c_lat, d_lat, z, wa, wb, wc, wd_, wo, x, gt1, tm=512)
        if not last:
            a_c = _chunk_gmlp(cza, sgu_ln_g[l], sgu_ln_b[l], w_spatial[l], b_spatial[l])
            b_c = _fourier_mix(czb)
            c_c = _conformer_conv(czcv, conv_w[l], conv_b[l], conv_ln_g[l], conv_ln_b[l])
            d_c = _context_attention(_heads(czq), kc, vc)
            xc = merge_branches(a_c, b_c, c_c, d_c, zc, wa, wb, wc, wd_, wo, xc, cgt1, tm=CTX_TM)

        h2 = _mod_norm(x, norm2_g[l], sc2, sh2)
        x = x + gt2 * _expert_choice_ffn(h2, w_router[l], wg, wu, wdn)
        if not last:
            hc2 = _mod_norm(xc, norm2_g[l], csc2, csh2)
            xc = xc + cgt2 * _expert_choice_ffn(hc2, w_router[l], wg, wu, wdn)
    return x * lax.rsqrt(jnp.mean(x * x, axis=-1, keepdims=True) + EPS) * final_norm_g


CTX_TM = 256
```

```python
import functools

import jax
import jax.numpy as jnp
from jax import lax
from jax.experimental import pallas as pl
from jax.experimental.pallas import tpu as pltpu

D_MODEL = 1024
DEPTH = 2
GRID_W = 64
HEAD_DIM = 64
N_GROUPS = 4
BRANCH_WIDTH = N_GROUPS * HEAD_DIM
N_BRANCH = 4
CHUNK = 128
CONV_WIDTH = 31
WIN_ROWS = 8
WIN_COLS = 16
QB_COLS = 16
KB_COLS = QB_COLS + WIN_COLS
ROPE_BASE = 10000.0
N_EXPERTS = 16
EXPERT_FF = 1024
EC_CAPACITY = 2
EPS = 1e-6
NEG_INF = -1e30
A_END = 2 * BRANCH_WIDTH
B_END = A_END + BRANCH_WIDTH
C_END = B_END + 2 * BRANCH_WIDTH
Q_END = C_END + BRANCH_WIDTH
K_END = Q_END + BRANCH_WIDTH
V_END = K_END + BRANCH_WIDTH
IN_COLS = V_END + N_BRANCH * D_MODEL
SPLITS = [A_END, B_END, C_END, Q_END, K_END, V_END]

VMEM_LIMIT_BYTES = 56 * 1024 * 1024
F32 = jnp.float32
BF16 = jnp.bfloat16


def _params(*sem):
    return pltpu.CompilerParams(dimension_semantics=sem, vmem_limit_bytes=VMEM_LIMIT_BYTES)


def _norm_inproj_kernel(x_ref, g_ref, sc_ref, sh_ref, w_ref, o_ref, *, col_chunk):
    x = x_ref[0]
    y = x * lax.rsqrt(jnp.mean(x * x, axis=-1, keepdims=True) + EPS) * g_ref[...]
    h = (y * (1.0 + sc_ref[0]) + sh_ref[0]).astype(BF16)
    n_cols = w_ref.shape[1]
    for c in range(n_cols // col_chunk):
        cols = slice(c * col_chunk, (c + 1) * col_chunk)
        o_ref[0, :, cols] = jnp.dot(h, w_ref[:, cols], preferred_element_type=F32).astype(o_ref.dtype)


def norm_inproj(x, g, sc, sh, w, *, tm, out_dtype=F32):
    bsz, n, d = x.shape
    cols = w.shape[1]
    col_chunk = 512 if cols % 512 == 0 else cols
    return pl.pallas_call(
        functools.partial(_norm_inproj_kernel, col_chunk=col_chunk),
        grid=(bsz, n // tm),
        in_specs=[
            pl.BlockSpec((1, tm, d), lambda b, i: (b, i, 0)),
            pl.BlockSpec((1, d), lambda b, i: (0, 0)),
            pl.BlockSpec((1, 1, d), lambda b, i: (b, 0, 0)),
            pl.BlockSpec((1, 1, d), lambda b, i: (b, 0, 0)),
            pl.BlockSpec((d, cols), lambda b, i: (0, 0)),
        ],
        out_specs=pl.BlockSpec((1, tm, cols), lambda b, i: (b, i, 0)),
        out_shape=jax.ShapeDtypeStruct((bsz, n, cols), out_dtype),
        compiler_params=_params("arbitrary", "arbitrary"),
        name="norm_inproj",
    )(x, g, sc, sh, w)


def _merge_kernel(a_ref, b_ref, c_ref, d_ref, ga_ref, gb_ref, gc_ref, gd_ref,
                  wa_ref, wb_ref, wc_ref, wd_ref, wo_ref, x_ref, gt_ref, o_ref):
    m = None
    for br, gz, w in ((a_ref, ga_ref, wa_ref), (b_ref, gb_ref, wb_ref),
                      (c_ref, gc_ref, wc_ref), (d_ref, gd_ref, wd_ref)):
        p = jnp.dot(br[0].astype(BF16), w[...], preferred_element_type=F32)
        t = jax.nn.sigmoid(gz[0].astype(F32)) * p
        m = t if m is None else m + t
    mix = jnp.dot(m.astype(BF16), wo_ref[...], preferred_element_type=F32)
    o_ref[0] = x_ref[0] + gt_ref[0] * mix


def merge_branches(a, b, cc, d, z, w_a, w_b, w_c, w_d, w_o, x, gt, *, tm):
    bsz, n, dm = x.shape
    w = a.shape[-1]
    g0 = V_END // dm
    br_spec = pl.BlockSpec((1, tm, w), lambda bi, i: (bi, i, 0))
    gate_specs = [pl.BlockSpec((1, tm, dm), functools.partial(lambda bi, i, k: (bi, i, g0 + k), k=k))
                  for k in range(N_BRANCH)]
    wbr_spec = pl.BlockSpec((w, dm), lambda bi, i: (0, 0))
    return pl.pallas_call(
        _merge_kernel,
        grid=(bsz, n // tm),
        in_specs=[br_spec] * 4 + gate_specs + [wbr_spec] * 4 + [
            pl.BlockSpec((dm, dm), lambda bi, i: (0, 0)),
            pl.BlockSpec((1, tm, dm), lambda bi, i: (bi, i, 0)),
            pl.BlockSpec((1, 1, dm), lambda bi, i: (bi, 0, 0)),
        ],
        out_specs=pl.BlockSpec((1, tm, dm), lambda bi, i: (bi, i, 0)),
        out_shape=jax.ShapeDtypeStruct((bsz, n, dm), F32),
        compiler_params=_params("arbitrary", "arbitrary"),
        name="merge_branches",
    )(a, b, cc, d, z, z, z, z, w_a, w_b, w_c, w_d, w_o, x, gt)


def _expert_ffn_kernel(xe_ref, gate_ref, wg_ref, wu_ref, wd_ref, o_ref, *, f_chunk):
    xe = xe_ref[0, 0].astype(BF16)
    ff = wg_ref.shape[2]
    acc = None
    for f in range(ff // f_chunk):
        cols = slice(f * f_chunk, (f + 1) * f_chunk)
        a = jnp.dot(xe, wg_ref[0, :, cols], preferred_element_type=F32)
        u = jnp.dot(xe, wu_ref[0, :, cols], preferred_element_type=F32)
        hm = (jax.nn.silu(a) * u).astype(BF16)
        t = jnp.dot(hm, wd_ref[0, cols, :], preferred_element_type=F32)
        acc = t if acc is None else acc + t
    o_ref[0, 0] = acc * gate_ref[0, 0]


def expert_ffn(xe, gate, wg, wu, wd):
    bsz, ne, cap, d = xe.shape
    ff = wg.shape[2]
    return pl.pallas_call(
        functools.partial(_expert_ffn_kernel, f_chunk=256),
        grid=(ne, bsz),
        in_specs=[
            pl.BlockSpec((1, 1, cap, d), lambda e, b: (b, e, 0, 0)),
            pl.BlockSpec((1, 1, cap, 1), lambda e, b: (b, e, 0, 0)),
            pl.BlockSpec((1, d, ff), lambda e, b: (e, 0, 0)),
            pl.BlockSpec((1, d, ff), lambda e, b: (e, 0, 0)),
            pl.BlockSpec((1, ff, d), lambda e, b: (e, 0, 0)),
        ],
        out_specs=pl.BlockSpec((1, 1, cap, d), lambda e, b: (b, e, 0, 0)),
        out_shape=jax.ShapeDtypeStruct((bsz, ne, cap, d), F32),
        compiler_params=_params("arbitrary", "arbitrary"),
        name="expert_ffn",
    )(xe, gate, wg, wu, wd)


ATT_QROWS = 4
ATT_QTOK = ATT_QROWS * GRID_W
ATT_KROWS = ATT_QROWS + WIN_ROWS
ATT_KBLK = ATT_KROWS * GRID_W // ATT_QTOK
_NT = (((1,), (1,)), ((), ()))


def _rope_tables(n):
    quarter = HEAD_DIM // 4
    inv = ROPE_BASE ** (-jnp.arange(quarter, dtype=F32) / quarter)
    t = jnp.arange(n)
    ang_r = (t // GRID_W).astype(F32)[:, None] * inv[None, :]
    ang_c = (t % GRID_W).astype(F32)[:, None] * inv[None, :]
    cos = jnp.concatenate([jnp.cos(ang_r)] * 2 + [jnp.cos(ang_c)] * 2, axis=1)
    sin = jnp.concatenate([-jnp.sin(ang_r), jnp.sin(ang_r), -jnp.sin(ang_c), jnp.sin(ang_c)], axis=1)
    return jnp.concatenate([cos, cos], axis=1), jnp.concatenate([sin, sin], axis=1)


def _attn_bias_tables(rpb, n_rows):
    import numpy as np
    n_blk = n_rows // ATT_QROWS
    tabs = []
    for j in (0, 1, n_blk - 1):
        ks = min(max(ATT_QROWS * j - WIN_ROWS // 2, 0), n_rows - ATT_KROWS)
        r = ATT_QROWS * j + np.arange(ATT_QROWS)
        kr0 = np.clip(r - WIN_ROWS // 2, 0, n_rows - WIN_ROWS)
        krow = ks + np.arange(ATT_KROWS)
        row_ok = (krow[None, :] >= kr0[:, None]) & (krow[None, :] < kr0[:, None] + WIN_ROWS)
        d_row = np.clip(krow[None, :] - r[:, None] + (WIN_ROWS - 1), 0, 2 * WIN_ROWS - 2)
        qc = np.arange(GRID_W)
        kc = np.arange(GRID_W)
        c_start = np.clip(qc - WIN_COLS // 2, 0, GRID_W - WIN_COLS)
        col_ok = (kc[None, :] >= c_start[:, None]) & (kc[None, :] < c_start[:, None] + WIN_COLS)
        d_col = np.clip(kc[None, :] - qc[:, None], 1 - WIN_COLS, WIN_COLS - 1) + (WIN_COLS - 1)
        dr = np.broadcast_to(d_row[:, None, :, None], (ATT_QROWS, GRID_W, ATT_KROWS, GRID_W)).reshape(ATT_QTOK, -1)
        dc = np.broadcast_to(d_col[None, :, None, :], (ATT_QROWS, GRID_W, ATT_KROWS, GRID_W)).reshape(ATT_QTOK, -1)
        ok = (row_ok[:, None, :, None] & col_ok[None, :, None, :]).reshape(ATT_QTOK, -1)
        tabs.append(jnp.where(ok[None], rpb[:, dr, dc].astype(F32), NEG_INF))
    return jnp.stack(tabs)


def _rope(x, cos_ref, sin_ref, first16):
    c = cos_ref[...]
    s = sin_ref[...]
    c2 = jnp.concatenate([c, c], axis=1)
    s2 = jnp.concatenate([s, s], axis=1)
    w = x.shape[1]
    partner = jnp.where(first16, pltpu.roll(x, w - HEAD_DIM // 4, 1), pltpu.roll(x, HEAD_DIM // 4, 1))
    return x * c2 + partner * s2


def _softmax_pv(scores_values, head_mask, acc):
    m = None
    for s, _ in scores_values:
        mx = jnp.max(s, axis=1, keepdims=True)
        m = mx if m is None else jnp.maximum(m, mx)
    den = None
    o = None
    for s, v in scores_values:
        p = jnp.exp(s - m)
        sm = jnp.sum(p, axis=1, keepdims=True)
        den = sm if den is None else den + sm
        t = jnp.dot(p.astype(BF16), v, preferred_element_type=F32)
        o = t if o is None else o + t
    return acc + jnp.where(head_mask, o * (1.0 / den), 0.0)


def _nattn_kernel(q_ref, k0_ref, k1_ref, k2_ref, v0_ref, v1_ref, v2_ref,
                  cq_ref, sq_ref, ck0_ref, ck1_ref, ck2_ref, sk0_ref, sk1_ref, sk2_ref,
                  kc_ref, vc_ref, bias_ref, o_ref):
    w = q_ref.shape[2]
    lane = lax.broadcasted_iota(jnp.int32, (1, w), 1)
    first16 = (lane % (HEAD_DIM // 2)) < (HEAD_DIM // 4)
    q = q_ref[0] * (HEAD_DIM ** -0.5)
    q_rot = _rope(q, cq_ref, sq_ref, first16)
    k_rot = jnp.concatenate([_rope(k0_ref[0], ck0_ref, sk0_ref, first16),
                             _rope(k1_ref[0], ck1_ref, sk1_ref, first16),
                             _rope(k2_ref[0], ck2_ref, sk2_ref, first16)], axis=0).astype(BF16)
    v = jnp.concatenate([v0_ref[0], v1_ref[0], v2_ref[0]], axis=0).astype(BF16)
    kc = kc_ref[0].astype(BF16)
    vc = vc_ref[0].astype(BF16)
    acc = jnp.zeros(q.shape, F32)
    for h in range(N_GROUPS):
        hm = (lane // HEAD_DIM) == h
        s_win = lax.dot_general(jnp.where(hm, q_rot, 0.0).astype(BF16), k_rot, _NT,
                                preferred_element_type=F32) + bias_ref[0, h]
        s_ctx = lax.dot_general(jnp.where(hm, q, 0.0).astype(BF16), kc, _NT, preferred_element_type=F32)
        acc = _softmax_pv([(s_win, v), (s_ctx, vc)], hm, acc)
    o_ref[0] = acc


def neighbourhood_attention(z, zc, rpb, *, q_col, k_col, v_col, kc_col, vc_col):
    bsz, n, _ = z.shape
    n_ctx = zc.shape[1]
    w = BRANCH_WIDTH
    n_blk = n // ATT_QTOK
    cos_t, sin_t = _rope_tables(n)
    bias = _attn_bias_tables(rpb, n // GRID_W)

    def kb(j):
        return jnp.clip(j - 1, 0, n_blk - ATT_KBLK)

    def zspec(col, off=None):
        if off is None:
            return pl.BlockSpec((1, ATT_QTOK, w), lambda b, j: (b, j, col))
        return pl.BlockSpec((1, ATT_QTOK, w), lambda b, j: (b, kb(j) + off, col))

    def tspec(off=None):
        if off is None:
            return pl.BlockSpec((ATT_QTOK, 2 * HEAD_DIM), lambda b, j: (j, 0))
        return pl.BlockSpec((ATT_QTOK, 2 * HEAD_DIM), lambda b, j: (kb(j) + off, 0))

    in_specs = ([zspec(q_col)] + [zspec(k_col, i) for i in range(ATT_KBLK)] + [zspec(v_col, i) for i in range(ATT_KBLK)]
                + [tspec(), tspec()] + [tspec(i) for i in range(ATT_KBLK)] * 2
                + [pl.BlockSpec((1, n_ctx, w), lambda b, j: (b, 0, kc_col)),
                   pl.BlockSpec((1, n_ctx, w), lambda b, j: (b, 0, vc_col)),
                   pl.BlockSpec((1, N_GROUPS, ATT_QTOK, ATT_KROWS * GRID_W),
                                lambda b, j: (jnp.minimum(j, 1) + j // (n_blk - 1), 0, 0, 0))])
    return pl.pallas_call(
        _nattn_kernel,
        grid=(bsz, n_blk),
        in_specs=in_specs,
        out_specs=pl.BlockSpec((1, ATT_QTOK, w), lambda b, j: (b, j, 0)),
        out_shape=jax.ShapeDtypeStruct((bsz, n, w), F32),
        compiler_params=_params("arbitrary", "arbitrary"),
        name="neighbourhood_attention",
    )(z, z, z, z, z, z, z, cos_t, sin_t, cos_t, cos_t, cos_t, sin_t, sin_t, sin_t, zc, zc, bias)


def _ctx_attn_kernel(q_ref, kc_ref, vc_ref, o_ref):
    w = q_ref.shape[2]
    lane = lax.broadcasted_iota(jnp.int32, (1, w), 1)
    q = q_ref[0] * (HEAD_DIM ** -0.5)
    kc = kc_ref[0].astype(BF16)
    vc = vc_ref[0].astype(BF16)
    acc = jnp.zeros(q.shape, F32)
    for h in range(N_GROUPS):
        hm = (lane // HEAD_DIM) == h
        s = lax.dot_general(jnp.where(hm, q, 0.0).astype(BF16), kc, _NT, preferred_element_type=F32)
        acc = _softmax_pv([(s, vc)], hm, acc)
    o_ref[0] = acc


def context_attention(zc, *, q_col, k_col, v_col):
    bsz, n_ctx, _ = zc.shape
    w = BRANCH_WIDTH
    return pl.pallas_call(
        _ctx_attn_kernel,
        grid=(bsz,),
        in_specs=[pl.BlockSpec((1, n_ctx, w), functools.partial(lambda b, c: (b, 0, c), c=c))
                  for c in (q_col, k_col, v_col)],
        out_specs=pl.BlockSpec((1, n_ctx, w), lambda b: (b, 0, 0)),
        out_shape=jax.ShapeDtypeStruct((bsz, n_ctx, w), F32),
        compiler_params=_params("arbitrary"),
        name="context_attention",
    )(zc, zc, zc)


def _layernorm(x, g, b):
    mu = jnp.mean(x, axis=-1, keepdims=True)
    var = jnp.mean(jnp.square(x - mu), axis=-1, keepdims=True)
    return (x - mu) * lax.rsqrt(var + EPS) * g + b


def _heads(z):
    return z.reshape(*z.shape[:-1], N_GROUPS, HEAD_DIM)


def _chunk_gmlp(z, ln_g, ln_b, w_s, b_s):
    z = jax.nn.gelu(z)
    u, v = jnp.split(z, 2, axis=-1)
    v = _layernorm(v, ln_g, ln_b)
    bsz, n, _ = v.shape
    v = v.reshape(bsz, n // CHUNK, CHUNK, N_GROUPS, HEAD_DIM)
    mixed = jnp.einsum('bcpgd,gqp->bcqgd', v, w_s) + b_s.T[None, None, :, :, None]
    return u * mixed.reshape(bsz, n, BRANCH_WIDTH)


def _fourier_mix(z):
    bsz, n, _ = z.shape
    zg = z.reshape(bsz, n, N_GROUPS, HEAD_DIM)
    f = jnp.fft.fft2(zg, axes=(1, 3), norm='ortho').real
    return f.reshape(bsz, n, BRANCH_WIDTH)


def _conformer_conv(z, conv_w, conv_b, ln_g, ln_b):
    a, gt = jnp.split(z, 2, axis=-1)
    y = a * jax.nn.sigmoid(gt)
    y = lax.conv_general_dilated(
        y, conv_w[:, None, :], window_strides=(1,),
        padding=[(CONV_WIDTH // 2, CONV_WIDTH // 2)],
        dimension_numbers=('NWC', 'WIO', 'NWC'),
        feature_group_count=BRANCH_WIDTH) + conv_b
    return jax.nn.silu(_layernorm(y, ln_g, ln_b))


def _axial_rope(x, rows, cols):
    half = HEAD_DIM // 2
    quarter = half // 2
    inv = ROPE_BASE ** (-jnp.arange(quarter, dtype=F32) / quarter)

    def rot(xp, pos):
        ang = pos.astype(F32)[:, None] * inv[None, :]
        cos = jnp.cos(ang)[None, :, None, :]
        sin = jnp.sin(ang)[None, :, None, :]
        x1, x2 = jnp.split(xp, 2, axis=-1)
        return jnp.concatenate([x1 * cos - x2 * sin, x1 * sin + x2 * cos], axis=-1)

    return jnp.concatenate([rot(x[..., :half], rows), rot(x[..., half:], cols)], axis=-1)


def _neighbourhood_attention(q, k, v, kc, vc, rpb):
    bsz, n, nh, dh = q.shape
    n_rows = n // GRID_W
    wr = min(WIN_ROWS, n_rows)
    n_cb = GRID_W // QB_COLS
    n_win = wr * KB_COLS
    t = jnp.arange(n)
    qs = q * (dh ** -0.5)
    q_rot = _axial_rope(qs, t // GRID_W, t % GRID_W)
    k_rot = _axial_rope(k, t // GRID_W, t % GRID_W)
    r = jnp.arange(n_rows)
    key_rows = jnp.clip(r - wr // 2, 0, n_rows - wr)[:, None] + jnp.arange(wr)[None, :]
    jb = jnp.arange(n_cb)
    key_cols = (jnp.clip(jb * QB_COLS - WIN_COLS // 2, 0, GRID_W - KB_COLS)[:, None]
                + jnp.arange(KB_COLS)[None, :])
    q_cols = jb[:, None] * QB_COLS + jnp.arange(QB_COLS)[None, :]
    c_start = jnp.clip(q_cols - WIN_COLS // 2, 0, GRID_W - WIN_COLS)
    kcol = key_cols[:, None, :]
    col_ok = (kcol >= c_start[..., None]) & (kcol < c_start[..., None] + WIN_COLS)
    d_row = key_rows - r[:, None] + (WIN_ROWS - 1)
    d_col = jnp.clip(kcol - q_cols[..., None], 1 - WIN_COLS, WIN_COLS - 1) + (WIN_COLS - 1)
    bias = rpb[:, d_row[:, None, None, :, None], d_col[None, :, :, None, :]].astype(F32)
    bias = jnp.where(col_ok[None, None, :, :, None, :], bias, NEG_INF)
    bias = bias.reshape(nh, n_rows, n_cb, QB_COLS, n_win)
    key_idx = (key_rows[:, None, :, None] * GRID_W + key_cols[None, :, None, :]).reshape(n_rows, n_cb, n_win)
    k_win = k_rot[:, key_idx]
    v_win = v[:, key_idx]
    q_rot = q_rot.reshape(bsz, n_rows, n_cb, QB_COLS, nh, dh)
    q_pl = qs.reshape(bsz, n_rows, n_cb, QB_COLS, nh, dh)
    s_win = jnp.einsum('brjqhd,brjkhd->bhrjqk', q_rot, k_win) + bias
    s_ctx = jnp.einsum('brjqhd,bchd->bhrjqc', q_pl, kc)
    p = jax.nn.softmax(jnp.concatenate([s_win, s_ctx], axis=-1), axis=-1)
    out = (jnp.einsum('bhrjqk,brjkhd->brjqhd', p[..., :n_win], v_win)
           + jnp.einsum('bhrjqc,bchd->brjqhd', p[..., n_win:], vc))
    return out.reshape(bsz, n, nh * dh)


def _context_attention(q, k, v):
    bsz, n, nh, dh = q.shape
    s = jnp.einsum('bqhd,bkhd->bhqk', q * (dh ** -0.5), k)
    p = jax.nn.softmax(s, axis=-1)
    return jnp.einsum('bhqk,bkhd->bqhd', p, v).reshape(bsz, n, nh * dh)


def _expert_choice_ffn(h, router, wg, wu, wd):
    bsz, n, d = h.shape
    cap = EC_CAPACITY * n // N_EXPERTS
    logits = jnp.einsum('bnd,de->bne', h, router, precision=lax.Precision.HIGHEST)
    aff = jax.nn.softmax(logits, axis=-1)
    gate, idx = lax.top_k(jnp.swapaxes(aff, 1, 2), cap)
    xe = jax.vmap(lambda hb, ib: hb[ib])(h, idx)
    y = expert_ffn(xe, gate[..., None], wg, wu, wd)
    return jax.vmap(lambda ib, yb: jnp.zeros((n, d), h.dtype).at[ib.reshape(-1)].add(yb.reshape(-1, d)))(idx, y)


def _mod_norm(x, g, sc, sh):
    y = x * lax.rsqrt(jnp.mean(x * x, axis=-1, keepdims=True) + EPS) * g
    return y * (1 + sc) + sh


def kernel(x, c, ctx, c_ctx, w_mod, b_mod, norm1_g, norm2_g, w_in, sgu_ln_g, sgu_ln_b, w_spatial, b_spatial,
           w_a_out, w_b_out, conv_w, conv_b, conv_ln_g, conv_ln_b, w_c_out, rpb, w_d_out, w_out, w_router,
           w_gate_e, w_up_e, w_down_e, final_norm_g):
    bsz = x.shape[0]
    xc = ctx
    silu_c = jax.nn.silu(c)
    silu_cc = jax.nn.silu(c_ctx)
    for l in range(DEPTH):
        last = l == DEPTH - 1
        hp = lax.Precision.HIGHEST
        mod = (jnp.dot(silu_c, w_mod[l], precision=hp) + b_mod[l])[:, None, :]
        sh1, sc1, gt1, sh2, sc2, gt2 = jnp.split(mod, 6, axis=-1)
        cmod = jnp.broadcast_to((jnp.dot(silu_cc, w_mod[l], precision=hp) + b_mod[l])[None, None, :],
                                (bsz, 1, 6 * D_MODEL))
        csh1, csc1, cgt1, csh2, csc2, cgt2 = jnp.split(cmod, 6, axis=-1)

        w_in_b = w_in[l].astype(BF16)
        wa, wb, wc, wd_, wo = (w.astype(BF16) for w in (w_a_out[l], w_b_out[l], w_c_out[l], w_d_out[l], w_out[l]))
        wg, wu, wdn = w_gate_e[l].astype(BF16), w_up_e[l].astype(BF16), w_down_e[l].astype(BF16)
        g1 = norm1_g[l][None, :]

        z = norm_inproj(x, g1, sc1, sh1, w_in_b, tm=256)
        za, zb, zcv, zq, zk, zv, _ = jnp.split(z, SPLITS, axis=-1)
        if last:
            zc = norm_inproj(xc, g1, csc1, csh1, w_in_b[:, Q_END:V_END], tm=CTX_TM)
            zkc, zvc = jnp.split(zc, 2, axis=-1)
        else:
            zc = norm_inproj(xc, g1, csc1, csh1, w_in_b, tm=CTX_TM)
            cza, czb, czcv, czq, zkc, zvc, _ = jnp.split(zc, SPLITS, axis=-1)
        qcol, kcol, vcol = C_END // BRANCH_WIDTH, Q_END // BRANCH_WIDTH, K_END // BRANCH_WIDTH
        d_lat = neighbourhood_attention(z, zc, rpb[l], q_col=qcol, k_col=kcol, v_col=vcol,
                                        kc_col=0 if last else kcol, vc_col=1 if last else vcol)
        a_lat = _chunk_gmlp(za, sgu_ln_g[l], sgu_ln_b[l], w_spatial[l], b_spatial[l])
        b_lat = _fourier_mix(zb)
        c_lat = _conformer_conv(zcv, conv_w[l], conv_b[l], conv_ln_g[l], conv_ln_b[l])
        x = merge_branches(a_lat, b_lat, c_lat, d_lat, z, wa, wb, wc, wd_, wo, x, gt1, tm=512)
        if not last:
            a_c = _chunk_gmlp(cza, sgu_ln_g[l], sgu_ln_b[l], w_spatial[l], b_spatial[l])
            b_c = _fourier_mix(czb)
            c_c = _conformer_conv(czcv, conv_w[l], conv_b[l], conv_ln_g[l], conv_ln_b[l])
            d_c = context_attention(zc, q_col=qcol, k_col=kcol, v_col=vcol)
            xc = merge_branches(a_c, b_c, c_c, d_c, zc, wa, wb, wc, wd_, wo, xc, cgt1, tm=CTX_TM)

        h2 = _mod_norm(x, norm2_g[l], sc2, sh2)
        x = x + gt2 * _expert_choice_ffn(h2, w_router[l], wg, wu, wdn)
        if not last:
            hc2 = _mod_norm(xc, norm2_g[l], csc2, csh2)
            xc = xc + cgt2 * _expert_choice_ffn(hc2, w_router[l], wg, wu, wdn)
    return x * lax.rsqrt(jnp.mean(x * x, axis=-1, keepdims=True) + EPS) * final_norm_g


CTX_TM = 256
```

```python
import functools

import jax
import jax.numpy as jnp
import numpy as np
from jax import lax
from jax.experimental import pallas as pl
from jax.experimental.pallas import tpu as pltpu

D_MODEL = 1024
DEPTH = 2
GRID_W = 64
HEAD_DIM = 64
N_GROUPS = 4
BRANCH_WIDTH = N_GROUPS * HEAD_DIM
N_BRANCH = 4
CHUNK = 128
CONV_WIDTH = 31
WIN_ROWS = 8
WIN_COLS = 16
QB_COLS = 16
KB_COLS = QB_COLS + WIN_COLS
ROPE_BASE = 10000.0
N_EXPERTS = 16
EXPERT_FF = 1024
EC_CAPACITY = 2
EPS = 1e-6
NEG_INF = -1e30
A_END = 2 * BRANCH_WIDTH
B_END = A_END + BRANCH_WIDTH
C_END = B_END + 2 * BRANCH_WIDTH
Q_END = C_END + BRANCH_WIDTH
K_END = Q_END + BRANCH_WIDTH
V_END = K_END + BRANCH_WIDTH
IN_COLS = V_END + N_BRANCH * D_MODEL
SPLITS = [A_END, B_END, C_END, Q_END, K_END, V_END]

VMEM_LIMIT_BYTES = 56 * 1024 * 1024
F32 = jnp.float32
BF16 = jnp.bfloat16


def _params(*sem):
    return pltpu.CompilerParams(dimension_semantics=sem, vmem_limit_bytes=VMEM_LIMIT_BYTES)


def _norm_inproj_kernel(x_ref, g_ref, sc_ref, sh_ref, w_ref, o_ref, *, col_chunk):
    x = x_ref[0]
    y = x * lax.rsqrt(jnp.mean(x * x, axis=-1, keepdims=True) + EPS) * g_ref[...]
    h = (y * (1.0 + sc_ref[0]) + sh_ref[0]).astype(BF16)
    n_cols = w_ref.shape[1]
    for c in range(n_cols // col_chunk):
        cols = slice(c * col_chunk, (c + 1) * col_chunk)
        o_ref[0, :, cols] = jnp.dot(h, w_ref[:, cols], preferred_element_type=F32).astype(o_ref.dtype)


def norm_inproj(x, g, sc, sh, w, *, tm, out_dtype=F32):
    bsz, n, d = x.shape
    cols = w.shape[1]
    col_chunk = 512 if cols % 512 == 0 else cols
    return pl.pallas_call(
        functools.partial(_norm_inproj_kernel, col_chunk=col_chunk),
        grid=(bsz, n // tm),
        in_specs=[
            pl.BlockSpec((1, tm, d), lambda b, i: (b, i, 0)),
            pl.BlockSpec((1, d), lambda b, i: (0, 0)),
            pl.BlockSpec((1, 1, d), lambda b, i: (b, 0, 0)),
            pl.BlockSpec((1, 1, d), lambda b, i: (b, 0, 0)),
            pl.BlockSpec((d, cols), lambda b, i: (0, 0)),
        ],
        out_specs=pl.BlockSpec((1, tm, cols), lambda b, i: (b, i, 0)),
        out_shape=jax.ShapeDtypeStruct((bsz, n, cols), out_dtype),
        compiler_params=_params("arbitrary", "arbitrary"),
        name="norm_inproj",
    )(x, g, sc, sh, w)


def _merge_kernel(a_ref, b_ref, c_ref, d_ref, ga_ref, gb_ref, gc_ref, gd_ref,
                  wa_ref, wb_ref, wc_ref, wd_ref, wo_ref, x_ref, gt_ref, o_ref):
    m = None
    for br, gz, w in ((a_ref, ga_ref, wa_ref), (b_ref, gb_ref, wb_ref),
                      (c_ref, gc_ref, wc_ref), (d_ref, gd_ref, wd_ref)):
        p = jnp.dot(br[0].astype(BF16), w[...], preferred_element_type=F32)
        t = jax.nn.sigmoid(gz[0].astype(F32)) * p
        m = t if m is None else m + t
    mix = jnp.dot(m.astype(BF16), wo_ref[...], preferred_element_type=F32)
    o_ref[0] = x_ref[0] + gt_ref[0] * mix


def merge_branches(a, b, cc, d, z, w_a, w_b, w_c, w_d, w_o, x, gt, *, tm):
    bsz, n, dm = x.shape
    w = a.shape[-1]
    g0 = V_END // dm
    br_spec = pl.BlockSpec((1, tm, w), lambda bi, i: (bi, i, 0))
    gate_specs = [pl.BlockSpec((1, tm, dm), functools.partial(lambda bi, i, k: (bi, i, g0 + k), k=k))
                  for k in range(N_BRANCH)]
    wbr_spec = pl.BlockSpec((w, dm), lambda bi, i: (0, 0))
    return pl.pallas_call(
        _merge_kernel,
        grid=(bsz, n // tm),
        in_specs=[br_spec] * 4 + gate_specs + [wbr_spec] * 4 + [
            pl.BlockSpec((dm, dm), lambda bi, i: (0, 0)),
            pl.BlockSpec((1, tm, dm), lambda bi, i: (bi, i, 0)),
            pl.BlockSpec((1, 1, dm), lambda bi, i: (bi, 0, 0)),
        ],
        out_specs=pl.BlockSpec((1, tm, dm), lambda bi, i: (bi, i, 0)),
        out_shape=jax.ShapeDtypeStruct((bsz, n, dm), F32),
        compiler_params=_params("arbitrary", "arbitrary"),
        name="merge_branches",
    )(a, b, cc, d, z, z, z, z, w_a, w_b, w_c, w_d, w_o, x, gt)


ROUTE_TB = 256
ROUTE_WIN = 64
ROUTE_ALIGN = 16
OFF_LANES = 128


def _split_bf16(v):
    hi = v.astype(BF16)
    return hi, (v - hi.astype(F32)).astype(BF16)


def _norm_router_kernel(x_ref, g_ref, sc_ref, sh_ref, rt_ref, h_ref, aff_ref):
    x = x_ref[0]
    y = x * lax.rsqrt(jnp.mean(x * x, axis=-1, keepdims=True) + EPS) * g_ref[...]
    h = y * (1.0 + sc_ref[0]) + sh_ref[0]
    h_hi, h_lo = _split_bf16(h)
    r_hi, r_lo = _split_bf16(rt_ref[...])
    logits = (lax.dot_general(r_hi, h_hi, _NT, preferred_element_type=F32)
              + lax.dot_general(r_hi, h_lo, _NT, preferred_element_type=F32)
              + lax.dot_general(r_lo, h_hi, _NT, preferred_element_type=F32))
    p = jnp.exp(logits - jnp.max(logits, axis=0, keepdims=True))
    aff_ref[0] = p / jnp.sum(p, axis=0, keepdims=True)
    h_ref[0] = h_hi


def norm_router(x, g, sc, sh, router_t, *, tm):
    bsz, n, d = x.shape
    ne = router_t.shape[0]
    return pl.pallas_call(
        _norm_router_kernel,
        grid=(bsz, n // tm),
        in_specs=[
            pl.BlockSpec((1, tm, d), lambda b, i: (b, i, 0)),
            pl.BlockSpec((1, d), lambda b, i: (0, 0)),
            pl.BlockSpec((1, 1, d), lambda b, i: (b, 0, 0)),
            pl.BlockSpec((1, 1, d), lambda b, i: (b, 0, 0)),
            pl.BlockSpec((ne, d), lambda b, i: (0, 0)),
        ],
        out_specs=[pl.BlockSpec((1, tm, d), lambda b, i: (b, i, 0)),
                   pl.BlockSpec((1, ne, tm), lambda b, i: (b, 0, i))],
        out_shape=[jax.ShapeDtypeStruct((bsz, n, d), BF16), jax.ShapeDtypeStruct((bsz, ne, n), F32)],
        compiler_params=_params("arbitrary", "arbitrary"),
        name="norm_router",
    )(x, g, sc, sh, router_t)


def _topk_kernel(aff_ref, slot_ref, off_ref, *, cap):
    a = aff_ref[0]
    ne, n = a.shape
    keys = pltpu.bitcast(a, jnp.int32)
    capf = jnp.float32(cap)

    def count(mask):
        return jnp.sum(jnp.where(mask, 1.0, 0.0), axis=1, keepdims=True)

    def bit_step(i, thr):
        cand = thr | (jnp.int32(1) << (30 - i))
        return jnp.where(count(keys >= cand) >= capf, cand, thr)

    thr = lax.fori_loop(0, 31, bit_step, jnp.zeros((ne, 1), jnp.int32))
    need = capf - count(keys > thr)
    r_i = lax.broadcasted_iota(jnp.int32, (ROUTE_TB, ROUTE_TB), 0)
    c_i = lax.broadcasted_iota(jnp.int32, (ROUTE_TB, ROUTE_TB), 1)
    tri = jnp.where(r_i < c_i, 1.0, 0.0).astype(BF16)
    lane = lax.broadcasted_iota(jnp.int32, (ne, OFF_LANES), 1)
    run_eq = jnp.zeros((ne, 1), F32)
    run_sel = jnp.zeros((ne, 1), F32)
    offs = jnp.zeros((ne, OFF_LANES), F32)
    for c in range(n // ROUTE_TB):
        cols = slice(c * ROUTE_TB, (c + 1) * ROUTE_TB)
        k_c = keys[:, cols]
        eq_c = jnp.where(k_c == thr, 1.0, 0.0)
        rank = jnp.dot(eq_c.astype(BF16), tri, preferred_element_type=F32) + run_eq
        sel = jnp.where(k_c > thr, 1.0, jnp.where(rank < need, eq_c, 0.0))
        pos = jnp.dot(sel.astype(BF16), tri, preferred_element_type=F32) + run_sel
        slot_ref[0, :, cols] = jnp.where(sel > 0.0, pos, -1.0).astype(jnp.int32)
        offs = jnp.where(lane == c, run_sel, offs)
        run_eq = run_eq + jnp.sum(eq_c, axis=1, keepdims=True)
        run_sel = run_sel + jnp.sum(sel, axis=1, keepdims=True)
    offs = jnp.where(lane == n // ROUTE_TB, run_sel, offs)
    off_ref[0] = offs.astype(jnp.int32)


def expert_topk(aff_t, cap):
    bsz, ne, n = aff_t.shape
    return pl.pallas_call(
        functools.partial(_topk_kernel, cap=cap),
        grid=(bsz,),
        in_specs=[pl.BlockSpec((1, ne, n), lambda b: (b, 0, 0))],
        out_specs=[pl.BlockSpec((1, ne, n), lambda b: (b, 0, 0)),
                   pl.BlockSpec((1, ne, OFF_LANES), lambda b: (b, 0, 0))],
        out_shape=[jax.ShapeDtypeStruct((bsz, ne, n), jnp.int32),
                   jax.ShapeDtypeStruct((bsz, ne, OFF_LANES), jnp.int32)],
        compiler_params=_params("arbitrary"),
        name="expert_topk",
    )(aff_t)


def _window_plan(off_ref, b, e, tb, w, rows_max):
    start = off_ref[b, e, tb]
    stop = off_ref[b, e, tb + 1]
    base = (start // ROUTE_ALIGN) * ROUTE_ALIGN
    rows = base + w * ROUTE_WIN
    active = rows < stop
    return pl.multiple_of(jnp.minimum(rows, rows_max), ROUTE_ALIGN), active


def _num_passes(off_ref, b, e_list, tb):
    n_pass = jnp.int32(0)
    for e in e_list:
        start = off_ref[b, e, tb]
        stop = off_ref[b, e, tb + 1]
        base = (start // ROUTE_ALIGN) * ROUTE_ALIGN
        n_pass = jnp.maximum(n_pass, (stop - base + ROUTE_WIN - 1) // ROUTE_WIN)
    return n_pass


def _gather_kernel(off_ref, h_ref, slot_ref, aff_ref, xe_ref, gate_ref, *, group):
    b, g, tb = pl.program_id(0), pl.program_id(1), pl.program_id(2)
    rows_max = xe_ref.shape[2] - ROUTE_WIN

    @pl.when(tb == 0)
    def _():
        xe_ref[...] = jnp.zeros(xe_ref.shape, xe_ref.dtype)
        gate_ref[...] = jnp.zeros(gate_ref.shape, gate_ref.dtype)

    sub = lax.broadcasted_iota(jnp.int32, (ROUTE_WIN, 1), 0)
    experts = [g * group + e for e in range(group)]

    def one_pass(w, carry):
        plans = [_window_plan(off_ref, b, ge, tb, w, rows_max) for ge in experts]
        hots = []
        for e, (rows, active) in enumerate(plans):
            want = jnp.where(active, rows + sub, -2)
            hots.append(slot_ref[0, e:e + 1, :] == want)
        onehot = jnp.concatenate([jnp.where(hm, 1.0, 0.0) for hm in hots], axis=0).astype(BF16)
        res = jnp.dot(onehot, h_ref[0], preferred_element_type=F32)
        for e, (rows, _) in enumerate(plans):
            win = pl.ds(rows, ROUTE_WIN)
            xe_ref[0, e, win, :] = xe_ref[0, e, win, :] + res[e * ROUTE_WIN:(e + 1) * ROUTE_WIN].astype(BF16)
            gsel = jnp.sum(jnp.where(hots[e], aff_ref[0, e:e + 1, :], 0.0), axis=1, keepdims=True)
            gate_ref[0, e, win, :] = gate_ref[0, e, win, :] + gsel
        return carry

    lax.fori_loop(0, _num_passes(off_ref, b, experts, tb), one_pass, 0)


def expert_gather(h, slot, aff_t, off, cap_pad, *, group=8):
    bsz, n, d = h.shape
    ne = slot.shape[1]
    grid_spec = pltpu.PrefetchScalarGridSpec(
        num_scalar_prefetch=1,
        grid=(bsz, ne // group, n // ROUTE_TB),
        in_specs=[
            pl.BlockSpec((1, ROUTE_TB, d), lambda b, g, t, off: (b, t, 0)),
            pl.BlockSpec((1, group, ROUTE_TB), lambda b, g, t, off: (b, g, t)),
            pl.BlockSpec((1, group, ROUTE_TB), lambda b, g, t, off: (b, g, t)),
        ],
        out_specs=[pl.BlockSpec((1, group, cap_pad, d), lambda b, g, t, off: (b, g, 0, 0)),
                   pl.BlockSpec((1, group, cap_pad, 1), lambda b, g, t, off: (b, g, 0, 0))],
    )
    return pl.pallas_call(
        functools.partial(_gather_kernel, group=group),
        grid_spec=grid_spec,
        out_shape=[jax.ShapeDtypeStruct((bsz, ne, cap_pad, d), BF16),
                   jax.ShapeDtypeStruct((bsz, ne, cap_pad, 1), F32)],
        compiler_params=_params("arbitrary", "arbitrary", "arbitrary"),
        name="expert_gather",
    )(off, h, slot, aff_t)


def _combine_kernel(off_ref, y_ref, slot_t_ref, x_ref, gt_ref, o_ref, stage_ref, acc_ref):
    b, tb = pl.program_id(0), pl.program_id(1)
    ne = y_ref.shape[1]
    rows_max = y_ref.shape[2] - ROUTE_WIN
    width = ne * ROUTE_WIN
    experts = list(range(ne))
    lane = lax.broadcasted_iota(jnp.int32, (1, width), 1)
    s1 = slot_t_ref[0] + 1
    hi = (s1 // ROUTE_WIN).astype(F32).astype(BF16)
    lo = (s1 % ROUTE_WIN).astype(F32).astype(BF16)
    e_row = lax.broadcasted_iota(jnp.int32, (ne, width), 0)
    e_lane = lax.broadcasted_iota(jnp.int32, (ne, width), 1) // ROUTE_WIN
    expand = jnp.where(e_row == e_lane, 1.0, 0.0).astype(BF16)
    s1_wide = (float(ROUTE_WIN) * jnp.dot(hi, expand, preferred_element_type=F32)
               + jnp.dot(lo, expand, preferred_element_type=F32))
    acc_ref[...] = jnp.zeros(acc_ref.shape, F32)

    def one_pass(w, carry):
        want = jnp.full((1, width), -1, jnp.int32)
        for e in experts:
            rows, active = _window_plan(off_ref, b, e, tb, w, rows_max)
            stage_ref[e * ROUTE_WIN:(e + 1) * ROUTE_WIN, :] = y_ref[0, e, pl.ds(rows, ROUTE_WIN), :]
            want = jnp.where((lane // ROUTE_WIN == e) & active, rows + lane % ROUTE_WIN + 1, want)
        onehot = jnp.where(s1_wide == want.astype(F32), 1.0, 0.0).astype(BF16)
        acc_ref[...] += jnp.dot(onehot, stage_ref[...], preferred_element_type=F32)
        return carry

    lax.fori_loop(0, _num_passes(off_ref, b, experts, tb), one_pass, 0)
    o_ref[0] = x_ref[0] + gt_ref[0] * acc_ref[...]


def expert_combine(y, slot_t, off, x, gt):
    bsz, n, d = x.shape
    ne, cap_pad = y.shape[1], y.shape[2]
    grid_spec = pltpu.PrefetchScalarGridSpec(
        num_scalar_prefetch=1,
        grid=(bsz, n // ROUTE_TB),
        in_specs=[
            pl.BlockSpec((1, ne, cap_pad, d), lambda b, t, off: (b, 0, 0, 0), pipeline_mode=pl.Buffered(1)),
            pl.BlockSpec((1, ROUTE_TB, ne), lambda b, t, off: (b, t, 0)),
            pl.BlockSpec((1, ROUTE_TB, d), lambda b, t, off: (b, t, 0)),
            pl.BlockSpec((1, 1, d), lambda b, t, off: (b, 0, 0)),
        ],
        out_specs=pl.BlockSpec((1, ROUTE_TB, d), lambda b, t, off: (b, t, 0)),
        scratch_shapes=[pltpu.VMEM((ne * ROUTE_WIN, d), BF16), pltpu.VMEM((ROUTE_TB, d), F32)],
    )
    return pl.pallas_call(
        _combine_kernel,
        grid_spec=grid_spec,
        out_shape=jax.ShapeDtypeStruct((bsz, n, d), F32),
        compiler_params=_params("arbitrary", "arbitrary"),
        name="expert_combine",
    )(off, y, slot_t, x, gt)


FFN_FBLK = 512


def _expert_ffn_kernel(*refs, n_sets):
    xe_refs, gate_refs = refs[:n_sets], refs[n_sets:2 * n_sets]
    wg_ref, wu_ref, wd_ref = refs[2 * n_sets:2 * n_sets + 3]
    o_refs = refs[2 * n_sets + 3:3 * n_sets + 3]
    acc_refs = refs[3 * n_sets + 3:]
    f = pl.program_id(1)
    wg = wg_ref[0].astype(BF16)
    wu = wu_ref[0].astype(BF16)
    wd = wd_ref[0].astype(BF16)
    for s in range(n_sets):
        for b in range(xe_refs[s].shape[0]):
            xe = xe_refs[s][b, 0]
            a = jnp.dot(xe, wg, preferred_element_type=F32)
            u = jnp.dot(xe, wu, preferred_element_type=F32)
            t = jnp.dot((jax.nn.silu(a) * u).astype(BF16), wd, preferred_element_type=F32)

            @pl.when(f == 0)
            def _():
                acc_refs[s][b] = t

            @pl.when(f > 0)
            def _():
                acc_refs[s][b] += t

    @pl.when(f == pl.num_programs(1) - 1)
    def _():
        for s in range(n_sets):
            for b in range(xe_refs[s].shape[0]):
                o_refs[s][b, 0] = (acc_refs[s][b] * gate_refs[s][b, 0]).astype(o_refs[s].dtype)


def expert_ffn(xes, gates, wg, wu, wd):
    n_sets = len(xes)
    ne, d, ff = wg.shape
    xe_specs = [pl.BlockSpec((xe.shape[0], 1) + xe.shape[2:], lambda e, f: (0, e, 0, 0)) for xe in xes]
    gate_specs = [pl.BlockSpec((g.shape[0], 1) + g.shape[2:], lambda e, f: (0, e, 0, 0)) for g in gates]
    return pl.pallas_call(
        functools.partial(_expert_ffn_kernel, n_sets=n_sets),
        grid=(ne, ff // FFN_FBLK),
        in_specs=xe_specs + gate_specs + [
            pl.BlockSpec((1, d, FFN_FBLK), lambda e, f: (e, 0, f)),
            pl.BlockSpec((1, d, FFN_FBLK), lambda e, f: (e, 0, f)),
            pl.BlockSpec((1, FFN_FBLK, d), lambda e, f: (e, f, 0)),
        ],
        out_specs=xe_specs,
        out_shape=[jax.ShapeDtypeStruct(xe.shape, BF16) for xe in xes],
        scratch_shapes=[pltpu.VMEM((xe.shape[0],) + xe.shape[2:], F32) for xe in xes],
        compiler_params=_params("arbitrary", "arbitrary"),
        name="expert_ffn",
    )(*xes, *gates, wg, wu, wd)


ATT_QROWS = 4
ATT_QTOK = ATT_QROWS * GRID_W
ATT_KROWS = ATT_QROWS + WIN_ROWS
ATT_KBLK = ATT_KROWS * GRID_W // ATT_QTOK
_NT = (((1,), (1,)), ((), ()))


def _rope_tables(n):
    quarter = HEAD_DIM // 4
    inv = ROPE_BASE ** (-jnp.arange(quarter, dtype=F32) / quarter)
    t = jnp.arange(n)
    ang_r = (t // GRID_W).astype(F32)[:, None] * inv[None, :]
    ang_c = (t % GRID_W).astype(F32)[:, None] * inv[None, :]
    cos = jnp.concatenate([jnp.cos(ang_r)] * 2 + [jnp.cos(ang_c)] * 2, axis=1)
    sin = jnp.concatenate([-jnp.sin(ang_r), jnp.sin(ang_r), -jnp.sin(ang_c), jnp.sin(ang_c)], axis=1)
    return jnp.concatenate([cos, cos], axis=1), jnp.concatenate([sin, sin], axis=1)


def _attn_bias_tables(rpb, n_rows):
    n_blk = n_rows // ATT_QROWS
    n_dr, n_dc = 2 * WIN_ROWS - 1, 2 * WIN_COLS - 1
    qc = np.arange(GRID_W)
    kc = np.arange(GRID_W)
    c_start = np.clip(qc - WIN_COLS // 2, 0, GRID_W - WIN_COLS)
    col_ok = (kc[None, :] >= c_start[:, None]) & (kc[None, :] < c_start[:, None] + WIN_COLS)
    d_col = np.clip(kc[None, :] - qc[:, None], 1 - WIN_COLS, WIN_COLS - 1) + (WIN_COLS - 1)
    col_sel = (d_col.reshape(-1)[None, :] == np.arange(n_dc)[:, None]).astype(np.float32)
    row_sel, oks = [], []
    for j in (0, 1, n_blk - 1):
        ks = min(max(ATT_QROWS * j - WIN_ROWS // 2, 0), n_rows - ATT_KROWS)
        r = ATT_QROWS * j + np.arange(ATT_QROWS)
        kr0 = np.clip(r - WIN_ROWS // 2, 0, n_rows - WIN_ROWS)
        krow = ks + np.arange(ATT_KROWS)
        row_ok = (krow[None, :] >= kr0[:, None]) & (krow[None, :] < kr0[:, None] + WIN_ROWS)
        d_row = np.clip(krow[None, :] - r[:, None] + (WIN_ROWS - 1), 0, n_dr - 1)
        row_sel.append((d_row.reshape(-1)[:, None] == np.arange(n_dr)[None, :]).astype(np.float32))
        oks.append(row_ok[:, None, :, None] & col_ok[None, :, None, :])
    hp = lax.Precision.HIGHEST
    a = jnp.einsum('cpr,hrd->chpd', jnp.asarray(np.stack(row_sel)), rpb.astype(F32), precision=hp)
    t = jnp.einsum('chpd,dx->chpx', a, jnp.asarray(col_sel), precision=hp)
    t = t.reshape(3, N_GROUPS, ATT_QROWS, ATT_KROWS, GRID_W, GRID_W).transpose(0, 1, 2, 4, 3, 5)
    ok = jnp.asarray(np.stack(oks))[:, None]
    return jnp.where(ok, t, NEG_INF).reshape(3, N_GROUPS, ATT_QTOK, ATT_KROWS * GRID_W)


def _rope(x, cos_ref, sin_ref, first16):
    c = cos_ref[...]
    s = sin_ref[...]
    c2 = jnp.concatenate([c, c], axis=1)
    s2 = jnp.concatenate([s, s], axis=1)
    w = x.shape[1]
    partner = jnp.where(first16, pltpu.roll(x, w - HEAD_DIM // 4, 1), pltpu.roll(x, HEAD_DIM // 4, 1))
    return x * c2 + partner * s2


def _softmax_pv(scores_values, head_mask, acc):
    m = None
    for s, _ in scores_values:
        mx = jnp.max(s, axis=1, keepdims=True)
        m = mx if m is None else jnp.maximum(m, mx)
    den = None
    o = None
    for s, v in scores_values:
        p = jnp.exp(s - m)
        sm = jnp.sum(p, axis=1, keepdims=True)
        den = sm if den is None else den + sm
        t = jnp.dot(p.astype(BF16), v, preferred_element_type=F32)
        o = t if o is None else o + t
    return acc + jnp.where(head_mask, o * (1.0 / den), 0.0)


def _nattn_kernel(q_ref, k0_ref, k1_ref, k2_ref, v0_ref, v1_ref, v2_ref,
                  cq_ref, sq_ref, ck0_ref, ck1_ref, ck2_ref, sk0_ref, sk1_ref, sk2_ref,
                  kc_ref, vc_ref, bias_ref, o_ref):
    w = q_ref.shape[2]
    lane = lax.broadcasted_iota(jnp.int32, (1, w), 1)
    first16 = (lane % (HEAD_DIM // 2)) < (HEAD_DIM // 4)
    q = q_ref[0] * (HEAD_DIM ** -0.5)
    q_rot = _rope(q, cq_ref, sq_ref, first16)
    k_rot = jnp.concatenate([_rope(k0_ref[0], ck0_ref, sk0_ref, first16),
                             _rope(k1_ref[0], ck1_ref, sk1_ref, first16),
                             _rope(k2_ref[0], ck2_ref, sk2_ref, first16)], axis=0).astype(BF16)
    v = jnp.concatenate([v0_ref[0], v1_ref[0], v2_ref[0]], axis=0).astype(BF16)
    kc = kc_ref[0].astype(BF16)
    vc = vc_ref[0].astype(BF16)
    acc = jnp.zeros(q.shape, F32)
    for h in range(N_GROUPS):
        hm = (lane // HEAD_DIM) == h
        s_win = lax.dot_general(jnp.where(hm, q_rot, 0.0).astype(BF16), k_rot, _NT,
                                preferred_element_type=F32) + bias_ref[0, h]
        s_ctx = lax.dot_general(jnp.where(hm, q, 0.0).astype(BF16), kc, _NT, preferred_element_type=F32)
        acc = _softmax_pv([(s_win, v), (s_ctx, vc)], hm, acc)
    o_ref[0] = acc


def neighbourhood_attention(z, zc, rpb, *, q_col, k_col, v_col, kc_col, vc_col):
    bsz, n, _ = z.shape
    n_ctx = zc.shape[1]
    w = BRANCH_WIDTH
    n_blk = n // ATT_QTOK
    cos_t, sin_t = _rope_tables(n)
    bias = _attn_bias_tables(rpb, n // GRID_W)

    def kb(j):
        return jnp.clip(j - 1, 0, n_blk - ATT_KBLK)

    def zspec(col, off=None):
        if off is None:
            return pl.BlockSpec((1, ATT_QTOK, w), lambda b, j: (b, j, col))
        return pl.BlockSpec((1, ATT_QTOK, w), lambda b, j: (b, kb(j) + off, col))

    def tspec(off=None):
        if off is None:
            return pl.BlockSpec((ATT_QTOK, 2 * HEAD_DIM), lambda b, j: (j, 0))
        return pl.BlockSpec((ATT_QTOK, 2 * HEAD_DIM), lambda b, j: (kb(j) + off, 0))

    in_specs = ([zspec(q_col)] + [zspec(k_col, i) for i in range(ATT_KBLK)] + [zspec(v_col, i) for i in range(ATT_KBLK)]
                + [tspec(), tspec()] + [tspec(i) for i in range(ATT_KBLK)] * 2
                + [pl.BlockSpec((1, n_ctx, w), lambda b, j: (b, 0, kc_col)),
                   pl.BlockSpec((1, n_ctx, w), lambda b, j: (b, 0, vc_col)),
                   pl.BlockSpec((1, N_GROUPS, ATT_QTOK, ATT_KROWS * GRID_W),
                                lambda b, j: (jnp.minimum(j, 1) + j // (n_blk - 1), 0, 0, 0))])
    return pl.pallas_call(
        _nattn_kernel,
        grid=(bsz, n_blk),
        in_specs=in_specs,
        out_specs=pl.BlockSpec((1, ATT_QTOK, w), lambda b, j: (b, j, 0)),
        out_shape=jax.ShapeDtypeStruct((bsz, n, w), F32),
        compiler_params=_params("arbitrary", "arbitrary"),
        name="neighbourhood_attention",
    )(z, z, z, z, z, z, z, cos_t, sin_t, cos_t, cos_t, cos_t, sin_t, sin_t, sin_t, zc, zc, bias)


def _ctx_attn_kernel(q_ref, kc_ref, vc_ref, o_ref):
    w = q_ref.shape[2]
    lane = lax.broadcasted_iota(jnp.int32, (1, w), 1)
    q = q_ref[0] * (HEAD_DIM ** -0.5)
    kc = kc_ref[0].astype(BF16)
    vc = vc_ref[0].astype(BF16)
    acc = jnp.zeros(q.shape, F32)
    for h in range(N_GROUPS):
        hm = (lane // HEAD_DIM) == h
        s = lax.dot_general(jnp.where(hm, q, 0.0).astype(BF16), kc, _NT, preferred_element_type=F32)
        acc = _softmax_pv([(s, vc)], hm, acc)
    o_ref[0] = acc


def context_attention(zc, *, q_col, k_col, v_col):
    bsz, n_ctx, _ = zc.shape
    w = BRANCH_WIDTH
    return pl.pallas_call(
        _ctx_attn_kernel,
        grid=(bsz,),
        in_specs=[pl.BlockSpec((1, n_ctx, w), functools.partial(lambda b, c: (b, 0, c), c=c))
                  for c in (q_col, k_col, v_col)],
        out_specs=pl.BlockSpec((1, n_ctx, w), lambda b: (b, 0, 0)),
        out_shape=jax.ShapeDtypeStruct((bsz, n_ctx, w), F32),
        compiler_params=_params("arbitrary"),
        name="context_attention",
    )(zc, zc, zc)


def _layernorm(x, g, b):
    mu = jnp.mean(x, axis=-1, keepdims=True)
    var = jnp.mean(jnp.square(x - mu), axis=-1, keepdims=True)
    return (x - mu) * lax.rsqrt(var + EPS) * g + b


def _heads(z):
    return z.reshape(*z.shape[:-1], N_GROUPS, HEAD_DIM)


def _chunk_gmlp(z, ln_g, ln_b, w_s, b_s):
    z = jax.nn.gelu(z)
    u, v = jnp.split(z, 2, axis=-1)
    v = _layernorm(v, ln_g, ln_b)
    bsz, n, _ = v.shape
    v = v.reshape(bsz, n // CHUNK, CHUNK, N_GROUPS, HEAD_DIM)
    mixed = jnp.einsum('bcpgd,gqp->bcqgd', v, w_s) + b_s.T[None, None, :, :, None]
    return u * mixed.reshape(bsz, n, BRANCH_WIDTH)


def _fourier_mix(z):
    bsz, n, _ = z.shape
    zg = z.reshape(bsz, n, N_GROUPS, HEAD_DIM)
    f = jnp.fft.fft2(zg, axes=(1, 3), norm='ortho').real
    return f.reshape(bsz, n, BRANCH_WIDTH)


def _conformer_conv(z, conv_w, conv_b, ln_g, ln_b):
    a, gt = jnp.split(z, 2, axis=-1)
    y = a * jax.nn.sigmoid(gt)
    y = lax.conv_general_dilated(
        y, conv_w[:, None, :], window_strides=(1,),
        padding=[(CONV_WIDTH // 2, CONV_WIDTH // 2)],
        dimension_numbers=('NWC', 'WIO', 'NWC'),
        feature_group_count=BRANCH_WIDTH) + conv_b
    return jax.nn.silu(_layernorm(y, ln_g, ln_b))


def _axial_rope(x, rows, cols):
    half = HEAD_DIM // 2
    quarter = half // 2
    inv = ROPE_BASE ** (-jnp.arange(quarter, dtype=F32) / quarter)

    def rot(xp, pos):
        ang = pos.astype(F32)[:, None] * inv[None, :]
        cos = jnp.cos(ang)[None, :, None, :]
        sin = jnp.sin(ang)[None, :, None, :]
        x1, x2 = jnp.split(xp, 2, axis=-1)
        return jnp.concatenate([x1 * cos - x2 * sin, x1 * sin + x2 * cos], axis=-1)

    return jnp.concatenate([rot(x[..., :half], rows), rot(x[..., half:], cols)], axis=-1)


def _neighbourhood_attention(q, k, v, kc, vc, rpb):
    bsz, n, nh, dh = q.shape
    n_rows = n // GRID_W
    wr = min(WIN_ROWS, n_rows)
    n_cb = GRID_W // QB_COLS
    n_win = wr * KB_COLS
    t = jnp.arange(n)
    qs = q * (dh ** -0.5)
    q_rot = _axial_rope(qs, t // GRID_W, t % GRID_W)
    k_rot = _axial_rope(k, t // GRID_W, t % GRID_W)
    r = jnp.arange(n_rows)
    key_rows = jnp.clip(r - wr // 2, 0, n_rows - wr)[:, None] + jnp.arange(wr)[None, :]
    jb = jnp.arange(n_cb)
    key_cols = (jnp.clip(jb * QB_COLS - WIN_COLS // 2, 0, GRID_W - KB_COLS)[:, None]
                + jnp.arange(KB_COLS)[None, :])
    q_cols = jb[:, None] * QB_COLS + jnp.arange(QB_COLS)[None, :]
    c_start = jnp.clip(q_cols - WIN_COLS // 2, 0, GRID_W - WIN_COLS)
    kcol = key_cols[:, None, :]
    col_ok = (kcol >= c_start[..., None]) & (kcol < c_start[..., None] + WIN_COLS)
    d_row = key_rows - r[:, None] + (WIN_ROWS - 1)
    d_col = jnp.clip(kcol - q_cols[..., None], 1 - WIN_COLS, WIN_COLS - 1) + (WIN_COLS - 1)
    bias = rpb[:, d_row[:, None, None, :, None], d_col[None, :, :, None, :]].astype(F32)
    bias = jnp.where(col_ok[None, None, :, :, None, :], bias, NEG_INF)
    bias = bias.reshape(nh, n_rows, n_cb, QB_COLS, n_win)
    key_idx = (key_rows[:, None, :, None] * GRID_W + key_cols[None, :, None, :]).reshape(n_rows, n_cb, n_win)
    k_win = k_rot[:, key_idx]
    v_win = v[:, key_idx]
    q_rot = q_rot.reshape(bsz, n_rows, n_cb, QB_COLS, nh, dh)
    q_pl = qs.reshape(bsz, n_rows, n_cb, QB_COLS, nh, dh)
    s_win = jnp.einsum('brjqhd,brjkhd->bhrjqk', q_rot, k_win) + bias
    s_ctx = jnp.einsum('brjqhd,bchd->bhrjqc', q_pl, kc)
    p = jax.nn.softmax(jnp.concatenate([s_win, s_ctx], axis=-1), axis=-1)
    out = (jnp.einsum('bhrjqk,brjkhd->brjqhd', p[..., :n_win], v_win)
           + jnp.einsum('bhrjqc,bchd->brjqhd', p[..., n_win:], vc))
    return out.reshape(bsz, n, nh * dh)


def _context_attention(q, k, v):
    bsz, n, nh, dh = q.shape
    s = jnp.einsum('bqhd,bkhd->bhqk', q * (dh ** -0.5), k)
    p = jax.nn.softmax(s, axis=-1)
    return jnp.einsum('bhqk,bkhd->bqhd', p, v).reshape(bsz, n, nh * dh)


def expert_route(x, g, sc, sh, router_t, *, tm):
    n = x.shape[1]
    cap = EC_CAPACITY * n // N_EXPERTS
    h, aff_t = norm_router(x, g, sc, sh, router_t, tm=tm)
    slot, off = expert_topk(aff_t, cap)
    xe, gate = expert_gather(h, slot, aff_t, off, cap + ROUTE_WIN)
    return xe, gate, jnp.swapaxes(slot, 1, 2), off


def kernel(x, c, ctx, c_ctx, w_mod, b_mod, norm1_g, norm2_g, w_in, sgu_ln_g, sgu_ln_b, w_spatial, b_spatial,
           w_a_out, w_b_out, conv_w, conv_b, conv_ln_g, conv_ln_b, w_c_out, rpb, w_d_out, w_out, w_router,
           w_gate_e, w_up_e, w_down_e, final_norm_g):
    bsz = x.shape[0]
    xc = ctx
    silu_c = jax.nn.silu(c)
    silu_cc = jax.nn.silu(c_ctx)
    for l in range(DEPTH):
        last = l == DEPTH - 1
        hp = lax.Precision.HIGHEST
        mod = (jnp.dot(silu_c, w_mod[l], precision=hp) + b_mod[l])[:, None, :]
        sh1, sc1, gt1, sh2, sc2, gt2 = jnp.split(mod, 6, axis=-1)
        cmod = jnp.broadcast_to((jnp.dot(silu_cc, w_mod[l], precision=hp) + b_mod[l])[None, None, :],
                                (bsz, 1, 6 * D_MODEL))
        csh1, csc1, cgt1, csh2, csc2, cgt2 = jnp.split(cmod, 6, axis=-1)

        w_in_b = w_in[l].astype(BF16)
        wa, wb, wc, wd_, wo = (w.astype(BF16) for w in (w_a_out[l], w_b_out[l], w_c_out[l], w_d_out[l], w_out[l]))
        g1 = norm1_g[l][None, :]

        z = norm_inproj(x, g1, sc1, sh1, w_in_b, tm=256)
        za, zb, zcv, zq, zk, zv, _ = jnp.split(z, SPLITS, axis=-1)
        if last:
            zc = norm_inproj(xc, g1, csc1, csh1, w_in_b[:, Q_END:V_END], tm=CTX_TM)
            zkc, zvc = jnp.split(zc, 2, axis=-1)
        else:
            zc = norm_inproj(xc, g1, csc1, csh1, w_in_b, tm=CTX_TM)
            cza, czb, czcv, czq, zkc, zvc, _ = jnp.split(zc, SPLITS, axis=-1)
        qcol, kcol, vcol = C_END // BRANCH_WIDTH, Q_END // BRANCH_WIDTH, K_END // BRANCH_WIDTH
        d_lat = neighbourhood_attention(z, zc, rpb[l], q_col=qcol, k_col=kcol, v_col=vcol,
                                        kc_col=0 if last else kcol, vc_col=1 if last else vcol)
        a_lat = _chunk_gmlp(za, sgu_ln_g[l], sgu_ln_b[l], w_spatial[l], b_spatial[l])
        b_lat = _fourier_mix(zb)
        c_lat = _conformer_conv(zcv, conv_w[l], conv_b[l], conv_ln_g[l], conv_ln_b[l])
        x = merge_branches(a_lat, b_lat, c_lat, d_lat, z, wa, wb, wc, wd_, wo, x, gt1, tm=512)
        if not last:
            a_c = _chunk_gmlp(cza, sgu_ln_g[l], sgu_ln_b[l], w_spatial[l], b_spatial[l])
            b_c = _fourier_mix(czb)
            c_c = _conformer_conv(czcv, conv_w[l], conv_b[l], conv_ln_g[l], conv_ln_b[l])
            d_c = context_attention(zc, q_col=qcol, k_col=kcol, v_col=vcol)
            xc = merge_branches(a_c, b_c, c_c, d_c, zc, wa, wb, wc, wd_, wo, xc, cgt1, tm=CTX_TM)

        g2 = norm2_g[l][None, :]
        router_t = w_router[l].T
        xe, gate, slot_t, off = expert_route(x, g2, sc2, sh2, router_t, tm=512)
        if last:
            (y,) = expert_ffn([xe], [gate], w_gate_e[l], w_up_e[l], w_down_e[l])
        else:
            xe_c, gate_c, slot_tc, off_c = expert_route(xc, g2, csc2, csh2, router_t, tm=CTX_TM)
            y, y_c = expert_ffn([xe, xe_c], [gate, gate_c], w_gate_e[l], w_up_e[l], w_down_e[l])
            xc = expert_combine(y_c, slot_tc, off_c, xc, cgt2)
        x = expert_combine(y, slot_t, off, x, gt2)
    return x * lax.rsqrt(jnp.mean(x * x, axis=-1, keepdims=True) + EPS) * final_norm_g


CTX_TM = 256
```

```python
import functools

import jax
import jax.numpy as jnp
import numpy as np
from jax import lax
from jax.experimental import pallas as pl
from jax.experimental.pallas import tpu as pltpu

D_MODEL = 1024
DEPTH = 2
GRID_W = 64
HEAD_DIM = 64
N_GROUPS = 4
BRANCH_WIDTH = N_GROUPS * HEAD_DIM
N_BRANCH = 4
CHUNK = 128
CONV_WIDTH = 31
WIN_ROWS = 8
WIN_COLS = 16
QB_COLS = 16
KB_COLS = QB_COLS + WIN_COLS
ROPE_BASE = 10000.0
N_EXPERTS = 16
EXPERT_FF = 1024
EC_CAPACITY = 2
EPS = 1e-6
NEG_INF = -1e30
A_END = 2 * BRANCH_WIDTH
B_END = A_END + BRANCH_WIDTH
C_END = B_END + 2 * BRANCH_WIDTH
Q_END = C_END + BRANCH_WIDTH
K_END = Q_END + BRANCH_WIDTH
V_END = K_END + BRANCH_WIDTH
IN_COLS = V_END + N_BRANCH * D_MODEL
SPLITS = [A_END, B_END, C_END, Q_END, K_END, V_END]
IN_WIDTHS = (A_END, B_END - A_END, C_END - B_END, V_END - C_END, IN_COLS - V_END)

VMEM_LIMIT_BYTES = 56 * 1024 * 1024
F32 = jnp.float32
BF16 = jnp.bfloat16


def _params(*sem):
    return pltpu.CompilerParams(dimension_semantics=sem, vmem_limit_bytes=VMEM_LIMIT_BYTES)


INPROJ_COL_CHUNK = 512


def _norm_inproj_kernel(x_ref, g_ref, sc_ref, sh_ref, w_ref, *o_refs):
    x = x_ref[0]
    y = x * lax.rsqrt(jnp.mean(x * x, axis=-1, keepdims=True) + EPS) * g_ref[...]
    h = (y * (1.0 + sc_ref[0]) + sh_ref[0]).astype(BF16)
    off = 0
    for o_ref in o_refs:
        width = o_ref.shape[2]
        for c0 in range(0, width, INPROJ_COL_CHUNK):
            cw = min(INPROJ_COL_CHUNK, width - c0)
            o_ref[0, :, c0:c0 + cw] = jnp.dot(h, w_ref[:, off + c0:off + c0 + cw],
                                              preferred_element_type=F32).astype(o_ref.dtype)
        off += width


def norm_inproj(x, g, sc, sh, w, widths, *, tm):
    bsz, n, d = x.shape
    cols = w.shape[1]
    assert sum(widths) == cols
    return pl.pallas_call(
        _norm_inproj_kernel,
        grid=(bsz, n // tm),
        in_specs=[
            pl.BlockSpec((1, tm, d), lambda b, i: (b, i, 0)),
            pl.BlockSpec((1, d), lambda b, i: (0, 0)),
            pl.BlockSpec((1, 1, d), lambda b, i: (b, 0, 0)),
            pl.BlockSpec((1, 1, d), lambda b, i: (b, 0, 0)),
            pl.BlockSpec((d, cols), lambda b, i: (0, 0), pipeline_mode=pl.Buffered(1)),
        ],
        out_specs=[pl.BlockSpec((1, tm, wd), lambda b, i: (b, i, 0)) for wd in widths],
        out_shape=[jax.ShapeDtypeStruct((bsz, n, wd), F32) for wd in widths],
        compiler_params=_params("arbitrary", "arbitrary"),
        name="norm_inproj",
    )(x, g, sc, sh, w)


def _merge_kernel(a_ref, b_ref, c_ref, d_ref, ga_ref, gb_ref, gc_ref, gd_ref,
                  wa_ref, wb_ref, wc_ref, wd_ref, wo_ref, x_ref, gt_ref, o_ref):
    m = None
    for br, gz, w in ((a_ref, ga_ref, wa_ref), (b_ref, gb_ref, wb_ref),
                      (c_ref, gc_ref, wc_ref), (d_ref, gd_ref, wd_ref)):
        p = jnp.dot(br[0].astype(BF16), w[...], preferred_element_type=F32)
        t = jax.nn.sigmoid(gz[0].astype(F32)) * p
        m = t if m is None else m + t
    mix = jnp.dot(m.astype(BF16), wo_ref[...], preferred_element_type=F32)
    o_ref[0] = x_ref[0] + gt_ref[0] * mix


def merge_branches(a, b, cc, d, z, w_a, w_b, w_c, w_d, w_o, x, gt, *, tm):
    bsz, n, dm = x.shape
    w = a.shape[-1]
    br_spec = pl.BlockSpec((1, tm, w), lambda bi, i: (bi, i, 0))
    gate_specs = [pl.BlockSpec((1, tm, dm), functools.partial(lambda bi, i, k: (bi, i, k), k=k))
                  for k in range(N_BRANCH)]
    wbr_spec = pl.BlockSpec((w, dm), lambda bi, i: (0, 0))
    return pl.pallas_call(
        _merge_kernel,
        grid=(bsz, n // tm),
        in_specs=[br_spec] * 4 + gate_specs + [wbr_spec] * 4 + [
            pl.BlockSpec((dm, dm), lambda bi, i: (0, 0)),
            pl.BlockSpec((1, tm, dm), lambda bi, i: (bi, i, 0)),
            pl.BlockSpec((1, 1, dm), lambda bi, i: (bi, 0, 0)),
        ],
        out_specs=pl.BlockSpec((1, tm, dm), lambda bi, i: (bi, i, 0)),
        out_shape=jax.ShapeDtypeStruct((bsz, n, dm), F32),
        compiler_params=_params("arbitrary", "arbitrary"),
        name="merge_branches",
    )(a, b, cc, d, z, z, z, z, w_a, w_b, w_c, w_d, w_o, x, gt)


ROUTE_TB = 256
ROUTE_WIN = 64
ROUTE_ALIGN = 16
OFF_LANES = 128
TOPK_EXP_STEPS = (64, 32, 16, 8, 4, 2, 1)
TOPK_BISECT_STEPS = 56


def _split_bf16(v):
    hi = v.astype(BF16)
    return hi, (v - hi.astype(F32)).astype(BF16)


def _norm_router_kernel(x_ref, g_ref, sc_ref, sh_ref, rt_ref, h_ref, aff_ref):
    x = x_ref[0]
    y = x * lax.rsqrt(jnp.mean(x * x, axis=-1, keepdims=True) + EPS) * g_ref[...]
    h = y * (1.0 + sc_ref[0]) + sh_ref[0]
    h_hi, h_lo = _split_bf16(h)
    r_hi, r_lo = _split_bf16(rt_ref[...])
    logits = (lax.dot_general(r_hi, h_hi, _NT, preferred_element_type=F32)
              + lax.dot_general(r_hi, h_lo, _NT, preferred_element_type=F32)
              + lax.dot_general(r_lo, h_hi, _NT, preferred_element_type=F32))
    p = jnp.exp(logits - jnp.max(logits, axis=0, keepdims=True))
    aff_ref[0] = p / jnp.sum(p, axis=0, keepdims=True)
    h_ref[0] = h_hi


def norm_router(x, g, sc, sh, router_t, *, tm):
    bsz, n, d = x.shape
    ne = router_t.shape[0]
    return pl.pallas_call(
        _norm_router_kernel,
        grid=(bsz, n // tm),
        in_specs=[
            pl.BlockSpec((1, tm, d), lambda b, i: (b, i, 0)),
            pl.BlockSpec((1, d), lambda b, i: (0, 0)),
            pl.BlockSpec((1, 1, d), lambda b, i: (b, 0, 0)),
            pl.BlockSpec((1, 1, d), lambda b, i: (b, 0, 0)),
            pl.BlockSpec((ne, d), lambda b, i: (0, 0)),
        ],
        out_specs=[pl.BlockSpec((1, tm, d), lambda b, i: (b, i, 0)),
                   pl.BlockSpec((1, ne, tm), lambda b, i: (b, 0, i))],
        out_shape=[jax.ShapeDtypeStruct((bsz, n, d), BF16), jax.ShapeDtypeStruct((bsz, ne, n), F32)],
        compiler_params=_params("arbitrary", "arbitrary"),
        name="norm_router",
    )(x, g, sc, sh, router_t)


def _topk_kernel(aff_ref, slot_ref, off_ref, *, cap):
    a = aff_ref[0]
    ne, n = a.shape
    capf = jnp.float32(cap)

    def count_ge(t):
        return jnp.sum(jnp.where(a >= t, 1.0, 0.0), axis=1, keepdims=True)

    hi = jnp.full((ne, 1), 2.0, F32)
    for s in TOPK_EXP_STEPS:
        cand = hi * (2.0 ** -s)
        hi = jnp.where(count_ge(cand) < capf, cand, hi)
    lo = jnp.where(hi <= 2.0 ** -126, 0.0, hi * 0.5)

    def bisect(_, carry):
        lo, hi = carry
        mid = 0.5 * (lo + hi)
        ok = count_ge(mid) >= capf
        return jnp.where(ok, mid, lo), jnp.where(ok, hi, mid)

    lo, hi = lax.fori_loop(0, TOPK_BISECT_STEPS, bisect, (lo, hi))
    need = capf - count_ge(hi)
    r_i = lax.broadcasted_iota(jnp.int32, (ROUTE_TB, ROUTE_TB), 0)
    c_i = lax.broadcasted_iota(jnp.int32, (ROUTE_TB, ROUTE_TB), 1)
    tri = jnp.where(r_i < c_i, 1.0, 0.0).astype(BF16)
    lane = lax.broadcasted_iota(jnp.int32, (ne, OFF_LANES), 1)
    run_eq = jnp.zeros((ne, 1), F32)
    run_sel = jnp.zeros((ne, 1), F32)
    offs = jnp.zeros((ne, OFF_LANES), F32)
    for c in range(n // ROUTE_TB):
        cols = slice(c * ROUTE_TB, (c + 1) * ROUTE_TB)
        a_c = a[:, cols]
        above = a_c >= hi
        eq_c = jnp.where(above, 0.0, jnp.where(a_c >= lo, 1.0, 0.0))
        rank = jnp.dot(eq_c.astype(BF16), tri, preferred_element_type=F32) + run_eq
        sel = jnp.where(above, 1.0, jnp.where(rank < need, eq_c, 0.0))
        pos = jnp.dot(sel.astype(BF16), tri, preferred_element_type=F32) + run_sel
        slot_ref[0, :, cols] = jnp.where(sel > 0.0, pos, -1.0).astype(jnp.int32)
        offs = jnp.where(lane == c, run_sel, offs)
        run_eq = run_eq + jnp.sum(eq_c, axis=1, keepdims=True)
        run_sel = run_sel + jnp.sum(sel, axis=1, keepdims=True)
    offs = jnp.where(lane == n // ROUTE_TB, run_sel, offs)
    off_ref[0] = offs.astype(jnp.int32)


def expert_topk(aff_t, cap):
    bsz, ne, n = aff_t.shape
    return pl.pallas_call(
        functools.partial(_topk_kernel, cap=cap),
        grid=(bsz,),
        in_specs=[pl.BlockSpec((1, ne, n), lambda b: (b, 0, 0))],
        out_specs=[pl.BlockSpec((1, ne, n), lambda b: (b, 0, 0)),
                   pl.BlockSpec((1, ne, OFF_LANES), lambda b: (b, 0, 0))],
        out_shape=[jax.ShapeDtypeStruct((bsz, ne, n), jnp.int32),
                   jax.ShapeDtypeStruct((bsz, ne, OFF_LANES), jnp.int32)],
        compiler_params=_params("arbitrary"),
        name="expert_topk",
    )(aff_t)


def _window_plan(off_ref, b, e, tb, w, rows_max):
    start = off_ref[b, e, tb]
    stop = off_ref[b, e, tb + 1]
    base = (start // ROUTE_ALIGN) * ROUTE_ALIGN
    rows = base + w * ROUTE_WIN
    active = rows < stop
    return pl.multiple_of(jnp.minimum(rows, rows_max), ROUTE_ALIGN), active


def _num_passes(off_ref, b, e_list, tb):
    n_pass = jnp.int32(0)
    for e in e_list:
        start = off_ref[b, e, tb]
        stop = off_ref[b, e, tb + 1]
        base = (start // ROUTE_ALIGN) * ROUTE_ALIGN
        n_pass = jnp.maximum(n_pass, (stop - base + ROUTE_WIN - 1) // ROUTE_WIN)
    return n_pass


def _gather_kernel(off_ref, h_ref, slot_ref, aff_ref, xe_ref, gate_ref, *, group):
    b, g, tb = pl.program_id(0), pl.program_id(1), pl.program_id(2)
    rows_max = xe_ref.shape[2] - ROUTE_WIN

    @pl.when(tb == 0)
    def _():
        xe_ref[...] = jnp.zeros(xe_ref.shape, xe_ref.dtype)
        gate_ref[...] = jnp.zeros(gate_ref.shape, gate_ref.dtype)

    sub = lax.broadcasted_iota(jnp.int32, (ROUTE_WIN, 1), 0)
    experts = [g * group + e for e in range(group)]

    def one_pass(w, carry):
        plans = [_window_plan(off_ref, b, ge, tb, w, rows_max) for ge in experts]
        hots = []
        for e, (rows, active) in enumerate(plans):
            want = jnp.where(active, rows + sub, -2)
            hots.append(slot_ref[0, e:e + 1, :] == want)
        onehot = jnp.concatenate([jnp.where(hm, 1.0, 0.0) for hm in hots], axis=0).astype(BF16)
        res = jnp.dot(onehot, h_ref[0], preferred_element_type=F32)
        for e, (rows, _) in enumerate(plans):
            win = pl.ds(rows, ROUTE_WIN)
            xe_ref[0, e, win, :] = xe_ref[0, e, win, :] + res[e * ROUTE_WIN:(e + 1) * ROUTE_WIN].astype(BF16)
            gsel = jnp.sum(jnp.where(hots[e], aff_ref[0, e:e + 1, :], 0.0), axis=1, keepdims=True)
            gate_ref[0, e, win, :] = gate_ref[0, e, win, :] + gsel
        return carry

    lax.fori_loop(0, _num_passes(off_ref, b, experts, tb), one_pass, 0)


def expert_gather(h, slot, aff_t, off, cap_pad, *, group=8):
    bsz, n, d = h.shape
    ne = slot.shape[1]
    grid_spec = pltpu.PrefetchScalarGridSpec(
        num_scalar_prefetch=1,
        grid=(bsz, ne // group, n // ROUTE_TB),
        in_specs=[
            pl.BlockSpec((1, ROUTE_TB, d), lambda b, g, t, off: (b, t, 0)),
            pl.BlockSpec((1, group, ROUTE_TB), lambda b, g, t, off: (b, g, t)),
            pl.BlockSpec((1, group, ROUTE_TB), lambda b, g, t, off: (b, g, t)),
        ],
        out_specs=[pl.BlockSpec((1, group, cap_pad, d), lambda b, g, t, off: (b, g, 0, 0)),
                   pl.BlockSpec((1, group, cap_pad, 1), lambda b, g, t, off: (b, g, 0, 0))],
    )
    return pl.pallas_call(
        functools.partial(_gather_kernel, group=group),
        grid_spec=grid_spec,
        out_shape=[jax.ShapeDtypeStruct((bsz, ne, cap_pad, d), BF16),
                   jax.ShapeDtypeStruct((bsz, ne, cap_pad, 1), F32)],
        compiler_params=_params("arbitrary", "arbitrary", "arbitrary"),
        name="expert_gather",
    )(off, h, slot, aff_t)


def _combine_kernel(off_ref, y_ref, slot_t_ref, x_ref, gt_ref, o_ref, stage_ref, acc_ref):
    b, tb = pl.program_id(0), pl.program_id(1)
    ne = y_ref.shape[1]
    rows_max = y_ref.shape[2] - ROUTE_WIN
    width = ne * ROUTE_WIN
    experts = list(range(ne))
    lane = lax.broadcasted_iota(jnp.int32, (1, width), 1)
    s1 = slot_t_ref[0] + 1
    hi = (s1 // ROUTE_WIN).astype(F32).astype(BF16)
    lo = (s1 % ROUTE_WIN).astype(F32).astype(BF16)
    e_row = lax.broadcasted_iota(jnp.int32, (ne, width), 0)
    e_lane = lax.broadcasted_iota(jnp.int32, (ne, width), 1) // ROUTE_WIN
    expand = jnp.where(e_row == e_lane, 1.0, 0.0).astype(BF16)
    s1_wide = (float(ROUTE_WIN) * jnp.dot(hi, expand, preferred_element_type=F32)
               + jnp.dot(lo, expand, preferred_element_type=F32))
    acc_ref[...] = jnp.zeros(acc_ref.shape, F32)

    def one_pass(w, carry):
        want = jnp.full((1, width), -1, jnp.int32)
        for e in experts:
            rows, active = _window_plan(off_ref, b, e, tb, w, rows_max)
            stage_ref[e * ROUTE_WIN:(e + 1) * ROUTE_WIN, :] = y_ref[0, e, pl.ds(rows, ROUTE_WIN), :]
            want = jnp.where((lane // ROUTE_WIN == e) & active, rows + lane % ROUTE_WIN + 1, want)
        onehot = jnp.where(s1_wide == want.astype(F32), 1.0, 0.0).astype(BF16)
        acc_ref[...] += jnp.dot(onehot, stage_ref[...], preferred_element_type=F32)
        return carry

    lax.fori_loop(0, _num_passes(off_ref, b, experts, tb), one_pass, 0)
    o_ref[0] = x_ref[0] + gt_ref[0] * acc_ref[...]


def expert_combine(y, slot_t, off, x, gt):
    bsz, n, d = x.shape
    ne, cap_pad = y.shape[1], y.shape[2]
    grid_spec = pltpu.PrefetchScalarGridSpec(
        num_scalar_prefetch=1,
        grid=(bsz, n // ROUTE_TB),
        in_specs=[
            pl.BlockSpec((1, ne, cap_pad, d), lambda b, t, off: (b, 0, 0, 0), pipeline_mode=pl.Buffered(1)),
            pl.BlockSpec((1, ROUTE_TB, ne), lambda b, t, off: (b, t, 0)),
            pl.BlockSpec((1, ROUTE_TB, d), lambda b, t, off: (b, t, 0)),
            pl.BlockSpec((1, 1, d), lambda b, t, off: (b, 0, 0)),
        ],
        out_specs=pl.BlockSpec((1, ROUTE_TB, d), lambda b, t, off: (b, t, 0)),
        scratch_shapes=[pltpu.VMEM((ne * ROUTE_WIN, d), BF16), pltpu.VMEM((ROUTE_TB, d), F32)],
    )
    return pl.pallas_call(
        _combine_kernel,
        grid_spec=grid_spec,
        out_shape=jax.ShapeDtypeStruct((bsz, n, d), F32),
        compiler_params=_params("arbitrary", "arbitrary"),
        name="expert_combine",
    )(off, y, slot_t, x, gt)


FFN_FBLK = 512


def _expert_ffn_kernel(*refs, n_sets):
    xe_refs, gate_refs = refs[:n_sets], refs[n_sets:2 * n_sets]
    wg_ref, wu_ref, wd_ref = refs[2 * n_sets:2 * n_sets + 3]
    o_refs = refs[2 * n_sets + 3:3 * n_sets + 3]
    acc_refs = refs[3 * n_sets + 3:]
    f = pl.program_id(1)
    wg = wg_ref[0].astype(BF16)
    wu = wu_ref[0].astype(BF16)
    wd = wd_ref[0].astype(BF16)
    for s in range(n_sets):
        for b in range(xe_refs[s].shape[0]):
            xe = xe_refs[s][b, 0]
            a = jnp.dot(xe, wg, preferred_element_type=F32)
            u = jnp.dot(xe, wu, preferred_element_type=F32)
            t = jnp.dot((jax.nn.silu(a) * u).astype(BF16), wd, preferred_element_type=F32)

            @pl.when(f == 0)
            def _():
                acc_refs[s][b] = t

            @pl.when(f > 0)
            def _():
                acc_refs[s][b] += t

    @pl.when(f == pl.num_programs(1) - 1)
    def _():
        for s in range(n_sets):
            for b in range(xe_refs[s].shape[0]):
                o_refs[s][b, 0] = (acc_refs[s][b] * gate_refs[s][b, 0]).astype(o_refs[s].dtype)


def expert_ffn(xes, gates, wg, wu, wd):
    n_sets = len(xes)
    ne, d, ff = wg.shape
    xe_specs = [pl.BlockSpec((xe.shape[0], 1) + xe.shape[2:], lambda e, f: (0, e, 0, 0)) for xe in xes]
    gate_specs = [pl.BlockSpec((g.shape[0], 1) + g.shape[2:], lambda e, f: (0, e, 0, 0)) for g in gates]
    return pl.pallas_call(
        functools.partial(_expert_ffn_kernel, n_sets=n_sets),
        grid=(ne, ff // FFN_FBLK),
        in_specs=xe_specs + gate_specs + [
            pl.BlockSpec((1, d, FFN_FBLK), lambda e, f: (e, 0, f)),
            pl.BlockSpec((1, d, FFN_FBLK), lambda e, f: (e, 0, f)),
            pl.BlockSpec((1, FFN_FBLK, d), lambda e, f: (e, f, 0)),
        ],
        out_specs=xe_specs,
        out_shape=[jax.ShapeDtypeStruct(xe.shape, BF16) for xe in xes],
        scratch_shapes=[pltpu.VMEM((xe.shape[0],) + xe.shape[2:], F32) for xe in xes],
        compiler_params=_params("arbitrary", "arbitrary"),
        name="expert_ffn",
    )(*xes, *gates, wg, wu, wd)


ATT_QROWS = 4
ATT_QTOK = ATT_QROWS * GRID_W
ATT_KROWS = ATT_QROWS + WIN_ROWS
ATT_KBLK = ATT_KROWS * GRID_W // ATT_QTOK
_NT = (((1,), (1,)), ((), ()))


def _rope_tables(n):
    quarter = HEAD_DIM // 4
    inv = ROPE_BASE ** (-jnp.arange(quarter, dtype=F32) / quarter)
    t = jnp.arange(n)
    ang_r = (t // GRID_W).astype(F32)[:, None] * inv[None, :]
    ang_c = (t % GRID_W).astype(F32)[:, None] * inv[None, :]
    cos = jnp.concatenate([jnp.cos(ang_r)] * 2 + [jnp.cos(ang_c)] * 2, axis=1)
    sin = jnp.concatenate([-jnp.sin(ang_r), jnp.sin(ang_r), -jnp.sin(ang_c), jnp.sin(ang_c)], axis=1)
    return jnp.concatenate([cos, cos], axis=1), jnp.concatenate([sin, sin], axis=1)


def _attn_bias_tables(rpb, n_rows):
    n_blk = n_rows // ATT_QROWS
    n_dr, n_dc = 2 * WIN_ROWS - 1, 2 * WIN_COLS - 1
    qc = np.arange(GRID_W)
    kc = np.arange(GRID_W)
    c_start = np.clip(qc - WIN_COLS // 2, 0, GRID_W - WIN_COLS)
    col_ok = (kc[None, :] >= c_start[:, None]) & (kc[None, :] < c_start[:, None] + WIN_COLS)
    d_col = np.clip(kc[None, :] - qc[:, None], 1 - WIN_COLS, WIN_COLS - 1) + (WIN_COLS - 1)
    col_sel = (d_col.reshape(-1)[None, :] == np.arange(n_dc)[:, None]).astype(np.float32)
    row_sel, oks = [], []
    for j in (0, 1, n_blk - 1):
        ks = min(max(ATT_QROWS * j - WIN_ROWS // 2, 0), n_rows - ATT_KROWS)
        r = ATT_QROWS * j + np.arange(ATT_QROWS)
        kr0 = np.clip(r - WIN_ROWS // 2, 0, n_rows - WIN_ROWS)
        krow = ks + np.arange(ATT_KROWS)
        row_ok = (krow[None, :] >= kr0[:, None]) & (krow[None, :] < kr0[:, None] + WIN_ROWS)
        d_row = np.clip(krow[None, :] - r[:, None] + (WIN_ROWS - 1), 0, n_dr - 1)
        row_sel.append((d_row.reshape(-1)[:, None] == np.arange(n_dr)[None, :]).astype(np.float32))
        oks.append(row_ok[:, None, :, None] & col_ok[None, :, None, :])
    hp = lax.Precision.HIGHEST
    a = jnp.einsum('cpr,hrd->chpd', jnp.asarray(np.stack(row_sel)), rpb.astype(F32), precision=hp)
    t = jnp.einsum('chpd,dx->chpx', a, jnp.asarray(col_sel), precision=hp)
    t = t.reshape(3, N_GROUPS, ATT_QROWS, ATT_KROWS, GRID_W, GRID_W).transpose(0, 1, 2, 4, 3, 5)
    ok = jnp.asarray(np.stack(oks))[:, None]
    return jnp.where(ok, t, NEG_INF).reshape(3, N_GROUPS, ATT_QTOK, ATT_KROWS * GRID_W)


def _rope(x, cos_ref, sin_ref, first16):
    c = cos_ref[...]
    s = sin_ref[...]
    c2 = jnp.concatenate([c, c], axis=1)
    s2 = jnp.concatenate([s, s], axis=1)
    w = x.shape[1]
    partner = jnp.where(first16, pltpu.roll(x, w - HEAD_DIM // 4, 1), pltpu.roll(x, HEAD_DIM // 4, 1))
    return x * c2 + partner * s2


def _softmax_pv(scores_values, head_mask, acc):
    m = None
    for s, _ in scores_values:
        mx = jnp.max(s, axis=1, keepdims=True)
        m = mx if m is None else jnp.maximum(m, mx)
    den = None
    o = None
    for s, v in scores_values:
        p = jnp.exp(s - m)
        sm = jnp.sum(p, axis=1, keepdims=True)
        den = sm if den is None else den + sm
        t = jnp.dot(p.astype(BF16), v, preferred_element_type=F32)
        o = t if o is None else o + t
    return acc + jnp.where(head_mask, o * (1.0 / den), 0.0)


def _nattn_kernel(q_ref, k0_ref, k1_ref, k2_ref, v0_ref, v1_ref, v2_ref,
                  cq_ref, sq_ref, ck0_ref, ck1_ref, ck2_ref, sk0_ref, sk1_ref, sk2_ref,
                  kc_ref, vc_ref, bias_ref, o_ref):
    w = q_ref.shape[2]
    lane = lax.broadcasted_iota(jnp.int32, (1, w), 1)
    first16 = (lane % (HEAD_DIM // 2)) < (HEAD_DIM // 4)
    q = q_ref[0] * (HEAD_DIM ** -0.5)
    q_rot = _rope(q, cq_ref, sq_ref, first16)
    k_rot = jnp.concatenate([_rope(k0_ref[0], ck0_ref, sk0_ref, first16),
                             _rope(k1_ref[0], ck1_ref, sk1_ref, first16),
                             _rope(k2_ref[0], ck2_ref, sk2_ref, first16)], axis=0).astype(BF16)
    v = jnp.concatenate([v0_ref[0], v1_ref[0], v2_ref[0]], axis=0).astype(BF16)
    kc = kc_ref[0].astype(BF16)
    vc = vc_ref[0].astype(BF16)
    acc = jnp.zeros(q.shape, F32)
    for h in range(N_GROUPS):
        hm = (lane // HEAD_DIM) == h
        s_win = lax.dot_general(jnp.where(hm, q_rot, 0.0).astype(BF16), k_rot, _NT,
                                preferred_element_type=F32) + bias_ref[0, h]
        s_ctx = lax.dot_general(jnp.where(hm, q, 0.0).astype(BF16), kc, _NT, preferred_element_type=F32)
        acc = _softmax_pv([(s_win, v), (s_ctx, vc)], hm, acc)
    o_ref[0] = acc


def neighbourhood_attention(z, zc, rpb, *, q_col, k_col, v_col, kc_col, vc_col):
    bsz, n, _ = z.shape
    n_ctx = zc.shape[1]
    w = BRANCH_WIDTH
    n_blk = n // ATT_QTOK
    cos_t, sin_t = _rope_tables(n)
    bias = _attn_bias_tables(rpb, n // GRID_W)

    def kb(j):
        return jnp.clip(j - 1, 0, n_blk - ATT_KBLK)

    def zspec(col, off=None):
        if off is None:
            return pl.BlockSpec((1, ATT_QTOK, w), lambda b, j: (b, j, col))
        return pl.BlockSpec((1, ATT_QTOK, w), lambda b, j: (b, kb(j) + off, col))

    def tspec(off=None):
        if off is None:
            return pl.BlockSpec((ATT_QTOK, 2 * HEAD_DIM), lambda b, j: (j, 0))
        return pl.BlockSpec((ATT_QTOK, 2 * HEAD_DIM), lambda b, j: (kb(j) + off, 0))

    in_specs = ([zspec(q_col)] + [zspec(k_col, i) for i in range(ATT_KBLK)] + [zspec(v_col, i) for i in range(ATT_KBLK)]
                + [tspec(), tspec()] + [tspec(i) for i in range(ATT_KBLK)] * 2
                + [pl.BlockSpec((1, n_ctx, w), lambda b, j: (b, 0, kc_col)),
                   pl.BlockSpec((1, n_ctx, w), lambda b, j: (b, 0, vc_col)),
                   pl.BlockSpec((1, N_GROUPS, ATT_QTOK, ATT_KROWS * GRID_W),
                                lambda b, j: (jnp.minimum(j, 1) + j // (n_blk - 1), 0, 0, 0))])
    return pl.pallas_call(
        _nattn_kernel,
        grid=(bsz, n_blk),
        in_specs=in_specs,
        out_specs=pl.BlockSpec((1, ATT_QTOK, w), lambda b, j: (b, j, 0)),
        out_shape=jax.ShapeDtypeStruct((bsz, n, w), F32),
        compiler_params=_params("arbitrary", "arbitrary"),
        name="neighbourhood_attention",
    )(z, z, z, z, z, z, z, cos_t, sin_t, cos_t, cos_t, cos_t, sin_t, sin_t, sin_t, zc, zc, bias)


def _ctx_attn_kernel(q_ref, kc_ref, vc_ref, o_ref):
    w = q_ref.shape[2]
    lane = lax.broadcasted_iota(jnp.int32, (1, w), 1)
    q = q_ref[0] * (HEAD_DIM ** -0.5)
    kc = kc_ref[0].astype(BF16)
    vc = vc_ref[0].astype(BF16)
    acc = jnp.zeros(q.shape, F32)
    for h in range(N_GROUPS):
        hm = (lane // HEAD_DIM) == h
        s = lax.dot_general(jnp.where(hm, q, 0.0).astype(BF16), kc, _NT, preferred_element_type=F32)
        acc = _softmax_pv([(s, vc)], hm, acc)
    o_ref[0] = acc


def context_attention(zc, *, q_col, k_col, v_col):
    bsz, n_ctx, _ = zc.shape
    w = BRANCH_WIDTH
    return pl.pallas_call(
        _ctx_attn_kernel,
        grid=(bsz,),
        in_specs=[pl.BlockSpec((1, n_ctx, w), functools.partial(lambda b, c: (b, 0, c), c=c))
                  for c in (q_col, k_col, v_col)],
        out_specs=pl.BlockSpec((1, n_ctx, w), lambda b: (b, 0, 0)),
        out_shape=jax.ShapeDtypeStruct((bsz, n_ctx, w), F32),
        compiler_params=_params("arbitrary"),
        name="context_attention",
    )(zc, zc, zc)


def _ln(v, g, b):
    mu = jnp.mean(v, axis=-1, keepdims=True)
    var = jnp.mean(jnp.square(v - mu), axis=-1, keepdims=True)
    return (v - mu) * lax.rsqrt(var + EPS) * g + b


def _gmlp_kernel(z_ref, lng_ref, lnb_ref, w_ref, bias_ref, o_ref):
    tm = z_ref.shape[1]
    z = jax.nn.gelu(z_ref[0])
    u = z[:, :BRANCH_WIDTH]
    v = _ln(z[:, BRANCH_WIDTH:], lng_ref[...], lnb_ref[...])
    group = lax.broadcasted_iota(jnp.int32, (1, BRANCH_WIDTH), 1) // HEAD_DIM
    wcat = w_ref[...]
    for c in range(tm // CHUNK):
        rows = slice(c * CHUNK, (c + 1) * CHUNK)
        vc = v[rows]
        vst = jnp.concatenate([jnp.where(group == g, vc, 0.0) for g in range(N_GROUPS)], axis=0).astype(BF16)
        mixed = jnp.dot(wcat, vst, preferred_element_type=F32) + bias_ref[...]
        o_ref[0, rows, :] = u[rows] * mixed


def chunk_gmlp(za, ln_g, ln_b, w_s, b_s, *, tm):
    bsz, n, w2 = za.shape
    w = w2 // 2
    wcat = jnp.transpose(w_s, (1, 0, 2)).reshape(CHUNK, N_GROUPS * CHUNK).astype(BF16)
    bias = jnp.repeat(b_s.T, HEAD_DIM, axis=1)
    return pl.pallas_call(
        _gmlp_kernel,
        grid=(bsz, n // tm),
        in_specs=[
            pl.BlockSpec((1, tm, w2), lambda b, i: (b, i, 0)),
            pl.BlockSpec((1, w), lambda b, i: (0, 0)),
            pl.BlockSpec((1, w), lambda b, i: (0, 0)),
            pl.BlockSpec((CHUNK, N_GROUPS * CHUNK), lambda b, i: (0, 0)),
            pl.BlockSpec((CHUNK, w), lambda b, i: (0, 0)),
        ],
        out_specs=pl.BlockSpec((1, tm, w), lambda b, i: (b, i, 0)),
        out_shape=jax.ShapeDtypeStruct((bsz, n, w), F32),
        compiler_params=_params("arbitrary", "arbitrary"),
        name="chunk_gmlp",
    )(za, ln_g[None, :], ln_b[None, :], wcat, bias)


CONV_HALO = 16
CONV_SUB = 64


def _conv_kernel(a_ref, g_ref, ap_ref, gp_ref, an_ref, gn_ref, w_ref, cb_ref, lng_ref, lnb_ref, o_ref, y_ref):
    i = pl.program_id(1)
    tm = a_ref.shape[1]

    def glu(a, g):
        return a * jax.nn.sigmoid(g)

    y_ref[0:CONV_HALO, :] = jnp.where(i > 0, glu(ap_ref[0], gp_ref[0]), 0.0)
    y_ref[CONV_HALO:CONV_HALO + tm, :] = glu(a_ref[0], g_ref[0])
    y_ref[CONV_HALO + tm:, :] = jnp.where(i < pl.num_programs(1) - 1, glu(an_ref[0], gn_ref[0]), 0.0)
    first = CONV_HALO - CONV_WIDTH // 2
    for r in range(0, tm, CONV_SUB):
        acc = None
        for j in range(CONV_WIDTH):
            t = y_ref[r + first + j:r + first + j + CONV_SUB, :] * w_ref[j:j + 1, :]
            acc = t if acc is None else acc + t
        y = _ln(acc + cb_ref[...], lng_ref[...], lnb_ref[...])
        o_ref[0, r:r + CONV_SUB, :] = jax.nn.silu(y)


def conformer_conv(zcv, conv_w, conv_b, ln_g, ln_b, *, tm):
    bsz, n, w2 = zcv.shape
    w = w2 // 2
    hb = tm // CONV_HALO
    n_hb = n // CONV_HALO

    def main(col):
        return pl.BlockSpec((1, tm, w), lambda b, i: (b, i, col))

    def prev(col):
        return pl.BlockSpec((1, CONV_HALO, w), lambda b, i: (b, jnp.maximum(i * hb - 1, 0), col))

    def nxt(col):
        return pl.BlockSpec((1, CONV_HALO, w), lambda b, i: (b, jnp.minimum((i + 1) * hb, n_hb - 1), col))

    vec = pl.BlockSpec((1, w), lambda b, i: (0, 0))
    return pl.pallas_call(
        _conv_kernel,
        grid=(bsz, n // tm),
        in_specs=[main(0), main(1), prev(0), prev(1), nxt(0), nxt(1),
                  pl.BlockSpec((CONV_WIDTH, w), lambda b, i: (0, 0)), vec, vec, vec],
        out_specs=pl.BlockSpec((1, tm, w), lambda b, i: (b, i, 0)),
        out_shape=jax.ShapeDtypeStruct((bsz, n, w), F32),
        scratch_shapes=[pltpu.VMEM((tm + 2 * CONV_HALO, w), F32)],
        compiler_params=_params("arbitrary", "arbitrary"),
        name="conformer_conv",
    )(zcv, zcv, zcv, zcv, zcv, zcv, conv_w, conv_b[None, :], ln_g[None, :], ln_b[None, :])


FOURIER_N2 = 128
FOURIER_FB = 8
FOURIER_KB = 4


def _np_split(m):
    m = jnp.asarray(m, F32)
    hi = m.astype(BF16)
    return hi, (m - hi.astype(F32)).astype(BF16)


def _dot3(a_hi, a_lo, b_hi, b_lo):
    return (jnp.dot(a_hi, b_hi, preferred_element_type=F32) + jnp.dot(a_hi, b_lo, preferred_element_type=F32)
            + jnp.dot(a_lo, b_hi, preferred_element_type=F32))


def _channel_dft_matrix():
    c = np.arange(HEAD_DIM)
    ang = 2.0 * np.pi * ((c[:, None] * c[None, :]) % HEAD_DIM) / HEAD_DIM
    eye = np.eye(N_GROUPS)
    return np.concatenate([np.kron(eye, np.cos(ang)), np.kron(eye, np.sin(ang))], axis=0)


def _fourier_stage1_kernel(x_ref, mh_ref, ml_ref, o_ref):
    w = BRANCH_WIDTH
    for f in range(mh_ref.shape[0]):
        x_hi, x_lo = _split_bf16(x_ref[0, :, f * w:(f + 1) * w])
        o_ref[0, :, f * w:(f + 1) * w] = _dot3(mh_ref[f], ml_ref[f], x_hi, x_lo)


def _fourier_stage2_kernel(br_ref, bi_ref, m2h_ref, m2l_ref, mdh_ref, mdl_ref, o_ref, *, scale):
    w = BRANCH_WIDTH
    for k in range(br_ref.shape[1]):
        b_hi, b_lo = _split_bf16(jnp.concatenate([br_ref[0, k], bi_ref[0, k]], axis=0))
        xs = _dot3(m2h_ref[...], m2l_ref[...], b_hi, b_lo)
        x_hi, x_lo = _split_bf16(jnp.concatenate([xs[:FOURIER_N2], xs[FOURIER_N2:]], axis=1))
        o_ref[0, :, k * w:(k + 1) * w] = _dot3(x_hi, x_lo, mdh_ref[...], mdl_ref[...]) * scale


def _fourier_direct_kernel(x_ref, mph_ref, mpl_ref, mdh_ref, mdl_ref, o_ref, *, scale):
    n = x_ref.shape[1]
    x_hi, x_lo = _split_bf16(x_ref[0])
    p = _dot3(mph_ref[...], mpl_ref[...], x_hi, x_lo)
    p_hi, p_lo = _split_bf16(jnp.concatenate([p[:n], p[n:]], axis=1))
    o_ref[0] = _dot3(p_hi, p_lo, mdh_ref[...], mdl_ref[...]) * scale


def fourier_mix(zb):
    bsz, n, w = zb.shape
    scale = float(1.0 / np.sqrt(n * HEAD_DIM))
    mdh, mdl = _np_split(_channel_dft_matrix())
    md_spec2 = pl.BlockSpec((2 * w, w), lambda b, j: (0, 0))
    if n <= 2 * FOURIER_N2:
        t = np.arange(n)
        ang = 2.0 * np.pi * ((t[:, None] * t[None, :]) % n) / n
        mph, mpl = _np_split(np.concatenate([np.cos(ang), -np.sin(ang)], axis=0))
        return pl.pallas_call(
            functools.partial(_fourier_direct_kernel, scale=scale),
            grid=(bsz,),
            in_specs=[pl.BlockSpec((1, n, w), lambda b: (b, 0, 0)),
                      pl.BlockSpec((2 * n, n), lambda b: (0, 0)), pl.BlockSpec((2 * n, n), lambda b: (0, 0)),
                      pl.BlockSpec((2 * w, w), lambda b: (0, 0)), pl.BlockSpec((2 * w, w), lambda b: (0, 0))],
            out_specs=pl.BlockSpec((1, n, w), lambda b: (b, 0, 0)),
            out_shape=jax.ShapeDtypeStruct((bsz, n, w), F32),
            compiler_params=_params("arbitrary"),
            name="fourier_direct",
        )(zb, mph, mpl, mdh, mdl)

    n1, n2 = n // FOURIER_N2, FOURIER_N2
    f, k1, s = np.arange(n2), np.arange(n1), np.arange(n1)
    ang1 = 2.0 * np.pi * ((k1[None, :, None] * (f[:, None, None] + n2 * s[None, None, :])) % n) / n
    m1h, m1l = _np_split(np.concatenate([np.cos(ang1), -np.sin(ang1)], axis=1))
    k2 = np.arange(n2)
    ang2 = 2.0 * np.pi * ((k2[:, None] * f[None, :]) % n2) / n2
    c2, s2 = np.cos(ang2), np.sin(ang2)
    m2h, m2l = _np_split(np.block([[c2, s2], [-s2, c2]]))

    b_st = pl.pallas_call(
        _fourier_stage1_kernel,
        grid=(bsz, n2 // FOURIER_FB),
        in_specs=[pl.BlockSpec((1, n1, FOURIER_FB * w), lambda b, j: (b, 0, j)),
                  pl.BlockSpec((FOURIER_FB, 2 * n1, n1), lambda b, j: (j, 0, 0)),
                  pl.BlockSpec((FOURIER_FB, 2 * n1, n1), lambda b, j: (j, 0, 0))],
        out_specs=pl.BlockSpec((1, 2 * n1, FOURIER_FB * w), lambda b, j: (b, 0, j)),
        out_shape=jax.ShapeDtypeStruct((bsz, 2 * n1, n2 * w), F32),
        compiler_params=_params("arbitrary", "arbitrary"),
        name="fourier_stage1",
    )(zb.reshape(bsz, n1, n2 * w), m1h, m1l)

    b_st = b_st.reshape(bsz, 2 * n1, n2, w)
    kb = FOURIER_KB
    out = pl.pallas_call(
        functools.partial(_fourier_stage2_kernel, scale=scale),
        grid=(bsz, n1 // kb),
        in_specs=[pl.BlockSpec((1, kb, n2, w), lambda b, j: (b, j, 0, 0)),
                  pl.BlockSpec((1, kb, n2, w), lambda b, j: (b, n1 // kb + j, 0, 0)),
                  pl.BlockSpec((2 * n2, 2 * n2), lambda b, j: (0, 0)),
                  pl.BlockSpec((2 * n2, 2 * n2), lambda b, j: (0, 0)),
                  md_spec2, md_spec2],
        out_specs=pl.BlockSpec((1, n2, kb * w), lambda b, j: (b, 0, j)),
        out_shape=jax.ShapeDtypeStruct((bsz, n2, n1 * w), F32),
        compiler_params=_params("arbitrary", "arbitrary"),
        name="fourier_stage2",
    )(b_st, b_st, m2h, m2l, mdh, mdl)
    return out.reshape(bsz, n, w)


def _layernorm(x, g, b):
    mu = jnp.mean(x, axis=-1, keepdims=True)
    var = jnp.mean(jnp.square(x - mu), axis=-1, keepdims=True)
    return (x - mu) * lax.rsqrt(var + EPS) * g + b


def _heads(z):
    return z.reshape(*z.shape[:-1], N_GROUPS, HEAD_DIM)


def _chunk_gmlp(z, ln_g, ln_b, w_s, b_s):
    z = jax.nn.gelu(z)
    u, v = jnp.split(z, 2, axis=-1)
    v = _layernorm(v, ln_g, ln_b)
    bsz, n, _ = v.shape
    v = v.reshape(bsz, n // CHUNK, CHUNK, N_GROUPS, HEAD_DIM)
    mixed = jnp.einsum('bcpgd,gqp->bcqgd', v, w_s) + b_s.T[None, None, :, :, None]
    return u * mixed.reshape(bsz, n, BRANCH_WIDTH)


def _fourier_mix(z):
    bsz, n, _ = z.shape
    zg = z.reshape(bsz, n, N_GROUPS, HEAD_DIM)
    f = jnp.fft.fft2(zg, axes=(1, 3), norm='ortho').real
    return f.reshape(bsz, n, BRANCH_WIDTH)


def _conformer_conv(z, conv_w, conv_b, ln_g, ln_b):
    a, gt = jnp.split(z, 2, axis=-1)
    y = a * jax.nn.sigmoid(gt)
    y = lax.conv_general_dilated(
        y, conv_w[:, None, :], window_strides=(1,),
        padding=[(CONV_WIDTH // 2, CONV_WIDTH // 2)],
        dimension_numbers=('NWC', 'WIO', 'NWC'),
        feature_group_count=BRANCH_WIDTH) + conv_b
    return jax.nn.silu(_layernorm(y, ln_g, ln_b))


def _axial_rope(x, rows, cols):
    half = HEAD_DIM // 2
    quarter = half // 2
    inv = ROPE_BASE ** (-jnp.arange(quarter, dtype=F32) / quarter)

    def rot(xp, pos):
        ang = pos.astype(F32)[:, None] * inv[None, :]
        cos = jnp.cos(ang)[None, :, None, :]
        sin = jnp.sin(ang)[None, :, None, :]
        x1, x2 = jnp.split(xp, 2, axis=-1)
        return jnp.concatenate([x1 * cos - x2 * sin, x1 * sin + x2 * cos], axis=-1)

    return jnp.concatenate([rot(x[..., :half], rows), rot(x[..., half:], cols)], axis=-1)


def _neighbourhood_attention(q, k, v, kc, vc, rpb):
    bsz, n, nh, dh = q.shape
    n_rows = n // GRID_W
    wr = min(WIN_ROWS, n_rows)
    n_cb = GRID_W // QB_COLS
    n_win = wr * KB_COLS
    t = jnp.arange(n)
    qs = q * (dh ** -0.5)
    q_rot = _axial_rope(qs, t // GRID_W, t % GRID_W)
    k_rot = _axial_rope(k, t // GRID_W, t % GRID_W)
    r = jnp.arange(n_rows)
    key_rows = jnp.clip(r - wr // 2, 0, n_rows - wr)[:, None] + jnp.arange(wr)[None, :]
    jb = jnp.arange(n_cb)
    key_cols = (jnp.clip(jb * QB_COLS - WIN_COLS // 2, 0, GRID_W - KB_COLS)[:, None]
                + jnp.arange(KB_COLS)[None, :])
    q_cols = jb[:, None] * QB_COLS + jnp.arange(QB_COLS)[None, :]
    c_start = jnp.clip(q_cols - WIN_COLS // 2, 0, GRID_W - WIN_COLS)
    kcol = key_cols[:, None, :]
    col_ok = (kcol >= c_start[..., None]) & (kcol < c_start[..., None] + WIN_COLS)
    d_row = key_rows - r[:, None] + (WIN_ROWS - 1)
    d_col = jnp.clip(kcol - q_cols[..., None], 1 - WIN_COLS, WIN_COLS - 1) + (WIN_COLS - 1)
    bias = rpb[:, d_row[:, None, None, :, None], d_col[None, :, :, None, :]].astype(F32)
    bias = jnp.where(col_ok[None, None, :, :, None, :], bias, NEG_INF)
    bias = bias.reshape(nh, n_rows, n_cb, QB_COLS, n_win)
    key_idx = (key_rows[:, None, :, None] * GRID_W + key_cols[None, :, None, :]).reshape(n_rows, n_cb, n_win)
    k_win = k_rot[:, key_idx]
    v_win = v[:, key_idx]
    q_rot = q_rot.reshape(bsz, n_rows, n_cb, QB_COLS, nh, dh)
    q_pl = qs.reshape(bsz, n_rows, n_cb, QB_COLS, nh, dh)
    s_win = jnp.einsum('brjqhd,brjkhd->bhrjqk', q_rot, k_win) + bias
    s_ctx = jnp.einsum('brjqhd,bchd->bhrjqc', q_pl, kc)
    p = jax.nn.softmax(jnp.concatenate([s_win, s_ctx], axis=-1), axis=-1)
    out = (jnp.einsum('bhrjqk,brjkhd->brjqhd', p[..., :n_win], v_win)
           + jnp.einsum('bhrjqc,bchd->brjqhd', p[..., n_win:], vc))
    return out.reshape(bsz, n, nh * dh)


def _context_attention(q, k, v):
    bsz, n, nh, dh = q.shape
    s = jnp.einsum('bqhd,bkhd->bhqk', q * (dh ** -0.5), k)
    p = jax.nn.softmax(s, axis=-1)
    return jnp.einsum('bhqk,bkhd->bqhd', p, v).reshape(bsz, n, nh * dh)


def expert_route(x, g, sc, sh, router_t, *, tm):
    n = x.shape[1]
    cap = EC_CAPACITY * n // N_EXPERTS
    h, aff_t = norm_router(x, g, sc, sh, router_t, tm=tm)
    slot, off = expert_topk(aff_t, cap)
    xe, gate = expert_gather(h, slot, aff_t, off, cap + ROUTE_WIN)
    return xe, gate, jnp.swapaxes(slot, 1, 2), off


def kernel(x, c, ctx, c_ctx, w_mod, b_mod, norm1_g, norm2_g, w_in, sgu_ln_g, sgu_ln_b, w_spatial, b_spatial,
           w_a_out, w_b_out, conv_w, conv_b, conv_ln_g, conv_ln_b, w_c_out, rpb, w_d_out, w_out, w_router,
           w_gate_e, w_up_e, w_down_e, final_norm_g):
    bsz = x.shape[0]
    xc = ctx
    silu_c = jax.nn.silu(c)
    silu_cc = jax.nn.silu(c_ctx)
    for l in range(DEPTH):
        last = l == DEPTH - 1
        hp = lax.Precision.HIGHEST
        mod = (jnp.dot(silu_c, w_mod[l], precision=hp) + b_mod[l])[:, None, :]
        sh1, sc1, gt1, sh2, sc2, gt2 = jnp.split(mod, 6, axis=-1)
        cmod = jnp.broadcast_to((jnp.dot(silu_cc, w_mod[l], precision=hp) + b_mod[l])[None, None, :],
                                (bsz, 1, 6 * D_MODEL))
        csh1, csc1, cgt1, csh2, csc2, cgt2 = jnp.split(cmod, 6, axis=-1)

        w_in_b = w_in[l].astype(BF16)
        wa, wb, wc, wd_, wo = (w.astype(BF16) for w in (w_a_out[l], w_b_out[l], w_c_out[l], w_d_out[l], w_out[l]))
        g1 = norm1_g[l][None, :]

        za, zb, zcv, zqkv, zg = norm_inproj(x, g1, sc1, sh1, w_in_b, IN_WIDTHS, tm=512)
        if last:
            (zqkv_c,) = norm_inproj(xc, g1, csc1, csh1, w_in_b[:, Q_END:V_END], (2 * BRANCH_WIDTH,), tm=CTX_TM)
            kc_col, vc_col = 0, 1
        else:
            cza, czb, czcv, zqkv_c, czg = norm_inproj(xc, g1, csc1, csh1, w_in_b, IN_WIDTHS, tm=CTX_TM)
            kc_col, vc_col = 1, 2
        d_lat = neighbourhood_attention(zqkv, zqkv_c, rpb[l], q_col=0, k_col=1, v_col=2,
                                        kc_col=kc_col, vc_col=vc_col)
        a_lat = chunk_gmlp(za, sgu_ln_g[l], sgu_ln_b[l], w_spatial[l], b_spatial[l], tm=512)
        b_lat = fourier_mix(zb)
        c_lat = conformer_conv(zcv, conv_w[l], conv_b[l], conv_ln_g[l], conv_ln_b[l], tm=512)
        x = merge_branches(a_lat, b_lat, c_lat, d_lat, zg, wa, wb, wc, wd_, wo, x, gt1, tm=512)
        if not last:
            a_c = chunk_gmlp(cza, sgu_ln_g[l], sgu_ln_b[l], w_spatial[l], b_spatial[l], tm=CTX_TM)
            b_c = fourier_mix(czb)
            c_c = conformer_conv(czcv, conv_w[l], conv_b[l], conv_ln_g[l], conv_ln_b[l], tm=CTX_TM)
            d_c = context_attention(zqkv_c, q_col=0, k_col=1, v_col=2)
            xc = merge_branches(a_c, b_c, c_c, d_c, czg, wa, wb, wc, wd_, wo, xc, cgt1, tm=CTX_TM)

        g2 = norm2_g[l][None, :]
        router_t = w_router[l].T
        xe, gate, slot_t, off = expert_route(x, g2, sc2, sh2, router_t, tm=512)
        if last:
            (y,) = expert_ffn([xe], [gate], w_gate_e[l], w_up_e[l], w_down_e[l])
        else:
            xe_c, gate_c, slot_tc, off_c = expert_route(xc, g2, csc2, csh2, router_t, tm=CTX_TM)
            y, y_c = expert_ffn([xe, xe_c], [gate, gate_c], w_gate_e[l], w_up_e[l], w_down_e[l])
            xc = expert_combine(y_c, slot_tc, off_c, xc, cgt2)
        x = expert_combine(y, slot_t, off, x, gt2)
    return x * lax.rsqrt(jnp.mean(x * x, axis=-1, keepdims=True) + EPS) * final_norm_g


CTX_TM = 256
```

```python
import functools

import jax
import jax.numpy as jnp
import numpy as np
from jax import lax
from jax.experimental import pallas as pl
from jax.experimental.pallas import tpu as pltpu

D_MODEL = 1024
DEPTH = 2
GRID_W = 64
HEAD_DIM = 64
N_GROUPS = 4
BRANCH_WIDTH = N_GROUPS * HEAD_DIM
N_BRANCH = 4
CHUNK = 128
CONV_WIDTH = 31
WIN_ROWS = 8
WIN_COLS = 16
QB_COLS = 16
KB_COLS = QB_COLS + WIN_COLS
ROPE_BASE = 10000.0
N_EXPERTS = 16
EXPERT_FF = 1024
EC_CAPACITY = 2
EPS = 1e-6
NEG_INF = -1e30
A_END = 2 * BRANCH_WIDTH
B_END = A_END + BRANCH_WIDTH
C_END = B_END + 2 * BRANCH_WIDTH
Q_END = C_END + BRANCH_WIDTH
K_END = Q_END + BRANCH_WIDTH
V_END = K_END + BRANCH_WIDTH
IN_COLS = V_END + N_BRANCH * D_MODEL
SPLITS = [A_END, B_END, C_END, Q_END, K_END, V_END]
IN_WIDTHS = (A_END, B_END - A_END, C_END - B_END, V_END - C_END, IN_COLS - V_END)
IN_DTYPES = (jnp.float32, jnp.float32, jnp.float32, jnp.bfloat16, jnp.bfloat16)

VMEM_LIMIT_BYTES = 56 * 1024 * 1024
F32 = jnp.float32
BF16 = jnp.bfloat16


def _params(*sem):
    return pltpu.CompilerParams(dimension_semantics=sem, vmem_limit_bytes=VMEM_LIMIT_BYTES)


MOD_ROWS = 8
MOD_TN = 1536


def _split_bf16(v):
    hi = v.astype(BF16)
    return hi, (v - hi.astype(F32)).astype(BF16)


def _dot3(a_hi, a_lo, b_hi, b_lo):
    return (jnp.dot(a_hi, b_hi, preferred_element_type=F32) + jnp.dot(a_hi, b_lo, preferred_element_type=F32)
            + jnp.dot(a_lo, b_hi, preferred_element_type=F32))


def _mod_kernel(c_ref, w_ref, b_ref, o_ref):
    s_hi, s_lo = _split_bf16(jax.nn.silu(c_ref[...]))
    w_hi, w_lo = _split_bf16(w_ref[0])
    o_ref[0] = _dot3(s_hi, s_lo, w_hi, w_lo) + b_ref[0]


def modulation(cond, w_mod, b_mod):
    n_layers, d, cols = w_mod.shape
    return pl.pallas_call(
        _mod_kernel,
        grid=(n_layers, cols // MOD_TN),
        in_specs=[pl.BlockSpec((MOD_ROWS, d), lambda l, j: (0, 0)),
                  pl.BlockSpec((1, d, MOD_TN), lambda l, j: (l, 0, j)),
                  pl.BlockSpec((1, 1, MOD_TN), lambda l, j: (l, 0, j))],
        out_specs=pl.BlockSpec((1, MOD_ROWS, MOD_TN), lambda l, j: (l, 0, j)),
        out_shape=jax.ShapeDtypeStruct((n_layers, MOD_ROWS, cols), F32),
        compiler_params=_params("arbitrary", "arbitrary"),
        name="modulation",
    )(cond, w_mod, b_mod[:, None, :])


INPROJ_COL_CHUNK = 512


def _norm_inproj_kernel(x_ref, g_ref, sc_ref, sh_ref, w_ref, *o_refs):
    x = x_ref[0]
    y = x * lax.rsqrt(jnp.mean(x * x, axis=-1, keepdims=True) + EPS) * g_ref[...]
    h = (y * (1.0 + sc_ref[0]) + sh_ref[0]).astype(BF16)
    off = 0
    for o_ref in o_refs:
        width = o_ref.shape[2]
        for c0 in range(0, width, INPROJ_COL_CHUNK):
            cw = min(INPROJ_COL_CHUNK, width - c0)
            o_ref[0, :, c0:c0 + cw] = jnp.dot(h, w_ref[:, off + c0:off + c0 + cw],
                                              preferred_element_type=F32).astype(o_ref.dtype)
        off += width


def norm_inproj(x, g, sc, sh, w, widths, dtypes, *, tm):
    bsz, n, d = x.shape
    cols = w.shape[1]
    assert sum(widths) == cols
    return pl.pallas_call(
        _norm_inproj_kernel,
        grid=(bsz, n // tm),
        in_specs=[
            pl.BlockSpec((1, tm, d), lambda b, i: (b, i, 0)),
            pl.BlockSpec((1, d), lambda b, i: (0, 0)),
            pl.BlockSpec((1, 1, d), lambda b, i: (b, 0, 0)),
            pl.BlockSpec((1, 1, d), lambda b, i: (b, 0, 0)),
            pl.BlockSpec((d, cols), lambda b, i: (0, 0), pipeline_mode=pl.Buffered(1)),
        ],
        out_specs=[pl.BlockSpec((1, tm, wd), lambda b, i: (b, i, 0)) for wd in widths],
        out_shape=[jax.ShapeDtypeStruct((bsz, n, wd), dt) for wd, dt in zip(widths, dtypes, strict=True)],
        compiler_params=_params("arbitrary", "arbitrary"),
        name="norm_inproj",
    )(x, g, sc, sh, w)


def _merge_kernel(a_ref, b_ref, c_ref, d_ref, ga_ref, gb_ref, gc_ref, gd_ref,
                  wa_ref, wb_ref, wc_ref, wd_ref, wo_ref, x_ref, gt_ref, o_ref):
    m = None
    for br, gz, w in ((a_ref, ga_ref, wa_ref), (b_ref, gb_ref, wb_ref),
                      (c_ref, gc_ref, wc_ref), (d_ref, gd_ref, wd_ref)):
        p = jnp.dot(br[0].astype(BF16), w[...], preferred_element_type=F32)
        t = jax.nn.sigmoid(gz[0].astype(F32)) * p
        m = t if m is None else m + t
    mix = jnp.dot(m.astype(BF16), wo_ref[...], preferred_element_type=F32)
    o_ref[0] = x_ref[0] + gt_ref[0] * mix


def merge_branches(a, b, cc, d, z, w_a, w_b, w_c, w_d, w_o, x, gt, *, tm):
    bsz, n, dm = x.shape
    w = a.shape[-1]
    br_spec = pl.BlockSpec((1, tm, w), lambda bi, i: (bi, i, 0))
    gate_specs = [pl.BlockSpec((1, tm, dm), functools.partial(lambda bi, i, k: (bi, i, k), k=k))
                  for k in range(N_BRANCH)]
    wbr_spec = pl.BlockSpec((w, dm), lambda bi, i: (0, 0))
    return pl.pallas_call(
        _merge_kernel,
        grid=(bsz, n // tm),
        in_specs=[br_spec] * 4 + gate_specs + [wbr_spec] * 4 + [
            pl.BlockSpec((dm, dm), lambda bi, i: (0, 0)),
            pl.BlockSpec((1, tm, dm), lambda bi, i: (bi, i, 0)),
            pl.BlockSpec((1, 1, dm), lambda bi, i: (bi, 0, 0)),
        ],
        out_specs=pl.BlockSpec((1, tm, dm), lambda bi, i: (bi, i, 0)),
        out_shape=jax.ShapeDtypeStruct((bsz, n, dm), F32),
        compiler_params=_params("arbitrary", "arbitrary"),
        name="merge_branches",
    )(a, b, cc, d, z, z, z, z, w_a, w_b, w_c, w_d, w_o, x, gt)


ROUTE_TB = 256
ROUTE_WIN = 64
ROUTE_ALIGN = 16
OFF_LANES = 128
TOPK_EXP_STEPS = (64, 32, 16, 8, 4, 2, 1)
TOPK_BISECT_STEPS = 56


def _norm_router_kernel(x_ref, g_ref, sc_ref, sh_ref, rt_ref, h_ref, aff_ref):
    x = x_ref[0]
    y = x * lax.rsqrt(jnp.mean(x * x, axis=-1, keepdims=True) + EPS) * g_ref[...]
    h = y * (1.0 + sc_ref[0]) + sh_ref[0]
    h_hi, h_lo = _split_bf16(h)
    r_hi, r_lo = _split_bf16(rt_ref[...])
    logits = (lax.dot_general(r_hi, h_hi, _NT, preferred_element_type=F32)
              + lax.dot_general(r_hi, h_lo, _NT, preferred_element_type=F32)
              + lax.dot_general(r_lo, h_hi, _NT, preferred_element_type=F32))
    p = jnp.exp(logits - jnp.max(logits, axis=0, keepdims=True))
    aff_ref[0] = p / jnp.sum(p, axis=0, keepdims=True)
    h_ref[0] = h_hi


def norm_router(x, g, sc, sh, router_t, *, tm):
    bsz, n, d = x.shape
    ne = router_t.shape[0]
    return pl.pallas_call(
        _norm_router_kernel,
        grid=(bsz, n // tm),
        in_specs=[
            pl.BlockSpec((1, tm, d), lambda b, i: (b, i, 0)),
            pl.BlockSpec((1, d), lambda b, i: (0, 0)),
            pl.BlockSpec((1, 1, d), lambda b, i: (b, 0, 0)),
            pl.BlockSpec((1, 1, d), lambda b, i: (b, 0, 0)),
            pl.BlockSpec((ne, d), lambda b, i: (0, 0)),
        ],
        out_specs=[pl.BlockSpec((1, tm, d), lambda b, i: (b, i, 0)),
                   pl.BlockSpec((1, ne, tm), lambda b, i: (b, 0, i))],
        out_shape=[jax.ShapeDtypeStruct((bsz, n, d), BF16), jax.ShapeDtypeStruct((bsz, ne, n), F32)],
        compiler_params=_params("arbitrary", "arbitrary"),
        name="norm_router",
    )(x, g, sc, sh, router_t)


def _topk_kernel(aff_ref, slot_ref, off_ref, *, cap):
    a = aff_ref[0]
    ne, n = a.shape
    capf = jnp.float32(cap)

    def count_ge(t):
        return jnp.sum(jnp.where(a >= t, 1.0, 0.0), axis=1, keepdims=True)

    hi = jnp.full((ne, 1), 2.0, F32)
    for s in TOPK_EXP_STEPS:
        cand = hi * (2.0 ** -s)
        hi = jnp.where(count_ge(cand) < capf, cand, hi)
    lo = jnp.where(hi <= 2.0 ** -126, 0.0, hi * 0.5)

    def bisect(_, carry):
        lo, hi = carry
        mid = 0.5 * (lo + hi)
        ok = count_ge(mid) >= capf
        return jnp.where(ok, mid, lo), jnp.where(ok, hi, mid)

    lo, hi = lax.fori_loop(0, TOPK_BISECT_STEPS, bisect, (lo, hi))
    need = capf - count_ge(hi)
    r_i = lax.broadcasted_iota(jnp.int32, (ROUTE_TB, ROUTE_TB), 0)
    c_i = lax.broadcasted_iota(jnp.int32, (ROUTE_TB, ROUTE_TB), 1)
    tri = jnp.where(r_i < c_i, 1.0, 0.0).astype(BF16)
    lane = lax.broadcasted_iota(jnp.int32, (ne, OFF_LANES), 1)
    run_eq = jnp.zeros((ne, 1), F32)
    run_sel = jnp.zeros((ne, 1), F32)
    offs = jnp.zeros((ne, OFF_LANES), F32)
    for c in range(n // ROUTE_TB):
        cols = slice(c * ROUTE_TB, (c + 1) * ROUTE_TB)
        a_c = a[:, cols]
        above = a_c >= hi
        eq_c = jnp.where(above, 0.0, jnp.where(a_c >= lo, 1.0, 0.0))
        rank = jnp.dot(eq_c.astype(BF16), tri, preferred_element_type=F32) + run_eq
        sel = jnp.where(above, 1.0, jnp.where(rank < need, eq_c, 0.0))
        pos = jnp.dot(sel.astype(BF16), tri, preferred_element_type=F32) + run_sel
        slot_ref[0, :, cols] = jnp.where(sel > 0.0, pos, -1.0).astype(jnp.int32)
        offs = jnp.where(lane == c, run_sel, offs)
        run_eq = run_eq + jnp.sum(eq_c, axis=1, keepdims=True)
        run_sel = run_sel + jnp.sum(sel, axis=1, keepdims=True)
    offs = jnp.where(lane == n // ROUTE_TB, run_sel, offs)
    off_ref[0] = offs.astype(jnp.int32)


def expert_topk(aff_t, cap):
    bsz, ne, n = aff_t.shape
    return pl.pallas_call(
        functools.partial(_topk_kernel, cap=cap),
        grid=(bsz,),
        in_specs=[pl.BlockSpec((1, ne, n), lambda b: (b, 0, 0))],
        out_specs=[pl.BlockSpec((1, ne, n), lambda b: (b, 0, 0)),
                   pl.BlockSpec((1, ne, OFF_LANES), lambda b: (b, 0, 0))],
        out_shape=[jax.ShapeDtypeStruct((bsz, ne, n), jnp.int32),
                   jax.ShapeDtypeStruct((bsz, ne, OFF_LANES), jnp.int32)],
        compiler_params=_params("arbitrary"),
        name="expert_topk",
    )(aff_t)


def _window_plan(off_ref, b, e, tb, w, rows_max):
    start = off_ref[b, e, tb]
    stop = off_ref[b, e, tb + 1]
    first = (start // ROUTE_ALIGN) * ROUTE_ALIGN + w * ROUTE_WIN
    active = first < stop
    return pl.multiple_of(jnp.minimum(first, rows_max), ROUTE_ALIGN), first, active


def _num_passes(off_ref, b, e_list, tb):
    n_pass = jnp.int32(0)
    for e in e_list:
        start = off_ref[b, e, tb]
        stop = off_ref[b, e, tb + 1]
        base = (start // ROUTE_ALIGN) * ROUTE_ALIGN
        n_pass = jnp.maximum(n_pass, (stop - base + ROUTE_WIN - 1) // ROUTE_WIN)
    return n_pass


def _gather_kernel(off_ref, h_ref, slot_ref, aff_ref, xe_ref, gate_ref, *, group):
    b, g, tb = pl.program_id(0), pl.program_id(1), pl.program_id(2)
    rows_max = xe_ref.shape[2] - ROUTE_WIN

    @pl.when(tb == 0)
    def _():
        xe_ref[...] = jnp.zeros(xe_ref.shape, xe_ref.dtype)
        gate_ref[...] = jnp.zeros(gate_ref.shape, gate_ref.dtype)

    sub = lax.broadcasted_iota(jnp.int32, (ROUTE_WIN, 1), 0)
    experts = [g * group + e for e in range(group)]

    def one_pass(w, carry):
        plans = [_window_plan(off_ref, b, ge, tb, w, rows_max) for ge in experts]
        hots = []
        for e, (rows, first, active) in enumerate(plans):
            ids = rows + sub
            want = jnp.where(active & (ids >= first), ids, -2)
            hots.append(slot_ref[0, e:e + 1, :] == want)
        onehot = jnp.concatenate([jnp.where(hm, 1.0, 0.0) for hm in hots], axis=0).astype(BF16)
        res = jnp.dot(onehot, h_ref[0], preferred_element_type=F32)
        for e, (rows, _, _) in enumerate(plans):
            win = pl.ds(rows, ROUTE_WIN)
            xe_ref[0, e, win, :] = xe_ref[0, e, win, :] + res[e * ROUTE_WIN:(e + 1) * ROUTE_WIN].astype(BF16)
            gsel = jnp.sum(jnp.where(hots[e], aff_ref[0, e:e + 1, :], 0.0), axis=1, keepdims=True)
            gate_ref[0, e, win, :] = gate_ref[0, e, win, :] + gsel
        return carry

    lax.fori_loop(0, _num_passes(off_ref, b, experts, tb), one_pass, 0)


def expert_gather(h, slot, aff_t, off, cap_pad, *, group=8):
    bsz, n, d = h.shape
    ne = slot.shape[1]
    grid_spec = pltpu.PrefetchScalarGridSpec(
        num_scalar_prefetch=1,
        grid=(bsz, ne // group, n // ROUTE_TB),
        in_specs=[
            pl.BlockSpec((1, ROUTE_TB, d), lambda b, g, t, off: (b, t, 0)),
            pl.BlockSpec((1, group, ROUTE_TB), lambda b, g, t, off: (b, g, t)),
            pl.BlockSpec((1, group, ROUTE_TB), lambda b, g, t, off: (b, g, t)),
        ],
        out_specs=[pl.BlockSpec((1, group, cap_pad, d), lambda b, g, t, off: (b, g, 0, 0)),
                   pl.BlockSpec((1, group, cap_pad, 1), lambda b, g, t, off: (b, g, 0, 0))],
    )
    return pl.pallas_call(
        functools.partial(_gather_kernel, group=group),
        grid_spec=grid_spec,
        out_shape=[jax.ShapeDtypeStruct((bsz, ne, cap_pad, d), BF16),
                   jax.ShapeDtypeStruct((bsz, ne, cap_pad, 1), F32)],
        compiler_params=_params("arbitrary", "arbitrary", "arbitrary"),
        name="expert_gather",
    )(off, h, slot, aff_t)


def _combine_kernel(off_ref, y_ref, slot_t_ref, x_ref, gt_ref, fg_ref, o_ref, stage_ref, acc_ref):
    b, tb = pl.program_id(0), pl.program_id(1)
    ne = y_ref.shape[1]
    rows_max = y_ref.shape[2] - ROUTE_WIN
    width = ne * ROUTE_WIN
    experts = list(range(ne))
    lane = lax.broadcasted_iota(jnp.int32, (1, width), 1)
    s1 = slot_t_ref[0] + 1
    hi = (s1 // ROUTE_WIN).astype(F32).astype(BF16)
    lo = (s1 % ROUTE_WIN).astype(F32).astype(BF16)
    e_row = lax.broadcasted_iota(jnp.int32, (ne, width), 0)
    e_lane = lax.broadcasted_iota(jnp.int32, (ne, width), 1) // ROUTE_WIN
    expand = jnp.where(e_row == e_lane, 1.0, 0.0).astype(BF16)
    s1_wide = (float(ROUTE_WIN) * jnp.dot(hi, expand, preferred_element_type=F32)
               + jnp.dot(lo, expand, preferred_element_type=F32))
    acc_ref[...] = jnp.zeros(acc_ref.shape, F32)

    def one_pass(w, carry):
        want = jnp.full((1, width), -1, jnp.int32)
        for e in experts:
            rows, first, active = _window_plan(off_ref, b, e, tb, w, rows_max)
            stage_ref[e * ROUTE_WIN:(e + 1) * ROUTE_WIN, :] = y_ref[0, e, pl.ds(rows, ROUTE_WIN), :]
            ids = rows + lane % ROUTE_WIN
            want = jnp.where((lane // ROUTE_WIN == e) & active & (ids >= first), ids + 1, want)
        onehot = jnp.where(s1_wide == want.astype(F32), 1.0, 0.0).astype(BF16)
        acc_ref[...] += jnp.dot(onehot, stage_ref[...], preferred_element_type=F32)
        return carry

    lax.fori_loop(0, _num_passes(off_ref, b, experts, tb), one_pass, 0)
    out = x_ref[0] + gt_ref[0] * acc_ref[...]
    if fg_ref is not None:
        out = out * lax.rsqrt(jnp.mean(out * out, axis=-1, keepdims=True) + EPS) * fg_ref[...]
    o_ref[0] = out


def _combine_kernel_plain(off_ref, y_ref, slot_t_ref, x_ref, gt_ref, o_ref, stage_ref, acc_ref):
    _combine_kernel(off_ref, y_ref, slot_t_ref, x_ref, gt_ref, None, o_ref, stage_ref, acc_ref)


def expert_combine(y, slot_t, off, x, gt, final_g=None):
    bsz, n, d = x.shape
    ne, cap_pad = y.shape[1], y.shape[2]
    grid_spec = pltpu.PrefetchScalarGridSpec(
        num_scalar_prefetch=1,
        grid=(bsz, n // ROUTE_TB),
        in_specs=[
            pl.BlockSpec((1, ne, cap_pad, d), lambda b, t, off: (b, 0, 0, 0), pipeline_mode=pl.Buffered(1)),
            pl.BlockSpec((1, ROUTE_TB, ne), lambda b, t, off: (b, t, 0)),
            pl.BlockSpec((1, ROUTE_TB, d), lambda b, t, off: (b, t, 0)),
            pl.BlockSpec((1, 1, d), lambda b, t, off: (b, 0, 0)),
        ] + ([] if final_g is None else [pl.BlockSpec((1, d), lambda b, t, off: (0, 0))]),
        out_specs=pl.BlockSpec((1, ROUTE_TB, d), lambda b, t, off: (b, t, 0)),
        scratch_shapes=[pltpu.VMEM((ne * ROUTE_WIN, d), BF16), pltpu.VMEM((ROUTE_TB, d), F32)],
    )
    args = (off, y, slot_t, x, gt) + (() if final_g is None else (final_g[None, :],))
    return pl.pallas_call(
        _combine_kernel_plain if final_g is None else _combine_kernel,
        grid_spec=grid_spec,
        out_shape=jax.ShapeDtypeStruct((bsz, n, d), F32),
        compiler_params=_params("arbitrary", "arbitrary"),
        name="expert_combine",
    )(*args)


FFN_FBLK = 512


def _expert_ffn_kernel(*refs, n_sets):
    xe_refs, gate_refs = refs[:n_sets], refs[n_sets:2 * n_sets]
    wg_ref, wu_ref, wd_ref = refs[2 * n_sets:2 * n_sets + 3]
    o_refs = refs[2 * n_sets + 3:3 * n_sets + 3]
    acc_refs = refs[3 * n_sets + 3:]
    f = pl.program_id(1)
    wg = wg_ref[0].astype(BF16)
    wu = wu_ref[0].astype(BF16)
    wd = wd_ref[0].astype(BF16)
    for s in range(n_sets):
        for b in range(xe_refs[s].shape[0]):
            xe = xe_refs[s][b, 0]
            a = jnp.dot(xe, wg, preferred_element_type=F32)
            u = jnp.dot(xe, wu, preferred_element_type=F32)
            t = jnp.dot((jax.nn.silu(a) * u).astype(BF16), wd, preferred_element_type=F32)

            @pl.when(f == 0)
            def _():
                acc_refs[s][b] = t

            @pl.when(f > 0)
            def _():
                acc_refs[s][b] += t

    @pl.when(f == pl.num_programs(1) - 1)
    def _():
        for s in range(n_sets):
            for b in range(xe_refs[s].shape[0]):
                o_refs[s][b, 0] = (acc_refs[s][b] * gate_refs[s][b, 0]).astype(o_refs[s].dtype)


def expert_ffn(xes, gates, wg, wu, wd):
    n_sets = len(xes)
    ne, d, ff = wg.shape
    xe_specs = [pl.BlockSpec((xe.shape[0], 1) + xe.shape[2:], lambda e, f: (0, e, 0, 0)) for xe in xes]
    gate_specs = [pl.BlockSpec((g.shape[0], 1) + g.shape[2:], lambda e, f: (0, e, 0, 0)) for g in gates]
    return pl.pallas_call(
        functools.partial(_expert_ffn_kernel, n_sets=n_sets),
        grid=(ne, ff // FFN_FBLK),
        in_specs=xe_specs + gate_specs + [
            pl.BlockSpec((1, d, FFN_FBLK), lambda e, f: (e, 0, f)),
            pl.BlockSpec((1, d, FFN_FBLK), lambda e, f: (e, 0, f)),
            pl.BlockSpec((1, FFN_FBLK, d), lambda e, f: (e, f, 0)),
        ],
        out_specs=xe_specs,
        out_shape=[jax.ShapeDtypeStruct(xe.shape, BF16) for xe in xes],
        scratch_shapes=[pltpu.VMEM((xe.shape[0],) + xe.shape[2:], F32) for xe in xes],
        compiler_params=_params("arbitrary", "arbitrary"),
        name="expert_ffn",
    )(*xes, *gates, wg, wu, wd)


ATT_QROWS = 4
ATT_QTOK = ATT_QROWS * GRID_W
ATT_KROWS = ATT_QROWS + WIN_ROWS
ATT_KBLK = ATT_KROWS * GRID_W // ATT_QTOK
_NT = (((1,), (1,)), ((), ()))


def _rope_tables(n):
    quarter = HEAD_DIM // 4
    inv = ROPE_BASE ** (-np.arange(quarter, dtype=np.float64) / quarter)
    t = np.arange(n)
    ang_r = (t // GRID_W)[:, None] * inv[None, :]
    ang_c = (t % GRID_W)[:, None] * inv[None, :]
    cos = np.concatenate([np.cos(ang_r)] * 2 + [np.cos(ang_c)] * 2, axis=1)
    sin = np.concatenate([-np.sin(ang_r), np.sin(ang_r), -np.sin(ang_c), np.sin(ang_c)], axis=1)
    return (jnp.asarray(np.concatenate([cos, cos], axis=1), F32),
            jnp.asarray(np.concatenate([sin, sin], axis=1), F32))


def _attn_bias_tables(rpb, n_rows):
    n_blk = n_rows // ATT_QROWS
    n_dr, n_dc = 2 * WIN_ROWS - 1, 2 * WIN_COLS - 1
    qc = np.arange(GRID_W)
    kc = np.arange(GRID_W)
    c_start = np.clip(qc - WIN_COLS // 2, 0, GRID_W - WIN_COLS)
    col_ok = (kc[None, :] >= c_start[:, None]) & (kc[None, :] < c_start[:, None] + WIN_COLS)
    d_col = np.clip(kc[None, :] - qc[:, None], 1 - WIN_COLS, WIN_COLS - 1) + (WIN_COLS - 1)
    col_sel = (d_col.reshape(-1)[None, :] == np.arange(n_dc)[:, None]).astype(np.float32)
    row_sel, oks = [], []
    for j in (0, 1, n_blk - 1):
        ks = min(max(ATT_QROWS * j - WIN_ROWS // 2, 0), n_rows - ATT_KROWS)
        r = ATT_QROWS * j + np.arange(ATT_QROWS)
        kr0 = np.clip(r - WIN_ROWS // 2, 0, n_rows - WIN_ROWS)
        krow = ks + np.arange(ATT_KROWS)
        row_ok = (krow[None, :] >= kr0[:, None]) & (krow[None, :] < kr0[:, None] + WIN_ROWS)
        d_row = np.clip(krow[None, :] - r[:, None] + (WIN_ROWS - 1), 0, n_dr - 1)
        row_sel.append((d_row.reshape(-1)[:, None] == np.arange(n_dr)[None, :]).astype(np.float32))
        oks.append(row_ok[:, None, :, None] & col_ok[None, :, None, :])
    hp = lax.Precision.HIGHEST
    a = jnp.einsum('cpr,hrd->chpd', jnp.asarray(np.stack(row_sel)), rpb.astype(F32), precision=hp)
    t = jnp.einsum('chpd,dx->chpx', a, jnp.asarray(col_sel), precision=hp)
    t = t.reshape(3, N_GROUPS, ATT_QROWS, ATT_KROWS, GRID_W, GRID_W).transpose(0, 1, 2, 4, 3, 5)
    ok = jnp.asarray(np.stack(oks))[:, None]
    return jnp.where(ok, t, NEG_INF).reshape(3, N_GROUPS, ATT_QTOK, ATT_KROWS * GRID_W)


def _rope(x, cos_ref, sin_ref, first16):
    c = cos_ref[...]
    s = sin_ref[...]
    c2 = jnp.concatenate([c, c], axis=1)
    s2 = jnp.concatenate([s, s], axis=1)
    w = x.shape[1]
    partner = jnp.where(first16, pltpu.roll(x, w - HEAD_DIM // 4, 1), pltpu.roll(x, HEAD_DIM // 4, 1))
    return x * c2 + partner * s2


def _softmax_pv(scores_values, head_mask, acc):
    m = None
    for s, _ in scores_values:
        mx = jnp.max(s, axis=1, keepdims=True)
        m = mx if m is None else jnp.maximum(m, mx)
    den = None
    o = None
    for s, v in scores_values:
        p = jnp.exp(s - m)
        sm = jnp.sum(p, axis=1, keepdims=True)
        den = sm if den is None else den + sm
        t = jnp.dot(p.astype(BF16), v, preferred_element_type=F32)
        o = t if o is None else o + t
    return acc + jnp.where(head_mask, o * (1.0 / den), 0.0)


def _nattn_kernel(q_ref, k0_ref, k1_ref, k2_ref, v0_ref, v1_ref, v2_ref,
                  cq_ref, sq_ref, ck0_ref, ck1_ref, ck2_ref, sk0_ref, sk1_ref, sk2_ref,
                  kc_ref, vc_ref, bias_ref, o_ref):
    w = q_ref.shape[2]
    lane = lax.broadcasted_iota(jnp.int32, (1, w), 1)
    first16 = (lane % (HEAD_DIM // 2)) < (HEAD_DIM // 4)
    q = q_ref[0].astype(F32) * (HEAD_DIM ** -0.5)
    q_rot = _rope(q, cq_ref, sq_ref, first16)
    k_rot = jnp.concatenate([_rope(k0_ref[0].astype(F32), ck0_ref, sk0_ref, first16),
                             _rope(k1_ref[0].astype(F32), ck1_ref, sk1_ref, first16),
                             _rope(k2_ref[0].astype(F32), ck2_ref, sk2_ref, first16)], axis=0).astype(BF16)
    v = jnp.concatenate([v0_ref[0], v1_ref[0], v2_ref[0]], axis=0).astype(BF16)
    kc = kc_ref[0].astype(BF16)
    vc = vc_ref[0].astype(BF16)
    acc = jnp.zeros(q.shape, F32)
    for h in range(N_GROUPS):
        hm = (lane // HEAD_DIM) == h
        s_win = lax.dot_general(jnp.where(hm, q_rot, 0.0).astype(BF16), k_rot, _NT,
                                preferred_element_type=F32) + bias_ref[0, h]
        s_ctx = lax.dot_general(jnp.where(hm, q, 0.0).astype(BF16), kc, _NT, preferred_element_type=F32)
        acc = _softmax_pv([(s_win, v), (s_ctx, vc)], hm, acc)
    o_ref[0] = acc.astype(o_ref.dtype)


def neighbourhood_attention(z, zc, rpb, *, q_col, k_col, v_col, kc_col, vc_col):
    bsz, n, _ = z.shape
    n_ctx = zc.shape[1]
    w = BRANCH_WIDTH
    n_blk = n // ATT_QTOK
    cos_t, sin_t = _rope_tables(n)
    bias = _attn_bias_tables(rpb, n // GRID_W)

    def kb(j):
        return jnp.clip(j - 1, 0, n_blk - ATT_KBLK)

    def zspec(col, off=None):
        if off is None:
            return pl.BlockSpec((1, ATT_QTOK, w), lambda b, j: (b, j, col))
        return pl.BlockSpec((1, ATT_QTOK, w), lambda b, j: (b, kb(j) + off, col))

    def tspec(off=None):
        if off is None:
            return pl.BlockSpec((ATT_QTOK, 2 * HEAD_DIM), lambda b, j: (j, 0))
        return pl.BlockSpec((ATT_QTOK, 2 * HEAD_DIM), lambda b, j: (kb(j) + off, 0))

    in_specs = ([zspec(q_col)] + [zspec(k_col, i) for i in range(ATT_KBLK)] + [zspec(v_col, i) for i in range(ATT_KBLK)]
                + [tspec(), tspec()] + [tspec(i) for i in range(ATT_KBLK)] * 2
                + [pl.BlockSpec((1, n_ctx, w), lambda b, j: (b, 0, kc_col)),
                   pl.BlockSpec((1, n_ctx, w), lambda b, j: (b, 0, vc_col)),
                   pl.BlockSpec((1, N_GROUPS, ATT_QTOK, ATT_KROWS * GRID_W),
                                lambda b, j: (jnp.minimum(j, 1) + j // (n_blk - 1), 0, 0, 0))])
    return pl.pallas_call(
        _nattn_kernel,
        grid=(bsz, n_blk),
        in_specs=in_specs,
        out_specs=pl.BlockSpec((1, ATT_QTOK, w), lambda b, j: (b, j, 0)),
        out_shape=jax.ShapeDtypeStruct((bsz, n, w), BF16),
        compiler_params=_params("arbitrary", "arbitrary"),
        name="neighbourhood_attention",
    )(z, z, z, z, z, z, z, cos_t, sin_t, cos_t, cos_t, cos_t, sin_t, sin_t, sin_t, zc, zc, bias)


def _ctx_attn_kernel(q_ref, kc_ref, vc_ref, o_ref):
    w = q_ref.shape[2]
    lane = lax.broadcasted_iota(jnp.int32, (1, w), 1)
    q = q_ref[0].astype(F32) * (HEAD_DIM ** -0.5)
    kc = kc_ref[0].astype(BF16)
    vc = vc_ref[0].astype(BF16)
    acc = jnp.zeros(q.shape, F32)
    for h in range(N_GROUPS):
        hm = (lane // HEAD_DIM) == h
        s = lax.dot_general(jnp.where(hm, q, 0.0).astype(BF16), kc, _NT, preferred_element_type=F32)
        acc = _softmax_pv([(s, vc)], hm, acc)
    o_ref[0] = acc.astype(o_ref.dtype)


def context_attention(zc, *, q_col, k_col, v_col):
    bsz, n_ctx, _ = zc.shape
    w = BRANCH_WIDTH
    return pl.pallas_call(
        _ctx_attn_kernel,
        grid=(bsz,),
        in_specs=[pl.BlockSpec((1, n_ctx, w), functools.partial(lambda b, c: (b, 0, c), c=c))
                  for c in (q_col, k_col, v_col)],
        out_specs=pl.BlockSpec((1, n_ctx, w), lambda b: (b, 0, 0)),
        out_shape=jax.ShapeDtypeStruct((bsz, n_ctx, w), BF16),
        compiler_params=_params("arbitrary"),
        name="context_attention",
    )(zc, zc, zc)


def _ln(v, g, b):
    mu = jnp.mean(v, axis=-1, keepdims=True)
    var = jnp.mean(jnp.square(v - mu), axis=-1, keepdims=True)
    return (v - mu) * lax.rsqrt(var + EPS) * g + b


def _gmlp_kernel(z_ref, lng_ref, lnb_ref, w_ref, bias_ref, o_ref):
    tm = z_ref.shape[1]
    z = jax.nn.gelu(z_ref[0])
    u = z[:, :BRANCH_WIDTH]
    v = _ln(z[:, BRANCH_WIDTH:], lng_ref[...], lnb_ref[...])
    group = lax.broadcasted_iota(jnp.int32, (1, BRANCH_WIDTH), 1) // HEAD_DIM
    wcat = w_ref[...]
    for c in range(tm // CHUNK):
        rows = slice(c * CHUNK, (c + 1) * CHUNK)
        vc = v[rows]
        vst = jnp.concatenate([jnp.where(group == g, vc, 0.0) for g in range(N_GROUPS)], axis=0).astype(BF16)
        mixed = jnp.dot(wcat, vst, preferred_element_type=F32) + bias_ref[...]
        o_ref[0, rows, :] = (u[rows] * mixed).astype(o_ref.dtype)


def chunk_gmlp(za, ln_g, ln_b, w_s, b_s, *, tm):
    bsz, n, w2 = za.shape
    w = w2 // 2
    wcat = jnp.transpose(w_s, (1, 0, 2)).reshape(CHUNK, N_GROUPS * CHUNK).astype(BF16)
    bias = jnp.repeat(b_s.T, HEAD_DIM, axis=1)
    return pl.pallas_call(
        _gmlp_kernel,
        grid=(bsz, n // tm),
        in_specs=[
            pl.BlockSpec((1, tm, w2), lambda b, i: (b, i, 0)),
            pl.BlockSpec((1, w), lambda b, i: (0, 0)),
            pl.BlockSpec((1, w), lambda b, i: (0, 0)),
            pl.BlockSpec((CHUNK, N_GROUPS * CHUNK), lambda b, i: (0, 0)),
            pl.BlockSpec((CHUNK, w), lambda b, i: (0, 0)),
        ],
        out_specs=pl.BlockSpec((1, tm, w), lambda b, i: (b, i, 0)),
        out_shape=jax.ShapeDtypeStruct((bsz, n, w), BF16),
        compiler_params=_params("arbitrary", "arbitrary"),
        name="chunk_gmlp",
    )(za, ln_g[None, :], ln_b[None, :], wcat, bias)


CONV_HALO = 16
CONV_SUB = 64


def _conv_kernel(a_ref, g_ref, ap_ref, gp_ref, an_ref, gn_ref, w_ref, cb_ref, lng_ref, lnb_ref, o_ref, y_ref):
    i = pl.program_id(1)
    tm = a_ref.shape[1]

    def glu(a, g):
        return a * jax.nn.sigmoid(g)

    y_ref[0:CONV_HALO, :] = jnp.where(i > 0, glu(ap_ref[0], gp_ref[0]), 0.0)
    y_ref[CONV_HALO:CONV_HALO + tm, :] = glu(a_ref[0], g_ref[0])
    y_ref[CONV_HALO + tm:, :] = jnp.where(i < pl.num_programs(1) - 1, glu(an_ref[0], gn_ref[0]), 0.0)
    first = CONV_HALO - CONV_WIDTH // 2
    for r in range(0, tm, CONV_SUB):
        acc = None
        for j in range(CONV_WIDTH):
            t = y_ref[r + first + j:r + first + j + CONV_SUB, :] * w_ref[j:j + 1, :]
            acc = t if acc is None else acc + t
        y = _ln(acc + cb_ref[...], lng_ref[...], lnb_ref[...])
        o_ref[0, r:r + CONV_SUB, :] = jax.nn.silu(y).astype(o_ref.dtype)


def conformer_conv(zcv, conv_w, conv_b, ln_g, ln_b, *, tm):
    bsz, n, w2 = zcv.shape
    w = w2 // 2
    hb = tm // CONV_HALO
    n_hb = n // CONV_HALO

    def main(col):
        return pl.BlockSpec((1, tm, w), lambda b, i: (b, i, col))

    def prev(col):
        return pl.BlockSpec((1, CONV_HALO, w), lambda b, i: (b, jnp.maximum(i * hb - 1, 0), col))

    def nxt(col):
        return pl.BlockSpec((1, CONV_HALO, w), lambda b, i: (b, jnp.minimum((i + 1) * hb, n_hb - 1), col))

    vec = pl.BlockSpec((1, w), lambda b, i: (0, 0))
    return pl.pallas_call(
        _conv_kernel,
        grid=(bsz, n // tm),
        in_specs=[main(0), main(1), prev(0), prev(1), nxt(0), nxt(1),
                  pl.BlockSpec((CONV_WIDTH, w), lambda b, i: (0, 0)), vec, vec, vec],
        out_specs=pl.BlockSpec((1, tm, w), lambda b, i: (b, i, 0)),
        out_shape=jax.ShapeDtypeStruct((bsz, n, w), BF16),
        scratch_shapes=[pltpu.VMEM((tm + 2 * CONV_HALO, w), F32)],
        compiler_params=_params("arbitrary", "arbitrary"),
        name="conformer_conv",
    )(zcv, zcv, zcv, zcv, zcv, zcv, conv_w, conv_b[None, :], ln_g[None, :], ln_b[None, :])


FOURIER_N2 = 128
FOURIER_FB = 8
FOURIER_KB = 8


def _np_split(m):
    m = jnp.asarray(m, F32)
    hi = m.astype(BF16)
    return hi, (m - hi.astype(F32)).astype(BF16)


def _channel_dft_matrix():
    c = np.arange(HEAD_DIM)
    ang = 2.0 * np.pi * ((c[:, None] * c[None, :]) % HEAD_DIM) / HEAD_DIM
    eye = np.eye(N_GROUPS)
    return np.concatenate([np.kron(eye, np.cos(ang)), np.kron(eye, np.sin(ang))], axis=0)


def _fourier_stage1_kernel(x_ref, mh_ref, ml_ref, o_ref):
    w = BRANCH_WIDTH
    for f in range(mh_ref.shape[0]):
        x_hi, x_lo = _split_bf16(x_ref[0, :, f, :])
        o_ref[0, :, f, :] = _dot3(mh_ref[f], ml_ref[f], x_hi, x_lo)


def _fourier_stage2_kernel(br_ref, bi_ref, m2h_ref, m2l_ref, mdh_ref, mdl_ref, o_ref, *, scale):
    w = BRANCH_WIDTH
    for k in range(br_ref.shape[1]):
        b_hi, b_lo = _split_bf16(jnp.concatenate([br_ref[0, k], bi_ref[0, k]], axis=0))
        xs = _dot3(m2h_ref[...], m2l_ref[...], b_hi, b_lo)
        x_hi, x_lo = _split_bf16(jnp.concatenate([xs[:FOURIER_N2], xs[FOURIER_N2:]], axis=1))
        o_ref[0, :, k, :] = _dot3(x_hi, x_lo, mdh_ref[...], mdl_ref[...]) * scale


def _fourier_direct_kernel(x_ref, mph_ref, mpl_ref, mdh_ref, mdl_ref, o_ref, *, scale):
    n = x_ref.shape[1]
    x_hi, x_lo = _split_bf16(x_ref[0])
    p = _dot3(mph_ref[...], mpl_ref[...], x_hi, x_lo)
    p_hi, p_lo = _split_bf16(jnp.concatenate([p[:n], p[n:]], axis=1))
    o_ref[0] = (_dot3(p_hi, p_lo, mdh_ref[...], mdl_ref[...]) * scale).astype(o_ref.dtype)


def fourier_mix(zb):
    bsz, n, w = zb.shape
    scale = float(1.0 / np.sqrt(n * HEAD_DIM))
    mdh, mdl = _np_split(_channel_dft_matrix())
    md_spec2 = pl.BlockSpec((2 * w, w), lambda b, j: (0, 0))
    if n <= 2 * FOURIER_N2:
        t = np.arange(n)
        ang = 2.0 * np.pi * ((t[:, None] * t[None, :]) % n) / n
        mph, mpl = _np_split(np.concatenate([np.cos(ang), -np.sin(ang)], axis=0))
        return pl.pallas_call(
            functools.partial(_fourier_direct_kernel, scale=scale),
            grid=(bsz,),
            in_specs=[pl.BlockSpec((1, n, w), lambda b: (b, 0, 0)),
                      pl.BlockSpec((2 * n, n), lambda b: (0, 0)), pl.BlockSpec((2 * n, n), lambda b: (0, 0)),
                      pl.BlockSpec((2 * w, w), lambda b: (0, 0)), pl.BlockSpec((2 * w, w), lambda b: (0, 0))],
            out_specs=pl.BlockSpec((1, n, w), lambda b: (b, 0, 0)),
            out_shape=jax.ShapeDtypeStruct((bsz, n, w), BF16),
            compiler_params=_params("arbitrary"),
            name="fourier_direct",
        )(zb, mph, mpl, mdh, mdl)

    n1, n2 = n // FOURIER_N2, FOURIER_N2
    f, k1, s = np.arange(n2), np.arange(n1), np.arange(n1)
    ang1 = 2.0 * np.pi * ((k1[None, :, None] * (f[:, None, None] + n2 * s[None, None, :])) % n) / n
    m1h, m1l = _np_split(np.concatenate([np.cos(ang1), -np.sin(ang1)], axis=1))
    k2 = np.arange(n2)
    ang2 = 2.0 * np.pi * ((k2[:, None] * f[None, :]) % n2) / n2
    c2, s2 = np.cos(ang2), np.sin(ang2)
    m2h, m2l = _np_split(np.block([[c2, s2], [-s2, c2]]))

    b_st = pl.pallas_call(
        _fourier_stage1_kernel,
        grid=(bsz, n2 // FOURIER_FB),
        in_specs=[pl.BlockSpec((1, n1, FOURIER_FB, w), lambda b, j: (b, 0, j, 0)),
                  pl.BlockSpec((FOURIER_FB, 2 * n1, n1), lambda b, j: (j, 0, 0)),
                  pl.BlockSpec((FOURIER_FB, 2 * n1, n1), lambda b, j: (j, 0, 0))],
        out_specs=pl.BlockSpec((1, 2 * n1, FOURIER_FB, w), lambda b, j: (b, 0, j, 0)),
        out_shape=jax.ShapeDtypeStruct((bsz, 2 * n1, n2, w), F32),
        compiler_params=_params("arbitrary", "arbitrary"),
        name="fourier_stage1",
    )(zb.reshape(bsz, n1, n2, w), m1h, m1l)

    kb = FOURIER_KB
    out = pl.pallas_call(
        functools.partial(_fourier_stage2_kernel, scale=scale),
        grid=(bsz, n1 // kb),
        in_specs=[pl.BlockSpec((1, kb, n2, w), lambda b, j: (b, j, 0, 0)),
                  pl.BlockSpec((1, kb, n2, w), lambda b, j: (b, n1 // kb + j, 0, 0)),
                  pl.BlockSpec((2 * n2, 2 * n2), lambda b, j: (0, 0)),
                  pl.BlockSpec((2 * n2, 2 * n2), lambda b, j: (0, 0)),
                  md_spec2, md_spec2],
        out_specs=pl.BlockSpec((1, n2, kb, w), lambda b, j: (b, 0, j, 0)),
        out_shape=jax.ShapeDtypeStruct((bsz, n2, n1, w), F32),
        compiler_params=_params("arbitrary", "arbitrary"),
        name="fourier_stage2",
    )(b_st, b_st, m2h, m2l, mdh, mdl)
    return out.reshape(bsz, n, w)


def expert_route(x, g, sc, sh, router_t, *, tm):
    n = x.shape[1]
    cap = EC_CAPACITY * n // N_EXPERTS
    h, aff_t = norm_router(x, g, sc, sh, router_t, tm=tm)
    slot, off = expert_topk(aff_t, cap)
    xe, gate = expert_gather(h, slot, aff_t, off, max(cap, ROUTE_WIN))
    return xe, gate, jnp.swapaxes(slot, 1, 2), off


def kernel(x, c, ctx, c_ctx, w_mod, b_mod, norm1_g, norm2_g, w_in, sgu_ln_g, sgu_ln_b, w_spatial, b_spatial,
           w_a_out, w_b_out, conv_w, conv_b, conv_ln_g, conv_ln_b, w_c_out, rpb, w_d_out, w_out, w_router,
           w_gate_e, w_up_e, w_down_e, final_norm_g):
    bsz = x.shape[0]
    xc = ctx
    assert bsz + 1 <= MOD_ROWS
    cond = jnp.concatenate([c, c_ctx[None, :], jnp.zeros((MOD_ROWS - bsz - 1, D_MODEL), F32)], axis=0)
    mod_all = modulation(cond, w_mod, b_mod)
    for l in range(DEPTH):
        last = l == DEPTH - 1
        sh1, sc1, gt1, sh2, sc2, gt2 = jnp.split(mod_all[l, :bsz, None, :], 6, axis=-1)
        cmod = jnp.broadcast_to(mod_all[l, bsz:bsz + 1, None, :], (bsz, 1, 6 * D_MODEL))
        csh1, csc1, cgt1, csh2, csc2, cgt2 = jnp.split(cmod, 6, axis=-1)

        w_in_b = w_in[l].astype(BF16)
        wa, wb, wc, wd_, wo = (w.astype(BF16) for w in (w_a_out[l], w_b_out[l], w_c_out[l], w_d_out[l], w_out[l]))
        g1 = norm1_g[l][None, :]

        za, zb, zcv, zqkv, zg = norm_inproj(x, g1, sc1, sh1, w_in_b, IN_WIDTHS, IN_DTYPES, tm=512)
        if last:
            (zqkv_c,) = norm_inproj(xc, g1, csc1, csh1, w_in_b[:, Q_END:V_END], (2 * BRANCH_WIDTH,), (BF16,),
                                    tm=CTX_TM)
            kc_col, vc_col = 0, 1
        else:
            cza, czb, czcv, zqkv_c, czg = norm_inproj(xc, g1, csc1, csh1, w_in_b, IN_WIDTHS, IN_DTYPES, tm=CTX_TM)
            kc_col, vc_col = 1, 2
        d_lat = neighbourhood_attention(zqkv, zqkv_c, rpb[l], q_col=0, k_col=1, v_col=2,
                                        kc_col=kc_col, vc_col=vc_col)
        a_lat = chunk_gmlp(za, sgu_ln_g[l], sgu_ln_b[l], w_spatial[l], b_spatial[l], tm=512)
        b_lat = fourier_mix(zb)
        c_lat = conformer_conv(zcv, conv_w[l], conv_b[l], conv_ln_g[l], conv_ln_b[l], tm=512)
        x = merge_branches(a_lat, b_lat, c_lat, d_lat, zg, wa, wb, wc, wd_, wo, x, gt1, tm=512)
        if not last:
            a_c = chunk_gmlp(cza, sgu_ln_g[l], sgu_ln_b[l], w_spatial[l], b_spatial[l], tm=CTX_TM)
            b_c = fourier_mix(czb)
            c_c = conformer_conv(czcv, conv_w[l], conv_b[l], conv_ln_g[l], conv_ln_b[l], tm=CTX_TM)
            d_c = context_attention(zqkv_c, q_col=0, k_col=1, v_col=2)
            xc = merge_branches(a_c, b_c, c_c, d_c, czg, wa, wb, wc, wd_, wo, xc, cgt1, tm=CTX_TM)

        g2 = norm2_g[l][None, :]
        router_t = w_router[l].T
        xe, gate, slot_t, off = expert_route(x, g2, sc2, sh2, router_t, tm=512)
        if last:
            (y,) = expert_ffn([xe], [gate], w_gate_e[l], w_up_e[l], w_down_e[l])
        else:
            xe_c, gate_c, slot_tc, off_c = expert_route(xc, g2, csc2, csh2, router_t, tm=CTX_TM)
            y, y_c = expert_ffn([xe, xe_c], [gate, gate_c], w_gate_e[l], w_up_e[l], w_down_e[l])
            xc = expert_combine(y_c, slot_tc, off_c, xc, cgt2)
        x = expert_combine(y, slot_t, off, x, gt2, final_norm_g if last else None)
    return x


CTX_TM = 256
```

```python
import functools

import jax
import jax.numpy as jnp
import numpy as np
from jax import lax
from jax.experimental import pallas as pl
from jax.experimental.pallas import tpu as pltpu

D_MODEL = 1024
DEPTH = 2
GRID_W = 64
HEAD_DIM = 64
N_GROUPS = 4
BRANCH_WIDTH = N_GROUPS * HEAD_DIM
N_BRANCH = 4
CHUNK = 128
CONV_WIDTH = 31
WIN_ROWS = 8
WIN_COLS = 16
QB_COLS = 16
KB_COLS = QB_COLS + WIN_COLS
ROPE_BASE = 10000.0
N_EXPERTS = 16
EXPERT_FF = 1024
EC_CAPACITY = 2
EPS = 1e-6
NEG_INF = -1e30
A_END = 2 * BRANCH_WIDTH
B_END = A_END + BRANCH_WIDTH
C_END = B_END + 2 * BRANCH_WIDTH
Q_END = C_END + BRANCH_WIDTH
K_END = Q_END + BRANCH_WIDTH
V_END = K_END + BRANCH_WIDTH
IN_COLS = V_END + N_BRANCH * D_MODEL
SPLITS = [A_END, B_END, C_END, Q_END, K_END, V_END]
IN_WIDTHS = (A_END, B_END - A_END, C_END - B_END, V_END - C_END, IN_COLS - V_END)
IN_DTYPES = (jnp.float32, jnp.float32, jnp.float32, jnp.bfloat16, jnp.bfloat16)

VMEM_LIMIT_BYTES = 56 * 1024 * 1024
F32 = jnp.float32
BF16 = jnp.bfloat16


def _params(*sem):
    return pltpu.CompilerParams(dimension_semantics=sem, vmem_limit_bytes=VMEM_LIMIT_BYTES)


MOD_ROWS = 8
MOD_TN = 1536


def _split_bf16(v):
    hi = v.astype(BF16)
    return hi, (v - hi.astype(F32)).astype(BF16)


def _dot3(a_hi, a_lo, b_hi, b_lo):
    return (jnp.dot(a_hi, b_hi, preferred_element_type=F32) + jnp.dot(a_hi, b_lo, preferred_element_type=F32)
            + jnp.dot(a_lo, b_hi, preferred_element_type=F32))


def _mod_kernel(c_ref, w_ref, b_ref, o_ref):
    s_hi, s_lo = _split_bf16(jax.nn.silu(c_ref[...]))
    w_hi, w_lo = _split_bf16(w_ref[0])
    o_ref[0] = _dot3(s_hi, s_lo, w_hi, w_lo) + b_ref[0]


def modulation(cond, w_mod, b_mod):
    n_layers, d, cols = w_mod.shape
    return pl.pallas_call(
        _mod_kernel,
        grid=(n_layers, cols // MOD_TN),
        in_specs=[pl.BlockSpec((MOD_ROWS, d), lambda l, j: (0, 0)),
                  pl.BlockSpec((1, d, MOD_TN), lambda l, j: (l, 0, j)),
                  pl.BlockSpec((1, 1, MOD_TN), lambda l, j: (l, 0, j))],
        out_specs=pl.BlockSpec((1, MOD_ROWS, MOD_TN), lambda l, j: (l, 0, j)),
        out_shape=jax.ShapeDtypeStruct((n_layers, MOD_ROWS, cols), F32),
        compiler_params=_params("arbitrary", "arbitrary"),
        name="modulation",
    )(cond, w_mod, b_mod[:, None, :])


INPROJ_COL_CHUNK = 512


def _norm_inproj_kernel(x_ref, g_ref, sc_ref, sh_ref, w_ref, *o_refs):
    x = x_ref[0]
    y = x * lax.rsqrt(jnp.mean(x * x, axis=-1, keepdims=True) + EPS) * g_ref[...]
    h = (y * (1.0 + sc_ref[0]) + sh_ref[0]).astype(BF16)
    off = 0
    for o_ref in o_refs:
        width = o_ref.shape[2]
        for c0 in range(0, width, INPROJ_COL_CHUNK):
            cw = min(INPROJ_COL_CHUNK, width - c0)
            o_ref[0, :, c0:c0 + cw] = jnp.dot(h, w_ref[:, off + c0:off + c0 + cw],
                                              preferred_element_type=F32).astype(o_ref.dtype)
        off += width


def norm_inproj(x, g, sc, sh, w, l, col0, widths, dtypes, *, tm):
    bsz, n, d = x.shape
    cols = sum(widths)
    assert col0 % cols == 0
    return pl.pallas_call(
        _norm_inproj_kernel,
        grid=(bsz, n // tm),
        in_specs=[
            pl.BlockSpec((1, tm, d), lambda b, i: (b, i, 0)),
            pl.BlockSpec((1, d), lambda b, i: (0, 0)),
            pl.BlockSpec((1, 1, d), lambda b, i: (b, 0, 0)),
            pl.BlockSpec((1, 1, d), lambda b, i: (b, 0, 0)),
            pl.BlockSpec((None, d, cols), lambda b, i: (l, 0, col0 // cols), pipeline_mode=pl.Buffered(1)),
        ],
        out_specs=[pl.BlockSpec((1, tm, wd), lambda b, i: (b, i, 0)) for wd in widths],
        out_shape=[jax.ShapeDtypeStruct((bsz, n, wd), dt) for wd, dt in zip(widths, dtypes, strict=True)],
        compiler_params=_params("arbitrary", "arbitrary"),
        name="norm_inproj",
    )(x, g, sc, sh, w)


def _merge_kernel(a_ref, b_ref, c_ref, d_ref, ga_ref, gb_ref, gc_ref, gd_ref,
                  wa_ref, wb_ref, wc_ref, wd_ref, wo_ref, x_ref, gt_ref, o_ref):
    m = None
    for br, gz, w in ((a_ref, ga_ref, wa_ref), (b_ref, gb_ref, wb_ref),
                      (c_ref, gc_ref, wc_ref), (d_ref, gd_ref, wd_ref)):
        p = jnp.dot(br[0].astype(BF16), w[...], preferred_element_type=F32)
        t = jax.nn.sigmoid(gz[0].astype(F32)) * p
        m = t if m is None else m + t
    mix = jnp.dot(m.astype(BF16), wo_ref[...], preferred_element_type=F32)
    o_ref[0] = x_ref[0] + gt_ref[0] * mix


def merge_branches(a, b, cc, d, z, w_a, w_b, w_c, w_d, w_o, x, gt, *, tm):
    bsz, n, dm = x.shape
    w = a.shape[-1]
    br_spec = pl.BlockSpec((1, tm, w), lambda bi, i: (bi, i, 0))
    gate_specs = [pl.BlockSpec((1, tm, dm), functools.partial(lambda bi, i, k: (bi, i, k), k=k))
                  for k in range(N_BRANCH)]
    wbr_spec = pl.BlockSpec((w, dm), lambda bi, i: (0, 0))
    return pl.pallas_call(
        _merge_kernel,
        grid=(bsz, n // tm),
        in_specs=[br_spec] * 4 + gate_specs + [wbr_spec] * 4 + [
            pl.BlockSpec((dm, dm), lambda bi, i: (0, 0)),
            pl.BlockSpec((1, tm, dm), lambda bi, i: (bi, i, 0)),
            pl.BlockSpec((1, 1, dm), lambda bi, i: (bi, 0, 0)),
        ],
        out_specs=pl.BlockSpec((1, tm, dm), lambda bi, i: (bi, i, 0)),
        out_shape=jax.ShapeDtypeStruct((bsz, n, dm), F32),
        compiler_params=_params("arbitrary", "arbitrary"),
        name="merge_branches",
    )(a, b, cc, d, z, z, z, z, w_a, w_b, w_c, w_d, w_o, x, gt)


ROUTE_TB = 256
ROUTE_WIN = 64
ROUTE_ALIGN = 16
OFF_LANES = 128
TOPK_EXP_STEPS = (64, 32, 16, 8, 4, 2, 1)
TOPK_BISECT_STEPS = 56


def _norm_router_kernel(x_ref, g_ref, sc_ref, sh_ref, rt_ref, h_ref, aff_ref):
    x = x_ref[0]
    y = x * lax.rsqrt(jnp.mean(x * x, axis=-1, keepdims=True) + EPS) * g_ref[...]
    h = y * (1.0 + sc_ref[0]) + sh_ref[0]
    h_hi, h_lo = _split_bf16(h)
    r_hi, r_lo = _split_bf16(rt_ref[...])
    logits = (lax.dot_general(r_hi, h_hi, _NT, preferred_element_type=F32)
              + lax.dot_general(r_hi, h_lo, _NT, preferred_element_type=F32)
              + lax.dot_general(r_lo, h_hi, _NT, preferred_element_type=F32))
    p = jnp.exp(logits - jnp.max(logits, axis=0, keepdims=True))
    aff_ref[0] = p / jnp.sum(p, axis=0, keepdims=True)
    h_ref[0] = h_hi


def norm_router(x, g, sc, sh, router_t, *, tm):
    bsz, n, d = x.shape
    ne = router_t.shape[0]
    return pl.pallas_call(
        _norm_router_kernel,
        grid=(bsz, n // tm),
        in_specs=[
            pl.BlockSpec((1, tm, d), lambda b, i: (b, i, 0)),
            pl.BlockSpec((1, d), lambda b, i: (0, 0)),
            pl.BlockSpec((1, 1, d), lambda b, i: (b, 0, 0)),
            pl.BlockSpec((1, 1, d), lambda b, i: (b, 0, 0)),
            pl.BlockSpec((ne, d), lambda b, i: (0, 0)),
        ],
        out_specs=[pl.BlockSpec((1, tm, d), lambda b, i: (b, i, 0)),
                   pl.BlockSpec((1, ne, tm), lambda b, i: (b, 0, i))],
        out_shape=[jax.ShapeDtypeStruct((bsz, n, d), BF16), jax.ShapeDtypeStruct((bsz, ne, n), F32)],
        compiler_params=_params("arbitrary", "arbitrary"),
        name="norm_router",
    )(x, g, sc, sh, router_t)


def _topk_kernel(aff_ref, slot_ref, off_ref, *, cap):
    a = aff_ref[0]
    ne, n = a.shape
    capf = jnp.float32(cap)

    def count_ge(t):
        return jnp.sum(jnp.where(a >= t, 1.0, 0.0), axis=1, keepdims=True)

    hi = jnp.full((ne, 1), 2.0, F32)
    for s in TOPK_EXP_STEPS:
        cand = hi * (2.0 ** -s)
        hi = jnp.where(count_ge(cand) < capf, cand, hi)
    lo = jnp.where(hi <= 2.0 ** -126, 0.0, hi * 0.5)

    def bisect(_, carry):
        lo, hi = carry
        mid = 0.5 * (lo + hi)
        ok = count_ge(mid) >= capf
        return jnp.where(ok, mid, lo), jnp.where(ok, hi, mid)

    lo, hi = lax.fori_loop(0, TOPK_BISECT_STEPS, bisect, (lo, hi))
    need = capf - count_ge(hi)
    r_i = lax.broadcasted_iota(jnp.int32, (ROUTE_TB, ROUTE_TB), 0)
    c_i = lax.broadcasted_iota(jnp.int32, (ROUTE_TB, ROUTE_TB), 1)
    tri = jnp.where(r_i < c_i, 1.0, 0.0).astype(BF16)
    lane = lax.broadcasted_iota(jnp.int32, (ne, OFF_LANES), 1)
    run_eq = jnp.zeros((ne, 1), F32)
    run_sel = jnp.zeros((ne, 1), F32)
    offs = jnp.zeros((ne, OFF_LANES), F32)
    for c in range(n // ROUTE_TB):
        cols = slice(c * ROUTE_TB, (c + 1) * ROUTE_TB)
        a_c = a[:, cols]
        above = a_c >= hi
        eq_c = jnp.where(above, 0.0, jnp.where(a_c >= lo, 1.0, 0.0))
        rank = jnp.dot(eq_c.astype(BF16), tri, preferred_element_type=F32) + run_eq
        sel = jnp.where(above, 1.0, jnp.where(rank < need, eq_c, 0.0))
        pos = jnp.dot(sel.astype(BF16), tri, preferred_element_type=F32) + run_sel
        slot_ref[0, :, cols] = jnp.where(sel > 0.0, pos, -1.0).astype(jnp.int32)
        offs = jnp.where(lane == c, run_sel, offs)
        run_eq = run_eq + jnp.sum(eq_c, axis=1, keepdims=True)
        run_sel = run_sel + jnp.sum(sel, axis=1, keepdims=True)
    offs = jnp.where(lane == n // ROUTE_TB, run_sel, offs)
    off_ref[0] = offs.astype(jnp.int32)


def expert_topk(aff_t, cap):
    bsz, ne, n = aff_t.shape
    return pl.pallas_call(
        functools.partial(_topk_kernel, cap=cap),
        grid=(bsz,),
        in_specs=[pl.BlockSpec((1, ne, n), lambda b: (b, 0, 0))],
        out_specs=[pl.BlockSpec((1, ne, n), lambda b: (b, 0, 0)),
                   pl.BlockSpec((1, ne, OFF_LANES), lambda b: (b, 0, 0))],
        out_shape=[jax.ShapeDtypeStruct((bsz, ne, n), jnp.int32),
                   jax.ShapeDtypeStruct((bsz, ne, OFF_LANES), jnp.int32)],
        compiler_params=_params("arbitrary"),
        name="expert_topk",
    )(aff_t)


def _window_plan(off_ref, b, e, tb, w, rows_max):
    start = off_ref[b, e, tb]
    stop = off_ref[b, e, tb + 1]
    first = (start // ROUTE_ALIGN) * ROUTE_ALIGN + w * ROUTE_WIN
    active = first < stop
    return pl.multiple_of(jnp.minimum(first, rows_max), ROUTE_ALIGN), first, active


def _num_passes(off_ref, b, e_list, tb):
    n_pass = jnp.int32(0)
    for e in e_list:
        start = off_ref[b, e, tb]
        stop = off_ref[b, e, tb + 1]
        base = (start // ROUTE_ALIGN) * ROUTE_ALIGN
        n_pass = jnp.maximum(n_pass, (stop - base + ROUTE_WIN - 1) // ROUTE_WIN)
    return n_pass


def _gather_kernel(off_ref, h_ref, slot_ref, aff_ref, xe_ref, gate_ref, *, group):
    b, g, tb = pl.program_id(0), pl.program_id(1), pl.program_id(2)
    rows_max = xe_ref.shape[2] - ROUTE_WIN

    @pl.when(tb == 0)
    def _():
        xe_ref[...] = jnp.zeros(xe_ref.shape, xe_ref.dtype)
        gate_ref[...] = jnp.zeros(gate_ref.shape, gate_ref.dtype)

    sub = lax.broadcasted_iota(jnp.int32, (ROUTE_WIN, 1), 0)
    experts = [g * group + e for e in range(group)]

    def one_pass(w, carry):
        plans = [_window_plan(off_ref, b, ge, tb, w, rows_max) for ge in experts]
        hots = []
        for e, (rows, first, active) in enumerate(plans):
            ids = rows + sub
            want = jnp.where(active & (ids >= first), ids, -2)
            hots.append(slot_ref[0, e:e + 1, :] == want)
        onehot = jnp.concatenate([jnp.where(hm, 1.0, 0.0) for hm in hots], axis=0).astype(BF16)
        res = jnp.dot(onehot, h_ref[0], preferred_element_type=F32)
        for e, (rows, _, _) in enumerate(plans):
            win = pl.ds(rows, ROUTE_WIN)
            xe_ref[0, e, win, :] = xe_ref[0, e, win, :] + res[e * ROUTE_WIN:(e + 1) * ROUTE_WIN].astype(BF16)
            gsel = jnp.sum(jnp.where(hots[e], aff_ref[0, e:e + 1, :], 0.0), axis=1, keepdims=True)
            gate_ref[0, e, win, :] = gate_ref[0, e, win, :] + gsel
        return carry

    lax.fori_loop(0, _num_passes(off_ref, b, experts, tb), one_pass, 0)


def expert_gather(h, slot, aff_t, off, cap_pad, *, group=8):
    bsz, n, d = h.shape
    ne = slot.shape[1]
    grid_spec = pltpu.PrefetchScalarGridSpec(
        num_scalar_prefetch=1,
        grid=(bsz, ne // group, n // ROUTE_TB),
        in_specs=[
            pl.BlockSpec((1, ROUTE_TB, d), lambda b, g, t, off: (b, t, 0)),
            pl.BlockSpec((1, group, ROUTE_TB), lambda b, g, t, off: (b, g, t)),
            pl.BlockSpec((1, group, ROUTE_TB), lambda b, g, t, off: (b, g, t)),
        ],
        out_specs=[pl.BlockSpec((1, group, cap_pad, d), lambda b, g, t, off: (b, g, 0, 0)),
                   pl.BlockSpec((1, group, cap_pad, 1), lambda b, g, t, off: (b, g, 0, 0))],
    )
    return pl.pallas_call(
        functools.partial(_gather_kernel, group=group),
        grid_spec=grid_spec,
        out_shape=[jax.ShapeDtypeStruct((bsz, ne, cap_pad, d), BF16),
                   jax.ShapeDtypeStruct((bsz, ne, cap_pad, 1), F32)],
        compiler_params=_params("arbitrary", "arbitrary", "arbitrary"),
        name="expert_gather",
    )(off, h, slot, aff_t)


def _combine_kernel(off_ref, y_ref, slot_t_ref, x_ref, gt_ref, fg_ref, o_ref, stage_ref, acc_ref):
    b, tb = pl.program_id(0), pl.program_id(1)
    ne = y_ref.shape[1]
    rows_max = y_ref.shape[2] - ROUTE_WIN
    width = ne * ROUTE_WIN
    experts = list(range(ne))
    lane = lax.broadcasted_iota(jnp.int32, (1, width), 1)
    s1 = slot_t_ref[0] + 1
    hi = (s1 // ROUTE_WIN).astype(F32).astype(BF16)
    lo = (s1 % ROUTE_WIN).astype(F32).astype(BF16)
    e_row = lax.broadcasted_iota(jnp.int32, (ne, width), 0)
    e_lane = lax.broadcasted_iota(jnp.int32, (ne, width), 1) // ROUTE_WIN
    expand = jnp.where(e_row == e_lane, 1.0, 0.0).astype(BF16)
    s1_wide = (float(ROUTE_WIN) * jnp.dot(hi, expand, preferred_element_type=F32)
               + jnp.dot(lo, expand, preferred_element_type=F32))
    acc_ref[...] = jnp.zeros(acc_ref.shape, F32)

    def one_pass(w, carry):
        want = jnp.full((1, width), -1, jnp.int32)
        for e in experts:
            rows, first, active = _window_plan(off_ref, b, e, tb, w, rows_max)
            stage_ref[e * ROUTE_WIN:(e + 1) * ROUTE_WIN, :] = y_ref[0, e, pl.ds(rows, ROUTE_WIN), :]
            ids = rows + lane % ROUTE_WIN
            want = jnp.where((lane // ROUTE_WIN == e) & active & (ids >= first), ids + 1, want)
        onehot = jnp.where(s1_wide == want.astype(F32), 1.0, 0.0).astype(BF16)
        acc_ref[...] += jnp.dot(onehot, stage_ref[...], preferred_element_type=F32)
        return carry

    lax.fori_loop(0, _num_passes(off_ref, b, experts, tb), one_pass, 0)
    out = x_ref[0] + gt_ref[0] * acc_ref[...]
    if fg_ref is not None:
        out = out * lax.rsqrt(jnp.mean(out * out, axis=-1, keepdims=True) + EPS) * fg_ref[...]
    o_ref[0] = out


def _combine_kernel_plain(off_ref, y_ref, slot_t_ref, x_ref, gt_ref, o_ref, stage_ref, acc_ref):
    _combine_kernel(off_ref, y_ref, slot_t_ref, x_ref, gt_ref, None, o_ref, stage_ref, acc_ref)


def expert_combine(y, slot_t, off, x, gt, final_g=None):
    bsz, n, d = x.shape
    ne, cap_pad = y.shape[1], y.shape[2]
    grid_spec = pltpu.PrefetchScalarGridSpec(
        num_scalar_prefetch=1,
        grid=(bsz, n // ROUTE_TB),
        in_specs=[
            pl.BlockSpec((1, ne, cap_pad, d), lambda b, t, off: (b, 0, 0, 0), pipeline_mode=pl.Buffered(1)),
            pl.BlockSpec((1, ROUTE_TB, ne), lambda b, t, off: (b, t, 0)),
            pl.BlockSpec((1, ROUTE_TB, d), lambda b, t, off: (b, t, 0)),
            pl.BlockSpec((1, 1, d), lambda b, t, off: (b, 0, 0)),
        ] + ([] if final_g is None else [pl.BlockSpec((1, d), lambda b, t, off: (0, 0))]),
        out_specs=pl.BlockSpec((1, ROUTE_TB, d), lambda b, t, off: (b, t, 0)),
        scratch_shapes=[pltpu.VMEM((ne * ROUTE_WIN, d), BF16), pltpu.VMEM((ROUTE_TB, d), F32)],
    )
    args = (off, y, slot_t, x, gt) + (() if final_g is None else (final_g[None, :],))
    return pl.pallas_call(
        _combine_kernel_plain if final_g is None else _combine_kernel,
        grid_spec=grid_spec,
        out_shape=jax.ShapeDtypeStruct((bsz, n, d), F32),
        compiler_params=_params("arbitrary", "arbitrary"),
        name="expert_combine",
    )(*args)


FFN_FBLK = 512


def _expert_ffn_kernel(*refs, n_sets):
    xe_refs, gate_refs = refs[:n_sets], refs[n_sets:2 * n_sets]
    wg_ref, wu_ref, wd_ref = refs[2 * n_sets:2 * n_sets + 3]
    o_refs = refs[2 * n_sets + 3:3 * n_sets + 3]
    acc_refs = refs[3 * n_sets + 3:]
    f = pl.program_id(1)
    wg = wg_ref[0].astype(BF16)
    wu = wu_ref[0].astype(BF16)
    wd = wd_ref[0].astype(BF16)
    for s in range(n_sets):
        for b in range(xe_refs[s].shape[0]):
            xe = xe_refs[s][b, 0]
            a = jnp.dot(xe, wg, preferred_element_type=F32)
            u = jnp.dot(xe, wu, preferred_element_type=F32)
            t = jnp.dot((jax.nn.silu(a) * u).astype(BF16), wd, preferred_element_type=F32)

            @pl.when(f == 0)
            def _():
                acc_refs[s][b] = t

            @pl.when(f > 0)
            def _():
                acc_refs[s][b] += t

    @pl.when(f == pl.num_programs(1) - 1)
    def _():
        for s in range(n_sets):
            for b in range(xe_refs[s].shape[0]):
                o_refs[s][b, 0] = (acc_refs[s][b] * gate_refs[s][b, 0]).astype(o_refs[s].dtype)


def expert_ffn(xes, gates, wg, wu, wd, l):
    n_sets = len(xes)
    _, ne, d, ff = wg.shape
    xe_specs = [pl.BlockSpec((xe.shape[0], 1) + xe.shape[2:], lambda e, f: (0, e, 0, 0)) for xe in xes]
    gate_specs = [pl.BlockSpec((g.shape[0], 1) + g.shape[2:], lambda e, f: (0, e, 0, 0)) for g in gates]
    return pl.pallas_call(
        functools.partial(_expert_ffn_kernel, n_sets=n_sets),
        grid=(ne, ff // FFN_FBLK),
        in_specs=xe_specs + gate_specs + [
            pl.BlockSpec((None, 1, d, FFN_FBLK), lambda e, f: (l, e, 0, f)),
            pl.BlockSpec((None, 1, d, FFN_FBLK), lambda e, f: (l, e, 0, f)),
            pl.BlockSpec((None, 1, FFN_FBLK, d), lambda e, f: (l, e, f, 0)),
        ],
        out_specs=xe_specs,
        out_shape=[jax.ShapeDtypeStruct(xe.shape, BF16) for xe in xes],
        scratch_shapes=[pltpu.VMEM((xe.shape[0],) + xe.shape[2:], F32) for xe in xes],
        compiler_params=_params("arbitrary", "arbitrary"),
        name="expert_ffn",
    )(*xes, *gates, wg, wu, wd)


ATT_QROWS = 4
ATT_QTOK = ATT_QROWS * GRID_W
ATT_KROWS = ATT_QROWS + WIN_ROWS
ATT_KBLK = ATT_KROWS * GRID_W // ATT_QTOK
_NT = (((1,), (1,)), ((), ()))


def _rope_tables(n):
    quarter = HEAD_DIM // 4
    inv = ROPE_BASE ** (-np.arange(quarter, dtype=np.float64) / quarter)
    t = np.arange(n)
    ang_r = (t // GRID_W)[:, None] * inv[None, :]
    ang_c = (t % GRID_W)[:, None] * inv[None, :]
    cos = np.concatenate([np.cos(ang_r)] * 2 + [np.cos(ang_c)] * 2, axis=1)
    sin = np.concatenate([-np.sin(ang_r), np.sin(ang_r), -np.sin(ang_c), np.sin(ang_c)], axis=1)
    return (jnp.asarray(np.concatenate([cos, cos], axis=1), F32),
            jnp.asarray(np.concatenate([sin, sin], axis=1), F32))


def _attn_bias_tables(rpb, n_rows):
    n_blk = n_rows // ATT_QROWS
    n_dr, n_dc = 2 * WIN_ROWS - 1, 2 * WIN_COLS - 1
    qc = np.arange(GRID_W)
    kc = np.arange(GRID_W)
    c_start = np.clip(qc - WIN_COLS // 2, 0, GRID_W - WIN_COLS)
    col_ok = (kc[None, :] >= c_start[:, None]) & (kc[None, :] < c_start[:, None] + WIN_COLS)
    d_col = np.clip(kc[None, :] - qc[:, None], 1 - WIN_COLS, WIN_COLS - 1) + (WIN_COLS - 1)
    col_sel = (d_col.reshape(-1)[None, :] == np.arange(n_dc)[:, None]).astype(np.float32)
    row_sel, oks = [], []
    for j in (0, 1, n_blk - 1):
        ks = min(max(ATT_QROWS * j - WIN_ROWS // 2, 0), n_rows - ATT_KROWS)
        r = ATT_QROWS * j + np.arange(ATT_QROWS)
        kr0 = np.clip(r - WIN_ROWS // 2, 0, n_rows - WIN_ROWS)
        krow = ks + np.arange(ATT_KROWS)
        row_ok = (krow[None, :] >= kr0[:, None]) & (krow[None, :] < kr0[:, None] + WIN_ROWS)
        d_row = np.clip(krow[None, :] - r[:, None] + (WIN_ROWS - 1), 0, n_dr - 1)
        row_sel.append((d_row.reshape(-1)[:, None] == np.arange(n_dr)[None, :]).astype(np.float32))
        oks.append(row_ok[:, None, :, None] & col_ok[None, :, None, :])
    hp = lax.Precision.HIGHEST
    a = jnp.einsum('cpr,hrd->chpd', jnp.asarray(np.stack(row_sel)), rpb.astype(F32), precision=hp)
    t = jnp.einsum('chpd,dx->chpx', a, jnp.asarray(col_sel), precision=hp)
    t = t.reshape(3, N_GROUPS, ATT_QROWS, ATT_KROWS, GRID_W, GRID_W).transpose(0, 1, 2, 4, 3, 5)
    ok = jnp.asarray(np.stack(oks))[:, None]
    return jnp.where(ok, t, NEG_INF).reshape(3, N_GROUPS, ATT_QTOK, ATT_KROWS * GRID_W)


def _rope(x, cos_ref, sin_ref, first16):
    c = cos_ref[...]
    s = sin_ref[...]
    c2 = jnp.concatenate([c, c], axis=1)
    s2 = jnp.concatenate([s, s], axis=1)
    w = x.shape[1]
    partner = jnp.where(first16, pltpu.roll(x, w - HEAD_DIM // 4, 1), pltpu.roll(x, HEAD_DIM // 4, 1))
    return x * c2 + partner * s2


def _softmax_pv(scores_values, head_mask, acc):
    m = None
    for s, _ in scores_values:
        mx = jnp.max(s, axis=1, keepdims=True)
        m = mx if m is None else jnp.maximum(m, mx)
    den = None
    o = None
    for s, v in scores_values:
        p = jnp.exp(s - m)
        sm = jnp.sum(p, axis=1, keepdims=True)
        den = sm if den is None else den + sm
        t = jnp.dot(p.astype(BF16), v, preferred_element_type=F32)
        o = t if o is None else o + t
    return acc + jnp.where(head_mask, o * (1.0 / den), 0.0)


def _nattn_kernel(q_ref, k0_ref, k1_ref, k2_ref, v0_ref, v1_ref, v2_ref,
                  cq_ref, sq_ref, ck0_ref, ck1_ref, ck2_ref, sk0_ref, sk1_ref, sk2_ref,
                  kc_ref, vc_ref, bias_ref, o_ref):
    w = q_ref.shape[2]
    lane = lax.broadcasted_iota(jnp.int32, (1, w), 1)
    first16 = (lane % (HEAD_DIM // 2)) < (HEAD_DIM // 4)
    q = q_ref[0].astype(F32) * (HEAD_DIM ** -0.5)
    q_rot = _rope(q, cq_ref, sq_ref, first16)
    k_rot = jnp.concatenate([_rope(k0_ref[0].astype(F32), ck0_ref, sk0_ref, first16),
                             _rope(k1_ref[0].astype(F32), ck1_ref, sk1_ref, first16),
                             _rope(k2_ref[0].astype(F32), ck2_ref, sk2_ref, first16)], axis=0).astype(BF16)
    v = jnp.concatenate([v0_ref[0], v1_ref[0], v2_ref[0]], axis=0).astype(BF16)
    kc = kc_ref[0].astype(BF16)
    vc = vc_ref[0].astype(BF16)
    acc = jnp.zeros(q.shape, F32)
    for h in range(N_GROUPS):
        hm = (lane // HEAD_DIM) == h
        s_win = lax.dot_general(jnp.where(hm, q_rot, 0.0).astype(BF16), k_rot, _NT,
                                preferred_element_type=F32) + bias_ref[0, h]
        s_ctx = lax.dot_general(jnp.where(hm, q, 0.0).astype(BF16), kc, _NT, preferred_element_type=F32)
        acc = _softmax_pv([(s_win, v), (s_ctx, vc)], hm, acc)
    o_ref[0] = acc.astype(o_ref.dtype)


def neighbourhood_attention(z, zc, rpb, *, q_col, k_col, v_col, kc_col, vc_col):
    bsz, n, _ = z.shape
    n_ctx = zc.shape[1]
    w = BRANCH_WIDTH
    n_blk = n // ATT_QTOK
    cos_t, sin_t = _rope_tables(n)
    bias = _attn_bias_tables(rpb, n // GRID_W)

    def kb(j):
        return jnp.clip(j - 1, 0, n_blk - ATT_KBLK)

    def zspec(col, off=None):
        if off is None:
            return pl.BlockSpec((1, ATT_QTOK, w), lambda b, j: (b, j, col))
        return pl.BlockSpec((1, ATT_QTOK, w), lambda b, j: (b, kb(j) + off, col))

    def tspec(off=None):
        if off is None:
            return pl.BlockSpec((ATT_QTOK, 2 * HEAD_DIM), lambda b, j: (j, 0))
        return pl.BlockSpec((ATT_QTOK, 2 * HEAD_DIM), lambda b, j: (kb(j) + off, 0))

    in_specs = ([zspec(q_col)] + [zspec(k_col, i) for i in range(ATT_KBLK)] + [zspec(v_col, i) for i in range(ATT_KBLK)]
                + [tspec(), tspec()] + [tspec(i) for i in range(ATT_KBLK)] * 2
                + [pl.BlockSpec((1, n_ctx, w), lambda b, j: (b, 0, kc_col)),
                   pl.BlockSpec((1, n_ctx, w), lambda b, j: (b, 0, vc_col)),
                   pl.BlockSpec((1, N_GROUPS, ATT_QTOK, ATT_KROWS * GRID_W),
                                lambda b, j: (jnp.minimum(j, 1) + j // (n_blk - 1), 0, 0, 0))])
    return pl.pallas_call(
        _nattn_kernel,
        grid=(bsz, n_blk),
        in_specs=in_specs,
        out_specs=pl.BlockSpec((1, ATT_QTOK, w), lambda b, j: (b, j, 0)),
        out_shape=jax.ShapeDtypeStruct((bsz, n, w), BF16),
        compiler_params=_params("arbitrary", "arbitrary"),
        name="neighbourhood_attention",
    )(z, z, z, z, z, z, z, cos_t, sin_t, cos_t, cos_t, cos_t, sin_t, sin_t, sin_t, zc, zc, bias)


def _ctx_attn_kernel(q_ref, kc_ref, vc_ref, o_ref):
    w = q_ref.shape[2]
    lane = lax.broadcasted_iota(jnp.int32, (1, w), 1)
    q = q_ref[0].astype(F32) * (HEAD_DIM ** -0.5)
    kc = kc_ref[0].astype(BF16)
    vc = vc_ref[0].astype(BF16)
    acc = jnp.zeros(q.shape, F32)
    for h in range(N_GROUPS):
        hm = (lane // HEAD_DIM) == h
        s = lax.dot_general(jnp.where(hm, q, 0.0).astype(BF16), kc, _NT, preferred_element_type=F32)
        acc = _softmax_pv([(s, vc)], hm, acc)
    o_ref[0] = acc.astype(o_ref.dtype)


def context_attention(zc, *, q_col, k_col, v_col):
    bsz, n_ctx, _ = zc.shape
    w = BRANCH_WIDTH
    return pl.pallas_call(
        _ctx_attn_kernel,
        grid=(bsz,),
        in_specs=[pl.BlockSpec((1, n_ctx, w), functools.partial(lambda b, c: (b, 0, c), c=c))
                  for c in (q_col, k_col, v_col)],
        out_specs=pl.BlockSpec((1, n_ctx, w), lambda b: (b, 0, 0)),
        out_shape=jax.ShapeDtypeStruct((bsz, n_ctx, w), BF16),
        compiler_params=_params("arbitrary"),
        name="context_attention",
    )(zc, zc, zc)


def _ln(v, g, b):
    mu = jnp.mean(v, axis=-1, keepdims=True)
    var = jnp.mean(jnp.square(v - mu), axis=-1, keepdims=True)
    return (v - mu) * lax.rsqrt(var + EPS) * g + b


def _gmlp_kernel(z_ref, lng_ref, lnb_ref, w_ref, bias_ref, o_ref):
    tm = z_ref.shape[1]
    z = jax.nn.gelu(z_ref[0])
    u = z[:, :BRANCH_WIDTH]
    v = _ln(z[:, BRANCH_WIDTH:], lng_ref[...], lnb_ref[...])
    group = lax.broadcasted_iota(jnp.int32, (1, BRANCH_WIDTH), 1) // HEAD_DIM
    wcat = w_ref[...]
    for c in range(tm // CHUNK):
        rows = slice(c * CHUNK, (c + 1) * CHUNK)
        vc = v[rows]
        vst = jnp.concatenate([jnp.where(group == g, vc, 0.0) for g in range(N_GROUPS)], axis=0).astype(BF16)
        mixed = jnp.dot(wcat, vst, preferred_element_type=F32) + bias_ref[...]
        o_ref[0, rows, :] = (u[rows] * mixed).astype(o_ref.dtype)


def chunk_gmlp(za, ln_g, ln_b, w_s, b_s, *, tm):
    bsz, n, w2 = za.shape
    w = w2 // 2
    wcat = jnp.transpose(w_s, (1, 0, 2)).reshape(CHUNK, N_GROUPS * CHUNK).astype(BF16)
    bias = jnp.repeat(b_s.T, HEAD_DIM, axis=1)
    return pl.pallas_call(
        _gmlp_kernel,
        grid=(bsz, n // tm),
        in_specs=[
            pl.BlockSpec((1, tm, w2), lambda b, i: (b, i, 0)),
            pl.BlockSpec((1, w), lambda b, i: (0, 0)),
            pl.BlockSpec((1, w), lambda b, i: (0, 0)),
            pl.BlockSpec((CHUNK, N_GROUPS * CHUNK), lambda b, i: (0, 0)),
            pl.BlockSpec((CHUNK, w), lambda b, i: (0, 0)),
        ],
        out_specs=pl.BlockSpec((1, tm, w), lambda b, i: (b, i, 0)),
        out_shape=jax.ShapeDtypeStruct((bsz, n, w), BF16),
        compiler_params=_params("arbitrary", "arbitrary"),
        name="chunk_gmlp",
    )(za, ln_g[None, :], ln_b[None, :], wcat, bias)


CONV_HALO = 16
CONV_SUB = 128
SUBLANES = 8


def _conv_kernel(a_ref, g_ref, ap_ref, gp_ref, an_ref, gn_ref, w_ref, cb_ref, lng_ref, lnb_ref, o_ref, y_ref):
    i = pl.program_id(1)
    tm = a_ref.shape[1]

    def glu(a, g):
        return a * jax.nn.sigmoid(g)

    y_ref[0:CONV_HALO, :] = jnp.where(i > 0, glu(ap_ref[0], gp_ref[0]), 0.0)
    y_ref[CONV_HALO:CONV_HALO + tm, :] = glu(a_ref[0], g_ref[0])
    y_ref[CONV_HALO + tm:, :] = jnp.where(i < pl.num_programs(1) - 1, glu(an_ref[0], gn_ref[0]), 0.0)
    first = CONV_HALO - CONV_WIDTH // 2
    nb = CONV_SUB + 2 * CONV_HALO
    for r in range(0, tm, CONV_SUB):
        blk = y_ref[r:r + nb, :]
        acc = None
        for res in range(SUBLANES):
            rot = blk if res == 0 else pltpu.roll(blk, nb - res, 0)
            for j in range(CONV_WIDTH):
                if (first + j) % SUBLANES == res:
                    a0 = first + j - res
                    t = rot[a0:a0 + CONV_SUB, :] * w_ref[j:j + 1, :]
                    acc = t if acc is None else acc + t
        y = _ln(acc + cb_ref[...], lng_ref[...], lnb_ref[...])
        o_ref[0, r:r + CONV_SUB, :] = jax.nn.silu(y).astype(o_ref.dtype)


def conformer_conv(zcv, conv_w, conv_b, ln_g, ln_b, *, tm):
    bsz, n, w2 = zcv.shape
    w = w2 // 2
    hb = tm // CONV_HALO
    n_hb = n // CONV_HALO

    def main(col):
        return pl.BlockSpec((1, tm, w), lambda b, i: (b, i, col))

    def prev(col):
        return pl.BlockSpec((1, CONV_HALO, w), lambda b, i: (b, jnp.maximum(i * hb - 1, 0), col))

    def nxt(col):
        return pl.BlockSpec((1, CONV_HALO, w), lambda b, i: (b, jnp.minimum((i + 1) * hb, n_hb - 1), col))

    vec = pl.BlockSpec((1, w), lambda b, i: (0, 0))
    return pl.pallas_call(
        _conv_kernel,
        grid=(bsz, n // tm),
        in_specs=[main(0), main(1), prev(0), prev(1), nxt(0), nxt(1),
                  pl.BlockSpec((CONV_WIDTH, w), lambda b, i: (0, 0)), vec, vec, vec],
        out_specs=pl.BlockSpec((1, tm, w), lambda b, i: (b, i, 0)),
        out_shape=jax.ShapeDtypeStruct((bsz, n, w), BF16),
        scratch_shapes=[pltpu.VMEM((tm + 2 * CONV_HALO, w), F32)],
        compiler_params=_params("arbitrary", "arbitrary"),
        name="conformer_conv",
    )(zcv, zcv, zcv, zcv, zcv, zcv, conv_w, conv_b[None, :], ln_g[None, :], ln_b[None, :])


FOURIER_N2 = 128
FOURIER_FB = 8
FOURIER_KB = 8


def _np_split(m):
    m = jnp.asarray(m, F32)
    hi = m.astype(BF16)
    return hi, (m - hi.astype(F32)).astype(BF16)


def _channel_dft_matrix():
    c = np.arange(HEAD_DIM)
    ang = 2.0 * np.pi * ((c[:, None] * c[None, :]) % HEAD_DIM) / HEAD_DIM
    eye = np.eye(N_GROUPS)
    return np.concatenate([np.kron(eye, np.cos(ang)), np.kron(eye, np.sin(ang))], axis=0)


def _fourier_stage1_kernel(x_ref, mh_ref, ml_ref, o_ref):
    w = BRANCH_WIDTH
    for f in range(mh_ref.shape[0]):
        x_hi, x_lo = _split_bf16(x_ref[0, :, f, :])
        o_ref[0, :, f, :] = _dot3(mh_ref[f], ml_ref[f], x_hi, x_lo)


def _fourier_stage2_kernel(br_ref, bi_ref, m2h_ref, m2l_ref, mdh_ref, mdl_ref, o_ref, *, scale):
    w = BRANCH_WIDTH
    for k in range(br_ref.shape[1]):
        b_hi, b_lo = _split_bf16(jnp.concatenate([br_ref[0, k], bi_ref[0, k]], axis=0))
        xs = _dot3(m2h_ref[...], m2l_ref[...], b_hi, b_lo)
        x_hi, x_lo = _split_bf16(jnp.concatenate([xs[:FOURIER_N2], xs[FOURIER_N2:]], axis=1))
        o_ref[0, :, k, :] = _dot3(x_hi, x_lo, mdh_ref[...], mdl_ref[...]) * scale


def _fourier_direct_kernel(x_ref, mph_ref, mpl_ref, mdh_ref, mdl_ref, o_ref, *, scale):
    n = x_ref.shape[1]
    x_hi, x_lo = _split_bf16(x_ref[0])
    p = _dot3(mph_ref[...], mpl_ref[...], x_hi, x_lo)
    p_hi, p_lo = _split_bf16(jnp.concatenate([p[:n], p[n:]], axis=1))
    o_ref[0] = (_dot3(p_hi, p_lo, mdh_ref[...], mdl_ref[...]) * scale).astype(o_ref.dtype)


def fourier_mix(zb):
    bsz, n, w = zb.shape
    scale = float(1.0 / np.sqrt(n * HEAD_DIM))
    mdh, mdl = _np_split(_channel_dft_matrix())
    md_spec2 = pl.BlockSpec((2 * w, w), lambda b, j: (0, 0))
    if n <= 2 * FOURIER_N2:
        t = np.arange(n)
        ang = 2.0 * np.pi * ((t[:, None] * t[None, :]) % n) / n
        mph, mpl = _np_split(np.concatenate([np.cos(ang), -np.sin(ang)], axis=0))
        return pl.pallas_call(
            functools.partial(_fourier_direct_kernel, scale=scale),
            grid=(bsz,),
            in_specs=[pl.BlockSpec((1, n, w), lambda b: (b, 0, 0)),
                      pl.BlockSpec((2 * n, n), lambda b: (0, 0)), pl.BlockSpec((2 * n, n), lambda b: (0, 0)),
                      pl.BlockSpec((2 * w, w), lambda b: (0, 0)), pl.BlockSpec((2 * w, w), lambda b: (0, 0))],
            out_specs=pl.BlockSpec((1, n, w), lambda b: (b, 0, 0)),
            out_shape=jax.ShapeDtypeStruct((bsz, n, w), BF16),
            compiler_params=_params("arbitrary"),
            name="fourier_direct",
        )(zb, mph, mpl, mdh, mdl)

    n1, n2 = n // FOURIER_N2, FOURIER_N2
    f, k1, s = np.arange(n2), np.arange(n1), np.arange(n1)
    ang1 = 2.0 * np.pi * ((k1[None, :, None] * (f[:, None, None] + n2 * s[None, None, :])) % n) / n
    m1h, m1l = _np_split(np.concatenate([np.cos(ang1), -np.sin(ang1)], axis=1))
    k2 = np.arange(n2)
    ang2 = 2.0 * np.pi * ((k2[:, None] * f[None, :]) % n2) / n2
    c2, s2 = np.cos(ang2), np.sin(ang2)
    m2h, m2l = _np_split(np.block([[c2, s2], [-s2, c2]]))

    b_st = pl.pallas_call(
        _fourier_stage1_kernel,
        grid=(bsz, n2 // FOURIER_FB),
        in_specs=[pl.BlockSpec((1, n1, FOURIER_FB, w), lambda b, j: (b, 0, j, 0)),
                  pl.BlockSpec((FOURIER_FB, 2 * n1, n1), lambda b, j: (j, 0, 0)),
                  pl.BlockSpec((FOURIER_FB, 2 * n1, n1), lambda b, j: (j, 0, 0))],
        out_specs=pl.BlockSpec((1, 2 * n1, FOURIER_FB, w), lambda b, j: (b, 0, j, 0)),
        out_shape=jax.ShapeDtypeStruct((bsz, 2 * n1, n2, w), F32),
        compiler_params=_params("arbitrary", "arbitrary"),
        name="fourier_stage1",
    )(zb.reshape(bsz, n1, n2, w), m1h, m1l)

    kb = FOURIER_KB
    out = pl.pallas_call(
        functools.partial(_fourier_stage2_kernel, scale=scale),
        grid=(bsz, n1 // kb),
        in_specs=[pl.BlockSpec((1, kb, n2, w), lambda b, j: (b, j, 0, 0)),
                  pl.BlockSpec((1, kb, n2, w), lambda b, j: (b, n1 // kb + j, 0, 0)),
                  pl.BlockSpec((2 * n2, 2 * n2), lambda b, j: (0, 0)),
                  pl.BlockSpec((2 * n2, 2 * n2), lambda b, j: (0, 0)),
                  md_spec2, md_spec2],
        out_specs=pl.BlockSpec((1, n2, kb, w), lambda b, j: (b, 0, j, 0)),
        out_shape=jax.ShapeDtypeStruct((bsz, n2, n1, w), F32),
        compiler_params=_params("arbitrary", "arbitrary"),
        name="fourier_stage2",
    )(b_st, b_st, m2h, m2l, mdh, mdl)
    return out.reshape(bsz, n, w)


def expert_route(x, g, sc, sh, router_t, *, tm):
    n = x.shape[1]
    cap = EC_CAPACITY * n // N_EXPERTS
    h, aff_t = norm_router(x, g, sc, sh, router_t, tm=tm)
    slot, off = expert_topk(aff_t, cap)
    xe, gate = expert_gather(h, slot, aff_t, off, max(cap, ROUTE_WIN))
    return xe, gate, jnp.swapaxes(slot, 1, 2), off


def kernel(x, c, ctx, c_ctx, w_mod, b_mod, norm1_g, norm2_g, w_in, sgu_ln_g, sgu_ln_b, w_spatial, b_spatial,
           w_a_out, w_b_out, conv_w, conv_b, conv_ln_g, conv_ln_b, w_c_out, rpb, w_d_out, w_out, w_router,
           w_gate_e, w_up_e, w_down_e, final_norm_g):
    bsz = x.shape[0]
    xc = ctx
    assert bsz + 1 <= MOD_ROWS
    cond = jnp.concatenate([c, c_ctx[None, :], jnp.zeros((MOD_ROWS - bsz - 1, D_MODEL), F32)], axis=0)
    mod_all = modulation(cond, w_mod, b_mod)
    w_in_b = w_in.astype(BF16)
    for l in range(DEPTH):
        last = l == DEPTH - 1
        sh1, sc1, gt1, sh2, sc2, gt2 = jnp.split(mod_all[l, :bsz, None, :], 6, axis=-1)
        cmod = jnp.broadcast_to(mod_all[l, bsz:bsz + 1, None, :], (bsz, 1, 6 * D_MODEL))
        csh1, csc1, cgt1, csh2, csc2, cgt2 = jnp.split(cmod, 6, axis=-1)

        wa, wb, wc, wd_, wo = (w.astype(BF16) for w in (w_a_out[l], w_b_out[l], w_c_out[l], w_d_out[l], w_out[l]))
        g1 = norm1_g[l][None, :]

        za, zb, zcv, zqkv, zg = norm_inproj(x, g1, sc1, sh1, w_in_b, l, 0, IN_WIDTHS, IN_DTYPES, tm=512)
        if last:
            (zqkv_c,) = norm_inproj(xc, g1, csc1, csh1, w_in_b, l, Q_END, (2 * BRANCH_WIDTH,), (BF16,), tm=CTX_TM)
            kc_col, vc_col = 0, 1
        else:
            cza, czb, czcv, zqkv_c, czg = norm_inproj(xc, g1, csc1, csh1, w_in_b, l, 0, IN_WIDTHS, IN_DTYPES,
                                                      tm=CTX_TM)
            kc_col, vc_col = 1, 2
        d_lat = neighbourhood_attention(zqkv, zqkv_c, rpb[l], q_col=0, k_col=1, v_col=2,
                                        kc_col=kc_col, vc_col=vc_col)
        a_lat = chunk_gmlp(za, sgu_ln_g[l], sgu_ln_b[l], w_spatial[l], b_spatial[l], tm=512)
        b_lat = fourier_mix(zb)
        c_lat = conformer_conv(zcv, conv_w[l], conv_b[l], conv_ln_g[l], conv_ln_b[l], tm=512)
        x = merge_branches(a_lat, b_lat, c_lat, d_lat, zg, wa, wb, wc, wd_, wo, x, gt1, tm=512)
        if not last:
            a_c = chunk_gmlp(cza, sgu_ln_g[l], sgu_ln_b[l], w_spatial[l], b_spatial[l], tm=CTX_TM)
            b_c = fourier_mix(czb)
            c_c = conformer_conv(czcv, conv_w[l], conv_b[l], conv_ln_g[l], conv_ln_b[l], tm=CTX_TM)
            d_c = context_attention(zqkv_c, q_col=0, k_col=1, v_col=2)
            xc = merge_branches(a_c, b_c, c_c, d_c, czg, wa, wb, wc, wd_, wo, xc, cgt1, tm=CTX_TM)

        g2 = norm2_g[l][None, :]
        router_t = w_router[l].T
        xe, gate, slot_t, off = expert_route(x, g2, sc2, sh2, router_t, tm=512)
        if last:
            (y,) = expert_ffn([xe], [gate], w_gate_e, w_up_e, w_down_e, l)
        else:
            xe_c, gate_c, slot_tc, off_c = expert_route(xc, g2, csc2, csh2, router_t, tm=CTX_TM)
            y, y_c = expert_ffn([xe, xe_c], [gate, gate_c], w_gate_e, w_up_e, w_down_e, l)
            xc = expert_combine(y_c, slot_tc, off_c, xc, cgt2)
        x = expert_combine(y, slot_t, off, x, gt2, final_norm_g if last else None)
    return x


CTX_TM = 256
```

```python
import functools

import jax
import jax.numpy as jnp
import numpy as np
from jax import lax
from jax.experimental import pallas as pl
from jax.experimental.pallas import tpu as pltpu

D_MODEL = 1024
DEPTH = 2
GRID_W = 64
HEAD_DIM = 64
N_GROUPS = 4
BRANCH_WIDTH = N_GROUPS * HEAD_DIM
N_BRANCH = 4
CHUNK = 128
CONV_WIDTH = 31
WIN_ROWS = 8
WIN_COLS = 16
QB_COLS = 16
KB_COLS = QB_COLS + WIN_COLS
ROPE_BASE = 10000.0
N_EXPERTS = 16
EXPERT_FF = 1024
EC_CAPACITY = 2
EPS = 1e-6
NEG_INF = -1e30
A_END = 2 * BRANCH_WIDTH
B_END = A_END + BRANCH_WIDTH
C_END = B_END + 2 * BRANCH_WIDTH
Q_END = C_END + BRANCH_WIDTH
K_END = Q_END + BRANCH_WIDTH
V_END = K_END + BRANCH_WIDTH
IN_COLS = V_END + N_BRANCH * D_MODEL
SPLITS = [A_END, B_END, C_END, Q_END, K_END, V_END]
IN_WIDTHS = (A_END, B_END - A_END, C_END - B_END, V_END - C_END, IN_COLS - V_END)
IN_DTYPES = (jnp.float32, jnp.float32, jnp.float32, jnp.bfloat16, jnp.bfloat16)

VMEM_LIMIT_BYTES = 56 * 1024 * 1024
F32 = jnp.float32
BF16 = jnp.bfloat16


def _params(*sem):
    return pltpu.CompilerParams(dimension_semantics=sem, vmem_limit_bytes=VMEM_LIMIT_BYTES)


MOD_ROWS = 8
MOD_TN = 1536


def _split_bf16(v):
    hi = v.astype(BF16)
    return hi, (v - hi.astype(F32)).astype(BF16)


def _dot3(a_hi, a_lo, b_hi, b_lo):
    return (jnp.dot(a_hi, b_hi, preferred_element_type=F32) + jnp.dot(a_hi, b_lo, preferred_element_type=F32)
            + jnp.dot(a_lo, b_hi, preferred_element_type=F32))


def _mod_kernel(c_ref, w_ref, b_ref, o_ref):
    s_hi, s_lo = _split_bf16(jax.nn.silu(c_ref[...]))
    w_hi, w_lo = _split_bf16(w_ref[0])
    o_ref[0] = _dot3(s_hi, s_lo, w_hi, w_lo) + b_ref[0]


def modulation(cond, w_mod, b_mod):
    n_layers, d, cols = w_mod.shape
    return pl.pallas_call(
        _mod_kernel,
        grid=(n_layers, cols // MOD_TN),
        in_specs=[pl.BlockSpec((MOD_ROWS, d), lambda l, j: (0, 0)),
                  pl.BlockSpec((1, d, MOD_TN), lambda l, j: (l, 0, j)),
                  pl.BlockSpec((1, 1, MOD_TN), lambda l, j: (l, 0, j))],
        out_specs=pl.BlockSpec((1, MOD_ROWS, MOD_TN), lambda l, j: (l, 0, j)),
        out_shape=jax.ShapeDtypeStruct((n_layers, MOD_ROWS, cols), F32),
        compiler_params=_params("arbitrary", "arbitrary"),
        name="modulation",
    )(cond, w_mod, b_mod[:, None, :])


INPROJ_COL_CHUNK = 512


def _norm_inproj_kernel(x_ref, g_ref, sc_ref, sh_ref, w_ref, *o_refs):
    x = x_ref[0]
    y = x * lax.rsqrt(jnp.mean(x * x, axis=-1, keepdims=True) + EPS) * g_ref[...]
    h = (y * (1.0 + sc_ref[0]) + sh_ref[0]).astype(BF16)
    off = 0
    for o_ref in o_refs:
        width = o_ref.shape[2]
        for c0 in range(0, width, INPROJ_COL_CHUNK):
            cw = min(INPROJ_COL_CHUNK, width - c0)
            o_ref[0, :, c0:c0 + cw] = jnp.dot(h, w_ref[:, off + c0:off + c0 + cw],
                                              preferred_element_type=F32).astype(o_ref.dtype)
        off += width


def norm_inproj(x, g, sc, sh, w, l, col0, widths, dtypes, *, tm):
    bsz, n, d = x.shape
    cols = sum(widths)
    assert col0 % cols == 0
    return pl.pallas_call(
        _norm_inproj_kernel,
        grid=(bsz, n // tm),
        in_specs=[
            pl.BlockSpec((1, tm, d), lambda b, i: (b, i, 0)),
            pl.BlockSpec((1, d), lambda b, i: (0, 0)),
            pl.BlockSpec((1, 1, d), lambda b, i: (b, 0, 0)),
            pl.BlockSpec((1, 1, d), lambda b, i: (b, 0, 0)),
            pl.BlockSpec((None, d, cols), lambda b, i: (l, 0, col0 // cols), pipeline_mode=pl.Buffered(1)),
        ],
        out_specs=[pl.BlockSpec((1, tm, wd), lambda b, i: (b, i, 0)) for wd in widths],
        out_shape=[jax.ShapeDtypeStruct((bsz, n, wd), dt) for wd, dt in zip(widths, dtypes, strict=True)],
        compiler_params=_params("arbitrary", "arbitrary"),
        name="norm_inproj",
    )(x, g, sc, sh, w)


def _merge_kernel(a_ref, b_ref, c_ref, d_ref, ga_ref, gb_ref, gc_ref, gd_ref,
                  wa_ref, wb_ref, wc_ref, wd_ref, wo_ref, x_ref, gt_ref,
                  g2_ref, sc2_ref, sh2_ref, rt_ref, o_ref, h_ref, aff_ref):
    m = None
    for br, gz, w in ((a_ref, ga_ref, wa_ref), (b_ref, gb_ref, wb_ref),
                      (c_ref, gc_ref, wc_ref), (d_ref, gd_ref, wd_ref)):
        p = jnp.dot(br[0].astype(BF16), w[...], preferred_element_type=F32)
        t = jax.nn.sigmoid(gz[0].astype(F32)) * p
        m = t if m is None else m + t
    mix = jnp.dot(m.astype(BF16), wo_ref[...], preferred_element_type=F32)
    x = x_ref[0] + gt_ref[0] * mix
    o_ref[0] = x
    y = x * lax.rsqrt(jnp.mean(x * x, axis=-1, keepdims=True) + EPS) * g2_ref[...]
    h = y * (1.0 + sc2_ref[0]) + sh2_ref[0]
    h_hi, h_lo = _split_bf16(h)
    r_hi, r_lo = _split_bf16(rt_ref[...])
    logits = (lax.dot_general(r_hi, h_hi, _NT, preferred_element_type=F32)
              + lax.dot_general(r_hi, h_lo, _NT, preferred_element_type=F32)
              + lax.dot_general(r_lo, h_hi, _NT, preferred_element_type=F32))
    pr = jnp.exp(logits - jnp.max(logits, axis=0, keepdims=True))
    aff_ref[0] = pr / jnp.sum(pr, axis=0, keepdims=True)
    h_ref[0] = h_hi


def merge_branches(a, b, cc, d, z, w_a, w_b, w_c, w_d, w_o, x, gt, g2, sc2, sh2, router_t, *, tm):
    bsz, n, dm = x.shape
    w = a.shape[-1]
    ne = router_t.shape[0]
    br_spec = pl.BlockSpec((1, tm, w), lambda bi, i: (bi, i, 0))
    gate_specs = [pl.BlockSpec((1, tm, dm), functools.partial(lambda bi, i, k: (bi, i, k), k=k))
                  for k in range(N_BRANCH)]
    wbr_spec = pl.BlockSpec((w, dm), lambda bi, i: (0, 0))
    row_spec = pl.BlockSpec((1, tm, dm), lambda bi, i: (bi, i, 0))
    mod_spec = pl.BlockSpec((1, 1, dm), lambda bi, i: (bi, 0, 0))
    return pl.pallas_call(
        _merge_kernel,
        grid=(bsz, n // tm),
        in_specs=[br_spec] * 4 + gate_specs + [wbr_spec] * 4 + [
            pl.BlockSpec((dm, dm), lambda bi, i: (0, 0)), row_spec, mod_spec,
            pl.BlockSpec((1, dm), lambda bi, i: (0, 0)), mod_spec, mod_spec,
            pl.BlockSpec((ne, dm), lambda bi, i: (0, 0)),
        ],
        out_specs=[row_spec, row_spec, pl.BlockSpec((1, ne, tm), lambda bi, i: (bi, 0, i))],
        out_shape=[jax.ShapeDtypeStruct((bsz, n, dm), F32), jax.ShapeDtypeStruct((bsz, n, dm), BF16),
                   jax.ShapeDtypeStruct((bsz, ne, n), F32)],
        compiler_params=_params("arbitrary", "arbitrary"),
        name="merge_branches",
    )(a, b, cc, d, z, z, z, z, w_a, w_b, w_c, w_d, w_o, x, gt, g2, sc2, sh2, router_t)


ROUTE_TB = 256
ROUTE_WIN = 64
ROUTE_ALIGN = 16
OFF_LANES = 128
TOPK_EXP_STEPS = (64, 32, 16, 8, 4, 2, 1)
TOPK_BISECT_STEPS = 56


def _topk_kernel(aff_ref, slot_ref, off_ref, *, cap):
    a = aff_ref[0]
    ne, n = a.shape
    capf = jnp.float32(cap)

    def count_ge(t):
        return jnp.sum(jnp.where(a >= t, 1.0, 0.0), axis=1, keepdims=True)

    hi = jnp.full((ne, 1), 2.0, F32)
    for s in TOPK_EXP_STEPS:
        cand = hi * (2.0 ** -s)
        hi = jnp.where(count_ge(cand) < capf, cand, hi)
    lo = jnp.where(hi <= 2.0 ** -126, 0.0, hi * 0.5)

    def bisect(_, carry):
        lo, hi = carry
        mid = 0.5 * (lo + hi)
        ok = count_ge(mid) >= capf
        return jnp.where(ok, mid, lo), jnp.where(ok, hi, mid)

    lo, hi = lax.fori_loop(0, TOPK_BISECT_STEPS, bisect, (lo, hi))
    need = capf - count_ge(hi)
    r_i = lax.broadcasted_iota(jnp.int32, (ROUTE_TB, ROUTE_TB), 0)
    c_i = lax.broadcasted_iota(jnp.int32, (ROUTE_TB, ROUTE_TB), 1)
    tri = jnp.where(r_i < c_i, 1.0, 0.0).astype(BF16)
    lane = lax.broadcasted_iota(jnp.int32, (ne, OFF_LANES), 1)
    run_eq = jnp.zeros((ne, 1), F32)
    run_sel = jnp.zeros((ne, 1), F32)
    offs = jnp.zeros((ne, OFF_LANES), F32)
    for c in range(n // ROUTE_TB):
        cols = slice(c * ROUTE_TB, (c + 1) * ROUTE_TB)
        a_c = a[:, cols]
        above = a_c >= hi
        eq_c = jnp.where(above, 0.0, jnp.where(a_c >= lo, 1.0, 0.0))
        rank = jnp.dot(eq_c.astype(BF16), tri, preferred_element_type=F32) + run_eq
        sel = jnp.where(above, 1.0, jnp.where(rank < need, eq_c, 0.0))
        pos = jnp.dot(sel.astype(BF16), tri, preferred_element_type=F32) + run_sel
        slot_ref[0, :, cols] = jnp.where(sel > 0.0, pos, -1.0).astype(jnp.int32)
        offs = jnp.where(lane == c, run_sel, offs)
        run_eq = run_eq + jnp.sum(eq_c, axis=1, keepdims=True)
        run_sel = run_sel + jnp.sum(sel, axis=1, keepdims=True)
    offs = jnp.where(lane == n // ROUTE_TB, run_sel, offs)
    off_ref[0] = offs.astype(jnp.int32)


def expert_topk(aff_t, cap):
    bsz, ne, n = aff_t.shape
    return pl.pallas_call(
        functools.partial(_topk_kernel, cap=cap),
        grid=(bsz,),
        in_specs=[pl.BlockSpec((1, ne, n), lambda b: (b, 0, 0))],
        out_specs=[pl.BlockSpec((1, ne, n), lambda b: (b, 0, 0)),
                   pl.BlockSpec((1, ne, OFF_LANES), lambda b: (b, 0, 0))],
        out_shape=[jax.ShapeDtypeStruct((bsz, ne, n), jnp.int32),
                   jax.ShapeDtypeStruct((bsz, ne, OFF_LANES), jnp.int32)],
        compiler_params=_params("arbitrary"),
        name="expert_topk",
    )(aff_t)


def _window_plan(off_ref, b, e, tb, w, rows_max):
    start = off_ref[b, e, tb]
    stop = off_ref[b, e, tb + 1]
    first = (start // ROUTE_ALIGN) * ROUTE_ALIGN + w * ROUTE_WIN
    active = first < stop
    return pl.multiple_of(jnp.minimum(first, rows_max), ROUTE_ALIGN), first, active


def _num_passes(off_ref, b, e_list, tb):
    n_pass = jnp.int32(0)
    for e in e_list:
        start = off_ref[b, e, tb]
        stop = off_ref[b, e, tb + 1]
        base = (start // ROUTE_ALIGN) * ROUTE_ALIGN
        n_pass = jnp.maximum(n_pass, (stop - base + ROUTE_WIN - 1) // ROUTE_WIN)
    return n_pass


def _gather_kernel(off_ref, h_ref, slot_ref, aff_ref, xe_ref, gate_ref, *, group):
    b, g, tb = pl.program_id(0), pl.program_id(1), pl.program_id(2)
    rows_max = xe_ref.shape[2] - ROUTE_WIN

    @pl.when(tb == 0)
    def _():
        xe_ref[...] = jnp.zeros(xe_ref.shape, xe_ref.dtype)
        gate_ref[...] = jnp.zeros(gate_ref.shape, gate_ref.dtype)

    sub = lax.broadcasted_iota(jnp.int32, (ROUTE_WIN, 1), 0)
    experts = [g * group + e for e in range(group)]

    def one_pass(w, carry):
        plans = [_window_plan(off_ref, b, ge, tb, w, rows_max) for ge in experts]
        hots = []
        for e, (rows, first, active) in enumerate(plans):
            ids = rows + sub
            want = jnp.where(active & (ids >= first), ids, -2)
            hots.append(slot_ref[0, e:e + 1, :] == want)
        onehot = jnp.concatenate([jnp.where(hm, 1.0, 0.0) for hm in hots], axis=0).astype(BF16)
        res = jnp.dot(onehot, h_ref[0], preferred_element_type=F32)
        for e, (rows, _, _) in enumerate(plans):
            win = pl.ds(rows, ROUTE_WIN)
            xe_ref[0, e, win, :] = xe_ref[0, e, win, :] + res[e * ROUTE_WIN:(e + 1) * ROUTE_WIN].astype(BF16)
            gsel = jnp.sum(jnp.where(hots[e], aff_ref[0, e:e + 1, :], 0.0), axis=1, keepdims=True)
            gate_ref[0, e, win, :] = gate_ref[0, e, win, :] + gsel
        return carry

    lax.fori_loop(0, _num_passes(off_ref, b, experts, tb), one_pass, 0)


def expert_gather(h, slot, aff_t, off, cap_pad, *, group=8):
    bsz, n, d = h.shape
    ne = slot.shape[1]
    grid_spec = pltpu.PrefetchScalarGridSpec(
        num_scalar_prefetch=1,
        grid=(bsz, ne // group, n // ROUTE_TB),
        in_specs=[
            pl.BlockSpec((1, ROUTE_TB, d), lambda b, g, t, off: (b, t, 0)),
            pl.BlockSpec((1, group, ROUTE_TB), lambda b, g, t, off: (b, g, t)),
            pl.BlockSpec((1, group, ROUTE_TB), lambda b, g, t, off: (b, g, t)),
        ],
        out_specs=[pl.BlockSpec((1, group, cap_pad, d), lambda b, g, t, off: (b, g, 0, 0)),
                   pl.BlockSpec((1, group, cap_pad, 1), lambda b, g, t, off: (b, g, 0, 0))],
    )
    return pl.pallas_call(
        functools.partial(_gather_kernel, group=group),
        grid_spec=grid_spec,
        out_shape=[jax.ShapeDtypeStruct((bsz, ne, cap_pad, d), BF16),
                   jax.ShapeDtypeStruct((bsz, ne, cap_pad, 1), F32)],
        compiler_params=_params("arbitrary", "arbitrary", "arbitrary"),
        name="expert_gather",
    )(off, h, slot, aff_t)


def _combine_kernel(off_ref, y_ref, slot_t_ref, x_ref, gt_ref, fg_ref, o_ref, stage_ref, acc_ref):
    b, tb = pl.program_id(0), pl.program_id(1)
    ne = y_ref.shape[1]
    rows_max = y_ref.shape[2] - ROUTE_WIN
    width = ne * ROUTE_WIN
    experts = list(range(ne))
    lane = lax.broadcasted_iota(jnp.int32, (1, width), 1)
    s1 = slot_t_ref[0] + 1
    hi = (s1 // ROUTE_WIN).astype(F32).astype(BF16)
    lo = (s1 % ROUTE_WIN).astype(F32).astype(BF16)
    e_row = lax.broadcasted_iota(jnp.int32, (ne, width), 0)
    e_lane = lax.broadcasted_iota(jnp.int32, (ne, width), 1) // ROUTE_WIN
    expand = jnp.where(e_row == e_lane, 1.0, 0.0).astype(BF16)
    s1_wide = (float(ROUTE_WIN) * jnp.dot(hi, expand, preferred_element_type=F32)
               + jnp.dot(lo, expand, preferred_element_type=F32))
    acc_ref[...] = jnp.zeros(acc_ref.shape, F32)

    def one_pass(w, carry):
        want = jnp.full((1, width), -1, jnp.int32)
        for e in experts:
            rows, first, active = _window_plan(off_ref, b, e, tb, w, rows_max)
            stage_ref[e * ROUTE_WIN:(e + 1) * ROUTE_WIN, :] = y_ref[0, e, pl.ds(rows, ROUTE_WIN), :]
            ids = rows + lane % ROUTE_WIN
            want = jnp.where((lane // ROUTE_WIN == e) & active & (ids >= first), ids + 1, want)
        onehot = jnp.where(s1_wide == want.astype(F32), 1.0, 0.0).astype(BF16)
        acc_ref[...] += jnp.dot(onehot, stage_ref[...], preferred_element_type=F32)
        return carry

    lax.fori_loop(0, _num_passes(off_ref, b, experts, tb), one_pass, 0)
    out = x_ref[0] + gt_ref[0] * acc_ref[...]
    if fg_ref is not None:
        out = out * lax.rsqrt(jnp.mean(out * out, axis=-1, keepdims=True) + EPS) * fg_ref[...]
    o_ref[0] = out


def _combine_kernel_plain(off_ref, y_ref, slot_t_ref, x_ref, gt_ref, o_ref, stage_ref, acc_ref):
    _combine_kernel(off_ref, y_ref, slot_t_ref, x_ref, gt_ref, None, o_ref, stage_ref, acc_ref)


def expert_combine(y, slot_t, off, x, gt, final_g=None):
    bsz, n, d = x.shape
    ne, cap_pad = y.shape[1], y.shape[2]
    grid_spec = pltpu.PrefetchScalarGridSpec(
        num_scalar_prefetch=1,
        grid=(bsz, n // ROUTE_TB),
        in_specs=[
            pl.BlockSpec((1, ne, cap_pad, d), lambda b, t, off: (b, 0, 0, 0), pipeline_mode=pl.Buffered(1)),
            pl.BlockSpec((1, ROUTE_TB, ne), lambda b, t, off: (b, t, 0)),
            pl.BlockSpec((1, ROUTE_TB, d), lambda b, t, off: (b, t, 0)),
            pl.BlockSpec((1, 1, d), lambda b, t, off: (b, 0, 0)),
        ] + ([] if final_g is None else [pl.BlockSpec((1, d), lambda b, t, off: (0, 0))]),
        out_specs=pl.BlockSpec((1, ROUTE_TB, d), lambda b, t, off: (b, t, 0)),
        scratch_shapes=[pltpu.VMEM((ne * ROUTE_WIN, d), BF16), pltpu.VMEM((ROUTE_TB, d), F32)],
    )
    args = (off, y, slot_t, x, gt) + (() if final_g is None else (final_g[None, :],))
    return pl.pallas_call(
        _combine_kernel_plain if final_g is None else _combine_kernel,
        grid_spec=grid_spec,
        out_shape=jax.ShapeDtypeStruct((bsz, n, d), F32),
        compiler_params=_params("arbitrary", "arbitrary"),
        name="expert_combine",
    )(*args)


FFN_FBLK = 512
FFN_MIN_ROWS = 256


def _expert_ffn_kernel(*refs, n_sets):
    xe_refs, gate_refs = refs[:n_sets], refs[n_sets:2 * n_sets]
    wg_ref, wu_ref, wd_ref = refs[2 * n_sets:2 * n_sets + 3]
    o_refs = refs[2 * n_sets + 3:3 * n_sets + 3]
    acc_refs = refs[3 * n_sets + 3:]
    f = pl.program_id(1)
    wg = wg_ref[0].astype(BF16)
    wu = wu_ref[0].astype(BF16)
    wd = wd_ref[0].astype(BF16)

    @pl.when(f == 0)
    def _():
        for acc in acc_refs:
            acc[...] = jnp.zeros(acc.shape, F32)

    blocks = [(s, b) for s in range(n_sets) for b in range(xe_refs[s].shape[0])]
    big = [sb for sb in blocks if xe_refs[sb[0]].shape[2] >= FFN_MIN_ROWS]
    small = [sb for sb in blocks if sb not in big]
    groups = [[sb] for sb in big[:-1]] + [big[-1:] + small] if big else [small]
    for group in groups:
        xs = [xe_refs[s][b, 0] for s, b in group]
        xe = xs[0] if len(xs) == 1 else jnp.concatenate(xs, axis=0)
        a = jnp.dot(xe, wg, preferred_element_type=F32)
        u = jnp.dot(xe, wu, preferred_element_type=F32)
        t = jnp.dot((jax.nn.silu(a) * u).astype(BF16), wd, preferred_element_type=F32)
        r0 = 0
        for s, b in group:
            rows = xe_refs[s].shape[2]
            acc_refs[s][b] += t[r0:r0 + rows]
            r0 += rows

    @pl.when(f == pl.num_programs(1) - 1)
    def _():
        for s in range(n_sets):
            for b in range(xe_refs[s].shape[0]):
                o_refs[s][b, 0] = (acc_refs[s][b] * gate_refs[s][b, 0]).astype(o_refs[s].dtype)


def expert_ffn(xes, gates, wg, wu, wd, l):
    n_sets = len(xes)
    _, ne, d, ff = wg.shape
    xe_specs = [pl.BlockSpec((xe.shape[0], 1) + xe.shape[2:], lambda e, f: (0, e, 0, 0)) for xe in xes]
    gate_specs = [pl.BlockSpec((g.shape[0], 1) + g.shape[2:], lambda e, f: (0, e, 0, 0)) for g in gates]
    return pl.pallas_call(
        functools.partial(_expert_ffn_kernel, n_sets=n_sets),
        grid=(ne, ff // FFN_FBLK),
        in_specs=xe_specs + gate_specs + [
            pl.BlockSpec((None, 1, d, FFN_FBLK), lambda e, f: (l, e, 0, f)),
            pl.BlockSpec((None, 1, d, FFN_FBLK), lambda e, f: (l, e, 0, f)),
            pl.BlockSpec((None, 1, FFN_FBLK, d), lambda e, f: (l, e, f, 0)),
        ],
        out_specs=xe_specs,
        out_shape=[jax.ShapeDtypeStruct(xe.shape, BF16) for xe in xes],
        scratch_shapes=[pltpu.VMEM((xe.shape[0],) + xe.shape[2:], F32) for xe in xes],
        compiler_params=_params("arbitrary", "arbitrary"),
        name="expert_ffn",
    )(*xes, *gates, wg, wu, wd)


ATT_QROWS = 4
ATT_QTOK = ATT_QROWS * GRID_W
ATT_KROWS = ATT_QROWS + WIN_ROWS
ATT_KBLK = ATT_KROWS * GRID_W // ATT_QTOK
_NT = (((1,), (1,)), ((), ()))


def _rope_tables(n):
    quarter = HEAD_DIM // 4
    inv = ROPE_BASE ** (-np.arange(quarter, dtype=np.float64) / quarter)
    t = np.arange(n)
    ang_r = (t // GRID_W)[:, None] * inv[None, :]
    ang_c = (t % GRID_W)[:, None] * inv[None, :]
    cos = np.concatenate([np.cos(ang_r)] * 2 + [np.cos(ang_c)] * 2, axis=1)
    sin = np.concatenate([-np.sin(ang_r), np.sin(ang_r), -np.sin(ang_c), np.sin(ang_c)], axis=1)
    return (jnp.asarray(np.concatenate([cos, cos], axis=1), F32),
            jnp.asarray(np.concatenate([sin, sin], axis=1), F32))


def _attn_bias_tables(rpb, n_rows):
    n_blk = n_rows // ATT_QROWS
    n_dr, n_dc = 2 * WIN_ROWS - 1, 2 * WIN_COLS - 1
    qc = np.arange(GRID_W)
    kc = np.arange(GRID_W)
    c_start = np.clip(qc - WIN_COLS // 2, 0, GRID_W - WIN_COLS)
    col_ok = (kc[None, :] >= c_start[:, None]) & (kc[None, :] < c_start[:, None] + WIN_COLS)
    d_col = np.clip(kc[None, :] - qc[:, None], 1 - WIN_COLS, WIN_COLS - 1) + (WIN_COLS - 1)
    col_sel = (d_col.reshape(-1)[None, :] == np.arange(n_dc)[:, None]).astype(np.float32)
    row_sel, oks = [], []
    for j in (0, 1, n_blk - 1):
        ks = min(max(ATT_QROWS * j - WIN_ROWS // 2, 0), n_rows - ATT_KROWS)
        r = ATT_QROWS * j + np.arange(ATT_QROWS)
        kr0 = np.clip(r - WIN_ROWS // 2, 0, n_rows - WIN_ROWS)
        krow = ks + np.arange(ATT_KROWS)
        row_ok = (krow[None, :] >= kr0[:, None]) & (krow[None, :] < kr0[:, None] + WIN_ROWS)
        d_row = np.clip(krow[None, :] - r[:, None] + (WIN_ROWS - 1), 0, n_dr - 1)
        row_sel.append((d_row.reshape(-1)[:, None] == np.arange(n_dr)[None, :]).astype(np.float32))
        oks.append(row_ok[:, None, :, None] & col_ok[None, :, None, :])
    hp = lax.Precision.HIGHEST
    a = jnp.einsum('cpr,hrd->chpd', jnp.asarray(np.stack(row_sel)), rpb.astype(F32), precision=hp)
    t = jnp.einsum('chpd,dx->chpx', a, jnp.asarray(col_sel), precision=hp)
    t = t.reshape(3, N_GROUPS, ATT_QROWS, ATT_KROWS, GRID_W, GRID_W).transpose(0, 1, 2, 4, 3, 5)
    ok = jnp.asarray(np.stack(oks))[:, None]
    return jnp.where(ok, t, NEG_INF).reshape(3, N_GROUPS, ATT_QTOK, ATT_KROWS * GRID_W)


def _rope(x, cos_ref, sin_ref, first16):
    c = cos_ref[...]
    s = sin_ref[...]
    c2 = jnp.concatenate([c, c], axis=1)
    s2 = jnp.concatenate([s, s], axis=1)
    w = x.shape[1]
    partner = jnp.where(first16, pltpu.roll(x, w - HEAD_DIM // 4, 1), pltpu.roll(x, HEAD_DIM // 4, 1))
    return x * c2 + partner * s2


def _softmax_pv(scores_values, head_mask, acc):
    m = None
    for s, _ in scores_values:
        mx = jnp.max(s, axis=1, keepdims=True)
        m = mx if m is None else jnp.maximum(m, mx)
    den = None
    o = None
    for s, v in scores_values:
        p = jnp.exp(s - m)
        sm = jnp.sum(p, axis=1, keepdims=True)
        den = sm if den is None else den + sm
        t = jnp.dot(p.astype(BF16), v, preferred_element_type=F32)
        o = t if o is None else o + t
    return acc + jnp.where(head_mask, o * (1.0 / den), 0.0)


def _nattn_kernel(q_ref, k0_ref, k1_ref, k2_ref, v0_ref, v1_ref, v2_ref,
                  cq_ref, sq_ref, ck0_ref, ck1_ref, ck2_ref, sk0_ref, sk1_ref, sk2_ref,
                  kc_ref, vc_ref, bias_ref, o_ref):
    w = q_ref.shape[2]
    lane = lax.broadcasted_iota(jnp.int32, (1, w), 1)
    first16 = (lane % (HEAD_DIM // 2)) < (HEAD_DIM // 4)
    q = q_ref[0].astype(F32) * (HEAD_DIM ** -0.5)
    q_rot = _rope(q, cq_ref, sq_ref, first16)
    k_rot = jnp.concatenate([_rope(k0_ref[0].astype(F32), ck0_ref, sk0_ref, first16),
                             _rope(k1_ref[0].astype(F32), ck1_ref, sk1_ref, first16),
                             _rope(k2_ref[0].astype(F32), ck2_ref, sk2_ref, first16)], axis=0).astype(BF16)
    v = jnp.concatenate([v0_ref[0], v1_ref[0], v2_ref[0]], axis=0).astype(BF16)
    kc = kc_ref[0].astype(BF16)
    vc = vc_ref[0].astype(BF16)
    acc = jnp.zeros(q.shape, F32)
    for h in range(N_GROUPS):
        hm = (lane // HEAD_DIM) == h
        s_win = lax.dot_general(jnp.where(hm, q_rot, 0.0).astype(BF16), k_rot, _NT,
                                preferred_element_type=F32) + bias_ref[0, h]
        s_ctx = lax.dot_general(jnp.where(hm, q, 0.0).astype(BF16), kc, _NT, preferred_element_type=F32)
        acc = _softmax_pv([(s_win, v), (s_ctx, vc)], hm, acc)
    o_ref[0] = acc.astype(o_ref.dtype)


def neighbourhood_attention(z, zc, rpb, *, q_col, k_col, v_col, kc_col, vc_col):
    bsz, n, _ = z.shape
    n_ctx = zc.shape[1]
    w = BRANCH_WIDTH
    n_blk = n // ATT_QTOK
    cos_t, sin_t = _rope_tables(n)
    bias = _attn_bias_tables(rpb, n // GRID_W)

    def kb(j):
        return jnp.clip(j - 1, 0, n_blk - ATT_KBLK)

    def zspec(col, off=None):
        if off is None:
            return pl.BlockSpec((1, ATT_QTOK, w), lambda b, j: (b, j, col))
        return pl.BlockSpec((1, ATT_QTOK, w), lambda b, j: (b, kb(j) + off, col))

    def tspec(off=None):
        if off is None:
            return pl.BlockSpec((ATT_QTOK, 2 * HEAD_DIM), lambda b, j: (j, 0))
        return pl.BlockSpec((ATT_QTOK, 2 * HEAD_DIM), lambda b, j: (kb(j) + off, 0))

    in_specs = ([zspec(q_col)] + [zspec(k_col, i) for i in range(ATT_KBLK)] + [zspec(v_col, i) for i in range(ATT_KBLK)]
                + [tspec(), tspec()] + [tspec(i) for i in range(ATT_KBLK)] * 2
                + [pl.BlockSpec((1, n_ctx, w), lambda b, j: (b, 0, kc_col)),
                   pl.BlockSpec((1, n_ctx, w), lambda b, j: (b, 0, vc_col)),
                   pl.BlockSpec((1, N_GROUPS, ATT_QTOK, ATT_KROWS * GRID_W),
                                lambda b, j: (jnp.minimum(j, 1) + j // (n_blk - 1), 0, 0, 0))])
    return pl.pallas_call(
        _nattn_kernel,
        grid=(bsz, n_blk),
        in_specs=in_specs,
        out_specs=pl.BlockSpec((1, ATT_QTOK, w), lambda b, j: (b, j, 0)),
        out_shape=jax.ShapeDtypeStruct((bsz, n, w), BF16),
        compiler_params=_params("arbitrary", "arbitrary"),
        name="neighbourhood_attention",
    )(z, z, z, z, z, z, z, cos_t, sin_t, cos_t, cos_t, cos_t, sin_t, sin_t, sin_t, zc, zc, bias)


def _ctx_attn_kernel(q_ref, kc_ref, vc_ref, o_ref):
    w = q_ref.shape[2]
    lane = lax.broadcasted_iota(jnp.int32, (1, w), 1)
    q = q_ref[0].astype(F32) * (HEAD_DIM ** -0.5)
    kc = kc_ref[0].astype(BF16)
    vc = vc_ref[0].astype(BF16)
    acc = jnp.zeros(q.shape, F32)
    for h in range(N_GROUPS):
        hm = (lane // HEAD_DIM) == h
        s = lax.dot_general(jnp.where(hm, q, 0.0).astype(BF16), kc, _NT, preferred_element_type=F32)
        acc = _softmax_pv([(s, vc)], hm, acc)
    o_ref[0] = acc.astype(o_ref.dtype)


def context_attention(zc, *, q_col, k_col, v_col):
    bsz, n_ctx, _ = zc.shape
    w = BRANCH_WIDTH
    return pl.pallas_call(
        _ctx_attn_kernel,
        grid=(bsz,),
        in_specs=[pl.BlockSpec((1, n_ctx, w), functools.partial(lambda b, c: (b, 0, c), c=c))
                  for c in (q_col, k_col, v_col)],
        out_specs=pl.BlockSpec((1, n_ctx, w), lambda b: (b, 0, 0)),
        out_shape=jax.ShapeDtypeStruct((bsz, n_ctx, w), BF16),
        compiler_params=_params("arbitrary"),
        name="context_attention",
    )(zc, zc, zc)


def _ln(v, g, b):
    mu = jnp.mean(v, axis=-1, keepdims=True)
    var = jnp.mean(jnp.square(v - mu), axis=-1, keepdims=True)
    return (v - mu) * lax.rsqrt(var + EPS) * g + b


def _gmlp_kernel(z_ref, lng_ref, lnb_ref, w_ref, bias_ref, o_ref):
    tm = z_ref.shape[1]
    z = jax.nn.gelu(z_ref[0])
    u = z[:, :BRANCH_WIDTH]
    v = _ln(z[:, BRANCH_WIDTH:], lng_ref[...], lnb_ref[...])
    group = lax.broadcasted_iota(jnp.int32, (1, BRANCH_WIDTH), 1) // HEAD_DIM
    wcat = w_ref[...]
    for c in range(tm // CHUNK):
        rows = slice(c * CHUNK, (c + 1) * CHUNK)
        vc = v[rows]
        vst = jnp.concatenate([jnp.where(group == g, vc, 0.0) for g in range(N_GROUPS)], axis=0).astype(BF16)
        mixed = jnp.dot(wcat, vst, preferred_element_type=F32) + bias_ref[...]
        o_ref[0, rows, :] = (u[rows] * mixed).astype(o_ref.dtype)


def chunk_gmlp(za, ln_g, ln_b, w_s, b_s, *, tm):
    bsz, n, w2 = za.shape
    w = w2 // 2
    wcat = jnp.transpose(w_s, (1, 0, 2)).reshape(CHUNK, N_GROUPS * CHUNK).astype(BF16)
    bias = jnp.repeat(b_s.T, HEAD_DIM, axis=1)
    return pl.pallas_call(
        _gmlp_kernel,
        grid=(bsz, n // tm),
        in_specs=[
            pl.BlockSpec((1, tm, w2), lambda b, i: (b, i, 0)),
            pl.BlockSpec((1, w), lambda b, i: (0, 0)),
            pl.BlockSpec((1, w), lambda b, i: (0, 0)),
            pl.BlockSpec((CHUNK, N_GROUPS * CHUNK), lambda b, i: (0, 0)),
            pl.BlockSpec((CHUNK, w), lambda b, i: (0, 0)),
        ],
        out_specs=pl.BlockSpec((1, tm, w), lambda b, i: (b, i, 0)),
        out_shape=jax.ShapeDtypeStruct((bsz, n, w), BF16),
        compiler_params=_params("arbitrary", "arbitrary"),
        name="chunk_gmlp",
    )(za, ln_g[None, :], ln_b[None, :], wcat, bias)


CONV_HALO = 16
CONV_SUB = 128
SUBLANES = 8
LANES = 128


def _conv_kernel(a_ref, g_ref, ap_ref, gp_ref, an_ref, gn_ref, w_ref, cb_ref, lng_ref, lnb_ref, o_ref, y_ref):
    i = pl.program_id(1)
    tm = a_ref.shape[1]

    def glu(a, g):
        return a * jax.nn.sigmoid(g)

    y_ref[0:CONV_HALO, :] = jnp.where(i > 0, glu(ap_ref[0], gp_ref[0]), 0.0)
    y_ref[CONV_HALO:CONV_HALO + tm, :] = glu(a_ref[0], g_ref[0])
    y_ref[CONV_HALO + tm:, :] = jnp.where(i < pl.num_programs(1) - 1, glu(an_ref[0], gn_ref[0]), 0.0)
    first = CONV_HALO - CONV_WIDTH // 2
    nb = CONV_SUB + 2 * CONV_HALO
    for r in range(0, tm, CONV_SUB):
        blk = y_ref[r:r + nb, :]
        acc = None
        for res in range(SUBLANES):
            rot = blk if res == 0 else pltpu.roll(blk, nb - res, 0)
            for j in range(CONV_WIDTH):
                if (first + j) % SUBLANES == res:
                    a0 = first + j - res
                    t = rot[a0:a0 + CONV_SUB, :] * w_ref[j:j + 1, :]
                    acc = t if acc is None else acc + t
        y = _ln(acc + cb_ref[...], lng_ref[...], lnb_ref[...])
        o_ref[0, r:r + CONV_SUB, :] = jax.nn.silu(y).astype(o_ref.dtype)


def conformer_conv(zcv, conv_w, conv_b, ln_g, ln_b, *, tm):
    bsz, n, w2 = zcv.shape
    w = w2 // 2
    hb = tm // CONV_HALO
    n_hb = n // CONV_HALO

    def main(col):
        return pl.BlockSpec((1, tm, w), lambda b, i: (b, i, col))

    def prev(col):
        return pl.BlockSpec((1, CONV_HALO, w), lambda b, i: (b, jnp.maximum(i * hb - 1, 0), col))

    def nxt(col):
        return pl.BlockSpec((1, CONV_HALO, w), lambda b, i: (b, jnp.minimum((i + 1) * hb, n_hb - 1), col))

    vec = pl.BlockSpec((1, w), lambda b, i: (0, 0))
    return pl.pallas_call(
        _conv_kernel,
        grid=(bsz, n // tm),
        in_specs=[main(0), main(1), prev(0), prev(1), nxt(0), nxt(1),
                  pl.BlockSpec((CONV_WIDTH, w), lambda b, i: (0, 0)), vec, vec, vec],
        out_specs=pl.BlockSpec((1, tm, w), lambda b, i: (b, i, 0)),
        out_shape=jax.ShapeDtypeStruct((bsz, n, w), BF16),
        scratch_shapes=[pltpu.VMEM((tm + 2 * CONV_HALO, w), F32)],
        compiler_params=_params("arbitrary", "arbitrary"),
        name="conformer_conv",
    )(zcv, zcv, zcv, zcv, zcv, zcv, conv_w, conv_b[None, :], ln_g[None, :], ln_b[None, :])


FOURIER_N2 = 128
FOURIER_UNROLL = 8
FOURIER_KB = 8


def _np_split(m):
    m = jnp.asarray(m, F32)
    hi = m.astype(BF16)
    return hi, (m - hi.astype(F32)).astype(BF16)


def _channel_dft_matrix():
    c = np.arange(HEAD_DIM)
    ang = 2.0 * np.pi * ((c[:, None] * c[None, :]) % HEAD_DIM) / HEAD_DIM
    eye = np.eye(N_GROUPS)
    return np.concatenate([np.kron(eye, np.cos(ang)), np.kron(eye, np.sin(ang))], axis=0)


def _fourier_stage1_kernel(xa_ref, xb_ref, mh_ref, ml_ref, o_ref, *, n1, n2):
    def one_fast_index(f, carry):
        rows = pl.ds(f, n1, stride=n2)
        x = jnp.concatenate([xa_ref[0, rows, :], xb_ref[0, rows, :]], axis=1)
        x_hi, x_lo = _split_bf16(x)
        res = _dot3(mh_ref[0, f], ml_ref[0, f], x_hi, x_lo)
        o_ref[0, 0, 0, rows, :] = res[:, :LANES]
        o_ref[0, 0, 1, rows, :] = res[:, LANES:]
        return carry

    lax.fori_loop(0, n2, one_fast_index, 0, unroll=FOURIER_UNROLL)


def _fourier_stage2_kernel(br_ref, bi_ref, m2h_ref, m2l_ref, mdh_ref, mdl_ref, o_ref, *, scale):
    for k in range(br_ref.shape[3]):
        b_re = jnp.concatenate([br_ref[0, 0, 0, k], br_ref[0, 0, 1, k]], axis=1)
        b_im = jnp.concatenate([bi_ref[0, 0, 0, k], bi_ref[0, 0, 1, k]], axis=1)
        b_hi, b_lo = _split_bf16(jnp.concatenate([b_re, b_im], axis=0))
        xs = _dot3(m2h_ref[...], m2l_ref[...], b_hi, b_lo)
        x_hi, x_lo = _split_bf16(jnp.concatenate([xs[:FOURIER_N2], xs[FOURIER_N2:]], axis=1))
        o_ref[0, :, k, :] = _dot3(x_hi, x_lo, mdh_ref[...], mdl_ref[...]) * scale


def _fourier_direct_kernel(x_ref, mph_ref, mpl_ref, mdh_ref, mdl_ref, o_ref, *, scale):
    n = x_ref.shape[1]
    x_hi, x_lo = _split_bf16(x_ref[0])
    p = _dot3(mph_ref[...], mpl_ref[...], x_hi, x_lo)
    p_hi, p_lo = _split_bf16(jnp.concatenate([p[:n], p[n:]], axis=1))
    o_ref[0] = (_dot3(p_hi, p_lo, mdh_ref[...], mdl_ref[...]) * scale).astype(o_ref.dtype)


def fourier_mix(zb):
    bsz, n, w = zb.shape
    scale = float(1.0 / np.sqrt(n * HEAD_DIM))
    mdh, mdl = _np_split(_channel_dft_matrix())
    md_spec2 = pl.BlockSpec((2 * w, w), lambda b, j: (0, 0))
    if n <= 2 * FOURIER_N2:
        t = np.arange(n)
        ang = 2.0 * np.pi * ((t[:, None] * t[None, :]) % n) / n
        mph, mpl = _np_split(np.concatenate([np.cos(ang), -np.sin(ang)], axis=0))
        return pl.pallas_call(
            functools.partial(_fourier_direct_kernel, scale=scale),
            grid=(bsz,),
            in_specs=[pl.BlockSpec((1, n, w), lambda b: (b, 0, 0)),
                      pl.BlockSpec((2 * n, n), lambda b: (0, 0)), pl.BlockSpec((2 * n, n), lambda b: (0, 0)),
                      pl.BlockSpec((2 * w, w), lambda b: (0, 0)), pl.BlockSpec((2 * w, w), lambda b: (0, 0))],
            out_specs=pl.BlockSpec((1, n, w), lambda b: (b, 0, 0)),
            out_shape=jax.ShapeDtypeStruct((bsz, n, w), BF16),
            compiler_params=_params("arbitrary"),
            name="fourier_direct",
        )(zb, mph, mpl, mdh, mdl)

    n1, n2 = n // FOURIER_N2, FOURIER_N2
    f, k1, s = np.arange(n2), np.arange(n1), np.arange(n1)
    ang1 = 2.0 * np.pi * ((k1[None, :, None] * (f[:, None, None] + n2 * s[None, None, :])) % n) / n
    m1h, m1l = _np_split(np.stack([np.cos(ang1), -np.sin(ang1)]))
    k2 = np.arange(n2)
    ang2 = 2.0 * np.pi * ((k2[:, None] * f[None, :]) % n2) / n2
    c2, s2 = np.cos(ang2), np.sin(ang2)
    m2h, m2l = _np_split(np.block([[c2, s2], [-s2, c2]]))

    m1_spec = pl.BlockSpec((1, n2, n1, n1), lambda b, p: (p, 0, 0, 0))
    n_half = w // LANES
    b_st = pl.pallas_call(
        functools.partial(_fourier_stage1_kernel, n1=n1, n2=n2),
        grid=(bsz, 2),
        in_specs=[pl.BlockSpec((1, n, LANES), lambda b, p: (b, 0, 0)),
                  pl.BlockSpec((1, n, LANES), lambda b, p: (b, 0, 1)), m1_spec, m1_spec],
        out_specs=pl.BlockSpec((1, 1, n_half, n, LANES), lambda b, p: (b, p, 0, 0, 0)),
        out_shape=jax.ShapeDtypeStruct((bsz, 2, n_half, n, LANES), F32),
        compiler_params=_params("arbitrary", "arbitrary"),
        name="fourier_stage1",
    )(zb, zb, m1h, m1l)

    kb = FOURIER_KB
    b_st = b_st.reshape(bsz, 2, n_half, n1, n2, LANES)
    out = pl.pallas_call(
        functools.partial(_fourier_stage2_kernel, scale=scale),
        grid=(bsz, n1 // kb),
        in_specs=[pl.BlockSpec((1, 1, n_half, kb, n2, LANES), lambda b, j: (b, 0, 0, j, 0, 0)),
                  pl.BlockSpec((1, 1, n_half, kb, n2, LANES), lambda b, j: (b, 1, 0, j, 0, 0)),
                  pl.BlockSpec((2 * n2, 2 * n2), lambda b, j: (0, 0)),
                  pl.BlockSpec((2 * n2, 2 * n2), lambda b, j: (0, 0)),
                  md_spec2, md_spec2],
        out_specs=pl.BlockSpec((1, n2, kb, w), lambda b, j: (b, 0, j, 0)),
        out_shape=jax.ShapeDtypeStruct((bsz, n2, n1, w), F32),
        compiler_params=_params("arbitrary", "arbitrary"),
        name="fourier_stage2",
    )(b_st, b_st, m2h, m2l, mdh, mdl)
    return out.reshape(bsz, n, w)


def expert_route(h, aff_t):
    n = h.shape[1]
    cap = EC_CAPACITY * n // N_EXPERTS
    slot, off = expert_topk(aff_t, cap)
    xe, gate = expert_gather(h, slot, aff_t, off, max(cap, ROUTE_WIN))
    return xe, gate, jnp.swapaxes(slot, 1, 2), off


def kernel(x, c, ctx, c_ctx, w_mod, b_mod, norm1_g, norm2_g, w_in, sgu_ln_g, sgu_ln_b, w_spatial, b_spatial,
           w_a_out, w_b_out, conv_w, conv_b, conv_ln_g, conv_ln_b, w_c_out, rpb, w_d_out, w_out, w_router,
           w_gate_e, w_up_e, w_down_e, final_norm_g):
    bsz = x.shape[0]
    xc = ctx
    assert bsz + 1 <= MOD_ROWS
    cond = jnp.concatenate([c, c_ctx[None, :], jnp.zeros((MOD_ROWS - bsz - 1, D_MODEL), F32)], axis=0)
    mod_all = modulation(cond, w_mod, b_mod)
    w_in_b = w_in.astype(BF16)
    for l in range(DEPTH):
        last = l == DEPTH - 1
        sh1, sc1, gt1, sh2, sc2, gt2 = jnp.split(mod_all[l, :bsz, None, :], 6, axis=-1)
        cmod = jnp.broadcast_to(mod_all[l, bsz:bsz + 1, None, :], (bsz, 1, 6 * D_MODEL))
        csh1, csc1, cgt1, csh2, csc2, cgt2 = jnp.split(cmod, 6, axis=-1)

        wa, wb, wc, wd_, wo = (w.astype(BF16) for w in (w_a_out[l], w_b_out[l], w_c_out[l], w_d_out[l], w_out[l]))
        g1 = norm1_g[l][None, :]

        za, zb, zcv, zqkv, zg = norm_inproj(x, g1, sc1, sh1, w_in_b, l, 0, IN_WIDTHS, IN_DTYPES, tm=512)
        if last:
            (zqkv_c,) = norm_inproj(xc, g1, csc1, csh1, w_in_b, l, Q_END, (2 * BRANCH_WIDTH,), (BF16,), tm=CTX_TM)
            kc_col, vc_col = 0, 1
        else:
            cza, czb, czcv, zqkv_c, czg = norm_inproj(xc, g1, csc1, csh1, w_in_b, l, 0, IN_WIDTHS, IN_DTYPES,
                                                      tm=CTX_TM)
            kc_col, vc_col = 1, 2
        d_lat = neighbourhood_attention(zqkv, zqkv_c, rpb[l], q_col=0, k_col=1, v_col=2,
                                        kc_col=kc_col, vc_col=vc_col)
        a_lat = chunk_gmlp(za, sgu_ln_g[l], sgu_ln_b[l], w_spatial[l], b_spatial[l], tm=512)
        b_lat = fourier_mix(zb)
        c_lat = conformer_conv(zcv, conv_w[l], conv_b[l], conv_ln_g[l], conv_ln_b[l], tm=512)
        g2 = norm2_g[l][None, :]
        router_t = w_router[l].T
        x, h2, aff_t = merge_branches(a_lat, b_lat, c_lat, d_lat, zg, wa, wb, wc, wd_, wo, x, gt1,
                                      g2, sc2, sh2, router_t, tm=512)
        if not last:
            a_c = chunk_gmlp(cza, sgu_ln_g[l], sgu_ln_b[l], w_spatial[l], b_spatial[l], tm=CTX_TM)
            b_c = fourier_mix(czb)
            c_c = conformer_conv(czcv, conv_w[l], conv_b[l], conv_ln_g[l], conv_ln_b[l], tm=CTX_TM)
            d_c = context_attention(zqkv_c, q_col=0, k_col=1, v_col=2)
            xc, hc2, aff_tc = merge_branches(a_c, b_c, c_c, d_c, czg, wa, wb, wc, wd_, wo, xc, cgt1,
                                             g2, csc2, csh2, router_t, tm=CTX_TM)

        xe, gate, slot_t, off = expert_route(h2, aff_t)
        if last:
            (y,) = expert_ffn([xe], [gate], w_gate_e, w_up_e, w_down_e, l)
        else:
            xe_c, gate_c, slot_tc, off_c = expert_route(hc2, aff_tc)
            y, y_c = expert_ffn([xe, xe_c], [gate, gate_c], w_gate_e, w_up_e, w_down_e, l)
            xc = expert_combine(y_c, slot_tc, off_c, xc, cgt2)
        x = expert_combine(y, slot_t, off, x, gt2, final_norm_g if last else None)
    return x


CTX_TM = 256
```

```python
import functools

import jax
import jax.numpy as jnp
import numpy as np
from jax import lax
from jax.experimental import pallas as pl
from jax.experimental.pallas import tpu as pltpu

D_MODEL = 1024
DEPTH = 2
GRID_W = 64
HEAD_DIM = 64
N_GROUPS = 4
BRANCH_WIDTH = N_GROUPS * HEAD_DIM
N_BRANCH = 4
CHUNK = 128
CONV_WIDTH = 31
WIN_ROWS = 8
WIN_COLS = 16
QB_COLS = 16
KB_COLS = QB_COLS + WIN_COLS
ROPE_BASE = 10000.0
N_EXPERTS = 16
EXPERT_FF = 1024
EC_CAPACITY = 2
EPS = 1e-6
NEG_INF = -1e30
A_END = 2 * BRANCH_WIDTH
B_END = A_END + BRANCH_WIDTH
C_END = B_END + 2 * BRANCH_WIDTH
Q_END = C_END + BRANCH_WIDTH
K_END = Q_END + BRANCH_WIDTH
V_END = K_END + BRANCH_WIDTH
IN_COLS = V_END + N_BRANCH * D_MODEL
SPLITS = [A_END, B_END, C_END, Q_END, K_END, V_END]
IN_WIDTHS = (A_END, B_END - A_END, C_END - B_END, V_END - C_END, IN_COLS - V_END)
IN_DTYPES = (jnp.float32, jnp.float32, jnp.float32, jnp.bfloat16, jnp.bfloat16)

VMEM_LIMIT_BYTES = 56 * 1024 * 1024
F32 = jnp.float32
BF16 = jnp.bfloat16


def _sigmoid(v):
    return 0.5 * jnp.tanh(0.5 * v) + 0.5


def _silu(v):
    return v * _sigmoid(v)


def _params(*sem):
    return pltpu.CompilerParams(dimension_semantics=sem, vmem_limit_bytes=VMEM_LIMIT_BYTES)


MOD_ROWS = 8
MOD_TN = 1536


def _split_bf16(v):
    hi = v.astype(BF16)
    return hi, (v - hi.astype(F32)).astype(BF16)


def _dot3(a_hi, a_lo, b_hi, b_lo):
    return (jnp.dot(a_hi, b_hi, preferred_element_type=F32) + jnp.dot(a_hi, b_lo, preferred_element_type=F32)
            + jnp.dot(a_lo, b_hi, preferred_element_type=F32))


def _mod_kernel(c_ref, w_ref, b_ref, o_ref):
    s_hi, s_lo = _split_bf16(_silu(c_ref[...]))
    w_hi, w_lo = _split_bf16(w_ref[0])
    o_ref[0] = _dot3(s_hi, s_lo, w_hi, w_lo) + b_ref[0]


def modulation(cond, w_mod, b_mod):
    n_layers, d, cols = w_mod.shape
    return pl.pallas_call(
        _mod_kernel,
        grid=(n_layers, cols // MOD_TN),
        in_specs=[pl.BlockSpec((MOD_ROWS, d), lambda l, j: (0, 0)),
                  pl.BlockSpec((1, d, MOD_TN), lambda l, j: (l, 0, j)),
                  pl.BlockSpec((1, 1, MOD_TN), lambda l, j: (l, 0, j))],
        out_specs=pl.BlockSpec((1, MOD_ROWS, MOD_TN), lambda l, j: (l, 0, j)),
        out_shape=jax.ShapeDtypeStruct((n_layers, MOD_ROWS, cols), F32),
        compiler_params=_params("arbitrary", "arbitrary"),
        name="modulation",
    )(cond, w_mod, b_mod[:, None, :])


INPROJ_COL_CHUNK = 512


def _norm_inproj_kernel(x_ref, g_ref, sc_ref, sh_ref, w_ref, *o_refs):
    x = x_ref[0]
    y = x * lax.rsqrt(jnp.mean(x * x, axis=-1, keepdims=True) + EPS) * g_ref[...]
    h = (y * (1.0 + sc_ref[0]) + sh_ref[0]).astype(BF16)
    off = 0
    for o_ref in o_refs:
        width = o_ref.shape[2]
        for c0 in range(0, width, INPROJ_COL_CHUNK):
            cw = min(INPROJ_COL_CHUNK, width - c0)
            o_ref[0, :, c0:c0 + cw] = jnp.dot(h, w_ref[:, off + c0:off + c0 + cw],
                                              preferred_element_type=F32).astype(o_ref.dtype)
        off += width


def norm_inproj(x, g, sc, sh, w, l, col0, widths, dtypes, *, tm):
    bsz, n, d = x.shape
    cols = sum(widths)
    assert col0 % cols == 0
    return pl.pallas_call(
        _norm_inproj_kernel,
        grid=(bsz, n // tm),
        in_specs=[
            pl.BlockSpec((1, tm, d), lambda b, i: (b, i, 0)),
            pl.BlockSpec((1, d), lambda b, i: (0, 0)),
            pl.BlockSpec((1, 1, d), lambda b, i: (b, 0, 0)),
            pl.BlockSpec((1, 1, d), lambda b, i: (b, 0, 0)),
            pl.BlockSpec((None, d, cols), lambda b, i: (l, 0, col0 // cols), pipeline_mode=pl.Buffered(1)),
        ],
        out_specs=[pl.BlockSpec((1, tm, wd), lambda b, i: (b, i, 0)) for wd in widths],
        out_shape=[jax.ShapeDtypeStruct((bsz, n, wd), dt) for wd, dt in zip(widths, dtypes, strict=True)],
        compiler_params=_params("arbitrary", "arbitrary"),
        name="norm_inproj",
    )(x, g, sc, sh, w)


def _merge_kernel(a_ref, b_ref, c_ref, d_ref, ga_ref, gb_ref, gc_ref, gd_ref,
                  wa_ref, wb_ref, wc_ref, wd_ref, wo_ref, x_ref, gt_ref,
                  g2_ref, sc2_ref, sh2_ref, rt_ref, o_ref, h_ref, aff_ref):
    m = None
    for br, gz, w in ((a_ref, ga_ref, wa_ref), (b_ref, gb_ref, wb_ref),
                      (c_ref, gc_ref, wc_ref), (d_ref, gd_ref, wd_ref)):
        p = jnp.dot(br[0].astype(BF16), w[...], preferred_element_type=F32)
        t = _sigmoid(gz[0].astype(F32)) * p
        m = t if m is None else m + t
    mix = jnp.dot(m.astype(BF16), wo_ref[...], preferred_element_type=F32)
    x = x_ref[0] + gt_ref[0] * mix
    o_ref[0] = x
    y = x * lax.rsqrt(jnp.mean(x * x, axis=-1, keepdims=True) + EPS) * g2_ref[...]
    h = y * (1.0 + sc2_ref[0]) + sh2_ref[0]
    h_hi, h_lo = _split_bf16(h)
    r_hi, r_lo = _split_bf16(rt_ref[...])
    logits = (lax.dot_general(r_hi, h_hi, _NT, preferred_element_type=F32)
              + lax.dot_general(r_hi, h_lo, _NT, preferred_element_type=F32)
              + lax.dot_general(r_lo, h_hi, _NT, preferred_element_type=F32))
    pr = jnp.exp(logits - jnp.max(logits, axis=0, keepdims=True))
    aff_ref[0] = pr / jnp.sum(pr, axis=0, keepdims=True)
    h_ref[0] = h_hi


def merge_branches(a, b, cc, d, z, w_a, w_b, w_c, w_d, w_o, x, gt, g2, sc2, sh2, router_t, *, tm):
    bsz, n, dm = x.shape
    w = a.shape[-1]
    ne = router_t.shape[0]
    br_spec = pl.BlockSpec((1, tm, w), lambda bi, i: (bi, i, 0))
    gate_specs = [pl.BlockSpec((1, tm, dm), functools.partial(lambda bi, i, k: (bi, i, k), k=k))
                  for k in range(N_BRANCH)]
    wbr_spec = pl.BlockSpec((w, dm), lambda bi, i: (0, 0))
    row_spec = pl.BlockSpec((1, tm, dm), lambda bi, i: (bi, i, 0))
    mod_spec = pl.BlockSpec((1, 1, dm), lambda bi, i: (bi, 0, 0))
    return pl.pallas_call(
        _merge_kernel,
        grid=(bsz, n // tm),
        in_specs=[br_spec] * 4 + gate_specs + [wbr_spec] * 4 + [
            pl.BlockSpec((dm, dm), lambda bi, i: (0, 0)), row_spec, mod_spec,
            pl.BlockSpec((1, dm), lambda bi, i: (0, 0)), mod_spec, mod_spec,
            pl.BlockSpec((ne, dm), lambda bi, i: (0, 0)),
        ],
        out_specs=[row_spec, row_spec, pl.BlockSpec((1, ne, tm), lambda bi, i: (bi, 0, i))],
        out_shape=[jax.ShapeDtypeStruct((bsz, n, dm), F32), jax.ShapeDtypeStruct((bsz, n, dm), BF16),
                   jax.ShapeDtypeStruct((bsz, ne, n), F32)],
        compiler_params=_params("arbitrary", "arbitrary"),
        name="merge_branches",
    )(a, b, cc, d, z, z, z, z, w_a, w_b, w_c, w_d, w_o, x, gt, g2, sc2, sh2, router_t)


ROUTE_TB = 256
ROUTE_WIN = 64
ROUTE_ALIGN = 16
OFF_LANES = 128
TOPK_EXP_STEPS = (64, 32, 16, 8, 4, 2, 1)
TOPK_BISECT_STEPS = 56


def _topk_kernel(aff_ref, slot_ref, off_ref, *, cap):
    a = aff_ref[0]
    ne, n = a.shape
    capf = jnp.float32(cap)

    def count_ge(t):
        return jnp.sum(jnp.where(a >= t, 1.0, 0.0), axis=1, keepdims=True)

    hi = jnp.full((ne, 1), 2.0, F32)
    for s in TOPK_EXP_STEPS:
        cand = hi * (2.0 ** -s)
        hi = jnp.where(count_ge(cand) < capf, cand, hi)
    lo = jnp.where(hi <= 2.0 ** -126, 0.0, hi * 0.5)

    def bisect(_, carry):
        lo, hi = carry
        mid = 0.5 * (lo + hi)
        ok = count_ge(mid) >= capf
        return jnp.where(ok, mid, lo), jnp.where(ok, hi, mid)

    lo, hi = lax.fori_loop(0, TOPK_BISECT_STEPS, bisect, (lo, hi))
    need = capf - count_ge(hi)
    r_i = lax.broadcasted_iota(jnp.int32, (ROUTE_TB, ROUTE_TB), 0)
    c_i = lax.broadcasted_iota(jnp.int32, (ROUTE_TB, ROUTE_TB), 1)
    tri = jnp.where(r_i < c_i, 1.0, 0.0).astype(BF16)
    lane = lax.broadcasted_iota(jnp.int32, (ne, OFF_LANES), 1)
    run_eq = jnp.zeros((ne, 1), F32)
    run_sel = jnp.zeros((ne, 1), F32)
    offs = jnp.zeros((ne, OFF_LANES), F32)
    for c in range(n // ROUTE_TB):
        cols = slice(c * ROUTE_TB, (c + 1) * ROUTE_TB)
        a_c = a[:, cols]
        above = a_c >= hi
        eq_c = jnp.where(above, 0.0, jnp.where(a_c >= lo, 1.0, 0.0))
        rank = jnp.dot(eq_c.astype(BF16), tri, preferred_element_type=F32) + run_eq
        sel = jnp.where(above, 1.0, jnp.where(rank < need, eq_c, 0.0))
        pos = jnp.dot(sel.astype(BF16), tri, preferred_element_type=F32) + run_sel
        slot_ref[0, :, cols] = jnp.where(sel > 0.0, pos, -1.0).astype(jnp.int32)
        offs = jnp.where(lane == c, run_sel, offs)
        run_eq = run_eq + jnp.sum(eq_c, axis=1, keepdims=True)
        run_sel = run_sel + jnp.sum(sel, axis=1, keepdims=True)
    offs = jnp.where(lane == n // ROUTE_TB, run_sel, offs)
    off_ref[0] = offs.astype(jnp.int32)


def expert_topk(aff_t, cap):
    bsz, ne, n = aff_t.shape
    return pl.pallas_call(
        functools.partial(_topk_kernel, cap=cap),
        grid=(bsz,),
        in_specs=[pl.BlockSpec((1, ne, n), lambda b: (b, 0, 0))],
        out_specs=[pl.BlockSpec((1, ne, n), lambda b: (b, 0, 0)),
                   pl.BlockSpec((1, ne, OFF_LANES), lambda b: (b, 0, 0))],
        out_shape=[jax.ShapeDtypeStruct((bsz, ne, n), jnp.int32),
                   jax.ShapeDtypeStruct((bsz, ne, OFF_LANES), jnp.int32)],
        compiler_params=_params("arbitrary"),
        name="expert_topk",
    )(aff_t)


def _window_plan(off_ref, b, e, tb, w, rows_max):
    start = off_ref[b, e, tb]
    stop = off_ref[b, e, tb + 1]
    first = (start // ROUTE_ALIGN) * ROUTE_ALIGN + w * ROUTE_WIN
    active = first < stop
    return pl.multiple_of(jnp.minimum(first, rows_max), ROUTE_ALIGN), first, active


def _num_passes(off_ref, b, e_list, tb):
    n_pass = jnp.int32(0)
    for e in e_list:
        start = off_ref[b, e, tb]
        stop = off_ref[b, e, tb + 1]
        base = (start // ROUTE_ALIGN) * ROUTE_ALIGN
        n_pass = jnp.maximum(n_pass, (stop - base + ROUTE_WIN - 1) // ROUTE_WIN)
    return n_pass


def _gather_kernel(off_ref, h_ref, slot_ref, aff_ref, xe_ref, gate_ref, *, group):
    b, g, tb = pl.program_id(0), pl.program_id(1), pl.program_id(2)
    rows_max = xe_ref.shape[2] - ROUTE_WIN

    @pl.when(tb == 0)
    def _():
        xe_ref[...] = jnp.zeros(xe_ref.shape, xe_ref.dtype)
        gate_ref[...] = jnp.zeros(gate_ref.shape, gate_ref.dtype)

    sub = lax.broadcasted_iota(jnp.int32, (ROUTE_WIN, 1), 0)
    experts = [g * group + e for e in range(group)]

    def one_pass(w, carry):
        plans = [_window_plan(off_ref, b, ge, tb, w, rows_max) for ge in experts]
        hots = []
        for e, (rows, first, active) in enumerate(plans):
            ids = rows + sub
            want = jnp.where(active & (ids >= first), ids, -2)
            hots.append(slot_ref[0, e:e + 1, :] == want)
        onehot = jnp.concatenate([jnp.where(hm, 1.0, 0.0) for hm in hots], axis=0).astype(BF16)
        res = jnp.dot(onehot, h_ref[0], preferred_element_type=F32)
        for e, (rows, _, _) in enumerate(plans):
            win = pl.ds(rows, ROUTE_WIN)
            xe_ref[0, e, win, :] = xe_ref[0, e, win, :] + res[e * ROUTE_WIN:(e + 1) * ROUTE_WIN].astype(BF16)
            gsel = jnp.sum(jnp.where(hots[e], aff_ref[0, e:e + 1, :], 0.0), axis=1, keepdims=True)
            gate_ref[0, e, win, :] = gate_ref[0, e, win, :] + gsel
        return carry

    lax.fori_loop(0, _num_passes(off_ref, b, experts, tb), one_pass, 0)


def expert_gather(h, slot, aff_t, off, cap_pad, *, group=8):
    bsz, n, d = h.shape
    ne = slot.shape[1]
    grid_spec = pltpu.PrefetchScalarGridSpec(
        num_scalar_prefetch=1,
        grid=(bsz, ne // group, n // ROUTE_TB),
        in_specs=[
            pl.BlockSpec((1, ROUTE_TB, d), lambda b, g, t, off: (b, t, 0)),
            pl.BlockSpec((1, group, ROUTE_TB), lambda b, g, t, off: (b, g, t)),
            pl.BlockSpec((1, group, ROUTE_TB), lambda b, g, t, off: (b, g, t)),
        ],
        out_specs=[pl.BlockSpec((1, group, cap_pad, d), lambda b, g, t, off: (b, g, 0, 0)),
                   pl.BlockSpec((1, group, cap_pad, 1), lambda b, g, t, off: (b, g, 0, 0))],
    )
    return pl.pallas_call(
        functools.partial(_gather_kernel, group=group),
        grid_spec=grid_spec,
        out_shape=[jax.ShapeDtypeStruct((bsz, ne, cap_pad, d), BF16),
                   jax.ShapeDtypeStruct((bsz, ne, cap_pad, 1), F32)],
        compiler_params=_params("arbitrary", "arbitrary", "arbitrary"),
        name="expert_gather",
    )(off, h, slot, aff_t)


def _combine_kernel(off_ref, y_ref, slot_t_ref, x_ref, gt_ref, fg_ref, o_ref, stage_ref, acc_ref):
    b, tb = pl.program_id(0), pl.program_id(1)
    ne = y_ref.shape[1]
    rows_max = y_ref.shape[2] - ROUTE_WIN
    width = ne * ROUTE_WIN
    experts = list(range(ne))
    lane = lax.broadcasted_iota(jnp.int32, (1, width), 1)
    s1 = slot_t_ref[0] + 1
    hi = (s1 // ROUTE_WIN).astype(F32).astype(BF16)
    lo = (s1 % ROUTE_WIN).astype(F32).astype(BF16)
    e_row = lax.broadcasted_iota(jnp.int32, (ne, width), 0)
    e_lane = lax.broadcasted_iota(jnp.int32, (ne, width), 1) // ROUTE_WIN
    expand = jnp.where(e_row == e_lane, 1.0, 0.0).astype(BF16)
    s1_wide = (float(ROUTE_WIN) * jnp.dot(hi, expand, preferred_element_type=F32)
               + jnp.dot(lo, expand, preferred_element_type=F32))
    acc_ref[...] = jnp.zeros(acc_ref.shape, F32)

    def one_pass(w, carry):
        want = jnp.full((1, width), -1, jnp.int32)
        for e in experts:
            rows, first, active = _window_plan(off_ref, b, e, tb, w, rows_max)
            stage_ref[e * ROUTE_WIN:(e + 1) * ROUTE_WIN, :] = y_ref[0, e, pl.ds(rows, ROUTE_WIN), :]
            ids = rows + lane % ROUTE_WIN
            want = jnp.where((lane // ROUTE_WIN == e) & active & (ids >= first), ids + 1, want)
        onehot = jnp.where(s1_wide == want.astype(F32), 1.0, 0.0).astype(BF16)
        acc_ref[...] += jnp.dot(onehot, stage_ref[...], preferred_element_type=F32)
        return carry

    lax.fori_loop(0, _num_passes(off_ref, b, experts, tb), one_pass, 0)
    out = x_ref[0] + gt_ref[0] * acc_ref[...]
    if fg_ref is not None:
        out = out * lax.rsqrt(jnp.mean(out * out, axis=-1, keepdims=True) + EPS) * fg_ref[...]
    o_ref[0] = out


def _combine_kernel_plain(off_ref, y_ref, slot_t_ref, x_ref, gt_ref, o_ref, stage_ref, acc_ref):
    _combine_kernel(off_ref, y_ref, slot_t_ref, x_ref, gt_ref, None, o_ref, stage_ref, acc_ref)


def expert_combine(y, slot_t, off, x, gt, final_g=None):
    bsz, n, d = x.shape
    ne, cap_pad = y.shape[1], y.shape[2]
    grid_spec = pltpu.PrefetchScalarGridSpec(
        num_scalar_prefetch=1,
        grid=(bsz, n // ROUTE_TB),
        in_specs=[
            pl.BlockSpec((1, ne, cap_pad, d), lambda b, t, off: (b, 0, 0, 0), pipeline_mode=pl.Buffered(1)),
            pl.BlockSpec((1, ROUTE_TB, ne), lambda b, t, off: (b, t, 0)),
            pl.BlockSpec((1, ROUTE_TB, d), lambda b, t, off: (b, t, 0)),
            pl.BlockSpec((1, 1, d), lambda b, t, off: (b, 0, 0)),
        ] + ([] if final_g is None else [pl.BlockSpec((1, d), lambda b, t, off: (0, 0))]),
        out_specs=pl.BlockSpec((1, ROUTE_TB, d), lambda b, t, off: (b, t, 0)),
        scratch_shapes=[pltpu.VMEM((ne * ROUTE_WIN, d), BF16), pltpu.VMEM((ROUTE_TB, d), F32)],
    )
    args = (off, y, slot_t, x, gt) + (() if final_g is None else (final_g[None, :],))
    return pl.pallas_call(
        _combine_kernel_plain if final_g is None else _combine_kernel,
        grid_spec=grid_spec,
        out_shape=jax.ShapeDtypeStruct((bsz, n, d), F32),
        compiler_params=_params("arbitrary", "arbitrary"),
        name="expert_combine",
    )(*args)


FFN_FBLK = 512
FFN_MIN_ROWS = 256


def _expert_ffn_kernel(*refs, n_sets, n_f):
    xe_refs, gate_refs = refs[:n_sets], refs[n_sets:2 * n_sets]
    wg_ref, wu_ref, wd_ref = refs[2 * n_sets:2 * n_sets + 3]
    o_refs = refs[2 * n_sets + 3:3 * n_sets + 3]
    acc_refs = refs[3 * n_sets + 3:]
    f = pl.program_id(1)
    blocks = [(s, b) for s in range(n_sets) for b in range(xe_refs[s].shape[0])]
    big = [sb for sb in blocks if xe_refs[sb[0]].shape[2] >= FFN_MIN_ROWS]
    small = [sb for sb in blocks if sb not in big]
    groups = [[sb] for sb in big[:-1]] + [big[-1:] + small] if big else [small]

    def hidden_block(first, last):
        wg = wg_ref[0].astype(BF16)
        wu = wu_ref[0].astype(BF16)
        wd = wd_ref[0].astype(BF16)
        for group in groups:
            xs = [xe_refs[s][b, 0] for s, b in group]
            xe = xs[0] if len(xs) == 1 else jnp.concatenate(xs, axis=0)
            a = jnp.dot(xe, wg, preferred_element_type=F32)
            u = jnp.dot(xe, wu, preferred_element_type=F32)
            t = jnp.dot((_silu(a) * u).astype(BF16), wd, preferred_element_type=F32)
            r0 = 0
            for s, b in group:
                rows = xe_refs[s].shape[2]
                part = t[r0:r0 + rows] if first else acc_refs[s][b] + t[r0:r0 + rows]
                if last:
                    o_refs[s][b, 0] = (part * gate_refs[s][b, 0]).astype(o_refs[s].dtype)
                else:
                    acc_refs[s][b] = part
                r0 += rows

    if n_f == 1:
        hidden_block(True, True)
        return
    pl.when(f == 0)(lambda: hidden_block(True, False))
    if n_f > 2:
        pl.when((f > 0) & (f < n_f - 1))(lambda: hidden_block(False, False))
    pl.when(f == n_f - 1)(lambda: hidden_block(False, True))


def expert_ffn(xes, gates, wg, wu, wd, l):
    n_sets = len(xes)
    _, ne, d, ff = wg.shape
    xe_specs = [pl.BlockSpec((xe.shape[0], 1) + xe.shape[2:], lambda e, f: (0, e, 0, 0)) for xe in xes]
    gate_specs = [pl.BlockSpec((g.shape[0], 1) + g.shape[2:], lambda e, f: (0, e, 0, 0)) for g in gates]
    return pl.pallas_call(
        functools.partial(_expert_ffn_kernel, n_sets=n_sets, n_f=ff // FFN_FBLK),
        grid=(ne, ff // FFN_FBLK),
        in_specs=xe_specs + gate_specs + [
            pl.BlockSpec((None, 1, d, FFN_FBLK), lambda e, f: (l, e, 0, f)),
            pl.BlockSpec((None, 1, d, FFN_FBLK), lambda e, f: (l, e, 0, f)),
            pl.BlockSpec((None, 1, FFN_FBLK, d), lambda e, f: (l, e, f, 0)),
        ],
        out_specs=xe_specs,
        out_shape=[jax.ShapeDtypeStruct(xe.shape, BF16) for xe in xes],
        scratch_shapes=[pltpu.VMEM((xe.shape[0],) + xe.shape[2:], F32) for xe in xes],
        compiler_params=_params("arbitrary", "arbitrary"),
        name="expert_ffn",
    )(*xes, *gates, wg, wu, wd)


ATT_QROWS = 4
ATT_QTOK = ATT_QROWS * GRID_W
ATT_KROWS = ATT_QROWS + WIN_ROWS
ATT_KBLK = ATT_KROWS * GRID_W // ATT_QTOK
_NT = (((1,), (1,)), ((), ()))


def _rope_tables(n):
    quarter = HEAD_DIM // 4
    inv = ROPE_BASE ** (-np.arange(quarter, dtype=np.float64) / quarter)
    t = np.arange(n)
    ang_r = (t // GRID_W)[:, None] * inv[None, :]
    ang_c = (t % GRID_W)[:, None] * inv[None, :]
    cos = np.concatenate([np.cos(ang_r)] * 2 + [np.cos(ang_c)] * 2, axis=1)
    sin = np.concatenate([-np.sin(ang_r), np.sin(ang_r), -np.sin(ang_c), np.sin(ang_c)], axis=1)
    return (jnp.asarray(np.concatenate([cos, cos], axis=1), F32),
            jnp.asarray(np.concatenate([sin, sin], axis=1), F32))


def _attn_bias_tables(rpb, n_rows):
    n_blk = n_rows // ATT_QROWS
    n_dr, n_dc = 2 * WIN_ROWS - 1, 2 * WIN_COLS - 1
    qc = np.arange(GRID_W)
    kc = np.arange(GRID_W)
    c_start = np.clip(qc - WIN_COLS // 2, 0, GRID_W - WIN_COLS)
    col_ok = (kc[None, :] >= c_start[:, None]) & (kc[None, :] < c_start[:, None] + WIN_COLS)
    d_col = np.clip(kc[None, :] - qc[:, None], 1 - WIN_COLS, WIN_COLS - 1) + (WIN_COLS - 1)
    col_sel = (d_col.reshape(-1)[None, :] == np.arange(n_dc)[:, None]).astype(np.float32)
    row_sel, oks = [], []
    for j in (0, 1, n_blk - 1):
        ks = min(max(ATT_QROWS * j - WIN_ROWS // 2, 0), n_rows - ATT_KROWS)
        r = ATT_QROWS * j + np.arange(ATT_QROWS)
        kr0 = np.clip(r - WIN_ROWS // 2, 0, n_rows - WIN_ROWS)
        krow = ks + np.arange(ATT_KROWS)
        row_ok = (krow[None, :] >= kr0[:, None]) & (krow[None, :] < kr0[:, None] + WIN_ROWS)
        d_row = np.clip(krow[None, :] - r[:, None] + (WIN_ROWS - 1), 0, n_dr - 1)
        row_sel.append((d_row.reshape(-1)[:, None] == np.arange(n_dr)[None, :]).astype(np.float32))
        oks.append(row_ok[:, None, :, None] & col_ok[None, :, None, :])
    hp = lax.Precision.HIGHEST
    a = jnp.einsum('cpr,hrd->chpd', jnp.asarray(np.stack(row_sel)), rpb.astype(F32), precision=hp)
    a = a.reshape(3, N_GROUPS, ATT_QROWS, ATT_KROWS, n_dc)
    t = jnp.einsum('chqkd,dxy->chqxky', a, jnp.asarray(col_sel.reshape(n_dc, GRID_W, GRID_W)), precision=hp)
    ok = jnp.asarray(np.stack(oks))[:, None]
    return jnp.where(ok, t, NEG_INF).reshape(3, N_GROUPS, ATT_QTOK, ATT_KROWS * GRID_W)


def _rope(x, cos_ref, sin_ref, first16):
    c = cos_ref[...]
    s = sin_ref[...]
    c2 = jnp.concatenate([c, c], axis=1)
    s2 = jnp.concatenate([s, s], axis=1)
    w = x.shape[1]
    partner = jnp.where(first16, pltpu.roll(x, w - HEAD_DIM // 4, 1), pltpu.roll(x, HEAD_DIM // 4, 1))
    return x * c2 + partner * s2


def _softmax_pv(scores_values, head_mask, acc):
    m = None
    for s, _ in scores_values:
        mx = jnp.max(s, axis=1, keepdims=True)
        m = mx if m is None else jnp.maximum(m, mx)
    den = None
    o = None
    for s, v in scores_values:
        p = jnp.exp(s - m)
        sm = jnp.sum(p, axis=1, keepdims=True)
        den = sm if den is None else den + sm
        t = jnp.dot(p.astype(BF16), v, preferred_element_type=F32)
        o = t if o is None else o + t
    return acc + jnp.where(head_mask, o * (1.0 / den), 0.0)


def _nattn_kernel(q_ref, k0_ref, k1_ref, k2_ref, v0_ref, v1_ref, v2_ref,
                  cq_ref, sq_ref, ck0_ref, ck1_ref, ck2_ref, sk0_ref, sk1_ref, sk2_ref,
                  kc_ref, vc_ref, bias_ref, o_ref):
    w = q_ref.shape[2]
    lane = lax.broadcasted_iota(jnp.int32, (1, w), 1)
    first16 = (lane % (HEAD_DIM // 2)) < (HEAD_DIM // 4)
    q = q_ref[0].astype(F32) * (HEAD_DIM ** -0.5)
    q_rot = _rope(q, cq_ref, sq_ref, first16)
    k_rot = jnp.concatenate([_rope(k0_ref[0].astype(F32), ck0_ref, sk0_ref, first16),
                             _rope(k1_ref[0].astype(F32), ck1_ref, sk1_ref, first16),
                             _rope(k2_ref[0].astype(F32), ck2_ref, sk2_ref, first16)], axis=0).astype(BF16)
    v = jnp.concatenate([v0_ref[0], v1_ref[0], v2_ref[0]], axis=0).astype(BF16)
    kc = kc_ref[0].astype(BF16)
    vc = vc_ref[0].astype(BF16)
    acc = jnp.zeros(q.shape, F32)
    for h in range(N_GROUPS):
        hm = (lane // HEAD_DIM) == h
        s_win = lax.dot_general(jnp.where(hm, q_rot, 0.0).astype(BF16), k_rot, _NT,
                                preferred_element_type=F32) + bias_ref[0, h]
        s_ctx = lax.dot_general(jnp.where(hm, q, 0.0).astype(BF16), kc, _NT, preferred_element_type=F32)
        acc = _softmax_pv([(s_win, v), (s_ctx, vc)], hm, acc)
    o_ref[0] = acc.astype(o_ref.dtype)


def neighbourhood_attention(z, zc, rpb, *, q_col, k_col, v_col, kc_col, vc_col):
    bsz, n, _ = z.shape
    n_ctx = zc.shape[1]
    w = BRANCH_WIDTH
    n_blk = n // ATT_QTOK
    cos_t, sin_t = _rope_tables(n)
    bias = _attn_bias_tables(rpb, n // GRID_W)

    def kb(j):
        return jnp.clip(j - 1, 0, n_blk - ATT_KBLK)

    def zspec(col, off=None):
        if off is None:
            return pl.BlockSpec((1, ATT_QTOK, w), lambda b, j: (b, j, col))
        return pl.BlockSpec((1, ATT_QTOK, w), lambda b, j: (b, kb(j) + off, col))

    def tspec(off=None):
        if off is None:
            return pl.BlockSpec((ATT_QTOK, 2 * HEAD_DIM), lambda b, j: (j, 0))
        return pl.BlockSpec((ATT_QTOK, 2 * HEAD_DIM), lambda b, j: (kb(j) + off, 0))

    in_specs = ([zspec(q_col)] + [zspec(k_col, i) for i in range(ATT_KBLK)] + [zspec(v_col, i) for i in range(ATT_KBLK)]
                + [tspec(), tspec()] + [tspec(i) for i in range(ATT_KBLK)] * 2
                + [pl.BlockSpec((1, n_ctx, w), lambda b, j: (b, 0, kc_col)),
                   pl.BlockSpec((1, n_ctx, w), lambda b, j: (b, 0, vc_col)),
                   pl.BlockSpec((1, N_GROUPS, ATT_QTOK, ATT_KROWS * GRID_W),
                                lambda b, j: (jnp.minimum(j, 1) + j // (n_blk - 1), 0, 0, 0))])
    return pl.pallas_call(
        _nattn_kernel,
        grid=(bsz, n_blk),
        in_specs=in_specs,
        out_specs=pl.BlockSpec((1, ATT_QTOK, w), lambda b, j: (b, j, 0)),
        out_shape=jax.ShapeDtypeStruct((bsz, n, w), BF16),
        compiler_params=_params("arbitrary", "arbitrary"),
        name="neighbourhood_attention",
    )(z, z, z, z, z, z, z, cos_t, sin_t, cos_t, cos_t, cos_t, sin_t, sin_t, sin_t, zc, zc, bias)


def _ctx_attn_kernel(q_ref, kc_ref, vc_ref, o_ref):
    w = q_ref.shape[2]
    lane = lax.broadcasted_iota(jnp.int32, (1, w), 1)
    q = q_ref[0].astype(F32) * (HEAD_DIM ** -0.5)
    kc = kc_ref[0].astype(BF16)
    vc = vc_ref[0].astype(BF16)
    acc = jnp.zeros(q.shape, F32)
    for h in range(N_GROUPS):
        hm = (lane // HEAD_DIM) == h
        s = lax.dot_general(jnp.where(hm, q, 0.0).astype(BF16), kc, _NT, preferred_element_type=F32)
        acc = _softmax_pv([(s, vc)], hm, acc)
    o_ref[0] = acc.astype(o_ref.dtype)


def context_attention(zc, *, q_col, k_col, v_col):
    bsz, n_ctx, _ = zc.shape
    w = BRANCH_WIDTH
    return pl.pallas_call(
        _ctx_attn_kernel,
        grid=(bsz,),
        in_specs=[pl.BlockSpec((1, n_ctx, w), functools.partial(lambda b, c: (b, 0, c), c=c))
                  for c in (q_col, k_col, v_col)],
        out_specs=pl.BlockSpec((1, n_ctx, w), lambda b: (b, 0, 0)),
        out_shape=jax.ShapeDtypeStruct((bsz, n_ctx, w), BF16),
        compiler_params=_params("arbitrary"),
        name="context_attention",
    )(zc, zc, zc)


def _ln(v, g, b):
    mu = jnp.mean(v, axis=-1, keepdims=True)
    var = jnp.mean(jnp.square(v - mu), axis=-1, keepdims=True)
    return (v - mu) * lax.rsqrt(var + EPS) * g + b


def _gmlp_kernel(z_ref, lng_ref, lnb_ref, w_ref, bias_ref, o_ref):
    tm = z_ref.shape[1]
    z = jax.nn.gelu(z_ref[0])
    u = z[:, :BRANCH_WIDTH]
    v = _ln(z[:, BRANCH_WIDTH:], lng_ref[...], lnb_ref[...])
    group = lax.broadcasted_iota(jnp.int32, (1, BRANCH_WIDTH), 1) // HEAD_DIM
    wcat = w_ref[...]
    for c in range(tm // CHUNK):
        rows = slice(c * CHUNK, (c + 1) * CHUNK)
        vc = v[rows]
        vst = jnp.concatenate([jnp.where(group == g, vc, 0.0) for g in range(N_GROUPS)], axis=0).astype(BF16)
        mixed = jnp.dot(wcat, vst, preferred_element_type=F32) + bias_ref[...]
        o_ref[0, rows, :] = (u[rows] * mixed).astype(o_ref.dtype)


def chunk_gmlp(za, ln_g, ln_b, w_s, b_s, *, tm):
    bsz, n, w2 = za.shape
    w = w2 // 2
    wcat = jnp.transpose(w_s, (1, 0, 2)).reshape(CHUNK, N_GROUPS * CHUNK).astype(BF16)
    bias = jnp.repeat(b_s.T, HEAD_DIM, axis=1)
    return pl.pallas_call(
        _gmlp_kernel,
        grid=(bsz, n // tm),
        in_specs=[
            pl.BlockSpec((1, tm, w2), lambda b, i: (b, i, 0)),
            pl.BlockSpec((1, w), lambda b, i: (0, 0)),
            pl.BlockSpec((1, w), lambda b, i: (0, 0)),
            pl.BlockSpec((CHUNK, N_GROUPS * CHUNK), lambda b, i: (0, 0)),
            pl.BlockSpec((CHUNK, w), lambda b, i: (0, 0)),
        ],
        out_specs=pl.BlockSpec((1, tm, w), lambda b, i: (b, i, 0)),
        out_shape=jax.ShapeDtypeStruct((bsz, n, w), BF16),
        compiler_params=_params("arbitrary", "arbitrary"),
        name="chunk_gmlp",
    )(za, ln_g[None, :], ln_b[None, :], wcat, bias)


CONV_HALO = 16
CONV_SUB = 128
SUBLANES = 8
LANES = 128


def _conv_kernel(a_ref, g_ref, ap_ref, gp_ref, an_ref, gn_ref, w_ref, cb_ref, lng_ref, lnb_ref, o_ref, y_ref):
    i = pl.program_id(1)
    tm = a_ref.shape[1]

    def glu(a, g):
        return a * _sigmoid(g)

    y_ref[0:CONV_HALO, :] = jnp.where(i > 0, glu(ap_ref[0], gp_ref[0]), 0.0)
    y_ref[CONV_HALO:CONV_HALO + tm, :] = glu(a_ref[0], g_ref[0])
    y_ref[CONV_HALO + tm:, :] = jnp.where(i < pl.num_programs(1) - 1, glu(an_ref[0], gn_ref[0]), 0.0)
    first = CONV_HALO - CONV_WIDTH // 2
    nb = CONV_SUB + 2 * CONV_HALO
    for r in range(0, tm, CONV_SUB):
        blk = y_ref[r:r + nb, :]
        acc = None
        for res in range(SUBLANES):
            rot = blk if res == 0 else pltpu.roll(blk, nb - res, 0)
            for j in range(CONV_WIDTH):
                if (first + j) % SUBLANES == res:
                    a0 = first + j - res
                    t = rot[a0:a0 + CONV_SUB, :] * w_ref[j:j + 1, :]
                    acc = t if acc is None else acc + t
        y = _ln(acc + cb_ref[...], lng_ref[...], lnb_ref[...])
        o_ref[0, r:r + CONV_SUB, :] = _silu(y).astype(o_ref.dtype)


def conformer_conv(zcv, conv_w, conv_b, ln_g, ln_b, *, tm):
    bsz, n, w2 = zcv.shape
    w = w2 // 2
    hb = tm // CONV_HALO
    n_hb = n // CONV_HALO

    def main(col):
        return pl.BlockSpec((1, tm, w), lambda b, i: (b, i, col))

    def prev(col):
        return pl.BlockSpec((1, CONV_HALO, w), lambda b, i: (b, jnp.maximum(i * hb - 1, 0), col))

    def nxt(col):
        return pl.BlockSpec((1, CONV_HALO, w), lambda b, i: (b, jnp.minimum((i + 1) * hb, n_hb - 1), col))

    vec = pl.BlockSpec((1, w), lambda b, i: (0, 0))
    return pl.pallas_call(
        _conv_kernel,
        grid=(bsz, n // tm),
        in_specs=[main(0), main(1), prev(0), prev(1), nxt(0), nxt(1),
                  pl.BlockSpec((CONV_WIDTH, w), lambda b, i: (0, 0)), vec, vec, vec],
        out_specs=pl.BlockSpec((1, tm, w), lambda b, i: (b, i, 0)),
        out_shape=jax.ShapeDtypeStruct((bsz, n, w), BF16),
        scratch_shapes=[pltpu.VMEM((tm + 2 * CONV_HALO, w), F32)],
        compiler_params=_params("arbitrary", "arbitrary"),
        name="conformer_conv",
    )(zcv, zcv, zcv, zcv, zcv, zcv, conv_w, conv_b[None, :], ln_g[None, :], ln_b[None, :])


FOURIER_N2 = 128
FOURIER_UNROLL = 8
FOURIER_KB = 8


def _np_split(m):
    m = jnp.asarray(m, F32)
    hi = m.astype(BF16)
    return hi, (m - hi.astype(F32)).astype(BF16)


def _channel_dft_matrix():
    c = np.arange(HEAD_DIM)
    ang = 2.0 * np.pi * ((c[:, None] * c[None, :]) % HEAD_DIM) / HEAD_DIM
    eye = np.eye(N_GROUPS)
    return np.concatenate([np.kron(eye, np.cos(ang)), np.kron(eye, np.sin(ang))], axis=0)


def _fourier_stage1_kernel(xa_ref, xb_ref, mh_ref, ml_ref, o_ref, *, n1, n2):
    def one_fast_index(f, carry):
        rows = pl.ds(f, n1, stride=n2)
        x = jnp.concatenate([xa_ref[0, rows, :], xb_ref[0, rows, :]], axis=1)
        x_hi, x_lo = _split_bf16(x)
        res = _dot3(mh_ref[0, f], ml_ref[0, f], x_hi, x_lo)
        o_ref[0, 0, 0, rows, :] = res[:, :LANES]
        o_ref[0, 0, 1, rows, :] = res[:, LANES:]
        return carry

    lax.fori_loop(0, n2, one_fast_index, 0, unroll=FOURIER_UNROLL)


def _fourier_stage2_kernel(br_ref, bi_ref, m2h_ref, m2l_ref, mdh_ref, mdl_ref, o_ref, *, scale):
    for k in range(br_ref.shape[3]):
        b_re = jnp.concatenate([br_ref[0, 0, 0, k], br_ref[0, 0, 1, k]], axis=1)
        b_im = jnp.concatenate([bi_ref[0, 0, 0, k], bi_ref[0, 0, 1, k]], axis=1)
        b_hi, b_lo = _split_bf16(jnp.concatenate([b_re, b_im], axis=0))
        xs = _dot3(m2h_ref[...], m2l_ref[...], b_hi, b_lo)
        x_hi, x_lo = _split_bf16(jnp.concatenate([xs[:FOURIER_N2], xs[FOURIER_N2:]], axis=1))
        o_ref[0, :, k, :] = _dot3(x_hi, x_lo, mdh_ref[...], mdl_ref[...]) * scale


def _fourier_direct_kernel(x_ref, mph_ref, mpl_ref, mdh_ref, mdl_ref, o_ref, *, scale):
    n = x_ref.shape[1]
    x_hi, x_lo = _split_bf16(x_ref[0])
    p = _dot3(mph_ref[...], mpl_ref[...], x_hi, x_lo)
    p_hi, p_lo = _split_bf16(jnp.concatenate([p[:n], p[n:]], axis=1))
    o_ref[0] = (_dot3(p_hi, p_lo, mdh_ref[...], mdl_ref[...]) * scale).astype(o_ref.dtype)


def fourier_mix(zb):
    bsz, n, w = zb.shape
    scale = float(1.0 / np.sqrt(n * HEAD_DIM))
    mdh, mdl = _np_split(_channel_dft_matrix())
    md_spec2 = pl.BlockSpec((2 * w, w), lambda b, j: (0, 0))
    if n <= 2 * FOURIER_N2:
        t = np.arange(n)
        ang = 2.0 * np.pi * ((t[:, None] * t[None, :]) % n) / n
        mph, mpl = _np_split(np.concatenate([np.cos(ang), -np.sin(ang)], axis=0))
        return pl.pallas_call(
            functools.partial(_fourier_direct_kernel, scale=scale),
            grid=(bsz,),
            in_specs=[pl.BlockSpec((1, n, w), lambda b: (b, 0, 0)),
                      pl.BlockSpec((2 * n, n), lambda b: (0, 0)), pl.BlockSpec((2 * n, n), lambda b: (0, 0)),
                      pl.BlockSpec((2 * w, w), lambda b: (0, 0)), pl.BlockSpec((2 * w, w), lambda b: (0, 0))],
            out_specs=pl.BlockSpec((1, n, w), lambda b: (b, 0, 0)),
            out_shape=jax.ShapeDtypeStruct((bsz, n, w), BF16),
            compiler_params=_params("arbitrary"),
            name="fourier_direct",
        )(zb, mph, mpl, mdh, mdl)

    n1, n2 = n // FOURIER_N2, FOURIER_N2
    f, k1, s = np.arange(n2), np.arange(n1), np.arange(n1)
    ang1 = 2.0 * np.pi * ((k1[None, :, None] * (f[:, None, None] + n2 * s[None, None, :])) % n) / n
    m1h, m1l = _np_split(np.stack([np.cos(ang1), -np.sin(ang1)]))
    k2 = np.arange(n2)
    ang2 = 2.0 * np.pi * ((k2[:, None] * f[None, :]) % n2) / n2
    c2, s2 = np.cos(ang2), np.sin(ang2)
    m2h, m2l = _np_split(np.block([[c2, s2], [-s2, c2]]))

    m1_spec = pl.BlockSpec((1, n2, n1, n1), lambda b, p: (p, 0, 0, 0))
    n_half = w // LANES
    b_st = pl.pallas_call(
        functools.partial(_fourier_stage1_kernel, n1=n1, n2=n2),
        grid=(bsz, 2),
        in_specs=[pl.BlockSpec((1, n, LANES), lambda b, p: (b, 0, 0)),
                  pl.BlockSpec((1, n, LANES), lambda b, p: (b, 0, 1)), m1_spec, m1_spec],
        out_specs=pl.BlockSpec((1, 1, n_half, n, LANES), lambda b, p: (b, p, 0, 0, 0)),
        out_shape=jax.ShapeDtypeStruct((bsz, 2, n_half, n, LANES), F32),
        compiler_params=_params("arbitrary", "arbitrary"),
        name="fourier_stage1",
    )(zb, zb, m1h, m1l)

    kb = FOURIER_KB
    b_st = b_st.reshape(bsz, 2, n_half, n1, n2, LANES)
    out = pl.pallas_call(
        functools.partial(_fourier_stage2_kernel, scale=scale),
        grid=(bsz, n1 // kb),
        in_specs=[pl.BlockSpec((1, 1, n_half, kb, n2, LANES), lambda b, j: (b, 0, 0, j, 0, 0)),
                  pl.BlockSpec((1, 1, n_half, kb, n2, LANES), lambda b, j: (b, 1, 0, j, 0, 0)),
                  pl.BlockSpec((2 * n2, 2 * n2), lambda b, j: (0, 0)),
                  pl.BlockSpec((2 * n2, 2 * n2), lambda b, j: (0, 0)),
                  md_spec2, md_spec2],
        out_specs=pl.BlockSpec((1, n2, kb, w), lambda b, j: (b, 0, j, 0)),
        out_shape=jax.ShapeDtypeStruct((bsz, n2, n1, w), F32),
        compiler_params=_params("arbitrary", "arbitrary"),
        name="fourier_stage2",
    )(b_st, b_st, m2h, m2l, mdh, mdl)
    return out.reshape(bsz, n, w)


def expert_route(h, aff_t):
    n = h.shape[1]
    cap = EC_CAPACITY * n // N_EXPERTS
    slot, off = expert_topk(aff_t, cap)
    xe, gate = expert_gather(h, slot, aff_t, off, max(cap, ROUTE_WIN))
    return xe, gate, jnp.swapaxes(slot, 1, 2), off


def kernel(x, c, ctx, c_ctx, w_mod, b_mod, norm1_g, norm2_g, w_in, sgu_ln_g, sgu_ln_b, w_spatial, b_spatial,
           w_a_out, w_b_out, conv_w, conv_b, conv_ln_g, conv_ln_b, w_c_out, rpb, w_d_out, w_out, w_router,
           w_gate_e, w_up_e, w_down_e, final_norm_g):
    bsz = x.shape[0]
    xc = ctx
    assert bsz + 1 <= MOD_ROWS
    cond = jnp.concatenate([c, c_ctx[None, :], jnp.zeros((MOD_ROWS - bsz - 1, D_MODEL), F32)], axis=0)
    mod_all = modulation(cond, w_mod, b_mod)
    w_in_b = w_in.astype(BF16)
    for l in range(DEPTH):
        last = l == DEPTH - 1
        sh1, sc1, gt1, sh2, sc2, gt2 = jnp.split(mod_all[l, :bsz, None, :], 6, axis=-1)
        cmod = jnp.broadcast_to(mod_all[l, bsz:bsz + 1, None, :], (bsz, 1, 6 * D_MODEL))
        csh1, csc1, cgt1, csh2, csc2, cgt2 = jnp.split(cmod, 6, axis=-1)

        wa, wb, wc, wd_, wo = (w.astype(BF16) for w in (w_a_out[l], w_b_out[l], w_c_out[l], w_d_out[l], w_out[l]))
        g1 = norm1_g[l][None, :]

        za, zb, zcv, zqkv, zg = norm_inproj(x, g1, sc1, sh1, w_in_b, l, 0, IN_WIDTHS, IN_DTYPES, tm=512)
        if last:
            (zqkv_c,) = norm_inproj(xc, g1, csc1, csh1, w_in_b, l, Q_END, (2 * BRANCH_WIDTH,), (BF16,), tm=CTX_TM)
            kc_col, vc_col = 0, 1
        else:
            cza, czb, czcv, zqkv_c, czg = norm_inproj(xc, g1, csc1, csh1, w_in_b, l, 0, IN_WIDTHS, IN_DTYPES,
                                                      tm=CTX_TM)
            kc_col, vc_col = 1, 2
        d_lat = neighbourhood_attention(zqkv, zqkv_c, rpb[l], q_col=0, k_col=1, v_col=2,
                                        kc_col=kc_col, vc_col=vc_col)
        a_lat = chunk_gmlp(za, sgu_ln_g[l], sgu_ln_b[l], w_spatial[l], b_spatial[l], tm=512)
        b_lat = fourier_mix(zb)
        c_lat = conformer_conv(zcv, conv_w[l], conv_b[l], conv_ln_g[l], conv_ln_b[l], tm=512)
        g2 = norm2_g[l][None, :]
        router_t = w_router[l].T
        x, h2, aff_t = merge_branches(a_lat, b_lat, c_lat, d_lat, zg, wa, wb, wc, wd_, wo, x, gt1,
                                      g2, sc2, sh2, router_t, tm=512)
        if not last:
            a_c = chunk_gmlp(cza, sgu_ln_g[l], sgu_ln_b[l], w_spatial[l], b_spatial[l], tm=CTX_TM)
            b_c = fourier_mix(czb)
            c_c = conformer_conv(czcv, conv_w[l], conv_b[l], conv_ln_g[l], conv_ln_b[l], tm=CTX_TM)
            d_c = context_attention(zqkv_c, q_col=0, k_col=1, v_col=2)
            xc, hc2, aff_tc = merge_branches(a_c, b_c, c_c, d_c, czg, wa, wb, wc, wd_, wo, xc, cgt1,
                                             g2, csc2, csh2, router_t, tm=CTX_TM)

        xe, gate, slot_t, off = expert_route(h2, aff_t)
        if last:
            (y,) = expert_ffn([xe], [gate], w_gate_e, w_up_e, w_down_e, l)
        else:
            xe_c, gate_c, slot_tc, off_c = expert_route(hc2, aff_tc)
            y, y_c = expert_ffn([xe, xe_c], [gate, gate_c], w_gate_e, w_up_e, w_down_e, l)
            xc = expert_combine(y_c, slot_tc, off_c, xc, cgt2)
        x = expert_combine(y, slot_t, off, x, gt2, final_norm_g if last else None)
    return x


CTX_TM = 256
```

```python
import functools

import jax
import jax.numpy as jnp
import numpy as np
from jax import lax
from jax.experimental import pallas as pl
from jax.experimental.pallas import tpu as pltpu

D_MODEL = 1024
DEPTH = 2
GRID_W = 64
HEAD_DIM = 64
N_GROUPS = 4
BRANCH_WIDTH = N_GROUPS * HEAD_DIM
N_BRANCH = 4
CHUNK = 128
CONV_WIDTH = 31
WIN_ROWS = 8
WIN_COLS = 16
QB_COLS = 16
KB_COLS = QB_COLS + WIN_COLS
ROPE_BASE = 10000.0
N_EXPERTS = 16
EXPERT_FF = 1024
EC_CAPACITY = 2
EPS = 1e-6
NEG_INF = -1e30
A_END = 2 * BRANCH_WIDTH
B_END = A_END + BRANCH_WIDTH
C_END = B_END + 2 * BRANCH_WIDTH
Q_END = C_END + BRANCH_WIDTH
K_END = Q_END + BRANCH_WIDTH
V_END = K_END + BRANCH_WIDTH
IN_COLS = V_END + N_BRANCH * D_MODEL
SPLITS = [A_END, B_END, C_END, Q_END, K_END, V_END]
IN_WIDTHS = (A_END, B_END - A_END, C_END - B_END, V_END - C_END, IN_COLS - V_END)
IN_DTYPES = (jnp.float32, jnp.float32, jnp.float32, jnp.bfloat16, jnp.bfloat16)
GATE_HALF = 0.5

VMEM_LIMIT_BYTES = 56 * 1024 * 1024
F32 = jnp.float32
BF16 = jnp.bfloat16


def _sigmoid(v):
    return 0.5 * jnp.tanh(0.5 * v) + 0.5


def _silu(v):
    return v * _sigmoid(v)


def _params(*sem):
    return pltpu.CompilerParams(dimension_semantics=sem, vmem_limit_bytes=VMEM_LIMIT_BYTES)


MOD_ROWS = 8
MOD_TN = 1536


def _split_bf16(v):
    hi = v.astype(BF16)
    return hi, (v - hi.astype(F32)).astype(BF16)


def _dot3(a_hi, a_lo, b_hi, b_lo):
    return (jnp.dot(a_hi, b_hi, preferred_element_type=F32) + jnp.dot(a_hi, b_lo, preferred_element_type=F32)
            + jnp.dot(a_lo, b_hi, preferred_element_type=F32))


def _mod_kernel(c_ref, w_ref, b_ref, o_ref):
    s_hi, s_lo = _split_bf16(_silu(c_ref[...]))
    w_hi, w_lo = _split_bf16(w_ref[0])
    o_ref[0] = _dot3(s_hi, s_lo, w_hi, w_lo) + b_ref[0]


def modulation(cond, w_mod, b_mod):
    n_layers, d, cols = w_mod.shape
    return pl.pallas_call(
        _mod_kernel,
        grid=(n_layers, cols // MOD_TN),
        in_specs=[pl.BlockSpec((MOD_ROWS, d), lambda l, j: (0, 0)),
                  pl.BlockSpec((1, d, MOD_TN), lambda l, j: (l, 0, j)),
                  pl.BlockSpec((1, 1, MOD_TN), lambda l, j: (l, 0, j))],
        out_specs=pl.BlockSpec((1, MOD_ROWS, MOD_TN), lambda l, j: (l, 0, j)),
        out_shape=jax.ShapeDtypeStruct((n_layers, MOD_ROWS, cols), F32),
        compiler_params=_params("arbitrary", "arbitrary"),
        name="modulation",
    )(cond, w_mod, b_mod[:, None, :])


INPROJ_COL_CHUNK = 512


def _norm_inproj_kernel(x_ref, g_ref, sc_ref, sh_ref, w_ref, *o_refs):
    x = x_ref[0]
    y = x * lax.rsqrt(jnp.mean(x * x, axis=-1, keepdims=True) + EPS) * g_ref[...]
    h = (y * (1.0 + sc_ref[0]) + sh_ref[0]).astype(BF16)
    off = 0
    for o_ref in o_refs:
        width = o_ref.shape[2]
        for c0 in range(0, width, INPROJ_COL_CHUNK):
            cw = min(INPROJ_COL_CHUNK, width - c0)
            o_ref[0, :, c0:c0 + cw] = jnp.dot(h, w_ref[:, off + c0:off + c0 + cw],
                                              preferred_element_type=F32).astype(o_ref.dtype)
        off += width


def norm_inproj(x, g, sc, sh, w, l, col0, widths, dtypes, *, tm):
    bsz, n, d = x.shape
    cols = sum(widths)
    assert col0 % cols == 0
    return pl.pallas_call(
        _norm_inproj_kernel,
        grid=(bsz, n // tm),
        in_specs=[
            pl.BlockSpec((1, tm, d), lambda b, i: (b, i, 0)),
            pl.BlockSpec((1, d), lambda b, i: (0, 0)),
            pl.BlockSpec((1, 1, d), lambda b, i: (b, 0, 0)),
            pl.BlockSpec((1, 1, d), lambda b, i: (b, 0, 0)),
            pl.BlockSpec((None, d, cols), lambda b, i: (l, 0, col0 // cols), pipeline_mode=pl.Buffered(1)),
        ],
        out_specs=[pl.BlockSpec((1, tm, wd), lambda b, i: (b, i, 0)) for wd in widths],
        out_shape=[jax.ShapeDtypeStruct((bsz, n, wd), dt) for wd, dt in zip(widths, dtypes, strict=True)],
        compiler_params=_params("arbitrary", "arbitrary"),
        name="norm_inproj",
    )(x, g, sc, sh, w)


def _merge_kernel(a_ref, b_ref, c_ref, d_ref, ga_ref, gb_ref, gc_ref, gd_ref,
                  wa_ref, wb_ref, wc_ref, wd_ref, wo_ref, x_ref, gt_ref,
                  g2_ref, sc2_ref, sh2_ref, rt_ref, o_ref, h_ref, aff_ref):
    m = None
    for br, gz, w in ((a_ref, ga_ref, wa_ref), (b_ref, gb_ref, wb_ref),
                      (c_ref, gc_ref, wc_ref), (d_ref, gd_ref, wd_ref)):
        hp = jnp.dot(br[0].astype(BF16), w[...], preferred_element_type=F32)
        t = hp * jnp.tanh(gz[0].astype(F32)) + hp
        m = t if m is None else m + t
    mix = jnp.dot(m.astype(BF16), wo_ref[...], preferred_element_type=F32)
    x = x_ref[0] + gt_ref[0] * mix
    o_ref[0] = x
    y = x * lax.rsqrt(jnp.mean(x * x, axis=-1, keepdims=True) + EPS) * g2_ref[...]
    h = y * (1.0 + sc2_ref[0]) + sh2_ref[0]
    h_hi, h_lo = _split_bf16(h)
    r_hi, r_lo = _split_bf16(rt_ref[...])
    ne = rt_ref.shape[0]
    both = lax.dot_general(jnp.concatenate([r_hi, r_lo], axis=0), h_hi, _NT, preferred_element_type=F32)
    logits = both[:ne] + both[ne:] + lax.dot_general(r_hi, h_lo, _NT, preferred_element_type=F32)
    pr = jnp.exp(logits - jnp.max(logits, axis=0, keepdims=True))
    aff_ref[0] = pr / jnp.sum(pr, axis=0, keepdims=True)
    h_ref[0] = h_hi


def merge_branches(a, b, cc, d, z, w_a, w_b, w_c, w_d, w_o, x, gt, g2, sc2, sh2, router_t, *, tm):
    bsz, n, dm = x.shape
    w = a.shape[-1]
    ne = router_t.shape[0]
    br_spec = pl.BlockSpec((1, tm, w), lambda bi, i: (bi, i, 0))
    gate_specs = [pl.BlockSpec((1, tm, dm), functools.partial(lambda bi, i, k: (bi, i, k), k=k))
                  for k in range(N_BRANCH)]
    wbr_spec = pl.BlockSpec((w, dm), lambda bi, i: (0, 0))
    row_spec = pl.BlockSpec((1, tm, dm), lambda bi, i: (bi, i, 0))
    mod_spec = pl.BlockSpec((1, 1, dm), lambda bi, i: (bi, 0, 0))
    return pl.pallas_call(
        _merge_kernel,
        grid=(bsz, n // tm),
        in_specs=[br_spec] * 4 + gate_specs + [wbr_spec] * 4 + [
            pl.BlockSpec((dm, dm), lambda bi, i: (0, 0)), row_spec, mod_spec,
            pl.BlockSpec((1, dm), lambda bi, i: (0, 0)), mod_spec, mod_spec,
            pl.BlockSpec((ne, dm), lambda bi, i: (0, 0)),
        ],
        out_specs=[row_spec, row_spec, pl.BlockSpec((1, ne, tm), lambda bi, i: (bi, 0, i))],
        out_shape=[jax.ShapeDtypeStruct((bsz, n, dm), F32), jax.ShapeDtypeStruct((bsz, n, dm), BF16),
                   jax.ShapeDtypeStruct((bsz, ne, n), F32)],
        compiler_params=_params("arbitrary", "arbitrary"),
        name="merge_branches",
    )(a, b, cc, d, z, z, z, z, w_a, w_b, w_c, w_d, w_o, x, gt, g2, sc2, sh2, router_t)


ROUTE_TB = 256
ROUTE_WIN = 64
ROUTE_ALIGN = 16
OFF_LANES = 128
TOPK_EXP_STEPS = (64, 32, 16, 8, 4, 2, 1)
TOPK_BISECT_STEPS = 56


def _topk_kernel(aff_ref, slot_ref, off_ref, *, cap):
    a = aff_ref[0]
    ne, n = a.shape
    capf = jnp.float32(cap)

    def count_ge(t):
        return jnp.sum(jnp.where(a >= t, 1.0, 0.0), axis=1, keepdims=True)

    hi = jnp.full((ne, 1), 2.0, F32)
    for s in TOPK_EXP_STEPS:
        cand = hi * (2.0 ** -s)
        hi = jnp.where(count_ge(cand) < capf, cand, hi)
    lo = jnp.where(hi <= 2.0 ** -126, 0.0, hi * 0.5)

    def bisect(_, carry):
        lo, hi = carry
        mid = 0.5 * (lo + hi)
        ok = count_ge(mid) >= capf
        return jnp.where(ok, mid, lo), jnp.where(ok, hi, mid)

    lo, hi = lax.fori_loop(0, TOPK_BISECT_STEPS, bisect, (lo, hi))
    need = capf - count_ge(hi)
    r_i = lax.broadcasted_iota(jnp.int32, (ROUTE_TB, ROUTE_TB), 0)
    c_i = lax.broadcasted_iota(jnp.int32, (ROUTE_TB, ROUTE_TB), 1)
    tri = jnp.where(r_i < c_i, 1.0, 0.0).astype(BF16)
    lane = lax.broadcasted_iota(jnp.int32, (ne, OFF_LANES), 1)
    run_eq = jnp.zeros((ne, 1), F32)
    run_sel = jnp.zeros((ne, 1), F32)
    offs = jnp.zeros((ne, OFF_LANES), F32)
    for c in range(n // ROUTE_TB):
        cols = slice(c * ROUTE_TB, (c + 1) * ROUTE_TB)
        a_c = a[:, cols]
        above = a_c >= hi
        eq_c = jnp.where(above, 0.0, jnp.where(a_c >= lo, 1.0, 0.0))
        rank = jnp.dot(eq_c.astype(BF16), tri, preferred_element_type=F32) + run_eq
        sel = jnp.where(above, 1.0, jnp.where(rank < need, eq_c, 0.0))
        pos = jnp.dot(sel.astype(BF16), tri, preferred_element_type=F32) + run_sel
        slot_ref[0, :, cols] = jnp.where(sel > 0.0, pos, -1.0).astype(jnp.int32)
        offs = jnp.where(lane == c, run_sel, offs)
        run_eq = run_eq + jnp.sum(eq_c, axis=1, keepdims=True)
        run_sel = run_sel + jnp.sum(sel, axis=1, keepdims=True)
    offs = jnp.where(lane == n // ROUTE_TB, run_sel, offs)
    off_ref[0] = offs.astype(jnp.int32)


def expert_topk(aff_t, cap):
    bsz, ne, n = aff_t.shape
    return pl.pallas_call(
        functools.partial(_topk_kernel, cap=cap),
        grid=(bsz,),
        in_specs=[pl.BlockSpec((1, ne, n), lambda b: (b, 0, 0))],
        out_specs=[pl.BlockSpec((1, ne, n), lambda b: (b, 0, 0)),
                   pl.BlockSpec((1, ne, OFF_LANES), lambda b: (b, 0, 0))],
        out_shape=[jax.ShapeDtypeStruct((bsz, ne, n), jnp.int32),
                   jax.ShapeDtypeStruct((bsz, ne, OFF_LANES), jnp.int32)],
        compiler_params=_params("arbitrary"),
        name="expert_topk",
    )(aff_t)


def _window_plan(off_ref, b, e, tb, w, rows_max):
    start = off_ref[b, e, tb]
    stop = off_ref[b, e, tb + 1]
    first = (start // ROUTE_ALIGN) * ROUTE_ALIGN + w * ROUTE_WIN
    active = first < stop
    return pl.multiple_of(jnp.minimum(first, rows_max), ROUTE_ALIGN), first, active


def _num_passes(off_ref, b, e_list, tb):
    n_pass = jnp.int32(0)
    for e in e_list:
        start = off_ref[b, e, tb]
        stop = off_ref[b, e, tb + 1]
        base = (start // ROUTE_ALIGN) * ROUTE_ALIGN
        n_pass = jnp.maximum(n_pass, (stop - base + ROUTE_WIN - 1) // ROUTE_WIN)
    return n_pass


def _gather_kernel(off_ref, h_ref, slot_ref, aff_ref, xe_ref, gate_ref, *, group):
    b, g, tb = pl.program_id(0), pl.program_id(1), pl.program_id(2)
    rows_max = xe_ref.shape[2] - ROUTE_WIN

    @pl.when(tb == 0)
    def _():
        xe_ref[...] = jnp.zeros(xe_ref.shape, xe_ref.dtype)
        gate_ref[...] = jnp.zeros(gate_ref.shape, gate_ref.dtype)

    sub = lax.broadcasted_iota(jnp.int32, (ROUTE_WIN, 1), 0)
    experts = [g * group + e for e in range(group)]

    def one_pass(w, carry):
        plans = [_window_plan(off_ref, b, ge, tb, w, rows_max) for ge in experts]
        hots = []
        for e, (rows, first, active) in enumerate(plans):
            ids = rows + sub
            want = jnp.where(active & (ids >= first), ids, -2)
            hots.append(slot_ref[0, e:e + 1, :] == want)
        onehot = jnp.concatenate([jnp.where(hm, 1.0, 0.0) for hm in hots], axis=0).astype(BF16)
        res = jnp.dot(onehot, h_ref[0], preferred_element_type=F32)
        for e, (rows, _, _) in enumerate(plans):
            win = pl.ds(rows, ROUTE_WIN)
            xe_ref[0, e, win, :] = xe_ref[0, e, win, :] + res[e * ROUTE_WIN:(e + 1) * ROUTE_WIN].astype(BF16)
            gsel = jnp.sum(jnp.where(hots[e], aff_ref[0, e:e + 1, :], 0.0), axis=1, keepdims=True)
            gate_ref[0, e, win, :] = gate_ref[0, e, win, :] + gsel
        return carry

    lax.fori_loop(0, _num_passes(off_ref, b, experts, tb), one_pass, 0)


def expert_gather(h, slot, aff_t, off, cap_pad, *, group=8):
    bsz, n, d = h.shape
    ne = slot.shape[1]
    grid_spec = pltpu.PrefetchScalarGridSpec(
        num_scalar_prefetch=1,
        grid=(bsz, ne // group, n // ROUTE_TB),
        in_specs=[
            pl.BlockSpec((1, ROUTE_TB, d), lambda b, g, t, off: (b, t, 0)),
            pl.BlockSpec((1, group, ROUTE_TB), lambda b, g, t, off: (b, g, t)),
            pl.BlockSpec((1, group, ROUTE_TB), lambda b, g, t, off: (b, g, t)),
        ],
        out_specs=[pl.BlockSpec((1, group, cap_pad, d), lambda b, g, t, off: (b, g, 0, 0)),
                   pl.BlockSpec((1, group, cap_pad, 1), lambda b, g, t, off: (b, g, 0, 0))],
    )
    return pl.pallas_call(
        functools.partial(_gather_kernel, group=group),
        grid_spec=grid_spec,
        out_shape=[jax.ShapeDtypeStruct((bsz, ne, cap_pad, d), BF16),
                   jax.ShapeDtypeStruct((bsz, ne, cap_pad, 1), F32)],
        compiler_params=_params("arbitrary", "arbitrary", "arbitrary"),
        name="expert_gather",
    )(off, h, slot, aff_t)


def _combine_kernel(off_ref, y_ref, slot_t_ref, x_ref, gt_ref, fg_ref, o_ref, stage_ref, acc_ref):
    b, tb = pl.program_id(0), pl.program_id(1)
    ne = y_ref.shape[1]
    rows_max = y_ref.shape[2] - ROUTE_WIN
    width = ne * ROUTE_WIN
    experts = list(range(ne))
    lane = lax.broadcasted_iota(jnp.int32, (1, width), 1)
    s1 = slot_t_ref[0] + 1
    hi = (s1 // ROUTE_WIN).astype(F32).astype(BF16)
    lo = (s1 % ROUTE_WIN).astype(F32).astype(BF16)
    e_row = lax.broadcasted_iota(jnp.int32, (ne, width), 0)
    e_lane = lax.broadcasted_iota(jnp.int32, (ne, width), 1) // ROUTE_WIN
    expand = jnp.where(e_row == e_lane, 1.0, 0.0).astype(BF16)
    s1_wide = (float(ROUTE_WIN) * jnp.dot(hi, expand, preferred_element_type=F32)
               + jnp.dot(lo, expand, preferred_element_type=F32))
    acc_ref[...] = jnp.zeros(acc_ref.shape, F32)

    def one_pass(w, carry):
        want = jnp.full((1, width), -1, jnp.int32)
        for e in experts:
            rows, first, active = _window_plan(off_ref, b, e, tb, w, rows_max)
            stage_ref[e * ROUTE_WIN:(e + 1) * ROUTE_WIN, :] = y_ref[0, e, pl.ds(rows, ROUTE_WIN), :]
            ids = rows + lane % ROUTE_WIN
            want = jnp.where((lane // ROUTE_WIN == e) & active & (ids >= first), ids + 1, want)
        onehot = jnp.where(s1_wide == want.astype(F32), 1.0, 0.0).astype(BF16)
        acc_ref[...] += jnp.dot(onehot, stage_ref[...], preferred_element_type=F32)
        return carry

    lax.fori_loop(0, _num_passes(off_ref, b, experts, tb), one_pass, 0)
    out = x_ref[0] + gt_ref[0] * acc_ref[...]
    if fg_ref is not None:
        out = out * lax.rsqrt(jnp.mean(out * out, axis=-1, keepdims=True) + EPS) * fg_ref[...]
    o_ref[0] = out


def _combine_kernel_plain(off_ref, y_ref, slot_t_ref, x_ref, gt_ref, o_ref, stage_ref, acc_ref):
    _combine_kernel(off_ref, y_ref, slot_t_ref, x_ref, gt_ref, None, o_ref, stage_ref, acc_ref)


def expert_combine(y, slot_t, off, x, gt, final_g=None):
    bsz, n, d = x.shape
    ne, cap_pad = y.shape[1], y.shape[2]
    grid_spec = pltpu.PrefetchScalarGridSpec(
        num_scalar_prefetch=1,
        grid=(bsz, n // ROUTE_TB),
        in_specs=[
            pl.BlockSpec((1, ne, cap_pad, d), lambda b, t, off: (b, 0, 0, 0), pipeline_mode=pl.Buffered(1)),
            pl.BlockSpec((1, ROUTE_TB, ne), lambda b, t, off: (b, t, 0)),
            pl.BlockSpec((1, ROUTE_TB, d), lambda b, t, off: (b, t, 0)),
            pl.BlockSpec((1, 1, d), lambda b, t, off: (b, 0, 0)),
        ] + ([] if final_g is None else [pl.BlockSpec((1, d), lambda b, t, off: (0, 0))]),
        out_specs=pl.BlockSpec((1, ROUTE_TB, d), lambda b, t, off: (b, t, 0)),
        scratch_shapes=[pltpu.VMEM((ne * ROUTE_WIN, d), BF16), pltpu.VMEM((ROUTE_TB, d), F32)],
    )
    args = (off, y, slot_t, x, gt) + (() if final_g is None else (final_g[None, :],))
    return pl.pallas_call(
        _combine_kernel_plain if final_g is None else _combine_kernel,
        grid_spec=grid_spec,
        out_shape=jax.ShapeDtypeStruct((bsz, n, d), F32),
        compiler_params=_params("arbitrary", "arbitrary"),
        name="expert_combine",
    )(*args)


FFN_FBLK = 512
FFN_MIN_ROWS = 256


def _expert_ffn_kernel(*refs, n_sets, n_f):
    xe_refs, gate_refs = refs[:n_sets], refs[n_sets:2 * n_sets]
    wg_ref, wu_ref, wd_ref = refs[2 * n_sets:2 * n_sets + 3]
    o_refs = refs[2 * n_sets + 3:3 * n_sets + 3]
    acc_refs = refs[3 * n_sets + 3:]
    f = pl.program_id(1)
    blocks = [(s, b) for s in range(n_sets) for b in range(xe_refs[s].shape[0])]
    big = [sb for sb in blocks if xe_refs[sb[0]].shape[2] >= FFN_MIN_ROWS]
    small = [sb for sb in blocks if sb not in big]
    groups = [[sb] for sb in big[:-1]] + [big[-1:] + small] if big else [small]

    def hidden_block(first, last):
        wg = wg_ref[0].astype(BF16)
        wu = wu_ref[0].astype(BF16)
        wd = wd_ref[0].astype(BF16)
        for group in groups:
            xs = [xe_refs[s][b, 0] for s, b in group]
            xe = xs[0] if len(xs) == 1 else jnp.concatenate(xs, axis=0)
            a = jnp.dot(xe, wg, preferred_element_type=F32)
            u = jnp.dot(xe, wu, preferred_element_type=F32)
            t = jnp.dot((_silu(a) * u).astype(BF16), wd, preferred_element_type=F32)
            r0 = 0
            for s, b in group:
                rows = xe_refs[s].shape[2]
                part = t[r0:r0 + rows] if first else acc_refs[s][b] + t[r0:r0 + rows]
                if last:
                    o_refs[s][b, 0] = (part * gate_refs[s][b, 0]).astype(o_refs[s].dtype)
                else:
                    acc_refs[s][b] = part
                r0 += rows

    if n_f == 1:
        hidden_block(True, True)
        return
    pl.when(f == 0)(lambda: hidden_block(True, False))
    if n_f > 2:
        pl.when((f > 0) & (f < n_f - 1))(lambda: hidden_block(False, False))
    pl.when(f == n_f - 1)(lambda: hidden_block(False, True))


def expert_ffn(xes, gates, wg, wu, wd, l):
    n_sets = len(xes)
    _, ne, d, ff = wg.shape
    xe_specs = [pl.BlockSpec((xe.shape[0], 1) + xe.shape[2:], lambda e, f: (0, e, 0, 0)) for xe in xes]
    gate_specs = [pl.BlockSpec((g.shape[0], 1) + g.shape[2:], lambda e, f: (0, e, 0, 0)) for g in gates]
    return pl.pallas_call(
        functools.partial(_expert_ffn_kernel, n_sets=n_sets, n_f=ff // FFN_FBLK),
        grid=(ne, ff // FFN_FBLK),
        in_specs=xe_specs + gate_specs + [
            pl.BlockSpec((None, 1, d, FFN_FBLK), lambda e, f: (l, e, 0, f)),
            pl.BlockSpec((None, 1, d, FFN_FBLK), lambda e, f: (l, e, 0, f)),
            pl.BlockSpec((None, 1, FFN_FBLK, d), lambda e, f: (l, e, f, 0)),
        ],
        out_specs=xe_specs,
        out_shape=[jax.ShapeDtypeStruct(xe.shape, BF16) for xe in xes],
        scratch_shapes=[pltpu.VMEM((xe.shape[0],) + xe.shape[2:], F32) for xe in xes],
        compiler_params=_params("arbitrary", "arbitrary"),
        name="expert_ffn",
    )(*xes, *gates, wg, wu, wd)


ATT_QROWS = 4
ATT_QTOK = ATT_QROWS * GRID_W
ATT_KROWS = ATT_QROWS + WIN_ROWS
ATT_KBLK = ATT_KROWS * GRID_W // ATT_QTOK
_NT = (((1,), (1,)), ((), ()))


def _rope_tables(n):
    quarter = HEAD_DIM // 4
    inv = ROPE_BASE ** (-np.arange(quarter, dtype=np.float64) / quarter)
    t = np.arange(n)
    ang_r = (t // GRID_W)[:, None] * inv[None, :]
    ang_c = (t % GRID_W)[:, None] * inv[None, :]
    cos = np.concatenate([np.cos(ang_r)] * 2 + [np.cos(ang_c)] * 2, axis=1)
    sin = np.concatenate([-np.sin(ang_r), np.sin(ang_r), -np.sin(ang_c), np.sin(ang_c)], axis=1)
    return (jnp.asarray(np.concatenate([cos, cos], axis=1), F32),
            jnp.asarray(np.concatenate([sin, sin], axis=1), F32))


def _attn_bias_tables(rpb, n_rows):
    n_blk = n_rows // ATT_QROWS
    n_dr, n_dc = 2 * WIN_ROWS - 1, 2 * WIN_COLS - 1
    qc = np.arange(GRID_W)
    kc = np.arange(GRID_W)
    c_start = np.clip(qc - WIN_COLS // 2, 0, GRID_W - WIN_COLS)
    col_ok = (kc[None, :] >= c_start[:, None]) & (kc[None, :] < c_start[:, None] + WIN_COLS)
    d_col = np.clip(kc[None, :] - qc[:, None], 1 - WIN_COLS, WIN_COLS - 1) + (WIN_COLS - 1)
    col_sel = (d_col.reshape(-1)[None, :] == np.arange(n_dc)[:, None]).astype(np.float32)
    row_sel, oks = [], []
    for j in (0, 1, n_blk - 1):
        ks = min(max(ATT_QROWS * j - WIN_ROWS // 2, 0), n_rows - ATT_KROWS)
        r = ATT_QROWS * j + np.arange(ATT_QROWS)
        kr0 = np.clip(r - WIN_ROWS // 2, 0, n_rows - WIN_ROWS)
        krow = ks + np.arange(ATT_KROWS)
        row_ok = (krow[None, :] >= kr0[:, None]) & (krow[None, :] < kr0[:, None] + WIN_ROWS)
        d_row = np.clip(krow[None, :] - r[:, None] + (WIN_ROWS - 1), 0, n_dr - 1)
        row_sel.append((d_row.reshape(-1)[:, None] == np.arange(n_dr)[None, :]).astype(np.float32))
        oks.append(row_ok[:, None, :, None] & col_ok[None, :, None, :])
    hp = lax.Precision.HIGHEST
    a = jnp.einsum('cpr,hrd->chpd', jnp.asarray(np.stack(row_sel)), rpb.astype(F32), precision=hp)
    a = a.reshape(3, N_GROUPS, ATT_QROWS, ATT_KROWS, n_dc)
    t = jnp.einsum('chqkd,dxy->chqxky', a, jnp.asarray(col_sel.reshape(n_dc, GRID_W, GRID_W)), precision=hp)
    ok = jnp.asarray(np.stack(oks))[:, None]
    return jnp.where(ok, t, NEG_INF).reshape(3, N_GROUPS, ATT_QTOK, ATT_KROWS * GRID_W)


def _rope(x, cos_ref, sin_ref, first16):
    c = cos_ref[...]
    s = sin_ref[...]
    c2 = jnp.concatenate([c, c], axis=1)
    s2 = jnp.concatenate([s, s], axis=1)
    w = x.shape[1]
    partner = jnp.where(first16, pltpu.roll(x, w - HEAD_DIM // 4, 1), pltpu.roll(x, HEAD_DIM // 4, 1))
    return x * c2 + partner * s2


ATT_HEAD_STACK = 2


def _stack_heads(x, lane, heads):
    return jnp.concatenate([jnp.where((lane // HEAD_DIM) == h, x, 0.0) for h in heads], axis=0).astype(BF16)


def _softmax_pv(scores_values, lane, heads):
    m = None
    for s, _ in scores_values:
        mx = jnp.max(s, axis=1, keepdims=True)
        m = mx if m is None else jnp.maximum(m, mx)
    den = None
    o = None
    for s, v in scores_values:
        p = jnp.exp(s - m)
        sm = jnp.sum(p, axis=1, keepdims=True)
        den = sm if den is None else den + sm
        t = jnp.dot(p.astype(BF16), v, preferred_element_type=F32)
        o = t if o is None else o + t
    o = o * (1.0 / den)
    rows = o.shape[0] // len(heads)
    out = None
    for i, h in enumerate(heads):
        t = jnp.where((lane // HEAD_DIM) == h, o[i * rows:(i + 1) * rows], 0.0)
        out = t if out is None else out + t
    return out


def _head_groups():
    return [tuple(range(h, h + ATT_HEAD_STACK)) for h in range(0, N_GROUPS, ATT_HEAD_STACK)]


def _nattn_kernel(q_ref, k0_ref, k1_ref, k2_ref, v0_ref, v1_ref, v2_ref,
                  cq_ref, sq_ref, ck0_ref, ck1_ref, ck2_ref, sk0_ref, sk1_ref, sk2_ref,
                  kc_ref, vc_ref, bias_ref, o_ref):
    w = q_ref.shape[2]
    lane = lax.broadcasted_iota(jnp.int32, (1, w), 1)
    first16 = (lane % (HEAD_DIM // 2)) < (HEAD_DIM // 4)
    q = q_ref[0].astype(F32) * (HEAD_DIM ** -0.5)
    q_rot = _rope(q, cq_ref, sq_ref, first16)
    k_rot = jnp.concatenate([_rope(k0_ref[0].astype(F32), ck0_ref, sk0_ref, first16),
                             _rope(k1_ref[0].astype(F32), ck1_ref, sk1_ref, first16),
                             _rope(k2_ref[0].astype(F32), ck2_ref, sk2_ref, first16)], axis=0).astype(BF16)
    v = jnp.concatenate([v0_ref[0], v1_ref[0], v2_ref[0]], axis=0).astype(BF16)
    kc = kc_ref[0].astype(BF16)
    vc = vc_ref[0].astype(BF16)
    acc = None
    for heads in _head_groups():
        bias = jnp.concatenate([bias_ref[0, h] for h in heads], axis=0)
        s_win = lax.dot_general(_stack_heads(q_rot, lane, heads), k_rot, _NT, preferred_element_type=F32) + bias
        s_ctx = lax.dot_general(_stack_heads(q, lane, heads), kc, _NT, preferred_element_type=F32)
        t = _softmax_pv([(s_win, v), (s_ctx, vc)], lane, heads)
        acc = t if acc is None else acc + t
    o_ref[0] = acc.astype(o_ref.dtype)


def neighbourhood_attention(z, zc, rpb, *, q_col, k_col, v_col, kc_col, vc_col):
    bsz, n, _ = z.shape
    n_ctx = zc.shape[1]
    w = BRANCH_WIDTH
    n_blk = n // ATT_QTOK
    cos_t, sin_t = _rope_tables(n)
    bias = _attn_bias_tables(rpb, n // GRID_W)

    def kb(j):
        return jnp.clip(j - 1, 0, n_blk - ATT_KBLK)

    def zspec(col, off=None):
        if off is None:
            return pl.BlockSpec((1, ATT_QTOK, w), lambda b, j: (b, j, col))
        return pl.BlockSpec((1, ATT_QTOK, w), lambda b, j: (b, kb(j) + off, col))

    def tspec(off=None):
        if off is None:
            return pl.BlockSpec((ATT_QTOK, 2 * HEAD_DIM), lambda b, j: (j, 0))
        return pl.BlockSpec((ATT_QTOK, 2 * HEAD_DIM), lambda b, j: (kb(j) + off, 0))

    in_specs = ([zspec(q_col)] + [zspec(k_col, i) for i in range(ATT_KBLK)] + [zspec(v_col, i) for i in range(ATT_KBLK)]
                + [tspec(), tspec()] + [tspec(i) for i in range(ATT_KBLK)] * 2
                + [pl.BlockSpec((1, n_ctx, w), lambda b, j: (b, 0, kc_col)),
                   pl.BlockSpec((1, n_ctx, w), lambda b, j: (b, 0, vc_col)),
                   pl.BlockSpec((1, N_GROUPS, ATT_QTOK, ATT_KROWS * GRID_W),
                                lambda b, j: (jnp.minimum(j, 1) + j // (n_blk - 1), 0, 0, 0))])
    return pl.pallas_call(
        _nattn_kernel,
        grid=(bsz, n_blk),
        in_specs=in_specs,
        out_specs=pl.BlockSpec((1, ATT_QTOK, w), lambda b, j: (b, j, 0)),
        out_shape=jax.ShapeDtypeStruct((bsz, n, w), BF16),
        compiler_params=_params("arbitrary", "arbitrary"),
        name="neighbourhood_attention",
    )(z, z, z, z, z, z, z, cos_t, sin_t, cos_t, cos_t, cos_t, sin_t, sin_t, sin_t, zc, zc, bias)


def _ctx_attn_kernel(q_ref, kc_ref, vc_ref, o_ref):
    w = q_ref.shape[2]
    lane = lax.broadcasted_iota(jnp.int32, (1, w), 1)
    q = q_ref[0].astype(F32) * (HEAD_DIM ** -0.5)
    kc = kc_ref[0].astype(BF16)
    vc = vc_ref[0].astype(BF16)
    acc = None
    for heads in _head_groups():
        s = lax.dot_general(_stack_heads(q, lane, heads), kc, _NT, preferred_element_type=F32)
        t = _softmax_pv([(s, vc)], lane, heads)
        acc = t if acc is None else acc + t
    o_ref[0] = acc.astype(o_ref.dtype)


def context_attention(zc, *, q_col, k_col, v_col):
    bsz, n_ctx, _ = zc.shape
    w = BRANCH_WIDTH
    return pl.pallas_call(
        _ctx_attn_kernel,
        grid=(bsz,),
        in_specs=[pl.BlockSpec((1, n_ctx, w), functools.partial(lambda b, c: (b, 0, c), c=c))
                  for c in (q_col, k_col, v_col)],
        out_specs=pl.BlockSpec((1, n_ctx, w), lambda b: (b, 0, 0)),
        out_shape=jax.ShapeDtypeStruct((bsz, n_ctx, w), BF16),
        compiler_params=_params("arbitrary"),
        name="context_attention",
    )(zc, zc, zc)


def _ln(v, g, b):
    mu = jnp.mean(v, axis=-1, keepdims=True)
    var = jnp.mean(jnp.square(v - mu), axis=-1, keepdims=True)
    return (v - mu) * lax.rsqrt(var + EPS) * g + b


def _gmlp_kernel(z_ref, lng_ref, lnb_ref, w_ref, bias_ref, o_ref):
    tm = z_ref.shape[1]
    z = jax.nn.gelu(z_ref[0])
    u = z[:, :BRANCH_WIDTH]
    v = _ln(z[:, BRANCH_WIDTH:], lng_ref[...], lnb_ref[...])
    group = lax.broadcasted_iota(jnp.int32, (1, BRANCH_WIDTH), 1) // HEAD_DIM
    wcat = w_ref[...]
    for c in range(tm // CHUNK):
        rows = slice(c * CHUNK, (c + 1) * CHUNK)
        vc = v[rows]
        vst = jnp.concatenate([jnp.where(group == g, vc, 0.0) for g in range(N_GROUPS)], axis=0).astype(BF16)
        mixed = jnp.dot(wcat, vst, preferred_element_type=F32) + bias_ref[...]
        o_ref[0, rows, :] = (u[rows] * mixed).astype(o_ref.dtype)


def chunk_gmlp(za, ln_g, ln_b, w_s, b_s, *, tm):
    bsz, n, w2 = za.shape
    w = w2 // 2
    wcat = jnp.transpose(w_s, (1, 0, 2)).reshape(CHUNK, N_GROUPS * CHUNK).astype(BF16)
    bias = jnp.repeat(b_s.T, HEAD_DIM, axis=1)
    return pl.pallas_call(
        _gmlp_kernel,
        grid=(bsz, n // tm),
        in_specs=[
            pl.BlockSpec((1, tm, w2), lambda b, i: (b, i, 0)),
            pl.BlockSpec((1, w), lambda b, i: (0, 0)),
            pl.BlockSpec((1, w), lambda b, i: (0, 0)),
            pl.BlockSpec((CHUNK, N_GROUPS * CHUNK), lambda b, i: (0, 0)),
            pl.BlockSpec((CHUNK, w), lambda b, i: (0, 0)),
        ],
        out_specs=pl.BlockSpec((1, tm, w), lambda b, i: (b, i, 0)),
        out_shape=jax.ShapeDtypeStruct((bsz, n, w), BF16),
        compiler_params=_params("arbitrary", "arbitrary"),
        name="chunk_gmlp",
    )(za, ln_g[None, :], ln_b[None, :], wcat, bias)


CONV_HALO = 16
CONV_SUB = 128
SUBLANES = 8
LANES = 128


def _conv_kernel(a_ref, g_ref, ap_ref, gp_ref, an_ref, gn_ref, w_ref, cb_ref, lng_ref, lnb_ref, o_ref, y_ref):
    i = pl.program_id(1)
    tm = a_ref.shape[1]

    def glu(a, g):
        return a * _sigmoid(g)

    y_ref[0:CONV_HALO, :] = jnp.where(i > 0, glu(ap_ref[0], gp_ref[0]), 0.0)
    y_ref[CONV_HALO:CONV_HALO + tm, :] = glu(a_ref[0], g_ref[0])
    y_ref[CONV_HALO + tm:, :] = jnp.where(i < pl.num_programs(1) - 1, glu(an_ref[0], gn_ref[0]), 0.0)
    first = CONV_HALO - CONV_WIDTH // 2
    nb = CONV_SUB + 2 * CONV_HALO
    for r in range(0, tm, CONV_SUB):
        blk = y_ref[r:r + nb, :]
        acc = None
        for res in range(SUBLANES):
            rot = blk if res == 0 else pltpu.roll(blk, nb - res, 0)
            for j in range(CONV_WIDTH):
                if (first + j) % SUBLANES == res:
                    a0 = first + j - res
                    t = rot[a0:a0 + CONV_SUB, :] * w_ref[j:j + 1, :]
                    acc = t if acc is None else acc + t
        y = _ln(acc + cb_ref[...], lng_ref[...], lnb_ref[...])
        o_ref[0, r:r + CONV_SUB, :] = _silu(y).astype(o_ref.dtype)


def conformer_conv(zcv, conv_w, conv_b, ln_g, ln_b, *, tm):
    bsz, n, w2 = zcv.shape
    w = w2 // 2
    hb = tm // CONV_HALO
    n_hb = n // CONV_HALO

    def main(col):
        return pl.BlockSpec((1, tm, w), lambda b, i: (b, i, col))

    def prev(col):
        return pl.BlockSpec((1, CONV_HALO, w), lambda b, i: (b, jnp.maximum(i * hb - 1, 0), col))

    def nxt(col):
        return pl.BlockSpec((1, CONV_HALO, w), lambda b, i: (b, jnp.minimum((i + 1) * hb, n_hb - 1), col))

    vec = pl.BlockSpec((1, w), lambda b, i: (0, 0))
    return pl.pallas_call(
        _conv_kernel,
        grid=(bsz, n // tm),
        in_specs=[main(0), main(1), prev(0), prev(1), nxt(0), nxt(1),
                  pl.BlockSpec((CONV_WIDTH, w), lambda b, i: (0, 0)), vec, vec, vec],
        out_specs=pl.BlockSpec((1, tm, w), lambda b, i: (b, i, 0)),
        out_shape=jax.ShapeDtypeStruct((bsz, n, w), BF16),
        scratch_shapes=[pltpu.VMEM((tm + 2 * CONV_HALO, w), F32)],
        compiler_params=_params("arbitrary", "arbitrary"),
        name="conformer_conv",
    )(zcv, zcv, zcv, zcv, zcv, zcv, conv_w, conv_b[None, :], ln_g[None, :], ln_b[None, :])


FOURIER_N2 = 128
FOURIER_UNROLL = 8
FOURIER_KB = 8


def _np_split(m):
    m = jnp.asarray(m, F32)
    hi = m.astype(BF16)
    return hi, (m - hi.astype(F32)).astype(BF16)


def _channel_dft_matrix():
    c = np.arange(HEAD_DIM)
    ang = 2.0 * np.pi * ((c[:, None] * c[None, :]) % HEAD_DIM) / HEAD_DIM
    eye = np.eye(N_GROUPS)
    return np.concatenate([np.kron(eye, np.cos(ang)), np.kron(eye, np.sin(ang))], axis=0)


def _fourier_stage1_kernel(xa_ref, xb_ref, mh_ref, ml_ref, o_ref, *, n1, n2):
    def one_fast_index(f, carry):
        rows = pl.ds(f, n1, stride=n2)
        x = jnp.concatenate([xa_ref[0, rows, :], xb_ref[0, rows, :]], axis=1)
        x_hi, x_lo = _split_bf16(x)
        res = _dot3(mh_ref[0, f], ml_ref[0, f], x_hi, x_lo)
        o_ref[0, 0, 0, rows, :] = res[:, :LANES]
        o_ref[0, 0, 1, rows, :] = res[:, LANES:]
        return carry

    lax.fori_loop(0, n2, one_fast_index, 0, unroll=FOURIER_UNROLL)


def _fourier_stage2_kernel(br_ref, bi_ref, m2h_ref, m2l_ref, mdh_ref, mdl_ref, o_ref, *, scale):
    for k in range(br_ref.shape[3]):
        b_re = jnp.concatenate([br_ref[0, 0, 0, k], br_ref[0, 0, 1, k]], axis=1)
        b_im = jnp.concatenate([bi_ref[0, 0, 0, k], bi_ref[0, 0, 1, k]], axis=1)
        b_hi, b_lo = _split_bf16(jnp.concatenate([b_re, b_im], axis=0))
        xs = _dot3(m2h_ref[...], m2l_ref[...], b_hi, b_lo)
        x_hi, x_lo = _split_bf16(jnp.concatenate([xs[:FOURIER_N2], xs[FOURIER_N2:]], axis=1))
        o_ref[0, :, k, :] = _dot3(x_hi, x_lo, mdh_ref[...], mdl_ref[...]) * scale


def _fourier_direct_kernel(x_ref, mph_ref, mpl_ref, mdh_ref, mdl_ref, o_ref, *, scale):
    n = x_ref.shape[1]
    x_hi, x_lo = _split_bf16(x_ref[0])
    p = _dot3(mph_ref[...], mpl_ref[...], x_hi, x_lo)
    p_hi, p_lo = _split_bf16(jnp.concatenate([p[:n], p[n:]], axis=1))
    o_ref[0] = (_dot3(p_hi, p_lo, mdh_ref[...], mdl_ref[...]) * scale).astype(o_ref.dtype)


def fourier_mix(zb):
    bsz, n, w = zb.shape
    scale = float(1.0 / np.sqrt(n * HEAD_DIM))
    mdh, mdl = _np_split(_channel_dft_matrix())
    md_spec2 = pl.BlockSpec((2 * w, w), lambda b, j: (0, 0))
    if n <= 2 * FOURIER_N2:
        t = np.arange(n)
        ang = 2.0 * np.pi * ((t[:, None] * t[None, :]) % n) / n
        mph, mpl = _np_split(np.concatenate([np.cos(ang), -np.sin(ang)], axis=0))
        return pl.pallas_call(
            functools.partial(_fourier_direct_kernel, scale=scale),
            grid=(bsz,),
            in_specs=[pl.BlockSpec((1, n, w), lambda b: (b, 0, 0)),
                      pl.BlockSpec((2 * n, n), lambda b: (0, 0)), pl.BlockSpec((2 * n, n), lambda b: (0, 0)),
                      pl.BlockSpec((2 * w, w), lambda b: (0, 0)), pl.BlockSpec((2 * w, w), lambda b: (0, 0))],
            out_specs=pl.BlockSpec((1, n, w), lambda b: (b, 0, 0)),
            out_shape=jax.ShapeDtypeStruct((bsz, n, w), BF16),
            compiler_params=_params("arbitrary"),
            name="fourier_direct",
        )(zb, mph, mpl, mdh, mdl)

    n1, n2 = n // FOURIER_N2, FOURIER_N2
    f, k1, s = np.arange(n2), np.arange(n1), np.arange(n1)
    ang1 = 2.0 * np.pi * ((k1[None, :, None] * (f[:, None, None] + n2 * s[None, None, :])) % n) / n
    m1h, m1l = _np_split(np.stack([np.cos(ang1), -np.sin(ang1)]))
    k2 = np.arange(n2)
    ang2 = 2.0 * np.pi * ((k2[:, None] * f[None, :]) % n2) / n2
    c2, s2 = np.cos(ang2), np.sin(ang2)
    m2h, m2l = _np_split(np.block([[c2, s2], [-s2, c2]]))

    m1_spec = pl.BlockSpec((1, n2, n1, n1), lambda b, p: (p, 0, 0, 0))
    n_half = w // LANES
    b_st = pl.pallas_call(
        functools.partial(_fourier_stage1_kernel, n1=n1, n2=n2),
        grid=(bsz, 2),
        in_specs=[pl.BlockSpec((1, n, LANES), lambda b, p: (b, 0, 0)),
                  pl.BlockSpec((1, n, LANES), lambda b, p: (b, 0, 1)), m1_spec, m1_spec],
        out_specs=pl.BlockSpec((1, 1, n_half, n, LANES), lambda b, p: (b, p, 0, 0, 0)),
        out_shape=jax.ShapeDtypeStruct((bsz, 2, n_half, n, LANES), F32),
        compiler_params=_params("arbitrary", "arbitrary"),
        name="fourier_stage1",
    )(zb, zb, m1h, m1l)

    kb = FOURIER_KB
    b_st = b_st.reshape(bsz, 2, n_half, n1, n2, LANES)
    out = pl.pallas_call(
        functools.partial(_fourier_stage2_kernel, scale=scale),
        grid=(bsz, n1 // kb),
        in_specs=[pl.BlockSpec((1, 1, n_half, kb, n2, LANES), lambda b, j: (b, 0, 0, j, 0, 0)),
                  pl.BlockSpec((1, 1, n_half, kb, n2, LANES), lambda b, j: (b, 1, 0, j, 0, 0)),
                  pl.BlockSpec((2 * n2, 2 * n2), lambda b, j: (0, 0)),
                  pl.BlockSpec((2 * n2, 2 * n2), lambda b, j: (0, 0)),
                  md_spec2, md_spec2],
        out_specs=pl.BlockSpec((1, n2, kb, w), lambda b, j: (b, 0, j, 0)),
        out_shape=jax.ShapeDtypeStruct((bsz, n2, n1, w), F32),
        compiler_params=_params("arbitrary", "arbitrary"),
        name="fourier_stage2",
    )(b_st, b_st, m2h, m2l, mdh, mdl)
    return out.reshape(bsz, n, w)


def expert_route(h, aff_t):
    n = h.shape[1]
    cap = EC_CAPACITY * n // N_EXPERTS
    slot, off = expert_topk(aff_t, cap)
    xe, gate = expert_gather(h, slot, aff_t, off, max(cap, ROUTE_WIN))
    return xe, gate, jnp.swapaxes(slot, 1, 2), off


def kernel(x, c, ctx, c_ctx, w_mod, b_mod, norm1_g, norm2_g, w_in, sgu_ln_g, sgu_ln_b, w_spatial, b_spatial,
           w_a_out, w_b_out, conv_w, conv_b, conv_ln_g, conv_ln_b, w_c_out, rpb, w_d_out, w_out, w_router,
           w_gate_e, w_up_e, w_down_e, final_norm_g):
    bsz = x.shape[0]
    xc = ctx
    assert bsz + 1 <= MOD_ROWS
    cond = jnp.concatenate([c, c_ctx[None, :], jnp.zeros((MOD_ROWS - bsz - 1, D_MODEL), F32)], axis=0)
    mod_all = modulation(cond, w_mod, b_mod)
    in_scale = jnp.concatenate([jnp.ones((V_END,), F32), jnp.full((IN_COLS - V_END,), GATE_HALF, F32)])
    w_in_b = (w_in * in_scale).astype(BF16)
    for l in range(DEPTH):
        last = l == DEPTH - 1
        sh1, sc1, gt1, sh2, sc2, gt2 = jnp.split(mod_all[l, :bsz, None, :], 6, axis=-1)
        cmod = jnp.broadcast_to(mod_all[l, bsz:bsz + 1, None, :], (bsz, 1, 6 * D_MODEL))
        csh1, csc1, cgt1, csh2, csc2, cgt2 = jnp.split(cmod, 6, axis=-1)

        wa, wb, wc, wd_ = ((w * GATE_HALF).astype(BF16) for w in (w_a_out[l], w_b_out[l], w_c_out[l], w_d_out[l]))
        wo = w_out[l].astype(BF16)
        g1 = norm1_g[l][None, :]

        za, zb, zcv, zqkv, zg = norm_inproj(x, g1, sc1, sh1, w_in_b, l, 0, IN_WIDTHS, IN_DTYPES, tm=512)
        if last:
            (zqkv_c,) = norm_inproj(xc, g1, csc1, csh1, w_in_b, l, Q_END, (2 * BRANCH_WIDTH,), (BF16,), tm=CTX_TM)
            kc_col, vc_col = 0, 1
        else:
            cza, czb, czcv, zqkv_c, czg = norm_inproj(xc, g1, csc1, csh1, w_in_b, l, 0, IN_WIDTHS, IN_DTYPES,
                                                      tm=CTX_TM)
            kc_col, vc_col = 1, 2
        d_lat = neighbourhood_attention(zqkv, zqkv_c, rpb[l], q_col=0, k_col=1, v_col=2,
                                        kc_col=kc_col, vc_col=vc_col)
        a_lat = chunk_gmlp(za, sgu_ln_g[l], sgu_ln_b[l], w_spatial[l], b_spatial[l], tm=512)
        b_lat = fourier_mix(zb)
        c_lat = conformer_conv(zcv, conv_w[l], conv_b[l], conv_ln_g[l], conv_ln_b[l], tm=512)
        g2 = norm2_g[l][None, :]
        router_t = w_router[l].T
        x, h2, aff_t = merge_branches(a_lat, b_lat, c_lat, d_lat, zg, wa, wb, wc, wd_, wo, x, gt1,
                                      g2, sc2, sh2, router_t, tm=512)
        if not last:
            a_c = chunk_gmlp(cza, sgu_ln_g[l], sgu_ln_b[l], w_spatial[l], b_spatial[l], tm=CTX_TM)
            b_c = fourier_mix(czb)
            c_c = conformer_conv(czcv, conv_w[l], conv_b[l], conv_ln_g[l], conv_ln_b[l], tm=CTX_TM)
            d_c = context_attention(zqkv_c, q_col=0, k_col=1, v_col=2)
            xc, hc2, aff_tc = merge_branches(a_c, b_c, c_c, d_c, czg, wa, wb, wc, wd_, wo, xc, cgt1,
                                             g2, csc2, csh2, router_t, tm=CTX_TM)

        xe, gate, slot_t, off = expert_route(h2, aff_t)
        if last:
            (y,) = expert_ffn([xe], [gate], w_gate_e, w_up_e, w_down_e, l)
        else:
            xe_c, gate_c, slot_tc, off_c = expert_route(hc2, aff_tc)
            y, y_c = expert_ffn([xe, xe_c], [gate, gate_c], w_gate_e, w_up_e, w_down_e, l)
            xc = expert_combine(y_c, slot_tc, off_c, xc, cgt2)
        x = expert_combine(y, slot_t, off, x, gt2, final_norm_g if last else None)
    return x


CTX_TM = 256
```

```python
import functools

import jax
import jax.numpy as jnp
import numpy as np
from jax import lax
from jax.experimental import pallas as pl
from jax.experimental.pallas import tpu as pltpu

D_MODEL = 1024
DEPTH = 2
GRID_W = 64
HEAD_DIM = 64
N_GROUPS = 4
BRANCH_WIDTH = N_GROUPS * HEAD_DIM
N_BRANCH = 4
CHUNK = 128
CONV_WIDTH = 31
WIN_ROWS = 8
WIN_COLS = 16
QB_COLS = 16
KB_COLS = QB_COLS + WIN_COLS
ROPE_BASE = 10000.0
N_EXPERTS = 16
EXPERT_FF = 1024
EC_CAPACITY = 2
EPS = 1e-6
NEG_INF = -1e30
A_END = 2 * BRANCH_WIDTH
B_END = A_END + BRANCH_WIDTH
C_END = B_END + 2 * BRANCH_WIDTH
Q_END = C_END + BRANCH_WIDTH
K_END = Q_END + BRANCH_WIDTH
V_END = K_END + BRANCH_WIDTH
IN_COLS = V_END + N_BRANCH * D_MODEL
SPLITS = [A_END, B_END, C_END, Q_END, K_END, V_END]
IN_WIDTHS = (A_END, B_END - A_END, C_END - B_END, V_END - C_END, IN_COLS - V_END)
IN_DTYPES = (jnp.float32, jnp.float32, jnp.float32, jnp.bfloat16, jnp.bfloat16)
GATE_HALF = 0.5

VMEM_LIMIT_BYTES = 56 * 1024 * 1024
F32 = jnp.float32
BF16 = jnp.bfloat16


def _sigmoid(v):
    return 0.5 * jnp.tanh(0.5 * v) + 0.5


def _silu(v):
    return v * _sigmoid(v)


def _params(*sem):
    return pltpu.CompilerParams(dimension_semantics=sem, vmem_limit_bytes=VMEM_LIMIT_BYTES)


MOD_ROWS = 8
MOD_TN = 1536


def _split_bf16(v):
    hi = v.astype(BF16)
    return hi, (v - hi.astype(F32)).astype(BF16)


def _dot3(a_hi, a_lo, b_hi, b_lo):
    return (jnp.dot(a_hi, b_hi, preferred_element_type=F32) + jnp.dot(a_hi, b_lo, preferred_element_type=F32)
            + jnp.dot(a_lo, b_hi, preferred_element_type=F32))


def _mod_kernel(c_ref, w_ref, b_ref, o_ref):
    s_hi, s_lo = _split_bf16(_silu(c_ref[...]))
    w_hi, w_lo = _split_bf16(w_ref[0])
    o_ref[0] = _dot3(s_hi, s_lo, w_hi, w_lo) + b_ref[0]


def modulation(cond, w_mod, b_mod):
    n_layers, d, cols = w_mod.shape
    return pl.pallas_call(
        _mod_kernel,
        grid=(n_layers, cols // MOD_TN),
        in_specs=[pl.BlockSpec((MOD_ROWS, d), lambda l, j: (0, 0)),
                  pl.BlockSpec((1, d, MOD_TN), lambda l, j: (l, 0, j)),
                  pl.BlockSpec((1, 1, MOD_TN), lambda l, j: (l, 0, j))],
        out_specs=pl.BlockSpec((1, MOD_ROWS, MOD_TN), lambda l, j: (l, 0, j)),
        out_shape=jax.ShapeDtypeStruct((n_layers, MOD_ROWS, cols), F32),
        compiler_params=_params("arbitrary", "arbitrary"),
        name="modulation",
    )(cond, w_mod, b_mod[:, None, :])


INPROJ_COL_CHUNK = 512


def _norm_inproj_kernel(x_ref, g_ref, sc_ref, sh_ref, w_ref, *o_refs):
    x = x_ref[0]
    y = x * lax.rsqrt(jnp.mean(x * x, axis=-1, keepdims=True) + EPS) * g_ref[...]
    h = (y * (1.0 + sc_ref[0]) + sh_ref[0]).astype(BF16)
    off = 0
    for o_ref in o_refs:
        width = o_ref.shape[2]
        for c0 in range(0, width, INPROJ_COL_CHUNK):
            cw = min(INPROJ_COL_CHUNK, width - c0)
            o_ref[0, :, c0:c0 + cw] = jnp.dot(h, w_ref[:, off + c0:off + c0 + cw],
                                              preferred_element_type=F32).astype(o_ref.dtype)
        off += width


def norm_inproj(x, g, sc, sh, w, l, col0, widths, dtypes, *, tm):
    bsz, n, d = x.shape
    cols = sum(widths)
    assert col0 % cols == 0
    return pl.pallas_call(
        _norm_inproj_kernel,
        grid=(bsz, n // tm),
        in_specs=[
            pl.BlockSpec((1, tm, d), lambda b, i: (b, i, 0)),
            pl.BlockSpec((1, d), lambda b, i: (0, 0)),
            pl.BlockSpec((1, 1, d), lambda b, i: (b, 0, 0)),
            pl.BlockSpec((1, 1, d), lambda b, i: (b, 0, 0)),
            pl.BlockSpec((None, d, cols), lambda b, i: (l, 0, col0 // cols), pipeline_mode=pl.Buffered(1)),
        ],
        out_specs=[pl.BlockSpec((1, tm, wd), lambda b, i: (b, i, 0)) for wd in widths],
        out_shape=[jax.ShapeDtypeStruct((bsz, n, wd), dt) for wd, dt in zip(widths, dtypes, strict=True)],
        compiler_params=_params("arbitrary", "arbitrary"),
        name="norm_inproj",
    )(x, g, sc, sh, w)


def _merge_kernel(a_ref, b_ref, c_ref, d_ref, ga_ref, gb_ref, gc_ref, gd_ref,
                  wa_ref, wb_ref, wc_ref, wd_ref, wo_ref, x_ref, gt_ref,
                  g2_ref, sc2_ref, sh2_ref, rt_ref, o_ref, h_ref, aff_ref):
    m = None
    for br, gz, w in ((a_ref, ga_ref, wa_ref), (b_ref, gb_ref, wb_ref),
                      (c_ref, gc_ref, wc_ref), (d_ref, gd_ref, wd_ref)):
        hp = jnp.dot(br[0].astype(BF16), w[...], preferred_element_type=F32)
        t = hp * jnp.tanh(gz[0].astype(F32)) + hp
        m = t if m is None else m + t
    mix = jnp.dot(m.astype(BF16), wo_ref[...], preferred_element_type=F32)
    x = x_ref[0] + gt_ref[0] * mix
    o_ref[0] = x
    y = x * lax.rsqrt(jnp.mean(x * x, axis=-1, keepdims=True) + EPS) * g2_ref[...]
    h = y * (1.0 + sc2_ref[0]) + sh2_ref[0]
    h_hi, h_lo = _split_bf16(h)
    r_hi, r_lo = _split_bf16(rt_ref[...])
    ne = rt_ref.shape[0]
    both = lax.dot_general(jnp.concatenate([r_hi, r_lo], axis=0), h_hi, _NT, preferred_element_type=F32)
    logits = both[:ne] + both[ne:] + lax.dot_general(r_hi, h_lo, _NT, preferred_element_type=F32)
    pr = jnp.exp(logits - jnp.max(logits, axis=0, keepdims=True))
    aff_ref[0] = pr / jnp.sum(pr, axis=0, keepdims=True)
    h_ref[0] = h_hi


def merge_branches(a, b, cc, d, z, w_a, w_b, w_c, w_d, w_o, x, gt, g2, sc2, sh2, router_t, *, tm):
    bsz, n, dm = x.shape
    w = a.shape[-1]
    ne = router_t.shape[0]
    br_spec = pl.BlockSpec((1, tm, w), lambda bi, i: (bi, i, 0))
    gate_specs = [pl.BlockSpec((1, tm, dm), functools.partial(lambda bi, i, k: (bi, i, k), k=k))
                  for k in range(N_BRANCH)]
    wbr_spec = pl.BlockSpec((w, dm), lambda bi, i: (0, 0))
    row_spec = pl.BlockSpec((1, tm, dm), lambda bi, i: (bi, i, 0))
    mod_spec = pl.BlockSpec((1, 1, dm), lambda bi, i: (bi, 0, 0))
    return pl.pallas_call(
        _merge_kernel,
        grid=(bsz, n // tm),
        in_specs=[br_spec] * 4 + gate_specs + [wbr_spec] * 4 + [
            pl.BlockSpec((dm, dm), lambda bi, i: (0, 0)), row_spec, mod_spec,
            pl.BlockSpec((1, dm), lambda bi, i: (0, 0)), mod_spec, mod_spec,
            pl.BlockSpec((ne, dm), lambda bi, i: (0, 0)),
        ],
        out_specs=[row_spec, row_spec, pl.BlockSpec((1, ne, tm), lambda bi, i: (bi, 0, i))],
        out_shape=[jax.ShapeDtypeStruct((bsz, n, dm), F32), jax.ShapeDtypeStruct((bsz, n, dm), BF16),
                   jax.ShapeDtypeStruct((bsz, ne, n), F32)],
        compiler_params=_params("arbitrary", "arbitrary"),
        name="merge_branches",
    )(a, b, cc, d, z, z, z, z, w_a, w_b, w_c, w_d, w_o, x, gt, g2, sc2, sh2, router_t)


ROUTE_TB = 256
ROUTE_WIN = 64
ROUTE_ALIGN = 16
OFF_LANES = 128
TOPK_EXP_STEPS = (64, 32, 16, 8, 4, 2, 1)
TOPK_BISECT_STEPS = 32


def _topk_kernel(aff_ref, slot_ref, off_ref, *, cap):
    a = aff_ref[0]
    ne, n = a.shape
    capf = jnp.float32(cap)

    def count_ge(t):
        return jnp.sum(jnp.where(a >= t, 1.0, 0.0), axis=1, keepdims=True)

    hi = jnp.full((ne, 1), 2.0, F32)
    for s in TOPK_EXP_STEPS:
        cand = hi * (2.0 ** -s)
        hi = jnp.where(count_ge(cand) < capf, cand, hi)
    lo = jnp.where(hi <= 2.0 ** -126, 0.0, hi * 0.5)

    def bisect(_, carry):
        lo, hi = carry
        mid = 0.5 * (lo + hi)
        ok = count_ge(mid) >= capf
        return jnp.where(ok, mid, lo), jnp.where(ok, hi, mid)

    lo, hi = lax.fori_loop(0, TOPK_BISECT_STEPS, bisect, (lo, hi))
    need = capf - count_ge(hi)
    r_i = lax.broadcasted_iota(jnp.int32, (ROUTE_TB, ROUTE_TB), 0)
    c_i = lax.broadcasted_iota(jnp.int32, (ROUTE_TB, ROUTE_TB), 1)
    tri = jnp.where(r_i < c_i, 1.0, 0.0).astype(BF16)
    lane = lax.broadcasted_iota(jnp.int32, (ne, OFF_LANES), 1)
    run_eq = jnp.zeros((ne, 1), F32)
    run_sel = jnp.zeros((ne, 1), F32)
    offs = jnp.zeros((ne, OFF_LANES), F32)
    for c in range(n // ROUTE_TB):
        cols = slice(c * ROUTE_TB, (c + 1) * ROUTE_TB)
        a_c = a[:, cols]
        above = a_c >= hi
        eq_c = jnp.where(above, 0.0, jnp.where(a_c >= lo, 1.0, 0.0))
        rank = jnp.dot(eq_c.astype(BF16), tri, preferred_element_type=F32) + run_eq
        sel = jnp.where(above, 1.0, jnp.where(rank < need, eq_c, 0.0))
        pos = jnp.dot(sel.astype(BF16), tri, preferred_element_type=F32) + run_sel
        slot_ref[0, :, cols] = jnp.where(sel > 0.0, pos, -1.0).astype(jnp.int32)
        offs = jnp.where(lane == c, run_sel, offs)
        run_eq = run_eq + jnp.sum(eq_c, axis=1, keepdims=True)
        run_sel = run_sel + jnp.sum(sel, axis=1, keepdims=True)
    offs = jnp.where(lane == n // ROUTE_TB, run_sel, offs)
    off_ref[0] = offs.astype(jnp.int32)


def expert_topk(aff_t, cap):
    bsz, ne, n = aff_t.shape
    return pl.pallas_call(
        functools.partial(_topk_kernel, cap=cap),
        grid=(bsz,),
        in_specs=[pl.BlockSpec((1, ne, n), lambda b: (b, 0, 0))],
        out_specs=[pl.BlockSpec((1, ne, n), lambda b: (b, 0, 0)),
                   pl.BlockSpec((1, ne, OFF_LANES), lambda b: (b, 0, 0))],
        out_shape=[jax.ShapeDtypeStruct((bsz, ne, n), jnp.int32),
                   jax.ShapeDtypeStruct((bsz, ne, OFF_LANES), jnp.int32)],
        compiler_params=_params("arbitrary"),
        name="expert_topk",
    )(aff_t)


def _window_plan(off_ref, b, e, tb, w, rows_max):
    start = off_ref[b, e, tb]
    stop = off_ref[b, e, tb + 1]
    first = (start // ROUTE_ALIGN) * ROUTE_ALIGN + w * ROUTE_WIN
    active = first < stop
    return pl.multiple_of(jnp.minimum(first, rows_max), ROUTE_ALIGN), first, active


def _num_passes(off_ref, b, e_list, tb):
    n_pass = jnp.int32(0)
    for e in e_list:
        start = off_ref[b, e, tb]
        stop = off_ref[b, e, tb + 1]
        base = (start // ROUTE_ALIGN) * ROUTE_ALIGN
        n_pass = jnp.maximum(n_pass, (stop - base + ROUTE_WIN - 1) // ROUTE_WIN)
    return n_pass


def _gather_kernel(off_ref, h_ref, slot_ref, aff_ref, xe_ref, gate_ref, *, group):
    b, g, tb = pl.program_id(0), pl.program_id(1), pl.program_id(2)
    rows_max = xe_ref.shape[2] - ROUTE_WIN

    @pl.when(tb == 0)
    def _():
        xe_ref[...] = jnp.zeros(xe_ref.shape, xe_ref.dtype)
        gate_ref[...] = jnp.zeros(gate_ref.shape, gate_ref.dtype)

    sub = lax.broadcasted_iota(jnp.int32, (ROUTE_WIN, 1), 0)
    experts = [g * group + e for e in range(group)]

    def one_pass(w, carry):
        plans = [_window_plan(off_ref, b, ge, tb, w, rows_max) for ge in experts]
        hots = []
        for e, (rows, first, active) in enumerate(plans):
            ids = rows + sub
            want = jnp.where(active & (ids >= first), ids, -2)
            hots.append(slot_ref[0, e:e + 1, :] == want)
        onehot = jnp.concatenate([jnp.where(hm, 1.0, 0.0) for hm in hots], axis=0).astype(BF16)
        res = jnp.dot(onehot, h_ref[0], preferred_element_type=F32)
        for e, (rows, _, _) in enumerate(plans):
            win = pl.ds(rows, ROUTE_WIN)
            xe_ref[0, e, win, :] = xe_ref[0, e, win, :] + res[e * ROUTE_WIN:(e + 1) * ROUTE_WIN].astype(BF16)
            gsel = jnp.sum(jnp.where(hots[e], aff_ref[0, e:e + 1, :], 0.0), axis=1, keepdims=True)
            gate_ref[0, e, win, :] = gate_ref[0, e, win, :] + gsel
        return carry

    lax.fori_loop(0, _num_passes(off_ref, b, experts, tb), one_pass, 0)


def expert_gather(h, slot, aff_t, off, cap_pad, *, group=8):
    bsz, n, d = h.shape
    ne = slot.shape[1]
    grid_spec = pltpu.PrefetchScalarGridSpec(
        num_scalar_prefetch=1,
        grid=(bsz, ne // group, n // ROUTE_TB),
        in_specs=[
            pl.BlockSpec((1, ROUTE_TB, d), lambda b, g, t, off: (b, t, 0)),
            pl.BlockSpec((1, group, ROUTE_TB), lambda b, g, t, off: (b, g, t)),
            pl.BlockSpec((1, group, ROUTE_TB), lambda b, g, t, off: (b, g, t)),
        ],
        out_specs=[pl.BlockSpec((1, group, cap_pad, d), lambda b, g, t, off: (b, g, 0, 0)),
                   pl.BlockSpec((1, group, cap_pad, 1), lambda b, g, t, off: (b, g, 0, 0))],
    )
    return pl.pallas_call(
        functools.partial(_gather_kernel, group=group),
        grid_spec=grid_spec,
        out_shape=[jax.ShapeDtypeStruct((bsz, ne, cap_pad, d), BF16),
                   jax.ShapeDtypeStruct((bsz, ne, cap_pad, 1), F32)],
        compiler_params=_params("arbitrary", "arbitrary", "arbitrary"),
        name="expert_gather",
    )(off, h, slot, aff_t)


def _combine_kernel(off_ref, y_ref, slot_t_ref, x_ref, gt_ref, fg_ref, o_ref, stage_ref, acc_ref):
    b, tb = pl.program_id(0), pl.program_id(1)
    ne = y_ref.shape[1]
    rows_max = y_ref.shape[2] - ROUTE_WIN
    width = ne * ROUTE_WIN
    experts = list(range(ne))
    lane = lax.broadcasted_iota(jnp.int32, (1, width), 1)
    s1 = slot_t_ref[0] + 1
    hi = (s1 // ROUTE_WIN).astype(F32).astype(BF16)
    lo = (s1 % ROUTE_WIN).astype(F32).astype(BF16)
    e_row = lax.broadcasted_iota(jnp.int32, (ne, width), 0)
    e_lane = lax.broadcasted_iota(jnp.int32, (ne, width), 1) // ROUTE_WIN
    expand = jnp.where(e_row == e_lane, 1.0, 0.0).astype(BF16)
    s1_wide = (float(ROUTE_WIN) * jnp.dot(hi, expand, preferred_element_type=F32)
               + jnp.dot(lo, expand, preferred_element_type=F32))
    acc_ref[...] = jnp.zeros(acc_ref.shape, F32)

    def one_pass(w, carry):
        want = jnp.full((1, width), -1, jnp.int32)
        for e in experts:
            rows, first, active = _window_plan(off_ref, b, e, tb, w, rows_max)
            stage_ref[e * ROUTE_WIN:(e + 1) * ROUTE_WIN, :] = y_ref[0, e, pl.ds(rows, ROUTE_WIN), :]
            ids = rows + lane % ROUTE_WIN
            want = jnp.where((lane // ROUTE_WIN == e) & active & (ids >= first), ids + 1, want)
        onehot = jnp.where(s1_wide == want.astype(F32), 1.0, 0.0).astype(BF16)
        acc_ref[...] += jnp.dot(onehot, stage_ref[...], preferred_element_type=F32)
        return carry

    lax.fori_loop(0, _num_passes(off_ref, b, experts, tb), one_pass, 0)
    out = x_ref[0] + gt_ref[0] * acc_ref[...]
    if fg_ref is not None:
        out = out * lax.rsqrt(jnp.mean(out * out, axis=-1, keepdims=True) + EPS) * fg_ref[...]
    o_ref[0] = out


def _combine_kernel_plain(off_ref, y_ref, slot_t_ref, x_ref, gt_ref, o_ref, stage_ref, acc_ref):
    _combine_kernel(off_ref, y_ref, slot_t_ref, x_ref, gt_ref, None, o_ref, stage_ref, acc_ref)


def expert_combine(y, slot_t, off, x, gt, final_g=None):
    bsz, n, d = x.shape
    ne, cap_pad = y.shape[1], y.shape[2]
    grid_spec = pltpu.PrefetchScalarGridSpec(
        num_scalar_prefetch=1,
        grid=(bsz, n // ROUTE_TB),
        in_specs=[
            pl.BlockSpec((1, ne, cap_pad, d), lambda b, t, off: (b, 0, 0, 0), pipeline_mode=pl.Buffered(1)),
            pl.BlockSpec((1, ROUTE_TB, ne), lambda b, t, off: (b, t, 0)),
            pl.BlockSpec((1, ROUTE_TB, d), lambda b, t, off: (b, t, 0)),
            pl.BlockSpec((1, 1, d), lambda b, t, off: (b, 0, 0)),
        ] + ([] if final_g is None else [pl.BlockSpec((1, d), lambda b, t, off: (0, 0))]),
        out_specs=pl.BlockSpec((1, ROUTE_TB, d), lambda b, t, off: (b, t, 0)),
        scratch_shapes=[pltpu.VMEM((ne * ROUTE_WIN, d), BF16), pltpu.VMEM((ROUTE_TB, d), F32)],
    )
    args = (off, y, slot_t, x, gt) + (() if final_g is None else (final_g[None, :],))
    return pl.pallas_call(
        _combine_kernel_plain if final_g is None else _combine_kernel,
        grid_spec=grid_spec,
        out_shape=jax.ShapeDtypeStruct((bsz, n, d), F32),
        compiler_params=_params("arbitrary", "arbitrary"),
        name="expert_combine",
    )(*args)


FFN_FBLK = 512
FFN_MIN_ROWS = 256


def _expert_ffn_kernel(*refs, n_sets, n_f):
    xe_refs, gate_refs = refs[:n_sets], refs[n_sets:2 * n_sets]
    wg_ref, wu_ref, wd_ref = refs[2 * n_sets:2 * n_sets + 3]
    o_refs = refs[2 * n_sets + 3:3 * n_sets + 3]
    acc_refs = refs[3 * n_sets + 3:]
    f = pl.program_id(1)
    blocks = [(s, b) for s in range(n_sets) for b in range(xe_refs[s].shape[0])]
    big = [sb for sb in blocks if xe_refs[sb[0]].shape[2] >= FFN_MIN_ROWS]
    small = [sb for sb in blocks if sb not in big]
    groups = [[sb] for sb in big[:-1]] + [big[-1:] + small] if big else [small]

    def hidden_block(first, last):
        wg = wg_ref[0].astype(BF16)
        wu = wu_ref[0].astype(BF16)
        wd = wd_ref[0].astype(BF16)
        for group in groups:
            xs = [xe_refs[s][b, 0] for s, b in group]
            xe = xs[0] if len(xs) == 1 else jnp.concatenate(xs, axis=0)
            a = jnp.dot(xe, wg, preferred_element_type=F32)
            u = jnp.dot(xe, wu, preferred_element_type=F32)
            t = jnp.dot((_silu(a) * u).astype(BF16), wd, preferred_element_type=F32)
            r0 = 0
            for s, b in group:
                rows = xe_refs[s].shape[2]
                part = t[r0:r0 + rows] if first else acc_refs[s][b] + t[r0:r0 + rows]
                if last:
                    o_refs[s][b, 0] = (part * gate_refs[s][b, 0]).astype(o_refs[s].dtype)
                else:
                    acc_refs[s][b] = part
                r0 += rows

    if n_f == 1:
        hidden_block(True, True)
        return
    pl.when(f == 0)(lambda: hidden_block(True, False))
    if n_f > 2:
        pl.when((f > 0) & (f < n_f - 1))(lambda: hidden_block(False, False))
    pl.when(f == n_f - 1)(lambda: hidden_block(False, True))


def expert_ffn(xes, gates, wg, wu, wd, l):
    n_sets = len(xes)
    _, ne, d, ff = wg.shape
    xe_specs = [pl.BlockSpec((xe.shape[0], 1) + xe.shape[2:], lambda e, f: (0, e, 0, 0)) for xe in xes]
    gate_specs = [pl.BlockSpec((g.shape[0], 1) + g.shape[2:], lambda e, f: (0, e, 0, 0)) for g in gates]
    return pl.pallas_call(
        functools.partial(_expert_ffn_kernel, n_sets=n_sets, n_f=ff // FFN_FBLK),
        grid=(ne, ff // FFN_FBLK),
        in_specs=xe_specs + gate_specs + [
            pl.BlockSpec((None, 1, d, FFN_FBLK), lambda e, f: (l, e, 0, f)),
            pl.BlockSpec((None, 1, d, FFN_FBLK), lambda e, f: (l, e, 0, f)),
            pl.BlockSpec((None, 1, FFN_FBLK, d), lambda e, f: (l, e, f, 0)),
        ],
        out_specs=xe_specs,
        out_shape=[jax.ShapeDtypeStruct(xe.shape, BF16) for xe in xes],
        scratch_shapes=[pltpu.VMEM((xe.shape[0],) + xe.shape[2:], F32) for xe in xes],
        compiler_params=_params("arbitrary", "arbitrary"),
        name="expert_ffn",
    )(*xes, *gates, wg, wu, wd)


ATT_QROWS = 4
ATT_QTOK = ATT_QROWS * GRID_W
ATT_KROWS = ATT_QROWS + WIN_ROWS
ATT_KBLK = ATT_KROWS * GRID_W // ATT_QTOK
_NT = (((1,), (1,)), ((), ()))


def _rope_tables(n):
    quarter = HEAD_DIM // 4
    inv = ROPE_BASE ** (-np.arange(quarter, dtype=np.float64) / quarter)
    t = np.arange(n)
    ang_r = (t // GRID_W)[:, None] * inv[None, :]
    ang_c = (t % GRID_W)[:, None] * inv[None, :]
    cos = np.concatenate([np.cos(ang_r)] * 2 + [np.cos(ang_c)] * 2, axis=1)
    sin = np.concatenate([-np.sin(ang_r), np.sin(ang_r), -np.sin(ang_c), np.sin(ang_c)], axis=1)
    return (jnp.asarray(np.concatenate([cos, cos], axis=1), F32),
            jnp.asarray(np.concatenate([sin, sin], axis=1), F32))


def _attn_bias_tables(rpb, n_rows):
    n_blk = n_rows // ATT_QROWS
    n_dr = 2 * WIN_ROWS - 1
    qc = np.arange(GRID_W)
    kc = np.arange(GRID_W)
    c_start = np.clip(qc - WIN_COLS // 2, 0, GRID_W - WIN_COLS)
    col_ok = (kc[None, :] >= c_start[:, None]) & (kc[None, :] < c_start[:, None] + WIN_COLS)
    assert np.all(np.abs(kc[None, :] - qc[:, None])[col_ok] < WIN_COLS)
    row_sel, oks = [], []
    for j in (0, 1, n_blk - 1):
        ks = min(max(ATT_QROWS * j - WIN_ROWS // 2, 0), n_rows - ATT_KROWS)
        r = ATT_QROWS * j + np.arange(ATT_QROWS)
        kr0 = np.clip(r - WIN_ROWS // 2, 0, n_rows - WIN_ROWS)
        krow = ks + np.arange(ATT_KROWS)
        row_ok = (krow[None, :] >= kr0[:, None]) & (krow[None, :] < kr0[:, None] + WIN_ROWS)
        d_row = np.clip(krow[None, :] - r[:, None] + (WIN_ROWS - 1), 0, n_dr - 1)
        row_sel.append((d_row.reshape(-1)[:, None] == np.arange(n_dr)[None, :]).astype(np.float32))
        oks.append(row_ok[:, None, :, None] & col_ok[None, :, None, :])
    a = jnp.einsum('cpr,hrd->chpd', jnp.asarray(np.stack(row_sel)), rpb.astype(F32), precision=lax.Precision.HIGHEST)
    a = a.reshape(3, N_GROUPS, ATT_QROWS, ATT_KROWS, 2 * WIN_COLS - 1)
    pad = jnp.zeros(a.shape[:-1] + (GRID_W - (2 * WIN_COLS - 1),), F32)
    v = jnp.concatenate([a[..., WIN_COLS - 1:], pad, jnp.roll(a, -1, axis=3)[..., :WIN_COLS - 1]], axis=-1)
    v = v.reshape(3, N_GROUPS, ATT_QROWS, ATT_KROWS * GRID_W)
    ok = jnp.asarray(np.stack(oks).reshape(3, ATT_QTOK, ATT_KROWS * GRID_W).astype(np.float32))
    return pl.pallas_call(
        _attn_bias_kernel,
        grid=(3, N_GROUPS),
        in_specs=[pl.BlockSpec((1, 1, ATT_QROWS, ATT_KROWS * GRID_W), lambda c, h: (c, h, 0, 0)),
                  pl.BlockSpec((1, ATT_QTOK, ATT_KROWS * GRID_W), lambda c, h: (c, 0, 0))],
        out_specs=pl.BlockSpec((1, 1, ATT_QTOK, ATT_KROWS * GRID_W), lambda c, h: (c, h, 0, 0)),
        out_shape=jax.ShapeDtypeStruct((3, N_GROUPS, ATT_QTOK, ATT_KROWS * GRID_W), F32),
        compiler_params=_params("arbitrary", "arbitrary"),
        name="attn_bias",
    )(v, ok)


def _attn_bias_kernel(v_ref, ok_ref, o_ref):
    width = v_ref.shape[3]
    for qr in range(ATT_QROWS):
        rows = slice(qr * GRID_W, (qr + 1) * GRID_W)
        base = jnp.broadcast_to(v_ref[0, 0, qr:qr + 1, :], (GRID_W, width))
        slab = pltpu.roll(base, 0, 1, stride=1, stride_axis=0)
        o_ref[0, 0, rows, :] = jnp.where(ok_ref[0, rows, :] > 0.0, slab, NEG_INF)


def _rope(x, cos_ref, sin_ref, first16):
    c = cos_ref[...]
    s = sin_ref[...]
    c2 = jnp.concatenate([c, c], axis=1)
    s2 = jnp.concatenate([s, s], axis=1)
    w = x.shape[1]
    partner = jnp.where(first16, pltpu.roll(x, w - HEAD_DIM // 4, 1), pltpu.roll(x, HEAD_DIM // 4, 1))
    return x * c2 + partner * s2


ATT_HEAD_STACK = 1


def _stack_heads(x, lane, heads):
    return jnp.concatenate([jnp.where((lane // HEAD_DIM) == h, x, 0.0) for h in heads], axis=0).astype(BF16)


def _softmax_pv(scores_values, lane, heads):
    m = None
    for s, _ in scores_values:
        mx = jnp.max(s, axis=1, keepdims=True)
        m = mx if m is None else jnp.maximum(m, mx)
    den = None
    o = None
    for s, v in scores_values:
        p = jnp.exp(s - m)
        sm = jnp.sum(p, axis=1, keepdims=True)
        den = sm if den is None else den + sm
        t = jnp.dot(p.astype(BF16), v, preferred_element_type=F32)
        o = t if o is None else o + t
    o = o * (1.0 / den)
    rows = o.shape[0] // len(heads)
    out = None
    for i, h in enumerate(heads):
        t = jnp.where((lane // HEAD_DIM) == h, o[i * rows:(i + 1) * rows], 0.0)
        out = t if out is None else out + t
    return out


def _head_groups():
    return [tuple(range(h, h + ATT_HEAD_STACK)) for h in range(0, N_GROUPS, ATT_HEAD_STACK)]


def _nattn_kernel(q_ref, k0_ref, k1_ref, k2_ref, v0_ref, v1_ref, v2_ref,
                  cq_ref, sq_ref, ck0_ref, ck1_ref, ck2_ref, sk0_ref, sk1_ref, sk2_ref,
                  kc_ref, vc_ref, bias_ref, o_ref):
    w = q_ref.shape[2]
    lane = lax.broadcasted_iota(jnp.int32, (1, w), 1)
    first16 = (lane % (HEAD_DIM // 2)) < (HEAD_DIM // 4)
    q = q_ref[0].astype(F32) * (HEAD_DIM ** -0.5)
    q_rot = _rope(q, cq_ref, sq_ref, first16)
    k_rot = jnp.concatenate([_rope(k0_ref[0].astype(F32), ck0_ref, sk0_ref, first16),
                             _rope(k1_ref[0].astype(F32), ck1_ref, sk1_ref, first16),
                             _rope(k2_ref[0].astype(F32), ck2_ref, sk2_ref, first16)], axis=0).astype(BF16)
    v = jnp.concatenate([v0_ref[0], v1_ref[0], v2_ref[0]], axis=0).astype(BF16)
    kc = kc_ref[0].astype(BF16)
    vc = vc_ref[0].astype(BF16)
    acc = None
    for heads in _head_groups():
        bias = jnp.concatenate([bias_ref[0, h] for h in heads], axis=0)
        s_win = lax.dot_general(_stack_heads(q_rot, lane, heads), k_rot, _NT, preferred_element_type=F32) + bias
        s_ctx = lax.dot_general(_stack_heads(q, lane, heads), kc, _NT, preferred_element_type=F32)
        t = _softmax_pv([(s_win, v), (s_ctx, vc)], lane, heads)
        acc = t if acc is None else acc + t
    o_ref[0] = acc.astype(o_ref.dtype)


def neighbourhood_attention(z, zc, rpb, *, q_col, k_col, v_col, kc_col, vc_col):
    bsz, n, _ = z.shape
    n_ctx = zc.shape[1]
    w = BRANCH_WIDTH
    n_blk = n // ATT_QTOK
    cos_t, sin_t = _rope_tables(n)
    bias = _attn_bias_tables(rpb, n // GRID_W)

    def kb(j):
        return jnp.clip(j - 1, 0, n_blk - ATT_KBLK)

    def zspec(col, off=None):
        if off is None:
            return pl.BlockSpec((1, ATT_QTOK, w), lambda b, j: (b, j, col))
        return pl.BlockSpec((1, ATT_QTOK, w), lambda b, j: (b, kb(j) + off, col))

    def tspec(off=None):
        if off is None:
            return pl.BlockSpec((ATT_QTOK, 2 * HEAD_DIM), lambda b, j: (j, 0))
        return pl.BlockSpec((ATT_QTOK, 2 * HEAD_DIM), lambda b, j: (kb(j) + off, 0))

    in_specs = ([zspec(q_col)] + [zspec(k_col, i) for i in range(ATT_KBLK)] + [zspec(v_col, i) for i in range(ATT_KBLK)]
                + [tspec(), tspec()] + [tspec(i) for i in range(ATT_KBLK)] * 2
                + [pl.BlockSpec((1, n_ctx, w), lambda b, j: (b, 0, kc_col)),
                   pl.BlockSpec((1, n_ctx, w), lambda b, j: (b, 0, vc_col)),
                   pl.BlockSpec((1, N_GROUPS, ATT_QTOK, ATT_KROWS * GRID_W),
                                lambda b, j: (jnp.minimum(j, 1) + j // (n_blk - 1), 0, 0, 0))])
    return pl.pallas_call(
        _nattn_kernel,
        grid=(bsz, n_blk),
        in_specs=in_specs,
        out_specs=pl.BlockSpec((1, ATT_QTOK, w), lambda b, j: (b, j, 0)),
        out_shape=jax.ShapeDtypeStruct((bsz, n, w), BF16),
        compiler_params=_params("arbitrary", "arbitrary"),
        name="neighbourhood_attention",
    )(z, z, z, z, z, z, z, cos_t, sin_t, cos_t, cos_t, cos_t, sin_t, sin_t, sin_t, zc, zc, bias)


def _ctx_attn_kernel(q_ref, kc_ref, vc_ref, o_ref):
    w = q_ref.shape[2]
    lane = lax.broadcasted_iota(jnp.int32, (1, w), 1)
    q = q_ref[0].astype(F32) * (HEAD_DIM ** -0.5)
    kc = kc_ref[0].astype(BF16)
    vc = vc_ref[0].astype(BF16)
    acc = None
    for heads in _head_groups():
        s = lax.dot_general(_stack_heads(q, lane, heads), kc, _NT, preferred_element_type=F32)
        t = _softmax_pv([(s, vc)], lane, heads)
        acc = t if acc is None else acc + t
    o_ref[0] = acc.astype(o_ref.dtype)


def context_attention(zc, *, q_col, k_col, v_col):
    bsz, n_ctx, _ = zc.shape
    w = BRANCH_WIDTH
    return pl.pallas_call(
        _ctx_attn_kernel,
        grid=(bsz,),
        in_specs=[pl.BlockSpec((1, n_ctx, w), functools.partial(lambda b, c: (b, 0, c), c=c))
                  for c in (q_col, k_col, v_col)],
        out_specs=pl.BlockSpec((1, n_ctx, w), lambda b: (b, 0, 0)),
        out_shape=jax.ShapeDtypeStruct((bsz, n_ctx, w), BF16),
        compiler_params=_params("arbitrary"),
        name="context_attention",
    )(zc, zc, zc)


def _ln(v, g, b):
    mu = jnp.mean(v, axis=-1, keepdims=True)
    var = jnp.mean(jnp.square(v - mu), axis=-1, keepdims=True)
    return (v - mu) * lax.rsqrt(var + EPS) * g + b


def _gmlp_kernel(z_ref, lng_ref, lnb_ref, w_ref, bias_ref, o_ref):
    tm = z_ref.shape[1]
    z = jax.nn.gelu(z_ref[0])
    u = z[:, :BRANCH_WIDTH]
    v = _ln(z[:, BRANCH_WIDTH:], lng_ref[...], lnb_ref[...])
    group = lax.broadcasted_iota(jnp.int32, (1, BRANCH_WIDTH), 1) // HEAD_DIM
    wcat = w_ref[...]
    for c in range(tm // CHUNK):
        rows = slice(c * CHUNK, (c + 1) * CHUNK)
        vc = v[rows]
        vst = jnp.concatenate([jnp.where(group == g, vc, 0.0) for g in range(N_GROUPS)], axis=0).astype(BF16)
        mixed = jnp.dot(wcat, vst, preferred_element_type=F32) + bias_ref[...]
        o_ref[0, rows, :] = (u[rows] * mixed).astype(o_ref.dtype)


def chunk_gmlp(za, ln_g, ln_b, w_s, b_s, *, tm):
    bsz, n, w2 = za.shape
    w = w2 // 2
    wcat = jnp.transpose(w_s, (1, 0, 2)).reshape(CHUNK, N_GROUPS * CHUNK).astype(BF16)
    bias = jnp.repeat(b_s.T, HEAD_DIM, axis=1)
    return pl.pallas_call(
        _gmlp_kernel,
        grid=(bsz, n // tm),
        in_specs=[
            pl.BlockSpec((1, tm, w2), lambda b, i: (b, i, 0)),
            pl.BlockSpec((1, w), lambda b, i: (0, 0)),
            pl.BlockSpec((1, w), lambda b, i: (0, 0)),
            pl.BlockSpec((CHUNK, N_GROUPS * CHUNK), lambda b, i: (0, 0)),
            pl.BlockSpec((CHUNK, w), lambda b, i: (0, 0)),
        ],
        out_specs=pl.BlockSpec((1, tm, w), lambda b, i: (b, i, 0)),
        out_shape=jax.ShapeDtypeStruct((bsz, n, w), BF16),
        compiler_params=_params("arbitrary", "arbitrary"),
        name="chunk_gmlp",
    )(za, ln_g[None, :], ln_b[None, :], wcat, bias)


CONV_HALO = 16
CONV_SUB = 128
SUBLANES = 8
LANES = 128


def _conv_kernel(a_ref, g_ref, ap_ref, gp_ref, an_ref, gn_ref, w_ref, cb_ref, lng_ref, lnb_ref, o_ref, y_ref):
    i = pl.program_id(1)
    tm = a_ref.shape[1]

    def glu(a, g):
        return a * _sigmoid(g)

    y_ref[0:CONV_HALO, :] = jnp.where(i > 0, glu(ap_ref[0], gp_ref[0]), 0.0)
    y_ref[CONV_HALO:CONV_HALO + tm, :] = glu(a_ref[0], g_ref[0])
    y_ref[CONV_HALO + tm:, :] = jnp.where(i < pl.num_programs(1) - 1, glu(an_ref[0], gn_ref[0]), 0.0)
    first = CONV_HALO - CONV_WIDTH // 2
    nb = CONV_SUB + 2 * CONV_HALO
    for r in range(0, tm, CONV_SUB):
        blk = y_ref[r:r + nb, :]
        acc = None
        for res in range(SUBLANES):
            rot = blk if res == 0 else pltpu.roll(blk, nb - res, 0)
            for j in range(CONV_WIDTH):
                if (first + j) % SUBLANES == res:
                    a0 = first + j - res
                    t = rot[a0:a0 + CONV_SUB, :] * w_ref[j:j + 1, :]
                    acc = t if acc is None else acc + t
        y = _ln(acc + cb_ref[...], lng_ref[...], lnb_ref[...])
        o_ref[0, r:r + CONV_SUB, :] = _silu(y).astype(o_ref.dtype)


def conformer_conv(zcv, conv_w, conv_b, ln_g, ln_b, *, tm):
    bsz, n, w2 = zcv.shape
    w = w2 // 2
    hb = tm // CONV_HALO
    n_hb = n // CONV_HALO

    def main(col):
        return pl.BlockSpec((1, tm, w), lambda b, i: (b, i, col))

    def prev(col):
        return pl.BlockSpec((1, CONV_HALO, w), lambda b, i: (b, jnp.maximum(i * hb - 1, 0), col))

    def nxt(col):
        return pl.BlockSpec((1, CONV_HALO, w), lambda b, i: (b, jnp.minimum((i + 1) * hb, n_hb - 1), col))

    vec = pl.BlockSpec((1, w), lambda b, i: (0, 0))
    return pl.pallas_call(
        _conv_kernel,
        grid=(bsz, n // tm),
        in_specs=[main(0), main(1), prev(0), prev(1), nxt(0), nxt(1),
                  pl.BlockSpec((CONV_WIDTH, w), lambda b, i: (0, 0)), vec, vec, vec],
        out_specs=pl.BlockSpec((1, tm, w), lambda b, i: (b, i, 0)),
        out_shape=jax.ShapeDtypeStruct((bsz, n, w), BF16),
        scratch_shapes=[pltpu.VMEM((tm + 2 * CONV_HALO, w), F32)],
        compiler_params=_params("arbitrary", "arbitrary"),
        name="conformer_conv",
    )(zcv, zcv, zcv, zcv, zcv, zcv, conv_w, conv_b[None, :], ln_g[None, :], ln_b[None, :])


FOURIER_N2 = 128
FOURIER_UNROLL = 8
FOURIER_KB = 8


def _np_split(m):
    m = jnp.asarray(m, F32)
    hi = m.astype(BF16)
    return hi, (m - hi.astype(F32)).astype(BF16)


def _channel_dft_matrix():
    c = np.arange(HEAD_DIM)
    ang = 2.0 * np.pi * ((c[:, None] * c[None, :]) % HEAD_DIM) / HEAD_DIM
    eye = np.eye(N_GROUPS)
    return np.concatenate([np.kron(eye, np.cos(ang)), np.kron(eye, np.sin(ang))], axis=0)


def _fourier_stage1_kernel(xa_ref, xb_ref, mh_ref, ml_ref, o_ref, *, n1, n2):
    def one_fast_index(f, carry):
        rows = pl.ds(f, n1, stride=n2)
        x = jnp.concatenate([xa_ref[0, rows, :], xb_ref[0, rows, :]], axis=1)
        x_hi, x_lo = _split_bf16(x)
        res = _dot3(mh_ref[0, f], ml_ref[0, f], x_hi, x_lo)
        o_ref[0, 0, 0, rows, :] = res[:, :LANES]
        o_ref[0, 0, 1, rows, :] = res[:, LANES:]
        return carry

    lax.fori_loop(0, n2, one_fast_index, 0, unroll=FOURIER_UNROLL)


def _fourier_stage2_kernel(br_ref, bi_ref, m2h_ref, m2l_ref, mdh_ref, mdl_ref, o_ref, *, scale):
    for k in range(br_ref.shape[3]):
        b_re = jnp.concatenate([br_ref[0, 0, 0, k], br_ref[0, 0, 1, k]], axis=1)
        b_im = jnp.concatenate([bi_ref[0, 0, 0, k], bi_ref[0, 0, 1, k]], axis=1)
        b_hi, b_lo = _split_bf16(jnp.concatenate([b_re, b_im], axis=0))
        xs = _dot3(m2h_ref[...], m2l_ref[...], b_hi, b_lo)
        x_hi, x_lo = _split_bf16(jnp.concatenate([xs[:FOURIER_N2], xs[FOURIER_N2:]], axis=1))
        o_ref[0, :, k, :] = _dot3(x_hi, x_lo, mdh_ref[...], mdl_ref[...]) * scale


def _fourier_direct_kernel(x_ref, mph_ref, mpl_ref, mdh_ref, mdl_ref, o_ref, *, scale):
    n = x_ref.shape[1]
    x_hi, x_lo = _split_bf16(x_ref[0])
    p = _dot3(mph_ref[...], mpl_ref[...], x_hi, x_lo)
    p_hi, p_lo = _split_bf16(jnp.concatenate([p[:n], p[n:]], axis=1))
    o_ref[0] = (_dot3(p_hi, p_lo, mdh_ref[...], mdl_ref[...]) * scale).astype(o_ref.dtype)


def fourier_mix(zb):
    bsz, n, w = zb.shape
    scale = float(1.0 / np.sqrt(n * HEAD_DIM))
    mdh, mdl = _np_split(_channel_dft_matrix())
    md_spec2 = pl.BlockSpec((2 * w, w), lambda b, j: (0, 0))
    if n <= 2 * FOURIER_N2:
        t = np.arange(n)
        ang = 2.0 * np.pi * ((t[:, None] * t[None, :]) % n) / n
        mph, mpl = _np_split(np.concatenate([np.cos(ang), -np.sin(ang)], axis=0))
        return pl.pallas_call(
            functools.partial(_fourier_direct_kernel, scale=scale),
            grid=(bsz,),
            in_specs=[pl.BlockSpec((1, n, w), lambda b: (b, 0, 0)),
                      pl.BlockSpec((2 * n, n), lambda b: (0, 0)), pl.BlockSpec((2 * n, n), lambda b: (0, 0)),
                      pl.BlockSpec((2 * w, w), lambda b: (0, 0)), pl.BlockSpec((2 * w, w), lambda b: (0, 0))],
            out_specs=pl.BlockSpec((1, n, w), lambda b: (b, 0, 0)),
            out_shape=jax.ShapeDtypeStruct((bsz, n, w), BF16),
            compiler_params=_params("arbitrary"),
            name="fourier_direct",
        )(zb, mph, mpl, mdh, mdl)

    n1, n2 = n // FOURIER_N2, FOURIER_N2
    f, k1, s = np.arange(n2), np.arange(n1), np.arange(n1)
    ang1 = 2.0 * np.pi * ((k1[None, :, None] * (f[:, None, None] + n2 * s[None, None, :])) % n) / n
    m1h, m1l = _np_split(np.stack([np.cos(ang1), -np.sin(ang1)]))
    k2 = np.arange(n2)
    ang2 = 2.0 * np.pi * ((k2[:, None] * f[None, :]) % n2) / n2
    c2, s2 = np.cos(ang2), np.sin(ang2)
    m2h, m2l = _np_split(np.block([[c2, s2], [-s2, c2]]))

    m1_spec = pl.BlockSpec((1, n2, n1, n1), lambda b, p: (p, 0, 0, 0))
    n_half = w // LANES
    b_st = pl.pallas_call(
        functools.partial(_fourier_stage1_kernel, n1=n1, n2=n2),
        grid=(bsz, 2),
        in_specs=[pl.BlockSpec((1, n, LANES), lambda b, p: (b, 0, 0)),
                  pl.BlockSpec((1, n, LANES), lambda b, p: (b, 0, 1)), m1_spec, m1_spec],
        out_specs=pl.BlockSpec((1, 1, n_half, n, LANES), lambda b, p: (b, p, 0, 0, 0)),
        out_shape=jax.ShapeDtypeStruct((bsz, 2, n_half, n, LANES), F32),
        compiler_params=_params("arbitrary", "arbitrary"),
        name="fourier_stage1",
    )(zb, zb, m1h, m1l)

    kb = FOURIER_KB
    b_st = b_st.reshape(bsz, 2, n_half, n1, n2, LANES)
    out = pl.pallas_call(
        functools.partial(_fourier_stage2_kernel, scale=scale),
        grid=(bsz, n1 // kb),
        in_specs=[pl.BlockSpec((1, 1, n_half, kb, n2, LANES), lambda b, j: (b, 0, 0, j, 0, 0)),
                  pl.BlockSpec((1, 1, n_half, kb, n2, LANES), lambda b, j: (b, 1, 0, j, 0, 0)),
                  pl.BlockSpec((2 * n2, 2 * n2), lambda b, j: (0, 0)),
                  pl.BlockSpec((2 * n2, 2 * n2), lambda b, j: (0, 0)),
                  md_spec2, md_spec2],
        out_specs=pl.BlockSpec((1, n2, kb, w), lambda b, j: (b, 0, j, 0)),
        out_shape=jax.ShapeDtypeStruct((bsz, n2, n1, w), F32),
        compiler_params=_params("arbitrary", "arbitrary"),
        name="fourier_stage2",
    )(b_st, b_st, m2h, m2l, mdh, mdl)
    return out.reshape(bsz, n, w)


def expert_route(h, aff_t):
    n = h.shape[1]
    cap = EC_CAPACITY * n // N_EXPERTS
    slot, off = expert_topk(aff_t, cap)
    xe, gate = expert_gather(h, slot, aff_t, off, max(cap, ROUTE_WIN))
    return xe, gate, jnp.swapaxes(slot, 1, 2), off


def kernel(x, c, ctx, c_ctx, w_mod, b_mod, norm1_g, norm2_g, w_in, sgu_ln_g, sgu_ln_b, w_spatial, b_spatial,
           w_a_out, w_b_out, conv_w, conv_b, conv_ln_g, conv_ln_b, w_c_out, rpb, w_d_out, w_out, w_router,
           w_gate_e, w_up_e, w_down_e, final_norm_g):
    bsz = x.shape[0]
    xc = ctx
    assert bsz + 1 <= MOD_ROWS
    cond = jnp.concatenate([c, c_ctx[None, :], jnp.zeros((MOD_ROWS - bsz - 1, D_MODEL), F32)], axis=0)
    mod_all = modulation(cond, w_mod, b_mod)
    in_scale = jnp.concatenate([jnp.ones((V_END,), F32), jnp.full((IN_COLS - V_END,), GATE_HALF, F32)])
    w_in_b = (w_in * in_scale).astype(BF16)
    for l in range(DEPTH):
        last = l == DEPTH - 1
        sh1, sc1, gt1, sh2, sc2, gt2 = jnp.split(mod_all[l, :bsz, None, :], 6, axis=-1)
        cmod = jnp.broadcast_to(mod_all[l, bsz:bsz + 1, None, :], (bsz, 1, 6 * D_MODEL))
        csh1, csc1, cgt1, csh2, csc2, cgt2 = jnp.split(cmod, 6, axis=-1)

        wa, wb, wc, wd_ = ((w * GATE_HALF).astype(BF16) for w in (w_a_out[l], w_b_out[l], w_c_out[l], w_d_out[l]))
        wo = w_out[l].astype(BF16)
        g1 = norm1_g[l][None, :]

        za, zb, zcv, zqkv, zg = norm_inproj(x, g1, sc1, sh1, w_in_b, l, 0, IN_WIDTHS, IN_DTYPES, tm=512)
        if last:
            (zqkv_c,) = norm_inproj(xc, g1, csc1, csh1, w_in_b, l, Q_END, (2 * BRANCH_WIDTH,), (BF16,), tm=CTX_TM)
            kc_col, vc_col = 0, 1
        else:
            cza, czb, czcv, zqkv_c, czg = norm_inproj(xc, g1, csc1, csh1, w_in_b, l, 0, IN_WIDTHS, IN_DTYPES,
                                                      tm=CTX_TM)
            kc_col, vc_col = 1, 2
        d_lat = neighbourhood_attention(zqkv, zqkv_c, rpb[l], q_col=0, k_col=1, v_col=2,
                                        kc_col=kc_col, vc_col=vc_col)
        a_lat = chunk_gmlp(za, sgu_ln_g[l], sgu_ln_b[l], w_spatial[l], b_spatial[l], tm=512)
        b_lat = fourier_mix(zb)
        c_lat = conformer_conv(zcv, conv_w[l], conv_b[l], conv_ln_g[l], conv_ln_b[l], tm=512)
        g2 = norm2_g[l][None, :]
        router_t = w_router[l].T
        x, h2, aff_t = merge_branches(a_lat, b_lat, c_lat, d_lat, zg, wa, wb, wc, wd_, wo, x, gt1,
                                      g2, sc2, sh2, router_t, tm=512)
        if not last:
            a_c = chunk_gmlp(cza, sgu_ln_g[l], sgu_ln_b[l], w_spatial[l], b_spatial[l], tm=CTX_TM)
            b_c = fourier_mix(czb)
            c_c = conformer_conv(czcv, conv_w[l], conv_b[l], conv_ln_g[l], conv_ln_b[l], tm=CTX_TM)
            d_c = context_attention(zqkv_c, q_col=0, k_col=1, v_col=2)
            xc, hc2, aff_tc = merge_branches(a_c, b_c, c_c, d_c, czg, wa, wb, wc, wd_, wo, xc, cgt1,
                                             g2, csc2, csh2, router_t, tm=CTX_TM)

        xe, gate, slot_t, off = expert_route(h2, aff_t)
        if last:
            (y,) = expert_ffn([xe], [gate], w_gate_e, w_up_e, w_down_e, l)
        else:
            xe_c, gate_c, slot_tc, off_c = expert_route(hc2, aff_tc)
            y, y_c = expert_ffn([xe, xe_c], [gate, gate_c], w_gate_e, w_up_e, w_down_e, l)
            xc = expert_combine(y_c, slot_tc, off_c, xc, cgt2)
        x = expert_combine(y, slot_t, off, x, gt2, final_norm_g if last else None)
    return x


CTX_TM = 256
```

```python
import functools

import jax
import jax.numpy as jnp
import numpy as np
from jax import lax
from jax.experimental import pallas as pl
from jax.experimental.pallas import tpu as pltpu

D_MODEL = 1024
DEPTH = 2
GRID_W = 64
HEAD_DIM = 64
N_GROUPS = 4
BRANCH_WIDTH = N_GROUPS * HEAD_DIM
N_BRANCH = 4
CHUNK = 128
CONV_WIDTH = 31
WIN_ROWS = 8
WIN_COLS = 16
QB_COLS = 16
KB_COLS = QB_COLS + WIN_COLS
ROPE_BASE = 10000.0
N_EXPERTS = 16
EXPERT_FF = 1024
EC_CAPACITY = 2
EPS = 1e-6
NEG_INF = -1e30
A_END = 2 * BRANCH_WIDTH
B_END = A_END + BRANCH_WIDTH
C_END = B_END + 2 * BRANCH_WIDTH
Q_END = C_END + BRANCH_WIDTH
K_END = Q_END + BRANCH_WIDTH
V_END = K_END + BRANCH_WIDTH
IN_COLS = V_END + N_BRANCH * D_MODEL
SPLITS = [A_END, B_END, C_END, Q_END, K_END, V_END]
IN_WIDTHS = (A_END, B_END - A_END, C_END - B_END, V_END - C_END, IN_COLS - V_END)
IN_DTYPES = (jnp.float32, jnp.float32, jnp.float32, jnp.bfloat16, jnp.bfloat16)
GATE_HALF = 0.5

VMEM_LIMIT_BYTES = 56 * 1024 * 1024
F32 = jnp.float32
BF16 = jnp.bfloat16


def _sigmoid(v):
    return 0.5 * jnp.tanh(0.5 * v) + 0.5


def _silu(v):
    return v * _sigmoid(v)


def _params(*sem):
    return pltpu.CompilerParams(dimension_semantics=sem, vmem_limit_bytes=VMEM_LIMIT_BYTES)


MOD_ROWS = 8
MOD_TN = 1536


def _split_bf16(v):
    hi = v.astype(BF16)
    return hi, (v - hi.astype(F32)).astype(BF16)


def _dot3(a_hi, a_lo, b_hi, b_lo):
    return (jnp.dot(a_hi, b_hi, preferred_element_type=F32) + jnp.dot(a_hi, b_lo, preferred_element_type=F32)
            + jnp.dot(a_lo, b_hi, preferred_element_type=F32))


def _mod_kernel(c_ref, w_ref, b_ref, o_ref):
    s_hi, s_lo = _split_bf16(_silu(c_ref[...]))
    w_hi, w_lo = _split_bf16(w_ref[0])
    o_ref[0] = _dot3(s_hi, s_lo, w_hi, w_lo) + b_ref[0]


def modulation(cond, w_mod, b_mod):
    n_layers, d, cols = w_mod.shape
    return pl.pallas_call(
        _mod_kernel,
        grid=(n_layers, cols // MOD_TN),
        in_specs=[pl.BlockSpec((MOD_ROWS, d), lambda l, j: (0, 0)),
                  pl.BlockSpec((1, d, MOD_TN), lambda l, j: (l, 0, j)),
                  pl.BlockSpec((1, 1, MOD_TN), lambda l, j: (l, 0, j))],
        out_specs=pl.BlockSpec((1, MOD_ROWS, MOD_TN), lambda l, j: (l, 0, j)),
        out_shape=jax.ShapeDtypeStruct((n_layers, MOD_ROWS, cols), F32),
        compiler_params=_params("arbitrary", "arbitrary"),
        name="modulation",
    )(cond, w_mod, b_mod[:, None, :])


INPROJ_COL_CHUNK = 512


def _norm_inproj_kernel(x_ref, g_ref, sc_ref, sh_ref, w_ref, *o_refs):
    x = x_ref[0]
    y = x * lax.rsqrt(jnp.mean(x * x, axis=-1, keepdims=True) + EPS) * g_ref[...]
    h = (y * (1.0 + sc_ref[0]) + sh_ref[0]).astype(BF16)
    off = 0
    for o_ref in o_refs:
        width = o_ref.shape[2]
        for c0 in range(0, width, INPROJ_COL_CHUNK):
            cw = min(INPROJ_COL_CHUNK, width - c0)
            o_ref[0, :, c0:c0 + cw] = jnp.dot(h, w_ref[:, off + c0:off + c0 + cw],
                                              preferred_element_type=F32).astype(o_ref.dtype)
        off += width


def norm_inproj(x, g, sc, sh, w, l, col0, widths, dtypes, *, tm):
    bsz, n, d = x.shape
    cols = sum(widths)
    assert col0 % cols == 0
    return pl.pallas_call(
        _norm_inproj_kernel,
        grid=(bsz, n // tm),
        in_specs=[
            pl.BlockSpec((1, tm, d), lambda b, i: (b, i, 0)),
            pl.BlockSpec((1, d), lambda b, i: (0, 0)),
            pl.BlockSpec((1, 1, d), lambda b, i: (b, 0, 0)),
            pl.BlockSpec((1, 1, d), lambda b, i: (b, 0, 0)),
            pl.BlockSpec((None, d, cols), lambda b, i: (l, 0, col0 // cols), pipeline_mode=pl.Buffered(1)),
        ],
        out_specs=[pl.BlockSpec((1, tm, wd), lambda b, i: (b, i, 0)) for wd in widths],
        out_shape=[jax.ShapeDtypeStruct((bsz, n, wd), dt) for wd, dt in zip(widths, dtypes, strict=True)],
        compiler_params=_params("arbitrary", "arbitrary"),
        name="norm_inproj",
    )(x, g, sc, sh, w)


def _merge_kernel(a_ref, b_ref, c_ref, d_ref, ga_ref, gb_ref, gc_ref, gd_ref,
                  wa_ref, wb_ref, wc_ref, wd_ref, wo_ref, x_ref, gt_ref,
                  g2_ref, sc2_ref, sh2_ref, rt_ref, o_ref, h_ref, aff_ref):
    m = None
    for br, gz, w in ((a_ref, ga_ref, wa_ref), (b_ref, gb_ref, wb_ref),
                      (c_ref, gc_ref, wc_ref), (d_ref, gd_ref, wd_ref)):
        hp = jnp.dot(br[0].astype(BF16), w[...], preferred_element_type=F32)
        t = hp * jnp.tanh(gz[0].astype(F32)) + hp
        m = t if m is None else m + t
    mix = jnp.dot(m.astype(BF16), wo_ref[...], preferred_element_type=F32)
    x = x_ref[0] + gt_ref[0] * mix
    o_ref[0] = x
    y = x * lax.rsqrt(jnp.mean(x * x, axis=-1, keepdims=True) + EPS) * g2_ref[...]
    h = y * (1.0 + sc2_ref[0]) + sh2_ref[0]
    h_hi, h_lo = _split_bf16(h)
    r_hi, r_lo = _split_bf16(rt_ref[...])
    ne = rt_ref.shape[0]
    both = lax.dot_general(jnp.concatenate([r_hi, r_lo], axis=0), h_hi, _NT, preferred_element_type=F32)
    logits = both[:ne] + both[ne:] + lax.dot_general(r_hi, h_lo, _NT, preferred_element_type=F32)
    pr = jnp.exp(logits - jnp.max(logits, axis=0, keepdims=True))
    aff_ref[0] = pr / jnp.sum(pr, axis=0, keepdims=True)
    h_ref[0] = h_hi


def merge_branches(a, b, cc, d, z, w_a, w_b, w_c, w_d, w_o, x, gt, g2, sc2, sh2, router_t, *, tm):
    bsz, n, dm = x.shape
    w = a.shape[-1]
    ne = router_t.shape[0]
    br_spec = pl.BlockSpec((1, tm, w), lambda bi, i: (bi, i, 0))
    gate_specs = [pl.BlockSpec((1, tm, dm), functools.partial(lambda bi, i, k: (bi, i, k), k=k))
                  for k in range(N_BRANCH)]
    wbr_spec = pl.BlockSpec((w, dm), lambda bi, i: (0, 0))
    row_spec = pl.BlockSpec((1, tm, dm), lambda bi, i: (bi, i, 0))
    mod_spec = pl.BlockSpec((1, 1, dm), lambda bi, i: (bi, 0, 0))
    return pl.pallas_call(
        _merge_kernel,
        grid=(bsz, n // tm),
        in_specs=[br_spec] * 4 + gate_specs + [wbr_spec] * 4 + [
            pl.BlockSpec((dm, dm), lambda bi, i: (0, 0)), row_spec, mod_spec,
            pl.BlockSpec((1, dm), lambda bi, i: (0, 0)), mod_spec, mod_spec,
            pl.BlockSpec((ne, dm), lambda bi, i: (0, 0)),
        ],
        out_specs=[row_spec, row_spec, pl.BlockSpec((1, ne, tm), lambda bi, i: (bi, 0, i))],
        out_shape=[jax.ShapeDtypeStruct((bsz, n, dm), F32), jax.ShapeDtypeStruct((bsz, n, dm), BF16),
                   jax.ShapeDtypeStruct((bsz, ne, n), F32)],
        compiler_params=_params("arbitrary", "arbitrary"),
        name="merge_branches",
    )(a, b, cc, d, z, z, z, z, w_a, w_b, w_c, w_d, w_o, x, gt, g2, sc2, sh2, router_t)


ROUTE_TB = 256
ROUTE_WIN = 64
ROUTE_ALIGN = 16
GATHER_SUB = 4
COMBINE_SUB = 2
OFF_LANES = 128
TOPK_EXP_STEPS = (64, 32, 16, 8, 4, 2, 1)
TOPK_BISECT_STEPS = 32


def _topk_kernel(aff_ref, slot_ref, off_ref, *, cap):
    a = aff_ref[0]
    ne, n = a.shape
    capf = jnp.float32(cap)

    def count_ge(t):
        return jnp.sum(jnp.where(a >= t, 1.0, 0.0), axis=1, keepdims=True)

    hi = jnp.full((ne, 1), 2.0, F32)
    for s in TOPK_EXP_STEPS:
        cand = hi * (2.0 ** -s)
        hi = jnp.where(count_ge(cand) < capf, cand, hi)
    lo = jnp.where(hi <= 2.0 ** -126, 0.0, hi * 0.5)

    def bisect(_, carry):
        lo, hi = carry
        mid = 0.5 * (lo + hi)
        ok = count_ge(mid) >= capf
        return jnp.where(ok, mid, lo), jnp.where(ok, hi, mid)

    lo, hi = lax.fori_loop(0, TOPK_BISECT_STEPS, bisect, (lo, hi))
    need = capf - count_ge(hi)
    r_i = lax.broadcasted_iota(jnp.int32, (ROUTE_TB, ROUTE_TB), 0)
    c_i = lax.broadcasted_iota(jnp.int32, (ROUTE_TB, ROUTE_TB), 1)
    tri = jnp.where(r_i < c_i, 1.0, 0.0).astype(BF16)
    lane = lax.broadcasted_iota(jnp.int32, (ne, OFF_LANES), 1)
    run_eq = jnp.zeros((ne, 1), F32)
    run_sel = jnp.zeros((ne, 1), F32)
    offs = jnp.zeros((ne, OFF_LANES), F32)
    for c in range(n // ROUTE_TB):
        cols = slice(c * ROUTE_TB, (c + 1) * ROUTE_TB)
        a_c = a[:, cols]
        above = a_c >= hi
        eq_c = jnp.where(above, 0.0, jnp.where(a_c >= lo, 1.0, 0.0))
        rank = jnp.dot(eq_c.astype(BF16), tri, preferred_element_type=F32) + run_eq
        sel = jnp.where(above, 1.0, jnp.where(rank < need, eq_c, 0.0))
        pos = jnp.dot(sel.astype(BF16), tri, preferred_element_type=F32) + run_sel
        slot_ref[0, :, cols] = jnp.where(sel > 0.0, pos, -1.0).astype(jnp.int32)
        offs = jnp.where(lane == c, run_sel, offs)
        run_eq = run_eq + jnp.sum(eq_c, axis=1, keepdims=True)
        run_sel = run_sel + jnp.sum(sel, axis=1, keepdims=True)
    offs = jnp.where(lane == n // ROUTE_TB, run_sel, offs)
    off_ref[0] = offs.astype(jnp.int32)


def expert_topk(aff_t, cap):
    bsz, ne, n = aff_t.shape
    return pl.pallas_call(
        functools.partial(_topk_kernel, cap=cap),
        grid=(bsz,),
        in_specs=[pl.BlockSpec((1, ne, n), lambda b: (b, 0, 0))],
        out_specs=[pl.BlockSpec((1, ne, n), lambda b: (b, 0, 0)),
                   pl.BlockSpec((1, ne, OFF_LANES), lambda b: (b, 0, 0))],
        out_shape=[jax.ShapeDtypeStruct((bsz, ne, n), jnp.int32),
                   jax.ShapeDtypeStruct((bsz, ne, OFF_LANES), jnp.int32)],
        compiler_params=_params("arbitrary"),
        name="expert_topk",
    )(aff_t)


def _window_plan(off_ref, b, e, tb, w, rows_max):
    start = off_ref[b, e, tb]
    stop = off_ref[b, e, tb + 1]
    first = (start // ROUTE_ALIGN) * ROUTE_ALIGN + w * ROUTE_WIN
    active = first < stop
    return pl.multiple_of(jnp.minimum(first, rows_max), ROUTE_ALIGN), first, active


def _num_passes(off_ref, b, e_list, tb):
    n_pass = jnp.int32(0)
    for e in e_list:
        start = off_ref[b, e, tb]
        stop = off_ref[b, e, tb + 1]
        base = (start // ROUTE_ALIGN) * ROUTE_ALIGN
        n_pass = jnp.maximum(n_pass, (stop - base + ROUTE_WIN - 1) // ROUTE_WIN)
    return n_pass


def _gather_kernel(off_ref, h_ref, slot_ref, aff_ref, xe_ref, gate_ref, *, group):
    b, g, step = pl.program_id(0), pl.program_id(1), pl.program_id(2)
    rows_max = xe_ref.shape[2] - ROUTE_WIN
    n_sub = h_ref.shape[1] // ROUTE_TB

    @pl.when(step == 0)
    def _():
        xe_ref[...] = jnp.zeros(xe_ref.shape, xe_ref.dtype)
        gate_ref[...] = jnp.zeros(gate_ref.shape, gate_ref.dtype)

    sub = lax.broadcasted_iota(jnp.int32, (ROUTE_WIN, 1), 0)
    experts = [g * group + e for e in range(group)]

    for j in range(n_sub):
        tb = step * n_sub + j
        tok = slice(j * ROUTE_TB, (j + 1) * ROUTE_TB)

        def one_pass(w, carry, tb=tb, tok=tok):
            plans = [_window_plan(off_ref, b, ge, tb, w, rows_max) for ge in experts]
            hots = []
            for e, (rows, first, active) in enumerate(plans):
                ids = rows + sub
                want = jnp.where(active & (ids >= first), ids, -2)
                hots.append(slot_ref[0, e:e + 1, tok] == want)
            onehot = jnp.concatenate([jnp.where(hm, 1.0, 0.0) for hm in hots], axis=0).astype(BF16)
            res = jnp.dot(onehot, h_ref[0, tok, :], preferred_element_type=F32)
            for e, (rows, _, _) in enumerate(plans):
                win = pl.ds(rows, ROUTE_WIN)
                xe_ref[0, e, win, :] = xe_ref[0, e, win, :] + res[e * ROUTE_WIN:(e + 1) * ROUTE_WIN].astype(BF16)
                gsel = jnp.sum(jnp.where(hots[e], aff_ref[0, e:e + 1, tok], 0.0), axis=1, keepdims=True)
                gate_ref[0, e, win, :] = gate_ref[0, e, win, :] + gsel
            return carry

        lax.fori_loop(0, _num_passes(off_ref, b, experts, tb), one_pass, 0)


def expert_gather(h, slot, aff_t, off, cap_pad, *, group=8):
    bsz, n, d = h.shape
    ne = slot.shape[1]
    tok = ROUTE_TB * min(GATHER_SUB, n // ROUTE_TB)
    grid_spec = pltpu.PrefetchScalarGridSpec(
        num_scalar_prefetch=1,
        grid=(bsz, ne // group, n // tok),
        in_specs=[
            pl.BlockSpec((1, tok, d), lambda b, g, t, off: (b, t, 0)),
            pl.BlockSpec((1, group, tok), lambda b, g, t, off: (b, g, t)),
            pl.BlockSpec((1, group, tok), lambda b, g, t, off: (b, g, t)),
        ],
        out_specs=[pl.BlockSpec((1, group, cap_pad, d), lambda b, g, t, off: (b, g, 0, 0)),
                   pl.BlockSpec((1, group, cap_pad, 1), lambda b, g, t, off: (b, g, 0, 0))],
    )
    return pl.pallas_call(
        functools.partial(_gather_kernel, group=group),
        grid_spec=grid_spec,
        out_shape=[jax.ShapeDtypeStruct((bsz, ne, cap_pad, d), BF16),
                   jax.ShapeDtypeStruct((bsz, ne, cap_pad, 1), F32)],
        compiler_params=_params("arbitrary", "arbitrary", "arbitrary"),
        name="expert_gather",
    )(off, h, slot, aff_t)


def _combine_kernel(off_ref, y_ref, slot_t_ref, x_ref, gt_ref, fg_ref, o_ref, stage_ref, acc_ref):
    b, step = pl.program_id(0), pl.program_id(1)
    ne = y_ref.shape[1]
    rows_max = y_ref.shape[2] - ROUTE_WIN
    width = ne * ROUTE_WIN
    n_sub = x_ref.shape[1] // ROUTE_TB
    experts = list(range(ne))
    lane = lax.broadcasted_iota(jnp.int32, (1, width), 1)
    e_row = lax.broadcasted_iota(jnp.int32, (ne, width), 0)
    e_lane = lax.broadcasted_iota(jnp.int32, (ne, width), 1) // ROUTE_WIN
    expand = jnp.where(e_row == e_lane, 1.0, 0.0).astype(BF16)

    for j in range(n_sub):
        tb = step * n_sub + j
        tok = slice(j * ROUTE_TB, (j + 1) * ROUTE_TB)
        s1 = slot_t_ref[0, tok, :] + 1
        hi = (s1 // ROUTE_WIN).astype(F32).astype(BF16)
        lo = (s1 % ROUTE_WIN).astype(F32).astype(BF16)
        s1_wide = (float(ROUTE_WIN) * jnp.dot(hi, expand, preferred_element_type=F32)
                   + jnp.dot(lo, expand, preferred_element_type=F32))
        acc_ref[...] = jnp.zeros(acc_ref.shape, F32)

        def one_pass(w, carry, tb=tb, s1_wide=s1_wide):
            want = jnp.full((1, width), -1, jnp.int32)
            for e in experts:
                rows, first, active = _window_plan(off_ref, b, e, tb, w, rows_max)
                stage_ref[e * ROUTE_WIN:(e + 1) * ROUTE_WIN, :] = y_ref[0, e, pl.ds(rows, ROUTE_WIN), :]
                ids = rows + lane % ROUTE_WIN
                want = jnp.where((lane // ROUTE_WIN == e) & active & (ids >= first), ids + 1, want)
            onehot = jnp.where(s1_wide == want.astype(F32), 1.0, 0.0).astype(BF16)
            acc_ref[...] += jnp.dot(onehot, stage_ref[...], preferred_element_type=F32)
            return carry

        lax.fori_loop(0, _num_passes(off_ref, b, experts, tb), one_pass, 0)
        out = x_ref[0, tok, :] + gt_ref[0] * acc_ref[...]
        if fg_ref is not None:
            out = out * lax.rsqrt(jnp.mean(out * out, axis=-1, keepdims=True) + EPS) * fg_ref[...]
        o_ref[0, tok, :] = out


def _combine_kernel_plain(off_ref, y_ref, slot_t_ref, x_ref, gt_ref, o_ref, stage_ref, acc_ref):
    _combine_kernel(off_ref, y_ref, slot_t_ref, x_ref, gt_ref, None, o_ref, stage_ref, acc_ref)


def expert_combine(y, slot_t, off, x, gt, final_g=None):
    bsz, n, d = x.shape
    ne, cap_pad = y.shape[1], y.shape[2]
    tok = ROUTE_TB * min(COMBINE_SUB, n // ROUTE_TB)
    grid_spec = pltpu.PrefetchScalarGridSpec(
        num_scalar_prefetch=1,
        grid=(bsz, n // tok),
        in_specs=[
            pl.BlockSpec((1, ne, cap_pad, d), lambda b, t, off: (b, 0, 0, 0), pipeline_mode=pl.Buffered(1)),
            pl.BlockSpec((1, tok, ne), lambda b, t, off: (b, t, 0)),
            pl.BlockSpec((1, tok, d), lambda b, t, off: (b, t, 0)),
            pl.BlockSpec((1, 1, d), lambda b, t, off: (b, 0, 0)),
        ] + ([] if final_g is None else [pl.BlockSpec((1, d), lambda b, t, off: (0, 0))]),
        out_specs=pl.BlockSpec((1, tok, d), lambda b, t, off: (b, t, 0)),
        scratch_shapes=[pltpu.VMEM((ne * ROUTE_WIN, d), BF16), pltpu.VMEM((ROUTE_TB, d), F32)],
    )
    args = (off, y, slot_t, x, gt) + (() if final_g is None else (final_g[None, :],))
    return pl.pallas_call(
        _combine_kernel_plain if final_g is None else _combine_kernel,
        grid_spec=grid_spec,
        out_shape=jax.ShapeDtypeStruct((bsz, n, d), F32),
        compiler_params=_params("arbitrary", "arbitrary"),
        name="expert_combine",
    )(*args)


FFN_FBLK = 512
FFN_MIN_ROWS = 256


def _expert_ffn_kernel(*refs, n_sets, n_f):
    xe_refs, gate_refs = refs[:n_sets], refs[n_sets:2 * n_sets]
    wg_ref, wu_ref, wd_ref = refs[2 * n_sets:2 * n_sets + 3]
    o_refs = refs[2 * n_sets + 3:3 * n_sets + 3]
    acc_refs = refs[3 * n_sets + 3:]
    f = pl.program_id(1)
    blocks = [(s, b) for s in range(n_sets) for b in range(xe_refs[s].shape[0])]
    big = [sb for sb in blocks if xe_refs[sb[0]].shape[2] >= FFN_MIN_ROWS]
    small = [sb for sb in blocks if sb not in big]
    groups = [[sb] for sb in big[:-1]] + [big[-1:] + small] if big else [small]

    def hidden_block(first, last):
        wg = wg_ref[0].astype(BF16)
        wu = wu_ref[0].astype(BF16)
        wd = wd_ref[0].astype(BF16)
        for group in groups:
            xs = [xe_refs[s][b, 0] for s, b in group]
            xe = xs[0] if len(xs) == 1 else jnp.concatenate(xs, axis=0)
            a = jnp.dot(xe, wg, preferred_element_type=F32)
            u = jnp.dot(xe, wu, preferred_element_type=F32)
            t = jnp.dot((_silu(a) * u).astype(BF16), wd, preferred_element_type=F32)
            r0 = 0
            for s, b in group:
                rows = xe_refs[s].shape[2]
                part = t[r0:r0 + rows] if first else acc_refs[s][b] + t[r0:r0 + rows]
                if last:
                    o_refs[s][b, 0] = (part * gate_refs[s][b, 0]).astype(o_refs[s].dtype)
                else:
                    acc_refs[s][b] = part
                r0 += rows

    if n_f == 1:
        hidden_block(True, True)
        return
    pl.when(f == 0)(lambda: hidden_block(True, False))
    if n_f > 2:
        pl.when((f > 0) & (f < n_f - 1))(lambda: hidden_block(False, False))
    pl.when(f == n_f - 1)(lambda: hidden_block(False, True))


def expert_ffn(xes, gates, wg, wu, wd, l):
    n_sets = len(xes)
    _, ne, d, ff = wg.shape
    xe_specs = [pl.BlockSpec((xe.shape[0], 1) + xe.shape[2:], lambda e, f: (0, e, 0, 0)) for xe in xes]
    gate_specs = [pl.BlockSpec((g.shape[0], 1) + g.shape[2:], lambda e, f: (0, e, 0, 0)) for g in gates]
    return pl.pallas_call(
        functools.partial(_expert_ffn_kernel, n_sets=n_sets, n_f=ff // FFN_FBLK),
        grid=(ne, ff // FFN_FBLK),
        in_specs=xe_specs + gate_specs + [
            pl.BlockSpec((None, 1, d, FFN_FBLK), lambda e, f: (l, e, 0, f)),
            pl.BlockSpec((None, 1, d, FFN_FBLK), lambda e, f: (l, e, 0, f)),
            pl.BlockSpec((None, 1, FFN_FBLK, d), lambda e, f: (l, e, f, 0)),
        ],
        out_specs=xe_specs,
        out_shape=[jax.ShapeDtypeStruct(xe.shape, BF16) for xe in xes],
        scratch_shapes=[pltpu.VMEM((xe.shape[0],) + xe.shape[2:], F32) for xe in xes],
        compiler_params=_params("arbitrary", "arbitrary"),
        name="expert_ffn",
    )(*xes, *gates, wg, wu, wd)


ATT_QROWS = 4
ATT_QTOK = ATT_QROWS * GRID_W
ATT_KROWS = ATT_QROWS + WIN_ROWS
ATT_KBLK = ATT_KROWS * GRID_W // ATT_QTOK
_NT = (((1,), (1,)), ((), ()))


def _rope_tables(n):
    quarter = HEAD_DIM // 4
    inv = ROPE_BASE ** (-np.arange(quarter, dtype=np.float64) / quarter)
    t = np.arange(n)
    ang_r = (t // GRID_W)[:, None] * inv[None, :]
    ang_c = (t % GRID_W)[:, None] * inv[None, :]
    cos = np.concatenate([np.cos(ang_r)] * 2 + [np.cos(ang_c)] * 2, axis=1)
    sin = np.concatenate([-np.sin(ang_r), np.sin(ang_r), -np.sin(ang_c), np.sin(ang_c)], axis=1)
    return (jnp.asarray(np.concatenate([cos, cos], axis=1), F32),
            jnp.asarray(np.concatenate([sin, sin], axis=1), F32))


def _attn_bias_tables(rpb, n_rows):
    n_blk = n_rows // ATT_QROWS
    n_dr = 2 * WIN_ROWS - 1
    qc = np.arange(GRID_W)
    kc = np.arange(GRID_W)
    c_start = np.clip(qc - WIN_COLS // 2, 0, GRID_W - WIN_COLS)
    col_ok = (kc[None, :] >= c_start[:, None]) & (kc[None, :] < c_start[:, None] + WIN_COLS)
    assert np.all(np.abs(kc[None, :] - qc[:, None])[col_ok] < WIN_COLS)
    row_sel, oks = [], []
    for j in (0, 1, n_blk - 1):
        ks = min(max(ATT_QROWS * j - WIN_ROWS // 2, 0), n_rows - ATT_KROWS)
        r = ATT_QROWS * j + np.arange(ATT_QROWS)
        kr0 = np.clip(r - WIN_ROWS // 2, 0, n_rows - WIN_ROWS)
        krow = ks + np.arange(ATT_KROWS)
        row_ok = (krow[None, :] >= kr0[:, None]) & (krow[None, :] < kr0[:, None] + WIN_ROWS)
        d_row = np.clip(krow[None, :] - r[:, None] + (WIN_ROWS - 1), 0, n_dr - 1)
        row_sel.append((d_row.reshape(-1)[:, None] == np.arange(n_dr)[None, :]).astype(np.float32))
        oks.append(row_ok[:, None, :, None] & col_ok[None, :, None, :])
    a = jnp.einsum('cpr,hrd->chpd', jnp.asarray(np.stack(row_sel)), rpb.astype(F32), precision=lax.Precision.HIGHEST)
    a = a.reshape(3, N_GROUPS, ATT_QROWS, ATT_KROWS, 2 * WIN_COLS - 1)
    pad = jnp.zeros(a.shape[:-1] + (GRID_W - (2 * WIN_COLS - 1),), F32)
    v = jnp.concatenate([a[..., WIN_COLS - 1:], pad, jnp.roll(a, -1, axis=3)[..., :WIN_COLS - 1]], axis=-1)
    v = v.reshape(3, N_GROUPS, ATT_QROWS, ATT_KROWS * GRID_W)
    ok = jnp.asarray(np.stack(oks).reshape(3, ATT_QTOK, ATT_KROWS * GRID_W).astype(np.float32))
    return pl.pallas_call(
        _attn_bias_kernel,
        grid=(3, N_GROUPS),
        in_specs=[pl.BlockSpec((1, 1, ATT_QROWS, ATT_KROWS * GRID_W), lambda c, h: (c, h, 0, 0)),
                  pl.BlockSpec((1, ATT_QTOK, ATT_KROWS * GRID_W), lambda c, h: (c, 0, 0))],
        out_specs=pl.BlockSpec((1, 1, ATT_QTOK, ATT_KROWS * GRID_W), lambda c, h: (c, h, 0, 0)),
        out_shape=jax.ShapeDtypeStruct((3, N_GROUPS, ATT_QTOK, ATT_KROWS * GRID_W), F32),
        compiler_params=_params("arbitrary", "arbitrary"),
        name="attn_bias",
    )(v, ok)


def _attn_bias_kernel(v_ref, ok_ref, o_ref):
    width = v_ref.shape[3]
    for qr in range(ATT_QROWS):
        rows = slice(qr * GRID_W, (qr + 1) * GRID_W)
        base = jnp.broadcast_to(v_ref[0, 0, qr:qr + 1, :], (GRID_W, width))
        slab = pltpu.roll(base, 0, 1, stride=1, stride_axis=0)
        o_ref[0, 0, rows, :] = jnp.where(ok_ref[0, rows, :] > 0.0, slab, NEG_INF)


def _rope(x, cos_ref, sin_ref, first16):
    c = cos_ref[...]
    s = sin_ref[...]
    c2 = jnp.concatenate([c, c], axis=1)
    s2 = jnp.concatenate([s, s], axis=1)
    w = x.shape[1]
    partner = jnp.where(first16, pltpu.roll(x, w - HEAD_DIM // 4, 1), pltpu.roll(x, HEAD_DIM // 4, 1))
    return x * c2 + partner * s2


ATT_HEAD_STACK = 1


def _stack_heads(x, lane, heads):
    return jnp.concatenate([jnp.where((lane // HEAD_DIM) == h, x, 0.0) for h in heads], axis=0).astype(BF16)


def _softmax_pv(scores_values, lane, heads):
    m = None
    for s, _ in scores_values:
        mx = jnp.max(s, axis=1, keepdims=True)
        m = mx if m is None else jnp.maximum(m, mx)
    den = None
    o = None
    for s, v in scores_values:
        p = jnp.exp(s - m)
        sm = jnp.sum(p, axis=1, keepdims=True)
        den = sm if den is None else den + sm
        t = jnp.dot(p.astype(BF16), v, preferred_element_type=F32)
        o = t if o is None else o + t
    o = o * (1.0 / den)
    rows = o.shape[0] // len(heads)
    out = None
    for i, h in enumerate(heads):
        t = jnp.where((lane // HEAD_DIM) == h, o[i * rows:(i + 1) * rows], 0.0)
        out = t if out is None else out + t
    return out


def _head_groups():
    return [tuple(range(h, h + ATT_HEAD_STACK)) for h in range(0, N_GROUPS, ATT_HEAD_STACK)]


def _nattn_kernel(q_ref, k0_ref, k1_ref, k2_ref, v0_ref, v1_ref, v2_ref,
                  cq_ref, sq_ref, ck0_ref, ck1_ref, ck2_ref, sk0_ref, sk1_ref, sk2_ref,
                  kc_ref, vc_ref, bias_ref, o_ref):
    w = q_ref.shape[2]
    lane = lax.broadcasted_iota(jnp.int32, (1, w), 1)
    first16 = (lane % (HEAD_DIM // 2)) < (HEAD_DIM // 4)
    q = q_ref[0].astype(F32) * (HEAD_DIM ** -0.5)
    q_rot = _rope(q, cq_ref, sq_ref, first16)
    k_rot = jnp.concatenate([_rope(k0_ref[0].astype(F32), ck0_ref, sk0_ref, first16),
                             _rope(k1_ref[0].astype(F32), ck1_ref, sk1_ref, first16),
                             _rope(k2_ref[0].astype(F32), ck2_ref, sk2_ref, first16)], axis=0).astype(BF16)
    v = jnp.concatenate([v0_ref[0], v1_ref[0], v2_ref[0]], axis=0).astype(BF16)
    kc = kc_ref[0].astype(BF16)
    vc = vc_ref[0].astype(BF16)
    acc = None
    for heads in _head_groups():
        bias = jnp.concatenate([bias_ref[0, h] for h in heads], axis=0)
        s_win = lax.dot_general(_stack_heads(q_rot, lane, heads), k_rot, _NT, preferred_element_type=F32) + bias
        s_ctx = lax.dot_general(_stack_heads(q, lane, heads), kc, _NT, preferred_element_type=F32)
        t = _softmax_pv([(s_win, v), (s_ctx, vc)], lane, heads)
        acc = t if acc is None else acc + t
    o_ref[0] = acc.astype(o_ref.dtype)


def neighbourhood_attention(z, zc, rpb, *, q_col, k_col, v_col, kc_col, vc_col):
    bsz, n, _ = z.shape
    n_ctx = zc.shape[1]
    w = BRANCH_WIDTH
    n_blk = n // ATT_QTOK
    cos_t, sin_t = _rope_tables(n)
    bias = _attn_bias_tables(rpb, n // GRID_W)

    def kb(j):
        return jnp.clip(j - 1, 0, n_blk - ATT_KBLK)

    def zspec(col, off=None):
        if off is None:
            return pl.BlockSpec((1, ATT_QTOK, w), lambda b, j: (b, j, col))
        return pl.BlockSpec((1, ATT_QTOK, w), lambda b, j: (b, kb(j) + off, col))

    def tspec(off=None):
        if off is None:
            return pl.BlockSpec((ATT_QTOK, 2 * HEAD_DIM), lambda b, j: (j, 0))
        return pl.BlockSpec((ATT_QTOK, 2 * HEAD_DIM), lambda b, j: (kb(j) + off, 0))

    in_specs = ([zspec(q_col)] + [zspec(k_col, i) for i in range(ATT_KBLK)] + [zspec(v_col, i) for i in range(ATT_KBLK)]
                + [tspec(), tspec()] + [tspec(i) for i in range(ATT_KBLK)] * 2
                + [pl.BlockSpec((1, n_ctx, w), lambda b, j: (b, 0, kc_col)),
                   pl.BlockSpec((1, n_ctx, w), lambda b, j: (b, 0, vc_col)),
                   pl.BlockSpec((1, N_GROUPS, ATT_QTOK, ATT_KROWS * GRID_W),
                                lambda b, j: (jnp.minimum(j, 1) + j // (n_blk - 1), 0, 0, 0))])
    return pl.pallas_call(
        _nattn_kernel,
        grid=(bsz, n_blk),
        in_specs=in_specs,
        out_specs=pl.BlockSpec((1, ATT_QTOK, w), lambda b, j: (b, j, 0)),
        out_shape=jax.ShapeDtypeStruct((bsz, n, w), BF16),
        compiler_params=_params("arbitrary", "arbitrary"),
        name="neighbourhood_attention",
    )(z, z, z, z, z, z, z, cos_t, sin_t, cos_t, cos_t, cos_t, sin_t, sin_t, sin_t, zc, zc, bias)


def _ctx_attn_kernel(q_ref, kc_ref, vc_ref, o_ref):
    w = q_ref.shape[2]
    lane = lax.broadcasted_iota(jnp.int32, (1, w), 1)
    q = q_ref[0].astype(F32) * (HEAD_DIM ** -0.5)
    kc = kc_ref[0].astype(BF16)
    vc = vc_ref[0].astype(BF16)
    acc = None
    for heads in _head_groups():
        s = lax.dot_general(_stack_heads(q, lane, heads), kc, _NT, preferred_element_type=F32)
        t = _softmax_pv([(s, vc)], lane, heads)
        acc = t if acc is None else acc + t
    o_ref[0] = acc.astype(o_ref.dtype)


def context_attention(zc, *, q_col, k_col, v_col):
    bsz, n_ctx, _ = zc.shape
    w = BRANCH_WIDTH
    return pl.pallas_call(
        _ctx_attn_kernel,
        grid=(bsz,),
        in_specs=[pl.BlockSpec((1, n_ctx, w), functools.partial(lambda b, c: (b, 0, c), c=c))
                  for c in (q_col, k_col, v_col)],
        out_specs=pl.BlockSpec((1, n_ctx, w), lambda b: (b, 0, 0)),
        out_shape=jax.ShapeDtypeStruct((bsz, n_ctx, w), BF16),
        compiler_params=_params("arbitrary"),
        name="context_attention",
    )(zc, zc, zc)


def _ln(v, g, b):
    mu = jnp.mean(v, axis=-1, keepdims=True)
    var = jnp.mean(jnp.square(v - mu), axis=-1, keepdims=True)
    return (v - mu) * lax.rsqrt(var + EPS) * g + b


def _gmlp_kernel(z_ref, lng_ref, lnb_ref, w_ref, bias_ref, o_ref):
    tm = z_ref.shape[1]
    z = jax.nn.gelu(z_ref[0])
    u = z[:, :BRANCH_WIDTH]
    v = _ln(z[:, BRANCH_WIDTH:], lng_ref[...], lnb_ref[...])
    group = lax.broadcasted_iota(jnp.int32, (1, BRANCH_WIDTH), 1) // HEAD_DIM
    wcat = w_ref[...]
    for c in range(tm // CHUNK):
        rows = slice(c * CHUNK, (c + 1) * CHUNK)
        vc = v[rows]
        vst = jnp.concatenate([jnp.where(group == g, vc, 0.0) for g in range(N_GROUPS)], axis=0).astype(BF16)
        mixed = jnp.dot(wcat, vst, preferred_element_type=F32) + bias_ref[...]
        o_ref[0, rows, :] = (u[rows] * mixed).astype(o_ref.dtype)


def chunk_gmlp(za, ln_g, ln_b, w_s, b_s, *, tm):
    bsz, n, w2 = za.shape
    w = w2 // 2
    wcat = jnp.transpose(w_s, (1, 0, 2)).reshape(CHUNK, N_GROUPS * CHUNK).astype(BF16)
    bias = jnp.repeat(b_s.T, HEAD_DIM, axis=1)
    return pl.pallas_call(
        _gmlp_kernel,
        grid=(bsz, n // tm),
        in_specs=[
            pl.BlockSpec((1, tm, w2), lambda b, i: (b, i, 0)),
            pl.BlockSpec((1, w), lambda b, i: (0, 0)),
            pl.BlockSpec((1, w), lambda b, i: (0, 0)),
            pl.BlockSpec((CHUNK, N_GROUPS * CHUNK), lambda b, i: (0, 0)),
            pl.BlockSpec((CHUNK, w), lambda b, i: (0, 0)),
        ],
        out_specs=pl.BlockSpec((1, tm, w), lambda b, i: (b, i, 0)),
        out_shape=jax.ShapeDtypeStruct((bsz, n, w), BF16),
        compiler_params=_params("arbitrary", "arbitrary"),
        name="chunk_gmlp",
    )(za, ln_g[None, :], ln_b[None, :], wcat, bias)


CONV_HALO = 16
CONV_SUB = 128
SUBLANES = 8
LANES = 128


def _conv_kernel(a_ref, g_ref, ap_ref, gp_ref, an_ref, gn_ref, w_ref, cb_ref, lng_ref, lnb_ref, o_ref, y_ref):
    i = pl.program_id(1)
    tm = a_ref.shape[1]

    def glu(a, g):
        return a * _sigmoid(g)

    y_ref[0:CONV_HALO, :] = jnp.where(i > 0, glu(ap_ref[0], gp_ref[0]), 0.0)
    y_ref[CONV_HALO:CONV_HALO + tm, :] = glu(a_ref[0], g_ref[0])
    y_ref[CONV_HALO + tm:, :] = jnp.where(i < pl.num_programs(1) - 1, glu(an_ref[0], gn_ref[0]), 0.0)
    first = CONV_HALO - CONV_WIDTH // 2
    nb = CONV_SUB + 2 * CONV_HALO
    for r in range(0, tm, CONV_SUB):
        blk = y_ref[r:r + nb, :]
        acc = None
        for res in range(SUBLANES):
            rot = blk if res == 0 else pltpu.roll(blk, nb - res, 0)
            for j in range(CONV_WIDTH):
                if (first + j) % SUBLANES == res:
                    a0 = first + j - res
                    t = rot[a0:a0 + CONV_SUB, :] * w_ref[j:j + 1, :]
                    acc = t if acc is None else acc + t
        y = _ln(acc + cb_ref[...], lng_ref[...], lnb_ref[...])
        o_ref[0, r:r + CONV_SUB, :] = _silu(y).astype(o_ref.dtype)


def conformer_conv(zcv, conv_w, conv_b, ln_g, ln_b, *, tm):
    bsz, n, w2 = zcv.shape
    w = w2 // 2
    hb = tm // CONV_HALO
    n_hb = n // CONV_HALO

    def main(col):
        return pl.BlockSpec((1, tm, w), lambda b, i: (b, i, col))

    def prev(col):
        return pl.BlockSpec((1, CONV_HALO, w), lambda b, i: (b, jnp.maximum(i * hb - 1, 0), col))

    def nxt(col):
        return pl.BlockSpec((1, CONV_HALO, w), lambda b, i: (b, jnp.minimum((i + 1) * hb, n_hb - 1), col))

    vec = pl.BlockSpec((1, w), lambda b, i: (0, 0))
    return pl.pallas_call(
        _conv_kernel,
        grid=(bsz, n // tm),
        in_specs=[main(0), main(1), prev(0), prev(1), nxt(0), nxt(1),
                  pl.BlockSpec((CONV_WIDTH, w), lambda b, i: (0, 0)), vec, vec, vec],
        out_specs=pl.BlockSpec((1, tm, w), lambda b, i: (b, i, 0)),
        out_shape=jax.ShapeDtypeStruct((bsz, n, w), BF16),
        scratch_shapes=[pltpu.VMEM((tm + 2 * CONV_HALO, w), F32)],
        compiler_params=_params("arbitrary", "arbitrary"),
        name="conformer_conv",
    )(zcv, zcv, zcv, zcv, zcv, zcv, conv_w, conv_b[None, :], ln_g[None, :], ln_b[None, :])


FOURIER_N2 = 128
FOURIER_UNROLL = 8
FOURIER_KB = 8


def _np_split(m):
    m = jnp.asarray(m, F32)
    hi = m.astype(BF16)
    return hi, (m - hi.astype(F32)).astype(BF16)


def _channel_dft_matrix():
    c = np.arange(HEAD_DIM)
    ang = 2.0 * np.pi * ((c[:, None] * c[None, :]) % HEAD_DIM) / HEAD_DIM
    eye = np.eye(N_GROUPS)
    return np.concatenate([np.kron(eye, np.cos(ang)), np.kron(eye, np.sin(ang))], axis=0)


def _fourier_stage1_kernel(xa_ref, xb_ref, mh_ref, ml_ref, o_ref, *, n1, n2):
    def one_fast_index(f, carry):
        rows = pl.ds(f, n1, stride=n2)
        x = jnp.concatenate([xa_ref[0, rows, :], xb_ref[0, rows, :]], axis=1)
        x_hi, x_lo = _split_bf16(x)
        res = _dot3(mh_ref[0, f], ml_ref[0, f], x_hi, x_lo)
        o_ref[0, 0, 0, rows, :] = res[:, :LANES]
        o_ref[0, 0, 1, rows, :] = res[:, LANES:]
        return carry

    lax.fori_loop(0, n2, one_fast_index, 0, unroll=FOURIER_UNROLL)


def _fourier_stage2_kernel(br_ref, bi_ref, m2h_ref, m2l_ref, mdh_ref, mdl_ref, o_ref, *, scale):
    for k in range(br_ref.shape[3]):
        b_re = jnp.concatenate([br_ref[0, 0, 0, k], br_ref[0, 0, 1, k]], axis=1)
        b_im = jnp.concatenate([bi_ref[0, 0, 0, k], bi_ref[0, 0, 1, k]], axis=1)
        b_hi, b_lo = _split_bf16(jnp.concatenate([b_re, b_im], axis=0))
        xs = _dot3(m2h_ref[...], m2l_ref[...], b_hi, b_lo)
        x_hi, x_lo = _split_bf16(jnp.concatenate([xs[:FOURIER_N2], xs[FOURIER_N2:]], axis=1))
        o_ref[0, :, k, :] = _dot3(x_hi, x_lo, mdh_ref[...], mdl_ref[...]) * scale


def _fourier_direct_kernel(x_ref, mph_ref, mpl_ref, mdh_ref, mdl_ref, o_ref, *, scale):
    n = x_ref.shape[1]
    x_hi, x_lo = _split_bf16(x_ref[0])
    p = _dot3(mph_ref[...], mpl_ref[...], x_hi, x_lo)
    p_hi, p_lo = _split_bf16(jnp.concatenate([p[:n], p[n:]], axis=1))
    o_ref[0] = (_dot3(p_hi, p_lo, mdh_ref[...], mdl_ref[...]) * scale).astype(o_ref.dtype)


def fourier_mix(zb):
    bsz, n, w = zb.shape
    scale = float(1.0 / np.sqrt(n * HEAD_DIM))
    mdh, mdl = _np_split(_channel_dft_matrix())
    md_spec2 = pl.BlockSpec((2 * w, w), lambda b, j: (0, 0))
    if n <= 2 * FOURIER_N2:
        t = np.arange(n)
        ang = 2.0 * np.pi * ((t[:, None] * t[None, :]) % n) / n
        mph, mpl = _np_split(np.concatenate([np.cos(ang), -np.sin(ang)], axis=0))
        return pl.pallas_call(
            functools.partial(_fourier_direct_kernel, scale=scale),
            grid=(bsz,),
            in_specs=[pl.BlockSpec((1, n, w), lambda b: (b, 0, 0)),
                      pl.BlockSpec((2 * n, n), lambda b: (0, 0)), pl.BlockSpec((2 * n, n), lambda b: (0, 0)),
                      pl.BlockSpec((2 * w, w), lambda b: (0, 0)), pl.BlockSpec((2 * w, w), lambda b: (0, 0))],
            out_specs=pl.BlockSpec((1, n, w), lambda b: (b, 0, 0)),
            out_shape=jax.ShapeDtypeStruct((bsz, n, w), BF16),
            compiler_params=_params("arbitrary"),
            name="fourier_direct",
        )(zb, mph, mpl, mdh, mdl)

    n1, n2 = n // FOURIER_N2, FOURIER_N2
    f, k1, s = np.arange(n2), np.arange(n1), np.arange(n1)
    ang1 = 2.0 * np.pi * ((k1[None, :, None] * (f[:, None, None] + n2 * s[None, None, :])) % n) / n
    m1h, m1l = _np_split(np.stack([np.cos(ang1), -np.sin(ang1)]))
    k2 = np.arange(n2)
    ang2 = 2.0 * np.pi * ((k2[:, None] * f[None, :]) % n2) / n2
    c2, s2 = np.cos(ang2), np.sin(ang2)
    m2h, m2l = _np_split(np.block([[c2, s2], [-s2, c2]]))

    m1_spec = pl.BlockSpec((1, n2, n1, n1), lambda b, p: (p, 0, 0, 0))
    n_half = w // LANES
    b_st = pl.pallas_call(
        functools.partial(_fourier_stage1_kernel, n1=n1, n2=n2),
        grid=(bsz, 2),
        in_specs=[pl.BlockSpec((1, n, LANES), lambda b, p: (b, 0, 0)),
                  pl.BlockSpec((1, n, LANES), lambda b, p: (b, 0, 1)), m1_spec, m1_spec],
        out_specs=pl.BlockSpec((1, 1, n_half, n, LANES), lambda b, p: (b, p, 0, 0, 0)),
        out_shape=jax.ShapeDtypeStruct((bsz, 2, n_half, n, LANES), F32),
        compiler_params=_params("arbitrary", "arbitrary"),
        name="fourier_stage1",
    )(zb, zb, m1h, m1l)

    kb = FOURIER_KB
    b_st = b_st.reshape(bsz, 2, n_half, n1, n2, LANES)
    out = pl.pallas_call(
        functools.partial(_fourier_stage2_kernel, scale=scale),
        grid=(bsz, n1 // kb),
        in_specs=[pl.BlockSpec((1, 1, n_half, kb, n2, LANES), lambda b, j: (b, 0, 0, j, 0, 0)),
                  pl.BlockSpec((1, 1, n_half, kb, n2, LANES), lambda b, j: (b, 1, 0, j, 0, 0)),
                  pl.BlockSpec((2 * n2, 2 * n2), lambda b, j: (0, 0)),
                  pl.BlockSpec((2 * n2, 2 * n2), lambda b, j: (0, 0)),
                  md_spec2, md_spec2],
        out_specs=pl.BlockSpec((1, n2, kb, w), lambda b, j: (b, 0, j, 0)),
        out_shape=jax.ShapeDtypeStruct((bsz, n2, n1, w), F32),
        compiler_params=_params("arbitrary", "arbitrary"),
        name="fourier_stage2",
    )(b_st, b_st, m2h, m2l, mdh, mdl)
    return out.reshape(bsz, n, w)


def expert_route(h, aff_t):
    n = h.shape[1]
    cap = EC_CAPACITY * n // N_EXPERTS
    slot, off = expert_topk(aff_t, cap)
    xe, gate = expert_gather(h, slot, aff_t, off, max(cap, ROUTE_WIN))
    return xe, gate, jnp.swapaxes(slot, 1, 2), off


def kernel(x, c, ctx, c_ctx, w_mod, b_mod, norm1_g, norm2_g, w_in, sgu_ln_g, sgu_ln_b, w_spatial, b_spatial,
           w_a_out, w_b_out, conv_w, conv_b, conv_ln_g, conv_ln_b, w_c_out, rpb, w_d_out, w_out, w_router,
           w_gate_e, w_up_e, w_down_e, final_norm_g):
    bsz = x.shape[0]
    xc = ctx
    assert bsz + 1 <= MOD_ROWS
    cond = jnp.concatenate([c, c_ctx[None, :], jnp.zeros((MOD_ROWS - bsz - 1, D_MODEL), F32)], axis=0)
    mod_all = modulation(cond, w_mod, b_mod)
    in_scale = jnp.concatenate([jnp.ones((V_END,), F32), jnp.full((IN_COLS - V_END,), GATE_HALF, F32)])
    w_in_b = (w_in * in_scale).astype(BF16)
    for l in range(DEPTH):
        last = l == DEPTH - 1
        sh1, sc1, gt1, sh2, sc2, gt2 = jnp.split(mod_all[l, :bsz, None, :], 6, axis=-1)
        cmod = jnp.broadcast_to(mod_all[l, bsz:bsz + 1, None, :], (bsz, 1, 6 * D_MODEL))
        csh1, csc1, cgt1, csh2, csc2, cgt2 = jnp.split(cmod, 6, axis=-1)

        wa, wb, wc, wd_ = ((w * GATE_HALF).astype(BF16) for w in (w_a_out[l], w_b_out[l], w_c_out[l], w_d_out[l]))
        wo = w_out[l].astype(BF16)
        g1 = norm1_g[l][None, :]

        za, zb, zcv, zqkv, zg = norm_inproj(x, g1, sc1, sh1, w_in_b, l, 0, IN_WIDTHS, IN_DTYPES, tm=512)
        if last:
            (zqkv_c,) = norm_inproj(xc, g1, csc1, csh1, w_in_b, l, Q_END, (2 * BRANCH_WIDTH,), (BF16,), tm=CTX_TM)
            kc_col, vc_col = 0, 1
        else:
            cza, czb, czcv, zqkv_c, czg = norm_inproj(xc, g1, csc1, csh1, w_in_b, l, 0, IN_WIDTHS, IN_DTYPES,
                                                      tm=CTX_TM)
            kc_col, vc_col = 1, 2
        d_lat = neighbourhood_attention(zqkv, zqkv_c, rpb[l], q_col=0, k_col=1, v_col=2,
                                        kc_col=kc_col, vc_col=vc_col)
        a_lat = chunk_gmlp(za, sgu_ln_g[l], sgu_ln_b[l], w_spatial[l], b_spatial[l], tm=512)
        b_lat = fourier_mix(zb)
        c_lat = conformer_conv(zcv, conv_w[l], conv_b[l], conv_ln_g[l], conv_ln_b[l], tm=512)
        g2 = norm2_g[l][None, :]
        router_t = w_router[l].T
        x, h2, aff_t = merge_branches(a_lat, b_lat, c_lat, d_lat, zg, wa, wb, wc, wd_, wo, x, gt1,
                                      g2, sc2, sh2, router_t, tm=512)
        if not last:
            a_c = chunk_gmlp(cza, sgu_ln_g[l], sgu_ln_b[l], w_spatial[l], b_spatial[l], tm=CTX_TM)
            b_c = fourier_mix(czb)
            c_c = conformer_conv(czcv, conv_w[l], conv_b[l], conv_ln_g[l], conv_ln_b[l], tm=CTX_TM)
            d_c = context_attention(zqkv_c, q_col=0, k_col=1, v_col=2)
            xc, hc2, aff_tc = merge_branches(a_c, b_c, c_c, d_c, czg, wa, wb, wc, wd_, wo, xc, cgt1,
                                             g2, csc2, csh2, router_t, tm=CTX_TM)

        xe, gate, slot_t, off = expert_route(h2, aff_t)
        if last:
            (y,) = expert_ffn([xe], [gate], w_gate_e, w_up_e, w_down_e, l)
        else:
            xe_c, gate_c, slot_tc, off_c = expert_route(hc2, aff_tc)
            y, y_c = expert_ffn([xe, xe_c], [gate, gate_c], w_gate_e, w_up_e, w_down_e, l)
            xc = expert_combine(y_c, slot_tc, off_c, xc, cgt2)
        x = expert_combine(y, slot_t, off, x, gt2, final_norm_g if last else None)
    return x


CTX_TM = 256
```

```python
import functools

import jax
import jax.numpy as jnp
import numpy as np
from jax import lax
from jax.experimental import pallas as pl
from jax.experimental.pallas import tpu as pltpu

D_MODEL = 1024
DEPTH = 2
GRID_W = 64
HEAD_DIM = 64
N_GROUPS = 4
BRANCH_WIDTH = N_GROUPS * HEAD_DIM
N_BRANCH = 4
CHUNK = 128
CONV_WIDTH = 31
WIN_ROWS = 8
WIN_COLS = 16
ROPE_BASE = 10000.0
N_EXPERTS = 16
EC_CAPACITY = 2
EPS = 1e-6
NEG_INF = -1e30
A_END = 2 * BRANCH_WIDTH
B_END = A_END + BRANCH_WIDTH
C_END = B_END + 2 * BRANCH_WIDTH
Q_END = C_END + BRANCH_WIDTH
K_END = Q_END + BRANCH_WIDTH
V_END = K_END + BRANCH_WIDTH
IN_COLS = V_END + N_BRANCH * D_MODEL
IN_WIDTHS = (A_END, B_END - A_END, C_END - B_END, V_END - C_END, IN_COLS - V_END)
IN_DTYPES = (jnp.float32, jnp.float32, jnp.float32, jnp.bfloat16, jnp.bfloat16)
GATE_HALF = 0.5

VMEM_LIMIT_BYTES = 56 * 1024 * 1024
F32 = jnp.float32
BF16 = jnp.bfloat16


PROJ_TM = 512
MIXER_TM = 1024


def _row_tile(n, target):
    tm = min(n, target)
    assert n % tm == 0
    return tm


def _sigmoid(v):
    return 0.5 * jnp.tanh(0.5 * v) + 0.5


def _silu(v):
    return v * _sigmoid(v)


def _params(*sem):
    return pltpu.CompilerParams(dimension_semantics=sem, vmem_limit_bytes=VMEM_LIMIT_BYTES)


MOD_ROWS = 8
MOD_TN = 1536


def _split_bf16(v):
    hi = v.astype(BF16)
    return hi, (v - hi.astype(F32)).astype(BF16)


def _dot3(a_hi, a_lo, b_hi, b_lo):
    return (jnp.dot(a_hi, b_hi, preferred_element_type=F32) + jnp.dot(a_hi, b_lo, preferred_element_type=F32)
            + jnp.dot(a_lo, b_hi, preferred_element_type=F32))


def _mod_kernel(c_ref, w_ref, b_ref, o_ref):
    s_hi, s_lo = _split_bf16(_silu(c_ref[...]))
    w_hi, w_lo = _split_bf16(w_ref[0])
    o_ref[0] = _dot3(s_hi, s_lo, w_hi, w_lo) + b_ref[0]


def modulation(cond, w_mod, b_mod):
    n_layers, d, cols = w_mod.shape
    return pl.pallas_call(
        _mod_kernel,
        grid=(n_layers, cols // MOD_TN),
        in_specs=[pl.BlockSpec((MOD_ROWS, d), lambda l, j: (0, 0)),
                  pl.BlockSpec((1, d, MOD_TN), lambda l, j: (l, 0, j)),
                  pl.BlockSpec((1, 1, MOD_TN), lambda l, j: (l, 0, j))],
        out_specs=pl.BlockSpec((1, MOD_ROWS, MOD_TN), lambda l, j: (l, 0, j)),
        out_shape=jax.ShapeDtypeStruct((n_layers, MOD_ROWS, cols), F32),
        compiler_params=_params("arbitrary", "arbitrary"),
        name="modulation",
    )(cond, w_mod, b_mod[:, None, :])


INPROJ_COL_CHUNK = 512


def _norm_inproj_kernel(x_ref, g_ref, sc_ref, sh_ref, w_ref, *o_refs):
    x = x_ref[0]
    y = x * lax.rsqrt(jnp.mean(x * x, axis=-1, keepdims=True) + EPS) * g_ref[...]
    h = (y * (1.0 + sc_ref[0]) + sh_ref[0]).astype(BF16)
    off = 0
    for o_ref in o_refs:
        width = o_ref.shape[2]
        for c0 in range(0, width, INPROJ_COL_CHUNK):
            cw = min(INPROJ_COL_CHUNK, width - c0)
            o_ref[0, :, c0:c0 + cw] = jnp.dot(h, w_ref[:, off + c0:off + c0 + cw],
                                              preferred_element_type=F32).astype(o_ref.dtype)
        off += width


def norm_inproj(x, g, sc, sh, w, l, col0, widths, dtypes, *, tm):
    bsz, n, d = x.shape
    cols = sum(widths)
    assert col0 % cols == 0
    return pl.pallas_call(
        _norm_inproj_kernel,
        grid=(bsz, n // tm),
        in_specs=[
            pl.BlockSpec((1, tm, d), lambda b, i: (b, i, 0)),
            pl.BlockSpec((1, d), lambda b, i: (0, 0)),
            pl.BlockSpec((1, 1, d), lambda b, i: (b, 0, 0)),
            pl.BlockSpec((1, 1, d), lambda b, i: (b, 0, 0)),
            pl.BlockSpec((None, d, cols), lambda b, i: (l, 0, col0 // cols), pipeline_mode=pl.Buffered(1)),
        ],
        out_specs=[pl.BlockSpec((1, tm, wd), lambda b, i: (b, i, 0)) for wd in widths],
        out_shape=[jax.ShapeDtypeStruct((bsz, n, wd), dt) for wd, dt in zip(widths, dtypes, strict=True)],
        compiler_params=_params("arbitrary", "arbitrary"),
        name="norm_inproj",
    )(x, g, sc, sh, w)


def _merge_kernel(a_ref, b_ref, c_ref, d_ref, ga_ref, gb_ref, gc_ref, gd_ref,
                  wa_ref, wb_ref, wc_ref, wd_ref, wo_ref, x_ref, gt_ref,
                  g2_ref, sc2_ref, sh2_ref, rt_ref, o_ref, h_ref, aff_ref):
    m = None
    for br, gz, w in ((a_ref, ga_ref, wa_ref), (b_ref, gb_ref, wb_ref),
                      (c_ref, gc_ref, wc_ref), (d_ref, gd_ref, wd_ref)):
        hp = jnp.dot(br[0].astype(BF16), w[...], preferred_element_type=F32)
        t = hp * jnp.tanh(gz[0].astype(F32)) + hp
        m = t if m is None else m + t
    mix = jnp.dot(m.astype(BF16), wo_ref[...], preferred_element_type=F32)
    x = x_ref[0] + gt_ref[0] * mix
    o_ref[0] = x
    y = x * lax.rsqrt(jnp.mean(x * x, axis=-1, keepdims=True) + EPS) * g2_ref[...]
    h = y * (1.0 + sc2_ref[0]) + sh2_ref[0]
    h_hi, h_lo = _split_bf16(h)
    r_hi, r_lo = _split_bf16(rt_ref[...])
    ne = rt_ref.shape[0]
    both = lax.dot_general(jnp.concatenate([r_hi, r_lo], axis=0), h_hi, _NT, preferred_element_type=F32)
    logits = both[:ne] + both[ne:] + lax.dot_general(r_hi, h_lo, _NT, preferred_element_type=F32)
    pr = jnp.exp(logits - jnp.max(logits, axis=0, keepdims=True))
    aff_ref[0] = pr / jnp.sum(pr, axis=0, keepdims=True)
    h_ref[0] = h_hi


def merge_branches(a, b, cc, d, z, w_a, w_b, w_c, w_d, w_o, x, gt, g2, sc2, sh2, router_t, *, tm):
    bsz, n, dm = x.shape
    w = a.shape[-1]
    ne = router_t.shape[0]
    br_spec = pl.BlockSpec((1, tm, w), lambda bi, i: (bi, i, 0))
    gate_specs = [pl.BlockSpec((1, tm, dm), functools.partial(lambda bi, i, k: (bi, i, k), k=k))
                  for k in range(N_BRANCH)]
    wbr_spec = pl.BlockSpec((w, dm), lambda bi, i: (0, 0))
    row_spec = pl.BlockSpec((1, tm, dm), lambda bi, i: (bi, i, 0))
    mod_spec = pl.BlockSpec((1, 1, dm), lambda bi, i: (bi, 0, 0))
    return pl.pallas_call(
        _merge_kernel,
        grid=(bsz, n // tm),
        in_specs=[br_spec] * 4 + gate_specs + [wbr_spec] * 4 + [
            pl.BlockSpec((dm, dm), lambda bi, i: (0, 0)), row_spec, mod_spec,
            pl.BlockSpec((1, dm), lambda bi, i: (0, 0)), mod_spec, mod_spec,
            pl.BlockSpec((ne, dm), lambda bi, i: (0, 0)),
        ],
        out_specs=[row_spec, row_spec, pl.BlockSpec((1, ne, tm), lambda bi, i: (bi, 0, i))],
        out_shape=[jax.ShapeDtypeStruct((bsz, n, dm), F32), jax.ShapeDtypeStruct((bsz, n, dm), BF16),
                   jax.ShapeDtypeStruct((bsz, ne, n), F32)],
        compiler_params=_params("arbitrary", "arbitrary"),
        name="merge_branches",
    )(a, b, cc, d, z, z, z, z, w_a, w_b, w_c, w_d, w_o, x, gt, g2, sc2, sh2, router_t)


ROUTE_TB = 256
ROUTE_WIN = 64
ROUTE_ALIGN = 16
GATHER_SUB = 8
COMBINE_SUB = 2
OFF_LANES = 128
TOPK_EXP_STEPS = (64, 32, 16, 8, 4, 2, 1)
TOPK_BISECT_STEPS = 32


def _topk_kernel(aff_ref, slot_ref, off_ref, *, cap):
    a = aff_ref[0]
    ne, n = a.shape
    capf = jnp.float32(cap)

    def count_ge(t):
        return jnp.sum(jnp.where(a >= t, 1.0, 0.0), axis=1, keepdims=True)

    hi = jnp.full((ne, 1), 2.0, F32)
    for s in TOPK_EXP_STEPS:
        cand = hi * (2.0 ** -s)
        hi = jnp.where(count_ge(cand) < capf, cand, hi)
    lo = jnp.where(hi <= 2.0 ** -126, 0.0, hi * 0.5)

    def bisect(_, carry):
        lo, hi = carry
        mid = 0.5 * (lo + hi)
        ok = count_ge(mid) >= capf
        return jnp.where(ok, mid, lo), jnp.where(ok, hi, mid)

    lo, hi = lax.fori_loop(0, TOPK_BISECT_STEPS, bisect, (lo, hi))
    need = capf - count_ge(hi)
    r_i = lax.broadcasted_iota(jnp.int32, (ROUTE_TB, ROUTE_TB), 0)
    c_i = lax.broadcasted_iota(jnp.int32, (ROUTE_TB, ROUTE_TB), 1)
    tri = jnp.where(r_i < c_i, 1.0, 0.0).astype(BF16)
    lane = lax.broadcasted_iota(jnp.int32, (ne, OFF_LANES), 1)
    run_eq = jnp.zeros((ne, 1), F32)
    run_sel = jnp.zeros((ne, 1), F32)
    offs = jnp.zeros((ne, OFF_LANES), F32)
    for c in range(n // ROUTE_TB):
        cols = slice(c * ROUTE_TB, (c + 1) * ROUTE_TB)
        a_c = a[:, cols]
        above = a_c >= hi
        eq_c = jnp.where(above, 0.0, jnp.where(a_c >= lo, 1.0, 0.0))
        rank = jnp.dot(eq_c.astype(BF16), tri, preferred_element_type=F32) + run_eq
        sel = jnp.where(above, 1.0, jnp.where(rank < need, eq_c, 0.0))
        pos = jnp.dot(sel.astype(BF16), tri, preferred_element_type=F32) + run_sel
        slot_ref[0, :, cols] = jnp.where(sel > 0.0, pos, -1.0).astype(jnp.int32)
        offs = jnp.where(lane == c, run_sel, offs)
        run_eq = run_eq + jnp.sum(eq_c, axis=1, keepdims=True)
        run_sel = run_sel + jnp.sum(sel, axis=1, keepdims=True)
    offs = jnp.where(lane == n // ROUTE_TB, run_sel, offs)
    off_ref[0] = offs.astype(jnp.int32)


def expert_topk(aff_t, cap):
    bsz, ne, n = aff_t.shape
    return pl.pallas_call(
        functools.partial(_topk_kernel, cap=cap),
        grid=(bsz,),
        in_specs=[pl.BlockSpec((1, ne, n), lambda b: (b, 0, 0))],
        out_specs=[pl.BlockSpec((1, ne, n), lambda b: (b, 0, 0)),
                   pl.BlockSpec((1, ne, OFF_LANES), lambda b: (b, 0, 0))],
        out_shape=[jax.ShapeDtypeStruct((bsz, ne, n), jnp.int32),
                   jax.ShapeDtypeStruct((bsz, ne, OFF_LANES), jnp.int32)],
        compiler_params=_params("arbitrary"),
        name="expert_topk",
    )(aff_t)


def _window_plan(off_ref, b, e, tb, w, rows_max):
    start = off_ref[b, e, tb]
    stop = off_ref[b, e, tb + 1]
    first = (start // ROUTE_ALIGN) * ROUTE_ALIGN + w * ROUTE_WIN
    active = first < stop
    return pl.multiple_of(jnp.minimum(first, rows_max), ROUTE_ALIGN), first, active


def _num_passes(off_ref, b, e_list, tb):
    n_pass = jnp.int32(0)
    for e in e_list:
        start = off_ref[b, e, tb]
        stop = off_ref[b, e, tb + 1]
        base = (start // ROUTE_ALIGN) * ROUTE_ALIGN
        n_pass = jnp.maximum(n_pass, (stop - base + ROUTE_WIN - 1) // ROUTE_WIN)
    return n_pass


def _gather_kernel(off_ref, h_ref, slot_ref, aff_ref, xe_ref, gate_ref, *, group):
    b, g, step = pl.program_id(0), pl.program_id(1), pl.program_id(2)
    rows_max = xe_ref.shape[2] - ROUTE_WIN
    n_sub = h_ref.shape[1] // ROUTE_TB

    @pl.when(step == 0)
    def _():
        xe_ref[...] = jnp.zeros(xe_ref.shape, xe_ref.dtype)
        gate_ref[...] = jnp.zeros(gate_ref.shape, gate_ref.dtype)

    sub = lax.broadcasted_iota(jnp.int32, (ROUTE_WIN, 1), 0)
    experts = [g * group + e for e in range(group)]

    for j in range(n_sub):
        tb = step * n_sub + j
        tok = slice(j * ROUTE_TB, (j + 1) * ROUTE_TB)

        def one_pass(w, carry, tb=tb, tok=tok):
            plans = [_window_plan(off_ref, b, ge, tb, w, rows_max) for ge in experts]
            hots = []
            for e, (rows, first, active) in enumerate(plans):
                ids = rows + sub
                want = jnp.where(active & (ids >= first), ids, -2)
                hots.append(slot_ref[0, e:e + 1, tok] == want)
            onehot = jnp.concatenate([jnp.where(hm, 1.0, 0.0) for hm in hots], axis=0).astype(BF16)
            res = jnp.dot(onehot, h_ref[0, tok, :], preferred_element_type=F32)
            for e, (rows, _, _) in enumerate(plans):
                win = pl.ds(rows, ROUTE_WIN)
                xe_ref[0, e, win, :] = xe_ref[0, e, win, :] + res[e * ROUTE_WIN:(e + 1) * ROUTE_WIN].astype(BF16)
                gsel = jnp.sum(jnp.where(hots[e], aff_ref[0, e:e + 1, tok], 0.0), axis=1, keepdims=True)
                gate_ref[0, e, win, :] = gate_ref[0, e, win, :] + gsel
            return carry

        lax.fori_loop(0, _num_passes(off_ref, b, experts, tb), one_pass, 0)


def expert_gather(h, slot, aff_t, off, cap_pad, *, group=8):
    bsz, n, d = h.shape
    ne = slot.shape[1]
    tok = ROUTE_TB * min(GATHER_SUB, n // ROUTE_TB)
    grid_spec = pltpu.PrefetchScalarGridSpec(
        num_scalar_prefetch=1,
        grid=(bsz, ne // group, n // tok),
        in_specs=[
            pl.BlockSpec((1, tok, d), lambda b, g, t, off: (b, t, 0)),
            pl.BlockSpec((1, group, tok), lambda b, g, t, off: (b, g, t)),
            pl.BlockSpec((1, group, tok), lambda b, g, t, off: (b, g, t)),
        ],
        out_specs=[pl.BlockSpec((1, group, cap_pad, d), lambda b, g, t, off: (b, g, 0, 0)),
                   pl.BlockSpec((1, group, cap_pad, 1), lambda b, g, t, off: (b, g, 0, 0))],
    )
    return pl.pallas_call(
        functools.partial(_gather_kernel, group=group),
        grid_spec=grid_spec,
        out_shape=[jax.ShapeDtypeStruct((bsz, ne, cap_pad, d), BF16),
                   jax.ShapeDtypeStruct((bsz, ne, cap_pad, 1), F32)],
        compiler_params=_params("arbitrary", "arbitrary", "arbitrary"),
        name="expert_gather",
    )(off, h, slot, aff_t)


def _combine_kernel(off_ref, y_ref, slot_t_ref, x_ref, gt_ref, fg_ref, o_ref, stage_ref, acc_ref):
    b, step = pl.program_id(0), pl.program_id(1)
    ne = y_ref.shape[1]
    rows_max = y_ref.shape[2] - ROUTE_WIN
    width = ne * ROUTE_WIN
    n_sub = x_ref.shape[1] // ROUTE_TB
    experts = list(range(ne))
    lane = lax.broadcasted_iota(jnp.int32, (1, width), 1)
    e_row = lax.broadcasted_iota(jnp.int32, (ne, width), 0)
    e_lane = lax.broadcasted_iota(jnp.int32, (ne, width), 1) // ROUTE_WIN
    expand = jnp.where(e_row == e_lane, 1.0, 0.0).astype(BF16)

    for j in range(n_sub):
        tb = step * n_sub + j
        tok = slice(j * ROUTE_TB, (j + 1) * ROUTE_TB)
        s1 = slot_t_ref[0, tok, :] + 1
        hi = (s1 // ROUTE_WIN).astype(F32).astype(BF16)
        lo = (s1 % ROUTE_WIN).astype(F32).astype(BF16)
        s1_wide = (float(ROUTE_WIN) * jnp.dot(hi, expand, preferred_element_type=F32)
                   + jnp.dot(lo, expand, preferred_element_type=F32))
        acc_ref[...] = jnp.zeros(acc_ref.shape, F32)

        def one_pass(w, carry, tb=tb, s1_wide=s1_wide):
            want = jnp.full((1, width), -1, jnp.int32)
            for e in experts:
                rows, first, active = _window_plan(off_ref, b, e, tb, w, rows_max)
                stage_ref[e * ROUTE_WIN:(e + 1) * ROUTE_WIN, :] = y_ref[0, e, pl.ds(rows, ROUTE_WIN), :]
                ids = rows + lane % ROUTE_WIN
                want = jnp.where((lane // ROUTE_WIN == e) & active & (ids >= first), ids + 1, want)
            onehot = jnp.where(s1_wide == want.astype(F32), 1.0, 0.0).astype(BF16)
            acc_ref[...] += jnp.dot(onehot, stage_ref[...], preferred_element_type=F32)
            return carry

        lax.fori_loop(0, _num_passes(off_ref, b, experts, tb), one_pass, 0)
        out = x_ref[0, tok, :] + gt_ref[0] * acc_ref[...]
        if fg_ref is not None:
            out = out * lax.rsqrt(jnp.mean(out * out, axis=-1, keepdims=True) + EPS) * fg_ref[...]
        o_ref[0, tok, :] = out


def _combine_kernel_plain(off_ref, y_ref, slot_t_ref, x_ref, gt_ref, o_ref, stage_ref, acc_ref):
    _combine_kernel(off_ref, y_ref, slot_t_ref, x_ref, gt_ref, None, o_ref, stage_ref, acc_ref)


def expert_combine(y, slot_t, off, x, gt, final_g=None):
    bsz, n, d = x.shape
    ne, cap_pad = y.shape[1], y.shape[2]
    tok = ROUTE_TB * min(COMBINE_SUB, n // ROUTE_TB)
    grid_spec = pltpu.PrefetchScalarGridSpec(
        num_scalar_prefetch=1,
        grid=(bsz, n // tok),
        in_specs=[
            pl.BlockSpec((1, ne, cap_pad, d), lambda b, t, off: (b, 0, 0, 0), pipeline_mode=pl.Buffered(1)),
            pl.BlockSpec((1, tok, ne), lambda b, t, off: (b, t, 0)),
            pl.BlockSpec((1, tok, d), lambda b, t, off: (b, t, 0)),
            pl.BlockSpec((1, 1, d), lambda b, t, off: (b, 0, 0)),
        ] + ([] if final_g is None else [pl.BlockSpec((1, d), lambda b, t, off: (0, 0))]),
        out_specs=pl.BlockSpec((1, tok, d), lambda b, t, off: (b, t, 0)),
        scratch_shapes=[pltpu.VMEM((ne * ROUTE_WIN, d), BF16), pltpu.VMEM((ROUTE_TB, d), F32)],
    )
    args = (off, y, slot_t, x, gt) + (() if final_g is None else (final_g[None, :],))
    return pl.pallas_call(
        _combine_kernel_plain if final_g is None else _combine_kernel,
        grid_spec=grid_spec,
        out_shape=jax.ShapeDtypeStruct((bsz, n, d), F32),
        compiler_params=_params("arbitrary", "arbitrary"),
        name="expert_combine",
    )(*args)


FFN_FBLK = 512
FFN_MIN_ROWS = 256


def _expert_ffn_kernel(*refs, n_sets, n_f):
    xe_refs, gate_refs = refs[:n_sets], refs[n_sets:2 * n_sets]
    wg_ref, wu_ref, wd_ref = refs[2 * n_sets:2 * n_sets + 3]
    o_refs = refs[2 * n_sets + 3:3 * n_sets + 3]
    acc_refs = refs[3 * n_sets + 3:]
    f = pl.program_id(1)
    blocks = [(s, b) for s in range(n_sets) for b in range(xe_refs[s].shape[0])]
    big = [sb for sb in blocks if xe_refs[sb[0]].shape[2] >= FFN_MIN_ROWS]
    small = [sb for sb in blocks if sb not in big]
    groups = [[sb] for sb in big[:-1]] + [big[-1:] + small] if big else [small]

    def hidden_block(first, last):
        wg = wg_ref[0].astype(BF16)
        wu = wu_ref[0].astype(BF16)
        wd = wd_ref[0].astype(BF16)
        for group in groups:
            xs = [xe_refs[s][b, 0] for s, b in group]
            xe = xs[0] if len(xs) == 1 else jnp.concatenate(xs, axis=0)
            a = jnp.dot(xe, wg, preferred_element_type=F32)
            u = jnp.dot(xe, wu, preferred_element_type=F32)
            t = jnp.dot((_silu(a) * u).astype(BF16), wd, preferred_element_type=F32)
            r0 = 0
            for s, b in group:
                rows = xe_refs[s].shape[2]
                part = t[r0:r0 + rows] if first else acc_refs[s][b] + t[r0:r0 + rows]
                if last:
                    o_refs[s][b, 0] = (part * gate_refs[s][b, 0]).astype(o_refs[s].dtype)
                else:
                    acc_refs[s][b] = part
                r0 += rows

    if n_f == 1:
        hidden_block(True, True)
        return
    pl.when(f == 0)(lambda: hidden_block(True, False))
    if n_f > 2:
        pl.when((f > 0) & (f < n_f - 1))(lambda: hidden_block(False, False))
    pl.when(f == n_f - 1)(lambda: hidden_block(False, True))


def expert_ffn(xes, gates, wg, wu, wd, l):
    n_sets = len(xes)
    _, ne, d, ff = wg.shape
    xe_specs = [pl.BlockSpec((xe.shape[0], 1) + xe.shape[2:], lambda e, f: (0, e, 0, 0)) for xe in xes]
    gate_specs = [pl.BlockSpec((g.shape[0], 1) + g.shape[2:], lambda e, f: (0, e, 0, 0)) for g in gates]
    return pl.pallas_call(
        functools.partial(_expert_ffn_kernel, n_sets=n_sets, n_f=ff // FFN_FBLK),
        grid=(ne, ff // FFN_FBLK),
        in_specs=xe_specs + gate_specs + [
            pl.BlockSpec((None, 1, d, FFN_FBLK), lambda e, f: (l, e, 0, f)),
            pl.BlockSpec((None, 1, d, FFN_FBLK), lambda e, f: (l, e, 0, f)),
            pl.BlockSpec((None, 1, FFN_FBLK, d), lambda e, f: (l, e, f, 0)),
        ],
        out_specs=xe_specs,
        out_shape=[jax.ShapeDtypeStruct(xe.shape, BF16) for xe in xes],
        scratch_shapes=[pltpu.VMEM((xe.shape[0],) + xe.shape[2:], F32) for xe in xes],
        compiler_params=_params("arbitrary", "arbitrary"),
        name="expert_ffn",
    )(*xes, *gates, wg, wu, wd)


ATT_QROWS = 4
ATT_QTOK = ATT_QROWS * GRID_W
ATT_KROWS = ATT_QROWS + WIN_ROWS
ATT_KBLK = ATT_KROWS * GRID_W // ATT_QTOK
_NT = (((1,), (1,)), ((), ()))


def _rope_tables(n):
    quarter = HEAD_DIM // 4
    inv = ROPE_BASE ** (-np.arange(quarter, dtype=np.float64) / quarter)
    t = np.arange(n)
    ang_r = (t // GRID_W)[:, None] * inv[None, :]
    ang_c = (t % GRID_W)[:, None] * inv[None, :]
    cos = np.concatenate([np.cos(ang_r)] * 2 + [np.cos(ang_c)] * 2, axis=1)
    sin = np.concatenate([-np.sin(ang_r), np.sin(ang_r), -np.sin(ang_c), np.sin(ang_c)], axis=1)
    return (jnp.asarray(np.concatenate([cos, cos], axis=1), F32),
            jnp.asarray(np.concatenate([sin, sin], axis=1), F32))


def _attn_bias_tables(rpb, n_rows):
    n_blk = n_rows // ATT_QROWS
    n_dr = 2 * WIN_ROWS - 1
    qc = np.arange(GRID_W)
    kc = np.arange(GRID_W)
    c_start = np.clip(qc - WIN_COLS // 2, 0, GRID_W - WIN_COLS)
    col_ok = (kc[None, :] >= c_start[:, None]) & (kc[None, :] < c_start[:, None] + WIN_COLS)
    assert np.all(np.abs(kc[None, :] - qc[:, None])[col_ok] < WIN_COLS)
    row_sel, oks = [], []
    for j in (0, 1, n_blk - 1):
        ks = min(max(ATT_QROWS * j - WIN_ROWS // 2, 0), n_rows - ATT_KROWS)
        r = ATT_QROWS * j + np.arange(ATT_QROWS)
        kr0 = np.clip(r - WIN_ROWS // 2, 0, n_rows - WIN_ROWS)
        krow = ks + np.arange(ATT_KROWS)
        row_ok = (krow[None, :] >= kr0[:, None]) & (krow[None, :] < kr0[:, None] + WIN_ROWS)
        d_row = np.clip(krow[None, :] - r[:, None] + (WIN_ROWS - 1), 0, n_dr - 1)
        row_sel.append((d_row.reshape(-1)[:, None] == np.arange(n_dr)[None, :]).astype(np.float32))
        oks.append(row_ok[:, None, :, None] & col_ok[None, :, None, :])
    a = jnp.einsum('cpr,hrd->chpd', jnp.asarray(np.stack(row_sel)), rpb.astype(F32), precision=lax.Precision.HIGHEST)
    a = a.reshape(3, N_GROUPS, ATT_QROWS, ATT_KROWS, 2 * WIN_COLS - 1)
    pad = jnp.zeros(a.shape[:-1] + (GRID_W - (2 * WIN_COLS - 1),), F32)
    v = jnp.concatenate([a[..., WIN_COLS - 1:], pad, jnp.roll(a, -1, axis=3)[..., :WIN_COLS - 1]], axis=-1)
    v = v.reshape(3, N_GROUPS, ATT_QROWS, ATT_KROWS * GRID_W)
    ok = jnp.asarray(np.stack(oks).reshape(3, ATT_QTOK, ATT_KROWS * GRID_W).astype(np.float32))
    return pl.pallas_call(
        _attn_bias_kernel,
        grid=(3, N_GROUPS),
        in_specs=[pl.BlockSpec((1, 1, ATT_QROWS, ATT_KROWS * GRID_W), lambda c, h: (c, h, 0, 0)),
                  pl.BlockSpec((1, ATT_QTOK, ATT_KROWS * GRID_W), lambda c, h: (c, 0, 0))],
        out_specs=pl.BlockSpec((1, 1, ATT_QTOK, ATT_KROWS * GRID_W), lambda c, h: (c, h, 0, 0)),
        out_shape=jax.ShapeDtypeStruct((3, N_GROUPS, ATT_QTOK, ATT_KROWS * GRID_W), F32),
        compiler_params=_params("arbitrary", "arbitrary"),
        name="attn_bias",
    )(v, ok)


def _attn_bias_kernel(v_ref, ok_ref, o_ref):
    width = v_ref.shape[3]
    for qr in range(ATT_QROWS):
        rows = slice(qr * GRID_W, (qr + 1) * GRID_W)
        base = jnp.broadcast_to(v_ref[0, 0, qr:qr + 1, :], (GRID_W, width))
        slab = pltpu.roll(base, 0, 1, stride=1, stride_axis=0)
        o_ref[0, 0, rows, :] = jnp.where(ok_ref[0, rows, :] > 0.0, slab, NEG_INF)


def _rope(x, cos_ref, sin_ref, first16):
    c = cos_ref[...]
    s = sin_ref[...]
    c2 = jnp.concatenate([c, c], axis=1)
    s2 = jnp.concatenate([s, s], axis=1)
    w = x.shape[1]
    partner = jnp.where(first16, pltpu.roll(x, w - HEAD_DIM // 4, 1), pltpu.roll(x, HEAD_DIM // 4, 1))
    return x * c2 + partner * s2


ATT_HEAD_STACK = 1


def _stack_heads(x, lane, heads):
    return jnp.concatenate([jnp.where((lane // HEAD_DIM) == h, x, 0.0) for h in heads], axis=0).astype(BF16)


def _softmax_pv(scores_values, lane, heads):
    m = None
    for s, _ in scores_values:
        mx = jnp.max(s, axis=1, keepdims=True)
        m = mx if m is None else jnp.maximum(m, mx)
    den = None
    o = None
    for s, v in scores_values:
        p = jnp.exp(s - m)
        sm = jnp.sum(p, axis=1, keepdims=True)
        den = sm if den is None else den + sm
        t = jnp.dot(p.astype(BF16), v, preferred_element_type=F32)
        o = t if o is None else o + t
    o = o * (1.0 / den)
    rows = o.shape[0] // len(heads)
    out = None
    for i, h in enumerate(heads):
        t = jnp.where((lane // HEAD_DIM) == h, o[i * rows:(i + 1) * rows], 0.0)
        out = t if out is None else out + t
    return out


def _head_groups():
    return [tuple(range(h, h + ATT_HEAD_STACK)) for h in range(0, N_GROUPS, ATT_HEAD_STACK)]


ATT_QSTEP = 2
ATT_SLAB = ATT_KBLK + ATT_QSTEP - 1


def _att_key_block(jq, n_blk):
    return jnp.clip(jq - 1, 0, n_blk - ATT_KBLK)


def _nattn_kernel(*refs, n_blk):
    q_ref = refs[0]
    k_refs, v_refs = refs[1:1 + ATT_SLAB], refs[1 + ATT_SLAB:1 + 2 * ATT_SLAB]
    cq_ref, sq_ref = refs[1 + 2 * ATT_SLAB:3 + 2 * ATT_SLAB]
    ck_refs = refs[3 + 2 * ATT_SLAB:3 + 3 * ATT_SLAB]
    sk_refs = refs[3 + 3 * ATT_SLAB:3 + 4 * ATT_SLAB]
    kc_ref, vc_ref = refs[3 + 4 * ATT_SLAB:5 + 4 * ATT_SLAB]
    bias_refs = refs[5 + 4 * ATT_SLAB:5 + 4 * ATT_SLAB + ATT_QSTEP]
    o_ref, k_scr, v_scr = refs[5 + 4 * ATT_SLAB + ATT_QSTEP:]
    j = pl.program_id(1)
    w = q_ref.shape[2]
    lane = lax.broadcasted_iota(jnp.int32, (1, w), 1)
    first16 = (lane % (HEAD_DIM // 2)) < (HEAD_DIM // 4)
    q_all = q_ref[0].astype(F32) * (HEAD_DIM ** -0.5)
    q_rot_all = _rope(q_all, cq_ref, sq_ref, first16)
    for i in range(ATT_SLAB):
        rows = slice(i * ATT_QTOK, (i + 1) * ATT_QTOK)
        k_scr[rows, :] = _rope(k_refs[i][0].astype(F32), ck_refs[i], sk_refs[i], first16).astype(BF16)
        v_scr[rows, :] = v_refs[i][0].astype(BF16)
    kc = kc_ref[0].astype(BF16)
    vc = vc_ref[0].astype(BF16)
    slab0 = _att_key_block(j * ATT_QSTEP, n_blk - ATT_QSTEP + 1)
    for i in range(ATT_QSTEP):
        rows = slice(i * ATT_QTOK, (i + 1) * ATT_QTOK)
        q, q_rot = q_all[rows], q_rot_all[rows]
        start = pl.multiple_of((_att_key_block(j * ATT_QSTEP + i, n_blk) - slab0) * ATT_QTOK, ATT_QTOK)
        keys = pl.ds(start, ATT_KBLK * ATT_QTOK)
        k_rot, v = k_scr[keys, :], v_scr[keys, :]
        acc = None
        for heads in _head_groups():
            bias = jnp.concatenate([bias_refs[i][0, h] for h in heads], axis=0)
            s_win = lax.dot_general(_stack_heads(q_rot, lane, heads), k_rot, _NT, preferred_element_type=F32) + bias
            s_ctx = lax.dot_general(_stack_heads(q, lane, heads), kc, _NT, preferred_element_type=F32)
            t = _softmax_pv([(s_win, v), (s_ctx, vc)], lane, heads)
            acc = t if acc is None else acc + t
        o_ref[0, rows, :] = acc.astype(o_ref.dtype)


def neighbourhood_attention(z, zc, rpb, *, q_col, k_col, v_col, kc_col, vc_col):
    bsz, n, _ = z.shape
    n_ctx = zc.shape[1]
    w = BRANCH_WIDTH
    n_blk = n // ATT_QTOK
    assert n_blk % ATT_QSTEP == 0 and n_blk >= ATT_SLAB
    cos_t, sin_t = _rope_tables(n)
    bias = _attn_bias_tables(rpb, n // GRID_W)
    q_tok = ATT_QSTEP * ATT_QTOK

    def slab(j):
        return _att_key_block(j * ATT_QSTEP, n_blk - ATT_QSTEP + 1)

    def kspec(col, off):
        return pl.BlockSpec((1, ATT_QTOK, w), lambda b, j: (b, slab(j) + off, col))

    def tspec(off):
        return pl.BlockSpec((ATT_QTOK, 2 * HEAD_DIM), lambda b, j: (slab(j) + off, 0))

    def bias_spec(i):
        def cfg(b, j):
            jq = j * ATT_QSTEP + i
            return (jnp.minimum(jq, 1) + jq // (n_blk - 1), 0, 0, 0)
        return pl.BlockSpec((1, N_GROUPS, ATT_QTOK, ATT_KROWS * GRID_W), cfg)

    qt_spec = pl.BlockSpec((q_tok, 2 * HEAD_DIM), lambda b, j: (j, 0))
    in_specs = ([pl.BlockSpec((1, q_tok, w), lambda b, j: (b, j, q_col))]
                + [kspec(k_col, i) for i in range(ATT_SLAB)] + [kspec(v_col, i) for i in range(ATT_SLAB)]
                + [qt_spec, qt_spec] + [tspec(i) for i in range(ATT_SLAB)] * 2
                + [pl.BlockSpec((1, n_ctx, w), lambda b, j: (b, 0, kc_col)),
                   pl.BlockSpec((1, n_ctx, w), lambda b, j: (b, 0, vc_col))]
                + [bias_spec(i) for i in range(ATT_QSTEP)])
    args = ([z] * (1 + 2 * ATT_SLAB) + [cos_t, sin_t] + [cos_t] * ATT_SLAB + [sin_t] * ATT_SLAB + [zc, zc]
            + [bias] * ATT_QSTEP)
    return pl.pallas_call(
        functools.partial(_nattn_kernel, n_blk=n_blk),
        grid=(bsz, n_blk // ATT_QSTEP),
        in_specs=in_specs,
        out_specs=pl.BlockSpec((1, q_tok, w), lambda b, j: (b, j, 0)),
        out_shape=jax.ShapeDtypeStruct((bsz, n, w), BF16),
        scratch_shapes=[pltpu.VMEM((ATT_SLAB * ATT_QTOK, w), BF16), pltpu.VMEM((ATT_SLAB * ATT_QTOK, w), BF16)],
        compiler_params=_params("arbitrary", "arbitrary"),
        name="neighbourhood_attention",
    )(*args)


def _ctx_attn_kernel(q_ref, kc_ref, vc_ref, o_ref):
    w = q_ref.shape[2]
    lane = lax.broadcasted_iota(jnp.int32, (1, w), 1)
    q = q_ref[0].astype(F32) * (HEAD_DIM ** -0.5)
    kc = kc_ref[0].astype(BF16)
    vc = vc_ref[0].astype(BF16)
    acc = None
    for heads in _head_groups():
        s = lax.dot_general(_stack_heads(q, lane, heads), kc, _NT, preferred_element_type=F32)
        t = _softmax_pv([(s, vc)], lane, heads)
        acc = t if acc is None else acc + t
    o_ref[0] = acc.astype(o_ref.dtype)


def context_attention(zc, *, q_col, k_col, v_col):
    bsz, n_ctx, _ = zc.shape
    w = BRANCH_WIDTH
    return pl.pallas_call(
        _ctx_attn_kernel,
        grid=(bsz,),
        in_specs=[pl.BlockSpec((1, n_ctx, w), functools.partial(lambda b, c: (b, 0, c), c=c))
                  for c in (q_col, k_col, v_col)],
        out_specs=pl.BlockSpec((1, n_ctx, w), lambda b: (b, 0, 0)),
        out_shape=jax.ShapeDtypeStruct((bsz, n_ctx, w), BF16),
        compiler_params=_params("arbitrary"),
        name="context_attention",
    )(zc, zc, zc)


def _ln(v, g, b):
    mu = jnp.mean(v, axis=-1, keepdims=True)
    var = jnp.mean(jnp.square(v - mu), axis=-1, keepdims=True)
    return (v - mu) * lax.rsqrt(var + EPS) * g + b


def _gmlp_kernel(z_ref, lng_ref, lnb_ref, w_ref, bias_ref, o_ref):
    tm = z_ref.shape[1]
    z = jax.nn.gelu(z_ref[0])
    u = z[:, :BRANCH_WIDTH]
    v = _ln(z[:, BRANCH_WIDTH:], lng_ref[...], lnb_ref[...])
    group = lax.broadcasted_iota(jnp.int32, (1, BRANCH_WIDTH), 1) // HEAD_DIM
    wcat = w_ref[...]
    for c in range(tm // CHUNK):
        rows = slice(c * CHUNK, (c + 1) * CHUNK)
        vc = v[rows]
        vst = jnp.concatenate([jnp.where(group == g, vc, 0.0) for g in range(N_GROUPS)], axis=0).astype(BF16)
        mixed = jnp.dot(wcat, vst, preferred_element_type=F32) + bias_ref[...]
        o_ref[0, rows, :] = (u[rows] * mixed).astype(o_ref.dtype)


def chunk_gmlp(za, ln_g, ln_b, w_s, b_s, *, tm):
    bsz, n, w2 = za.shape
    w = w2 // 2
    wcat = jnp.transpose(w_s, (1, 0, 2)).reshape(CHUNK, N_GROUPS * CHUNK).astype(BF16)
    bias = jnp.repeat(b_s.T, HEAD_DIM, axis=1)
    return pl.pallas_call(
        _gmlp_kernel,
        grid=(bsz, n // tm),
        in_specs=[
            pl.BlockSpec((1, tm, w2), lambda b, i: (b, i, 0)),
            pl.BlockSpec((1, w), lambda b, i: (0, 0)),
            pl.BlockSpec((1, w), lambda b, i: (0, 0)),
            pl.BlockSpec((CHUNK, N_GROUPS * CHUNK), lambda b, i: (0, 0)),
            pl.BlockSpec((CHUNK, w), lambda b, i: (0, 0)),
        ],
        out_specs=pl.BlockSpec((1, tm, w), lambda b, i: (b, i, 0)),
        out_shape=jax.ShapeDtypeStruct((bsz, n, w), BF16),
        compiler_params=_params("arbitrary", "arbitrary"),
        name="chunk_gmlp",
    )(za, ln_g[None, :], ln_b[None, :], wcat, bias)


CONV_HALO = 16
CONV_SUB = 128
SUBLANES = 8
LANES = 128


def _conv_kernel(a_ref, g_ref, ap_ref, gp_ref, an_ref, gn_ref, w_ref, cb_ref, lng_ref, lnb_ref, o_ref, y_ref):
    i = pl.program_id(1)
    tm = a_ref.shape[1]

    def glu(a, g):
        return a * _sigmoid(g)

    y_ref[0:CONV_HALO, :] = jnp.where(i > 0, glu(ap_ref[0], gp_ref[0]), 0.0)
    y_ref[CONV_HALO:CONV_HALO + tm, :] = glu(a_ref[0], g_ref[0])
    y_ref[CONV_HALO + tm:, :] = jnp.where(i < pl.num_programs(1) - 1, glu(an_ref[0], gn_ref[0]), 0.0)
    first = CONV_HALO - CONV_WIDTH // 2
    nb = CONV_SUB + 2 * CONV_HALO
    for r in range(0, tm, CONV_SUB):
        blk = y_ref[r:r + nb, :]
        acc = None
        for res in range(SUBLANES):
            rot = blk if res == 0 else pltpu.roll(blk, nb - res, 0)
            for j in range(CONV_WIDTH):
                if (first + j) % SUBLANES == res:
                    a0 = first + j - res
                    t = rot[a0:a0 + CONV_SUB, :] * w_ref[j:j + 1, :]
                    acc = t if acc is None else acc + t
        y = _ln(acc + cb_ref[...], lng_ref[...], lnb_ref[...])
        o_ref[0, r:r + CONV_SUB, :] = _silu(y).astype(o_ref.dtype)


def conformer_conv(zcv, conv_w, conv_b, ln_g, ln_b, *, tm):
    bsz, n, w2 = zcv.shape
    w = w2 // 2
    hb = tm // CONV_HALO
    n_hb = n // CONV_HALO

    def main(col):
        return pl.BlockSpec((1, tm, w), lambda b, i: (b, i, col))

    def prev(col):
        return pl.BlockSpec((1, CONV_HALO, w), lambda b, i: (b, jnp.maximum(i * hb - 1, 0), col))

    def nxt(col):
        return pl.BlockSpec((1, CONV_HALO, w), lambda b, i: (b, jnp.minimum((i + 1) * hb, n_hb - 1), col))

    vec = pl.BlockSpec((1, w), lambda b, i: (0, 0))
    return pl.pallas_call(
        _conv_kernel,
        grid=(bsz, n // tm),
        in_specs=[main(0), main(1), prev(0), prev(1), nxt(0), nxt(1),
                  pl.BlockSpec((CONV_WIDTH, w), lambda b, i: (0, 0)), vec, vec, vec],
        out_specs=pl.BlockSpec((1, tm, w), lambda b, i: (b, i, 0)),
        out_shape=jax.ShapeDtypeStruct((bsz, n, w), BF16),
        scratch_shapes=[pltpu.VMEM((tm + 2 * CONV_HALO, w), F32)],
        compiler_params=_params("arbitrary", "arbitrary"),
        name="conformer_conv",
    )(zcv, zcv, zcv, zcv, zcv, zcv, conv_w, conv_b[None, :], ln_g[None, :], ln_b[None, :])


FOURIER_N2 = 128
FOURIER_UNROLL = 8
FOURIER_KB = 8


def _np_split(m):
    m = jnp.asarray(m, F32)
    hi = m.astype(BF16)
    return hi, (m - hi.astype(F32)).astype(BF16)


def _channel_dft_matrix():
    c = np.arange(HEAD_DIM)
    ang = 2.0 * np.pi * ((c[:, None] * c[None, :]) % HEAD_DIM) / HEAD_DIM
    eye = np.eye(N_GROUPS)
    return np.concatenate([np.kron(eye, np.cos(ang)), np.kron(eye, np.sin(ang))], axis=0)


def _fourier_stage1_kernel(xa_ref, xb_ref, mh_ref, ml_ref, o_ref, *, n1, n2):
    def one_fast_index(f, carry):
        rows = pl.ds(f, n1, stride=n2)
        x = jnp.concatenate([xa_ref[0, rows, :], xb_ref[0, rows, :]], axis=1)
        x_hi, x_lo = _split_bf16(x)
        res = _dot3(mh_ref[0, f], ml_ref[0, f], x_hi, x_lo)
        o_ref[0, 0, 0, rows, :] = res[:, :LANES]
        o_ref[0, 0, 1, rows, :] = res[:, LANES:]
        return carry

    lax.fori_loop(0, n2, one_fast_index, 0, unroll=FOURIER_UNROLL)


def _fourier_stage2_kernel(br_ref, bi_ref, m2h_ref, m2l_ref, mdh_ref, mdl_ref, o_ref, *, scale):
    for k in range(br_ref.shape[3]):
        b_re = jnp.concatenate([br_ref[0, 0, 0, k], br_ref[0, 0, 1, k]], axis=1)
        b_im = jnp.concatenate([bi_ref[0, 0, 0, k], bi_ref[0, 0, 1, k]], axis=1)
        b_hi, b_lo = _split_bf16(jnp.concatenate([b_re, b_im], axis=0))
        xs = _dot3(m2h_ref[...], m2l_ref[...], b_hi, b_lo)
        x_hi, x_lo = _split_bf16(jnp.concatenate([xs[:FOURIER_N2], xs[FOURIER_N2:]], axis=1))
        o_ref[0, :, k, :] = _dot3(x_hi, x_lo, mdh_ref[...], mdl_ref[...]) * scale


def _fourier_direct_kernel(x_ref, mph_ref, mpl_ref, mdh_ref, mdl_ref, o_ref, *, scale):
    n = x_ref.shape[1]
    x_hi, x_lo = _split_bf16(x_ref[0])
    p = _dot3(mph_ref[...], mpl_ref[...], x_hi, x_lo)
    p_hi, p_lo = _split_bf16(jnp.concatenate([p[:n], p[n:]], axis=1))
    o_ref[0] = (_dot3(p_hi, p_lo, mdh_ref[...], mdl_ref[...]) * scale).astype(o_ref.dtype)


def fourier_mix(zb):
    bsz, n, w = zb.shape
    scale = float(1.0 / np.sqrt(n * HEAD_DIM))
    mdh, mdl = _np_split(_channel_dft_matrix())
    md_spec2 = pl.BlockSpec((2 * w, w), lambda b, j: (0, 0))
    if n <= 2 * FOURIER_N2:
        t = np.arange(n)
        ang = 2.0 * np.pi * ((t[:, None] * t[None, :]) % n) / n
        mph, mpl = _np_split(np.concatenate([np.cos(ang), -np.sin(ang)], axis=0))
        return pl.pallas_call(
            functools.partial(_fourier_direct_kernel, scale=scale),
            grid=(bsz,),
            in_specs=[pl.BlockSpec((1, n, w), lambda b: (b, 0, 0)),
                      pl.BlockSpec((2 * n, n), lambda b: (0, 0)), pl.BlockSpec((2 * n, n), lambda b: (0, 0)),
                      pl.BlockSpec((2 * w, w), lambda b: (0, 0)), pl.BlockSpec((2 * w, w), lambda b: (0, 0))],
            out_specs=pl.BlockSpec((1, n, w), lambda b: (b, 0, 0)),
            out_shape=jax.ShapeDtypeStruct((bsz, n, w), BF16),
            compiler_params=_params("arbitrary"),
            name="fourier_direct",
        )(zb, mph, mpl, mdh, mdl)

    n1, n2 = n // FOURIER_N2, FOURIER_N2
    f, k1, s = np.arange(n2), np.arange(n1), np.arange(n1)
    ang1 = 2.0 * np.pi * ((k1[None, :, None] * (f[:, None, None] + n2 * s[None, None, :])) % n) / n
    m1h, m1l = _np_split(np.stack([np.cos(ang1), -np.sin(ang1)]))
    k2 = np.arange(n2)
    ang2 = 2.0 * np.pi * ((k2[:, None] * f[None, :]) % n2) / n2
    c2, s2 = np.cos(ang2), np.sin(ang2)
    m2h, m2l = _np_split(np.block([[c2, s2], [-s2, c2]]))

    m1_spec = pl.BlockSpec((1, n2, n1, n1), lambda b, p: (p, 0, 0, 0))
    n_half = w // LANES
    b_st = pl.pallas_call(
        functools.partial(_fourier_stage1_kernel, n1=n1, n2=n2),
        grid=(bsz, 2),
        in_specs=[pl.BlockSpec((1, n, LANES), lambda b, p: (b, 0, 0)),
                  pl.BlockSpec((1, n, LANES), lambda b, p: (b, 0, 1)), m1_spec, m1_spec],
        out_specs=pl.BlockSpec((1, 1, n_half, n, LANES), lambda b, p: (b, p, 0, 0, 0)),
        out_shape=jax.ShapeDtypeStruct((bsz, 2, n_half, n, LANES), F32),
        compiler_params=_params("arbitrary", "arbitrary"),
        name="fourier_stage1",
    )(zb, zb, m1h, m1l)

    kb = FOURIER_KB
    b_st = b_st.reshape(bsz, 2, n_half, n1, n2, LANES)
    out = pl.pallas_call(
        functools.partial(_fourier_stage2_kernel, scale=scale),
        grid=(bsz, n1 // kb),
        in_specs=[pl.BlockSpec((1, 1, n_half, kb, n2, LANES), lambda b, j: (b, 0, 0, j, 0, 0)),
                  pl.BlockSpec((1, 1, n_half, kb, n2, LANES), lambda b, j: (b, 1, 0, j, 0, 0)),
                  pl.BlockSpec((2 * n2, 2 * n2), lambda b, j: (0, 0)),
                  pl.BlockSpec((2 * n2, 2 * n2), lambda b, j: (0, 0)),
                  md_spec2, md_spec2],
        out_specs=pl.BlockSpec((1, n2, kb, w), lambda b, j: (b, 0, j, 0)),
        out_shape=jax.ShapeDtypeStruct((bsz, n2, n1, w), F32),
        compiler_params=_params("arbitrary", "arbitrary"),
        name="fourier_stage2",
    )(b_st, b_st, m2h, m2l, mdh, mdl)
    return out.reshape(bsz, n, w)


def expert_route(h, aff_t):
    n = h.shape[1]
    cap = EC_CAPACITY * n // N_EXPERTS
    slot, off = expert_topk(aff_t, cap)
    xe, gate = expert_gather(h, slot, aff_t, off, max(cap, ROUTE_WIN))
    return xe, gate, jnp.swapaxes(slot, 1, 2), off


def kernel(x, c, ctx, c_ctx, w_mod, b_mod, norm1_g, norm2_g, w_in, sgu_ln_g, sgu_ln_b, w_spatial, b_spatial,
           w_a_out, w_b_out, conv_w, conv_b, conv_ln_g, conv_ln_b, w_c_out, rpb, w_d_out, w_out, w_router,
           w_gate_e, w_up_e, w_down_e, final_norm_g):
    bsz = x.shape[0]
    xc = ctx
    assert bsz + 1 <= MOD_ROWS
    cond = jnp.concatenate([c, c_ctx[None, :], jnp.zeros((MOD_ROWS - bsz - 1, D_MODEL), F32)], axis=0)
    mod_all = modulation(cond, w_mod, b_mod)
    in_scale = jnp.concatenate([jnp.ones((V_END,), F32), jnp.full((IN_COLS - V_END,), GATE_HALF, F32)])
    w_in_b = (w_in * in_scale).astype(BF16)
    for l in range(DEPTH):
        last = l == DEPTH - 1
        sh1, sc1, gt1, sh2, sc2, gt2 = jnp.split(mod_all[l, :bsz, None, :], 6, axis=-1)
        cmod = jnp.broadcast_to(mod_all[l, bsz:bsz + 1, None, :], (bsz, 1, 6 * D_MODEL))
        csh1, csc1, cgt1, csh2, csc2, cgt2 = jnp.split(cmod, 6, axis=-1)

        wa, wb, wc, wd_ = ((w * GATE_HALF).astype(BF16) for w in (w_a_out[l], w_b_out[l], w_c_out[l], w_d_out[l]))
        wo = w_out[l].astype(BF16)
        g1 = norm1_g[l][None, :]

        tm_x, tm_c = _row_tile(x.shape[1], PROJ_TM), _row_tile(xc.shape[1], PROJ_TM)
        tmix_x, tmix_c = _row_tile(x.shape[1], MIXER_TM), _row_tile(xc.shape[1], MIXER_TM)
        za, zb, zcv, zqkv, zg = norm_inproj(x, g1, sc1, sh1, w_in_b, l, 0, IN_WIDTHS, IN_DTYPES, tm=tm_x)
        if last:
            (zqkv_c,) = norm_inproj(xc, g1, csc1, csh1, w_in_b, l, Q_END, (2 * BRANCH_WIDTH,), (BF16,), tm=tm_c)
            kc_col, vc_col = 0, 1
        else:
            cza, czb, czcv, zqkv_c, czg = norm_inproj(xc, g1, csc1, csh1, w_in_b, l, 0, IN_WIDTHS, IN_DTYPES,
                                                      tm=tm_c)
            kc_col, vc_col = 1, 2
        d_lat = neighbourhood_attention(zqkv, zqkv_c, rpb[l], q_col=0, k_col=1, v_col=2,
                                        kc_col=kc_col, vc_col=vc_col)
        a_lat = chunk_gmlp(za, sgu_ln_g[l], sgu_ln_b[l], w_spatial[l], b_spatial[l], tm=tmix_x)
        b_lat = fourier_mix(zb)
        c_lat = conformer_conv(zcv, conv_w[l], conv_b[l], conv_ln_g[l], conv_ln_b[l], tm=tmix_x)
        g2 = norm2_g[l][None, :]
        router_t = w_router[l].T
        x, h2, aff_t = merge_branches(a_lat, b_lat, c_lat, d_lat, zg, wa, wb, wc, wd_, wo, x, gt1,
                                      g2, sc2, sh2, router_t, tm=tm_x)
        if not last:
            a_c = chunk_gmlp(cza, sgu_ln_g[l], sgu_ln_b[l], w_spatial[l], b_spatial[l], tm=tmix_c)
            b_c = fourier_mix(czb)
            c_c = conformer_conv(czcv, conv_w[l], conv_b[l], conv_ln_g[l], conv_ln_b[l], tm=tmix_c)
            d_c = context_attention(zqkv_c, q_col=0, k_col=1, v_col=2)
            xc, hc2, aff_tc = merge_branches(a_c, b_c, c_c, d_c, czg, wa, wb, wc, wd_, wo, xc, cgt1,
                                             g2, csc2, csh2, router_t, tm=tm_c)

        xe, gate, slot_t, off = expert_route(h2, aff_t)
        if last:
            (y,) = expert_ffn([xe], [gate], w_gate_e, w_up_e, w_down_e, l)
        else:
            xe_c, gate_c, slot_tc, off_c = expert_route(hc2, aff_tc)
            y, y_c = expert_ffn([xe, xe_c], [gate, gate_c], w_gate_e, w_up_e, w_down_e, l)
            xc = expert_combine(y_c, slot_tc, off_c, xc, cgt2)
        x = expert_combine(y, slot_t, off, x, gt2, final_norm_g if last else None)
    return x
```

```python
import functools

import jax
import jax.numpy as jnp
import numpy as np
from jax import lax
from jax.experimental import pallas as pl
from jax.experimental.pallas import tpu as pltpu

D_MODEL = 1024
DEPTH = 2
GRID_W = 64
HEAD_DIM = 64
N_GROUPS = 4
BRANCH_WIDTH = N_GROUPS * HEAD_DIM
N_BRANCH = 4
CHUNK = 128
CONV_WIDTH = 31
WIN_ROWS = 8
WIN_COLS = 16
ROPE_BASE = 10000.0
N_EXPERTS = 16
EC_CAPACITY = 2
EPS = 1e-6
NEG_INF = -1e30
A_END = 2 * BRANCH_WIDTH
B_END = A_END + BRANCH_WIDTH
C_END = B_END + 2 * BRANCH_WIDTH
Q_END = C_END + BRANCH_WIDTH
K_END = Q_END + BRANCH_WIDTH
V_END = K_END + BRANCH_WIDTH
IN_COLS = V_END + N_BRANCH * D_MODEL
IN_WIDTHS = (A_END, B_END - A_END, C_END - B_END, V_END - C_END, IN_COLS - V_END)
IN_DTYPES = (jnp.float32, jnp.float32, jnp.float32, jnp.bfloat16, jnp.bfloat16)
GATE_HALF = 0.5

VMEM_LIMIT_BYTES = 56 * 1024 * 1024
F32 = jnp.float32
BF16 = jnp.bfloat16


PROJ_TM = 512
MIXER_TM = 1024


def _row_tile(n, target):
    tm = min(n, target)
    assert n % tm == 0
    return tm


def _sigmoid(v):
    return 0.5 * jnp.tanh(0.5 * v) + 0.5


def _silu(v):
    return v * _sigmoid(v)


def _params(*sem):
    return pltpu.CompilerParams(dimension_semantics=sem, vmem_limit_bytes=VMEM_LIMIT_BYTES)


MOD_ROWS = 8
MOD_TN = 1536


def _split_bf16(v):
    hi = v.astype(BF16)
    return hi, (v - hi.astype(F32)).astype(BF16)


def _dot3(a_hi, a_lo, b_hi, b_lo):
    return (jnp.dot(a_hi, b_hi, preferred_element_type=F32) + jnp.dot(a_hi, b_lo, preferred_element_type=F32)
            + jnp.dot(a_lo, b_hi, preferred_element_type=F32))


def _mod_kernel(c_ref, w_ref, b_ref, o_ref):
    s_hi, s_lo = _split_bf16(_silu(c_ref[...]))
    w_hi, w_lo = _split_bf16(w_ref[0])
    o_ref[0] = _dot3(s_hi, s_lo, w_hi, w_lo) + b_ref[0]


def modulation(cond, w_mod, b_mod):
    n_layers, d, cols = w_mod.shape
    return pl.pallas_call(
        _mod_kernel,
        grid=(n_layers, cols // MOD_TN),
        in_specs=[pl.BlockSpec((MOD_ROWS, d), lambda l, j: (0, 0)),
                  pl.BlockSpec((1, d, MOD_TN), lambda l, j: (l, 0, j)),
                  pl.BlockSpec((1, 1, MOD_TN), lambda l, j: (l, 0, j))],
        out_specs=pl.BlockSpec((1, MOD_ROWS, MOD_TN), lambda l, j: (l, 0, j)),
        out_shape=jax.ShapeDtypeStruct((n_layers, MOD_ROWS, cols), F32),
        compiler_params=_params("arbitrary", "arbitrary"),
        name="modulation",
    )(cond, w_mod, b_mod[:, None, :])


INPROJ_COL_CHUNK = 512


def _norm_inproj_kernel(x_ref, g_ref, sc_ref, sh_ref, w_ref, *o_refs):
    x = x_ref[0]
    y = x * lax.rsqrt(jnp.mean(x * x, axis=-1, keepdims=True) + EPS) * g_ref[...]
    h = (y * (1.0 + sc_ref[0]) + sh_ref[0]).astype(BF16)
    off = 0
    for o_ref in o_refs:
        width = o_ref.shape[2]
        for c0 in range(0, width, INPROJ_COL_CHUNK):
            cw = min(INPROJ_COL_CHUNK, width - c0)
            o_ref[0, :, c0:c0 + cw] = jnp.dot(h, w_ref[:, off + c0:off + c0 + cw],
                                              preferred_element_type=F32).astype(o_ref.dtype)
        off += width


def norm_inproj(x, g, sc, sh, w, l, col0, widths, dtypes, *, tm):
    bsz, n, d = x.shape
    cols = sum(widths)
    assert col0 % cols == 0
    return pl.pallas_call(
        _norm_inproj_kernel,
        grid=(bsz, n // tm),
        in_specs=[
            pl.BlockSpec((1, tm, d), lambda b, i: (b, i, 0)),
            pl.BlockSpec((1, d), lambda b, i: (0, 0)),
            pl.BlockSpec((1, 1, d), lambda b, i: (b, 0, 0)),
            pl.BlockSpec((1, 1, d), lambda b, i: (b, 0, 0)),
            pl.BlockSpec((None, d, cols), lambda b, i: (l, 0, col0 // cols), pipeline_mode=pl.Buffered(1)),
        ],
        out_specs=[pl.BlockSpec((1, tm, wd), lambda b, i: (b, i, 0)) for wd in widths],
        out_shape=[jax.ShapeDtypeStruct((bsz, n, wd), dt) for wd, dt in zip(widths, dtypes, strict=True)],
        compiler_params=_params("arbitrary", "arbitrary"),
        name="norm_inproj",
    )(x, g, sc, sh, w)


def _merge_kernel(a_ref, b_ref, c_ref, d_ref, ga_ref, gb_ref, gc_ref, gd_ref,
                  wa_ref, wb_ref, wc_ref, wd_ref, wo_ref, x_ref, gt_ref,
                  g2_ref, sc2_ref, sh2_ref, rt_ref, o_ref, h_ref, aff_ref):
    m = None
    for br, gz, w in ((a_ref, ga_ref, wa_ref), (b_ref, gb_ref, wb_ref),
                      (c_ref, gc_ref, wc_ref), (d_ref, gd_ref, wd_ref)):
        hp = jnp.dot(br[0].astype(BF16), w[...], preferred_element_type=F32)
        t = hp * jnp.tanh(gz[0].astype(F32)) + hp
        m = t if m is None else m + t
    mix = jnp.dot(m.astype(BF16), wo_ref[...], preferred_element_type=F32)
    x = x_ref[0] + gt_ref[0] * mix
    o_ref[0] = x
    y = x * lax.rsqrt(jnp.mean(x * x, axis=-1, keepdims=True) + EPS) * g2_ref[...]
    h = y * (1.0 + sc2_ref[0]) + sh2_ref[0]
    h_hi, h_lo = _split_bf16(h)
    r_hi, r_lo = _split_bf16(rt_ref[...])
    ne = rt_ref.shape[0]
    both = lax.dot_general(jnp.concatenate([r_hi, r_lo], axis=0), h_hi, _NT, preferred_element_type=F32)
    logits = both[:ne] + both[ne:] + lax.dot_general(r_hi, h_lo, _NT, preferred_element_type=F32)
    pr = jnp.exp(logits - jnp.max(logits, axis=0, keepdims=True))
    aff_ref[0] = pr / jnp.sum(pr, axis=0, keepdims=True)
    h_ref[0] = h_hi


def merge_branches(a, b, cc, d, z, w_a, w_b, w_c, w_d, w_o, x, gt, g2, sc2, sh2, router_t, *, tm):
    bsz, n, dm = x.shape
    w = a.shape[-1]
    ne = router_t.shape[0]
    br_spec = pl.BlockSpec((1, tm, w), lambda bi, i: (bi, i, 0))
    gate_specs = [pl.BlockSpec((1, tm, dm), functools.partial(lambda bi, i, k: (bi, i, k), k=k))
                  for k in range(N_BRANCH)]
    wbr_spec = pl.BlockSpec((w, dm), lambda bi, i: (0, 0))
    row_spec = pl.BlockSpec((1, tm, dm), lambda bi, i: (bi, i, 0))
    mod_spec = pl.BlockSpec((1, 1, dm), lambda bi, i: (bi, 0, 0))
    return pl.pallas_call(
        _merge_kernel,
        grid=(bsz, n // tm),
        in_specs=[br_spec] * 4 + gate_specs + [wbr_spec] * 4 + [
            pl.BlockSpec((dm, dm), lambda bi, i: (0, 0)), row_spec, mod_spec,
            pl.BlockSpec((1, dm), lambda bi, i: (0, 0)), mod_spec, mod_spec,
            pl.BlockSpec((ne, dm), lambda bi, i: (0, 0)),
        ],
        out_specs=[row_spec, row_spec, pl.BlockSpec((1, ne, tm), lambda bi, i: (bi, 0, i))],
        out_shape=[jax.ShapeDtypeStruct((bsz, n, dm), F32), jax.ShapeDtypeStruct((bsz, n, dm), BF16),
                   jax.ShapeDtypeStruct((bsz, ne, n), F32)],
        compiler_params=_params("arbitrary", "arbitrary"),
        name="merge_branches",
    )(a, b, cc, d, z, z, z, z, w_a, w_b, w_c, w_d, w_o, x, gt, g2, sc2, sh2, router_t)


ROUTE_TB = 256
ROUTE_WIN = 64
ROUTE_ALIGN = 16
GATHER_SUB = 8
COMBINE_SUB = 2
OFF_LANES = 128
TOPK_EXP_STEPS = (64, 32, 16, 8, 4, 2, 1)
TOPK_BISECT_STEPS = 32


def _topk_kernel(aff_ref, slot_ref, off_ref, *, cap):
    a = aff_ref[0]
    ne, n = a.shape
    capf = jnp.float32(cap)

    def count_ge(t):
        return jnp.sum(jnp.where(a >= t, 1.0, 0.0), axis=1, keepdims=True)

    hi = jnp.full((ne, 1), 2.0, F32)
    for s in TOPK_EXP_STEPS:
        cand = hi * (2.0 ** -s)
        hi = jnp.where(count_ge(cand) < capf, cand, hi)
    lo = jnp.where(hi <= 2.0 ** -126, 0.0, hi * 0.5)

    def bisect(_, carry):
        lo, hi = carry
        mid = 0.5 * (lo + hi)
        ok = count_ge(mid) >= capf
        return jnp.where(ok, mid, lo), jnp.where(ok, hi, mid)

    lo, hi = lax.fori_loop(0, TOPK_BISECT_STEPS, bisect, (lo, hi))
    need = capf - count_ge(hi)
    r_i = lax.broadcasted_iota(jnp.int32, (ROUTE_TB, ROUTE_TB), 0)
    c_i = lax.broadcasted_iota(jnp.int32, (ROUTE_TB, ROUTE_TB), 1)
    tri = jnp.where(r_i < c_i, 1.0, 0.0).astype(BF16)
    lane = lax.broadcasted_iota(jnp.int32, (ne, OFF_LANES), 1)
    run_eq = jnp.zeros((ne, 1), F32)
    run_sel = jnp.zeros((ne, 1), F32)
    offs = jnp.zeros((ne, OFF_LANES), F32)
    for c in range(n // ROUTE_TB):
        cols = slice(c * ROUTE_TB, (c + 1) * ROUTE_TB)
        a_c = a[:, cols]
        above = a_c >= hi
        eq_c = jnp.where(above, 0.0, jnp.where(a_c >= lo, 1.0, 0.0))
        rank = jnp.dot(eq_c.astype(BF16), tri, preferred_element_type=F32) + run_eq
        sel = jnp.where(above, 1.0, jnp.where(rank < need, eq_c, 0.0))
        pos = jnp.dot(sel.astype(BF16), tri, preferred_element_type=F32) + run_sel
        slot_ref[0, :, cols] = jnp.where(sel > 0.0, pos, -1.0).astype(jnp.int32)
        offs = jnp.where(lane == c, run_sel, offs)
        run_eq = run_eq + jnp.sum(eq_c, axis=1, keepdims=True)
        run_sel = run_sel + jnp.sum(sel, axis=1, keepdims=True)
    offs = jnp.where(lane == n // ROUTE_TB, run_sel, offs)
    off_ref[0] = offs.astype(jnp.int32)


def expert_topk(aff_t, cap):
    bsz, ne, n = aff_t.shape
    return pl.pallas_call(
        functools.partial(_topk_kernel, cap=cap),
        grid=(bsz,),
        in_specs=[pl.BlockSpec((1, ne, n), lambda b: (b, 0, 0))],
        out_specs=[pl.BlockSpec((1, ne, n), lambda b: (b, 0, 0)),
                   pl.BlockSpec((1, ne, OFF_LANES), lambda b: (b, 0, 0))],
        out_shape=[jax.ShapeDtypeStruct((bsz, ne, n), jnp.int32),
                   jax.ShapeDtypeStruct((bsz, ne, OFF_LANES), jnp.int32)],
        compiler_params=_params("arbitrary"),
        name="expert_topk",
    )(aff_t)


def _window_plan(off_ref, b, e, tb, w, rows_max):
    start = off_ref[b, e, tb]
    stop = off_ref[b, e, tb + 1]
    first = (start // ROUTE_ALIGN) * ROUTE_ALIGN + w * ROUTE_WIN
    active = first < stop
    return pl.multiple_of(jnp.minimum(first, rows_max), ROUTE_ALIGN), first, active


def _num_passes(off_ref, b, e_list, tb):
    n_pass = jnp.int32(0)
    for e in e_list:
        start = off_ref[b, e, tb]
        stop = off_ref[b, e, tb + 1]
        base = (start // ROUTE_ALIGN) * ROUTE_ALIGN
        n_pass = jnp.maximum(n_pass, (stop - base + ROUTE_WIN - 1) // ROUTE_WIN)
    return n_pass


def _gather_kernel(off_ref, h_ref, slot_ref, aff_ref, xe_ref, gate_ref, *, group):
    b, g, step = pl.program_id(0), pl.program_id(1), pl.program_id(2)
    rows_max = xe_ref.shape[2] - ROUTE_WIN
    n_sub = h_ref.shape[1] // ROUTE_TB

    @pl.when(step == 0)
    def _():
        xe_ref[...] = jnp.zeros(xe_ref.shape, xe_ref.dtype)
        gate_ref[...] = jnp.zeros(gate_ref.shape, gate_ref.dtype)

    sub = lax.broadcasted_iota(jnp.int32, (ROUTE_WIN, 1), 0)
    experts = [g * group + e for e in range(group)]

    for j in range(n_sub):
        tb = step * n_sub + j
        tok = slice(j * ROUTE_TB, (j + 1) * ROUTE_TB)

        def one_pass(w, carry, tb=tb, tok=tok):
            plans = [_window_plan(off_ref, b, ge, tb, w, rows_max) for ge in experts]
            hots = []
            for e, (rows, first, active) in enumerate(plans):
                ids = rows + sub
                want = jnp.where(active & (ids >= first), ids, -2)
                hots.append(slot_ref[0, e:e + 1, tok] == want)
            onehot = jnp.concatenate([jnp.where(hm, 1.0, 0.0) for hm in hots], axis=0).astype(BF16)
            res = jnp.dot(onehot, h_ref[0, tok, :], preferred_element_type=F32)
            for e, (rows, _, _) in enumerate(plans):
                win = pl.ds(rows, ROUTE_WIN)
                xe_ref[0, e, win, :] = xe_ref[0, e, win, :] + res[e * ROUTE_WIN:(e + 1) * ROUTE_WIN].astype(BF16)
                gsel = jnp.sum(jnp.where(hots[e], aff_ref[0, e:e + 1, tok], 0.0), axis=1, keepdims=True)
                gate_ref[0, e, win, :] = gate_ref[0, e, win, :] + gsel
            return carry

        lax.fori_loop(0, _num_passes(off_ref, b, experts, tb), one_pass, 0)


def expert_gather(h, slot, aff_t, off, cap_pad, *, group=8):
    bsz, n, d = h.shape
    ne = slot.shape[1]
    tok = ROUTE_TB * min(GATHER_SUB, n // ROUTE_TB)
    grid_spec = pltpu.PrefetchScalarGridSpec(
        num_scalar_prefetch=1,
        grid=(bsz, ne // group, n // tok),
        in_specs=[
            pl.BlockSpec((1, tok, d), lambda b, g, t, off: (b, t, 0)),
            pl.BlockSpec((1, group, tok), lambda b, g, t, off: (b, g, t)),
            pl.BlockSpec((1, group, tok), lambda b, g, t, off: (b, g, t)),
        ],
        out_specs=[pl.BlockSpec((1, group, cap_pad, d), lambda b, g, t, off: (b, g, 0, 0)),
                   pl.BlockSpec((1, group, cap_pad, 1), lambda b, g, t, off: (b, g, 0, 0))],
    )
    return pl.pallas_call(
        functools.partial(_gather_kernel, group=group),
        grid_spec=grid_spec,
        out_shape=[jax.ShapeDtypeStruct((bsz, ne, cap_pad, d), BF16),
                   jax.ShapeDtypeStruct((bsz, ne, cap_pad, 1), F32)],
        compiler_params=_params("arbitrary", "arbitrary", "arbitrary"),
        name="expert_gather",
    )(off, h, slot, aff_t)


def _combine_kernel(off_ref, y_ref, slot_t_ref, x_ref, gt_ref, fg_ref, o_ref, stage_ref, acc_ref):
    b, step = pl.program_id(0), pl.program_id(1)
    ne = y_ref.shape[1]
    rows_max = y_ref.shape[2] - ROUTE_WIN
    width = ne * ROUTE_WIN
    n_sub = x_ref.shape[1] // ROUTE_TB
    experts = list(range(ne))
    lane = lax.broadcasted_iota(jnp.int32, (1, width), 1)
    e_row = lax.broadcasted_iota(jnp.int32, (ne, width), 0)
    e_lane = lax.broadcasted_iota(jnp.int32, (ne, width), 1) // ROUTE_WIN
    expand = jnp.where(e_row == e_lane, 1.0, 0.0).astype(BF16)

    for j in range(n_sub):
        tb = step * n_sub + j
        tok = slice(j * ROUTE_TB, (j + 1) * ROUTE_TB)
        s1 = slot_t_ref[0, tok, :] + 1
        hi = (s1 // ROUTE_WIN).astype(F32).astype(BF16)
        lo = (s1 % ROUTE_WIN).astype(F32).astype(BF16)
        s1_wide = (float(ROUTE_WIN) * jnp.dot(hi, expand, preferred_element_type=F32)
                   + jnp.dot(lo, expand, preferred_element_type=F32))
        acc_ref[...] = jnp.zeros(acc_ref.shape, F32)

        def one_pass(w, carry, tb=tb, s1_wide=s1_wide):
            want = jnp.full((1, width), -1, jnp.int32)
            for e in experts:
                rows, first, active = _window_plan(off_ref, b, e, tb, w, rows_max)
                stage_ref[e * ROUTE_WIN:(e + 1) * ROUTE_WIN, :] = y_ref[0, e, pl.ds(rows, ROUTE_WIN), :]
                ids = rows + lane % ROUTE_WIN
                want = jnp.where((lane // ROUTE_WIN == e) & active & (ids >= first), ids + 1, want)
            onehot = jnp.where(s1_wide == want.astype(F32), 1.0, 0.0).astype(BF16)
            acc_ref[...] += jnp.dot(onehot, stage_ref[...], preferred_element_type=F32)
            return carry

        lax.fori_loop(0, _num_passes(off_ref, b, experts, tb), one_pass, 0)
        out = x_ref[0, tok, :] + gt_ref[0] * acc_ref[...]
        if fg_ref is not None:
            out = out * lax.rsqrt(jnp.mean(out * out, axis=-1, keepdims=True) + EPS) * fg_ref[...]
        o_ref[0, tok, :] = out


def _combine_kernel_plain(off_ref, y_ref, slot_t_ref, x_ref, gt_ref, o_ref, stage_ref, acc_ref):
    _combine_kernel(off_ref, y_ref, slot_t_ref, x_ref, gt_ref, None, o_ref, stage_ref, acc_ref)


def expert_combine(y, slot_t, off, x, gt, final_g=None):
    bsz, n, d = x.shape
    ne, cap_pad = y.shape[1], y.shape[2]
    tok = ROUTE_TB * min(COMBINE_SUB, n // ROUTE_TB)
    grid_spec = pltpu.PrefetchScalarGridSpec(
        num_scalar_prefetch=1,
        grid=(bsz, n // tok),
        in_specs=[
            pl.BlockSpec((1, ne, cap_pad, d), lambda b, t, off: (b, 0, 0, 0), pipeline_mode=pl.Buffered(1)),
            pl.BlockSpec((1, tok, ne), lambda b, t, off: (b, t, 0)),
            pl.BlockSpec((1, tok, d), lambda b, t, off: (b, t, 0)),
            pl.BlockSpec((1, 1, d), lambda b, t, off: (b, 0, 0)),
        ] + ([] if final_g is None else [pl.BlockSpec((1, d), lambda b, t, off: (0, 0))]),
        out_specs=pl.BlockSpec((1, tok, d), lambda b, t, off: (b, t, 0)),
        scratch_shapes=[pltpu.VMEM((ne * ROUTE_WIN, d), BF16), pltpu.VMEM((ROUTE_TB, d), F32)],
    )
    args = (off, y, slot_t, x, gt) + (() if final_g is None else (final_g[None, :],))
    return pl.pallas_call(
        _combine_kernel_plain if final_g is None else _combine_kernel,
        grid_spec=grid_spec,
        out_shape=jax.ShapeDtypeStruct((bsz, n, d), F32),
        compiler_params=_params("arbitrary", "arbitrary"),
        name="expert_combine",
    )(*args)


FFN_FBLK = 512
FFN_MIN_ROWS = 256


def _expert_ffn_kernel(*refs, n_sets, n_f):
    xe_refs, gate_refs = refs[:n_sets], refs[n_sets:2 * n_sets]
    wg_ref, wu_ref, wd_ref = refs[2 * n_sets:2 * n_sets + 3]
    o_refs = refs[2 * n_sets + 3:3 * n_sets + 3]
    acc_refs = refs[3 * n_sets + 3:]
    f = pl.program_id(1)
    blocks = [(s, b) for s in range(n_sets) for b in range(xe_refs[s].shape[0])]
    big = [sb for sb in blocks if xe_refs[sb[0]].shape[2] >= FFN_MIN_ROWS]
    small = [sb for sb in blocks if sb not in big]
    groups = [[sb] for sb in big[:-1]] + [big[-1:] + small] if big else [small]

    def hidden_block(first, last):
        wg = wg_ref[0].astype(BF16)
        wu = wu_ref[0].astype(BF16)
        wd = wd_ref[0].astype(BF16)
        for group in groups:
            xs = [xe_refs[s][b, 0] for s, b in group]
            xe = xs[0] if len(xs) == 1 else jnp.concatenate(xs, axis=0)
            a = jnp.dot(xe, wg, preferred_element_type=F32)
            u = jnp.dot(xe, wu, preferred_element_type=F32)
            t = jnp.dot((_silu(a) * u).astype(BF16), wd, preferred_element_type=F32)
            r0 = 0
            for s, b in group:
                rows = xe_refs[s].shape[2]
                part = t[r0:r0 + rows] if first else acc_refs[s][b] + t[r0:r0 + rows]
                if last:
                    o_refs[s][b, 0] = (part * gate_refs[s][b, 0]).astype(o_refs[s].dtype)
                else:
                    acc_refs[s][b] = part
                r0 += rows

    if n_f == 1:
        hidden_block(True, True)
        return
    pl.when(f == 0)(lambda: hidden_block(True, False))
    if n_f > 2:
        pl.when((f > 0) & (f < n_f - 1))(lambda: hidden_block(False, False))
    pl.when(f == n_f - 1)(lambda: hidden_block(False, True))


def expert_ffn(xes, gates, wg, wu, wd, l):
    n_sets = len(xes)
    _, ne, d, ff = wg.shape
    xe_specs = [pl.BlockSpec((xe.shape[0], 1) + xe.shape[2:], lambda e, f: (0, e, 0, 0)) for xe in xes]
    gate_specs = [pl.BlockSpec((g.shape[0], 1) + g.shape[2:], lambda e, f: (0, e, 0, 0)) for g in gates]
    return pl.pallas_call(
        functools.partial(_expert_ffn_kernel, n_sets=n_sets, n_f=ff // FFN_FBLK),
        grid=(ne, ff // FFN_FBLK),
        in_specs=xe_specs + gate_specs + [
            pl.BlockSpec((None, 1, d, FFN_FBLK), lambda e, f: (l, e, 0, f)),
            pl.BlockSpec((None, 1, d, FFN_FBLK), lambda e, f: (l, e, 0, f)),
            pl.BlockSpec((None, 1, FFN_FBLK, d), lambda e, f: (l, e, f, 0)),
        ],
        out_specs=xe_specs,
        out_shape=[jax.ShapeDtypeStruct(xe.shape, BF16) for xe in xes],
        scratch_shapes=[pltpu.VMEM((xe.shape[0],) + xe.shape[2:], F32) for xe in xes],
        compiler_params=_params("arbitrary", "arbitrary"),
        name="expert_ffn",
    )(*xes, *gates, wg, wu, wd)


ATT_QROWS = 4
ATT_QTOK = ATT_QROWS * GRID_W
ATT_KROWS = ATT_QROWS + WIN_ROWS
ATT_KBLK = ATT_KROWS * GRID_W // ATT_QTOK
_NT = (((1,), (1,)), ((), ()))

def _rope_tables(n):
    quarter = HEAD_DIM // 4
    inv = ROPE_BASE ** (-np.arange(quarter, dtype=np.float64) / quarter)
    t = np.arange(n)
    ang_r = (t // GRID_W)[:, None] * inv[None, :]
    ang_c = (t % GRID_W)[:, None] * inv[None, :]
    cos = np.concatenate([np.cos(ang_r)] * 2 + [np.cos(ang_c)] * 2, axis=1)
    sin = np.concatenate([-np.sin(ang_r), np.sin(ang_r), -np.sin(ang_c), np.sin(ang_c)], axis=1)
    return (jnp.asarray(np.concatenate([cos, cos], axis=1), F32),
            jnp.asarray(np.concatenate([sin, sin], axis=1), F32))


def _attn_bias_tables(rpb, n_rows):
    n_blk = n_rows // ATT_QROWS
    n_dr = 2 * WIN_ROWS - 1
    qc = np.arange(GRID_W)
    kc = np.arange(GRID_W)
    c_start = np.clip(qc - WIN_COLS // 2, 0, GRID_W - WIN_COLS)
    col_ok = (kc[None, :] >= c_start[:, None]) & (kc[None, :] < c_start[:, None] + WIN_COLS)
    assert np.all(np.abs(kc[None, :] - qc[:, None])[col_ok] < WIN_COLS)
    row_sel, oks = [], []
    for j in (0, 1, n_blk - 1):
        ks = min(max(ATT_QROWS * j - WIN_ROWS // 2, 0), n_rows - ATT_KROWS)
        r = ATT_QROWS * j + np.arange(ATT_QROWS)
        kr0 = np.clip(r - WIN_ROWS // 2, 0, n_rows - WIN_ROWS)
        krow = ks + np.arange(ATT_KROWS)
        row_ok = (krow[None, :] >= kr0[:, None]) & (krow[None, :] < kr0[:, None] + WIN_ROWS)
        d_row = np.clip(krow[None, :] - r[:, None] + (WIN_ROWS - 1), 0, n_dr - 1)
        row_sel.append((d_row.reshape(-1)[:, None] == np.arange(n_dr)[None, :]).astype(np.float32))
        oks.append(row_ok[:, None, :, None] & col_ok[None, :, None, :])
    a = jnp.einsum('cpr,hrd->chpd', jnp.asarray(np.stack(row_sel)), rpb.astype(F32), precision=lax.Precision.HIGHEST)
    a = a.reshape(3, N_GROUPS, ATT_QROWS, ATT_KROWS, 2 * WIN_COLS - 1)
    pad = jnp.zeros(a.shape[:-1] + (GRID_W - (2 * WIN_COLS - 1),), F32)
    v = jnp.concatenate([a[..., WIN_COLS - 1:], pad, jnp.roll(a, -1, axis=3)[..., :WIN_COLS - 1]], axis=-1)
    v = v.reshape(3, N_GROUPS, ATT_QROWS, ATT_KROWS * GRID_W)
    ok = jnp.asarray(np.stack(oks).reshape(3, ATT_QTOK, ATT_KROWS * GRID_W).astype(np.float32))
    return pl.pallas_call(
        _attn_bias_kernel,
        grid=(3, N_GROUPS),
        in_specs=[pl.BlockSpec((1, 1, ATT_QROWS, ATT_KROWS * GRID_W), lambda c, h: (c, h, 0, 0)),
                  pl.BlockSpec((1, ATT_QTOK, ATT_KROWS * GRID_W), lambda c, h: (c, 0, 0))],
        out_specs=pl.BlockSpec((1, 1, ATT_QTOK, ATT_KROWS * GRID_W), lambda c, h: (c, h, 0, 0)),
        out_shape=jax.ShapeDtypeStruct((3, N_GROUPS, ATT_QTOK, ATT_KROWS * GRID_W), F32),
        compiler_params=_params("arbitrary", "arbitrary"),
        name="attn_bias",
    )(v, ok)


def _attn_bias_kernel(v_ref, ok_ref, o_ref):
    width = v_ref.shape[3]
    for qr in range(ATT_QROWS):
        rows = slice(qr * GRID_W, (qr + 1) * GRID_W)
        base = jnp.broadcast_to(v_ref[0, 0, qr:qr + 1, :], (GRID_W, width))
        slab = pltpu.roll(base, 0, 1, stride=1, stride_axis=0)
        o_ref[0, 0, rows, :] = jnp.where(ok_ref[0, rows, :] > 0.0, slab, NEG_INF)


def _rope(x, cos_ref, sin_ref, first16):
    c = cos_ref[...]
    s = sin_ref[...]
    c2 = jnp.concatenate([c, c], axis=1)
    s2 = jnp.concatenate([s, s], axis=1)
    w = x.shape[1]
    partner = jnp.where(first16, pltpu.roll(x, w - HEAD_DIM // 4, 1), pltpu.roll(x, HEAD_DIM // 4, 1))
    return x * c2 + partner * s2


ATT_HEAD_STACK = 1


def _stack_heads(x, lane, heads):
    return jnp.concatenate([jnp.where((lane // HEAD_DIM) == h, x, 0.0) for h in heads], axis=0).astype(BF16)


def _softmax_pv(scores_values, lane, heads):
    m = None
    for s, _ in scores_values:
        mx = jnp.max(s, axis=1, keepdims=True)
        m = mx if m is None else jnp.maximum(m, mx)
    den = None
    o = None
    for s, v in scores_values:
        p = jnp.exp(s - m)
        sm = jnp.sum(p, axis=1, keepdims=True)
        den = sm if den is None else den + sm
        t = jnp.dot(p.astype(BF16), v, preferred_element_type=F32)
        o = t if o is None else o + t
    o = o * (1.0 / den)
    rows = o.shape[0] // len(heads)
    out = None
    for i, h in enumerate(heads):
        t = jnp.where((lane // HEAD_DIM) == h, o[i * rows:(i + 1) * rows], 0.0)
        out = t if out is None else out + t
    return out


def _head_groups():
    return [tuple(range(h, h + ATT_HEAD_STACK)) for h in range(0, N_GROUPS, ATT_HEAD_STACK)]


ATT_QSTEP = 4
ATT_SLAB = ATT_KBLK + ATT_QSTEP - 1


def _att_key_block(jq, n_blk):
    return jnp.clip(jq - 1, 0, n_blk - ATT_KBLK)


def _nattn_kernel(*refs, n_blk):
    q_ref = refs[0]
    k_refs, v_refs = refs[1:1 + ATT_SLAB], refs[1 + ATT_SLAB:1 + 2 * ATT_SLAB]
    cq_ref, sq_ref = refs[1 + 2 * ATT_SLAB:3 + 2 * ATT_SLAB]
    ck_refs = refs[3 + 2 * ATT_SLAB:3 + 3 * ATT_SLAB]
    sk_refs = refs[3 + 3 * ATT_SLAB:3 + 4 * ATT_SLAB]
    kc_ref, vc_ref = refs[3 + 4 * ATT_SLAB:5 + 4 * ATT_SLAB]
    bias_refs = refs[5 + 4 * ATT_SLAB:5 + 4 * ATT_SLAB + ATT_QSTEP]
    o_ref, k_scr, v_scr = refs[5 + 4 * ATT_SLAB + ATT_QSTEP:]
    j = pl.program_id(1)
    w = q_ref.shape[2]
    lane = lax.broadcasted_iota(jnp.int32, (1, w), 1)
    first16 = (lane % (HEAD_DIM // 2)) < (HEAD_DIM // 4)
    q_all = q_ref[0].astype(F32) * (HEAD_DIM ** -0.5)
    q_rot_all = _rope(q_all, cq_ref, sq_ref, first16)
    for i in range(ATT_SLAB):
        rows = slice(i * ATT_QTOK, (i + 1) * ATT_QTOK)
        k_scr[rows, :] = _rope(k_refs[i][0].astype(F32), ck_refs[i], sk_refs[i], first16).astype(BF16)
        v_scr[rows, :] = v_refs[i][0].astype(BF16)
    kc = kc_ref[0].astype(BF16)
    vc = vc_ref[0].astype(BF16)
    slab0 = _att_key_block(j * ATT_QSTEP, n_blk - ATT_QSTEP + 1)
    for i in range(ATT_QSTEP):
        rows = slice(i * ATT_QTOK, (i + 1) * ATT_QTOK)
        q, q_rot = q_all[rows], q_rot_all[rows]
        start = pl.multiple_of((_att_key_block(j * ATT_QSTEP + i, n_blk) - slab0) * ATT_QTOK, ATT_QTOK)
        keys = pl.ds(start, ATT_KBLK * ATT_QTOK)
        k_rot, v = k_scr[keys, :], v_scr[keys, :]
        acc = None
        for heads in _head_groups():
            bias = jnp.concatenate([bias_refs[i][0, h] for h in heads], axis=0)
            s_win = lax.dot_general(_stack_heads(q_rot, lane, heads), k_rot, _NT, preferred_element_type=F32) + bias
            s_ctx = lax.dot_general(_stack_heads(q, lane, heads), kc, _NT, preferred_element_type=F32)
            t = _softmax_pv([(s_win, v), (s_ctx, vc)], lane, heads)
            acc = t if acc is None else acc + t
        o_ref[0, rows, :] = acc.astype(o_ref.dtype)


def neighbourhood_attention(z, zc, rpb, *, q_col, k_col, v_col, kc_col, vc_col):
    bsz, n, _ = z.shape
    n_ctx = zc.shape[1]
    w = BRANCH_WIDTH
    n_blk = n // ATT_QTOK
    assert n_blk % ATT_QSTEP == 0 and n_blk >= ATT_SLAB
    cos_t, sin_t = _rope_tables(n)
    bias = _attn_bias_tables(rpb, n // GRID_W)
    q_tok = ATT_QSTEP * ATT_QTOK

    def slab(j):
        return _att_key_block(j * ATT_QSTEP, n_blk - ATT_QSTEP + 1)

    def kspec(col, off):
        return pl.BlockSpec((1, ATT_QTOK, w), lambda b, j: (b, slab(j) + off, col))

    def tspec(off):
        return pl.BlockSpec((ATT_QTOK, 2 * HEAD_DIM), lambda b, j: (slab(j) + off, 0))

    def bias_spec(i):
        def cfg(b, j):
            jq = j * ATT_QSTEP + i
            return (jnp.minimum(jq, 1) + jq // (n_blk - 1), 0, 0, 0)
        return pl.BlockSpec((1, N_GROUPS, ATT_QTOK, ATT_KROWS * GRID_W), cfg)

    qt_spec = pl.BlockSpec((q_tok, 2 * HEAD_DIM), lambda b, j: (j, 0))
    in_specs = ([pl.BlockSpec((1, q_tok, w), lambda b, j: (b, j, q_col))]
                + [kspec(k_col, i) for i in range(ATT_SLAB)] + [kspec(v_col, i) for i in range(ATT_SLAB)]
                + [qt_spec, qt_spec] + [tspec(i) for i in range(ATT_SLAB)] * 2
                + [pl.BlockSpec((1, n_ctx, w), lambda b, j: (b, 0, kc_col)),
                   pl.BlockSpec((1, n_ctx, w), lambda b, j: (b, 0, vc_col))]
                + [bias_spec(i) for i in range(ATT_QSTEP)])
    args = ([z] * (1 + 2 * ATT_SLAB) + [cos_t, sin_t] + [cos_t] * ATT_SLAB + [sin_t] * ATT_SLAB + [zc, zc]
            + [bias] * ATT_QSTEP)
    return pl.pallas_call(
        functools.partial(_nattn_kernel, n_blk=n_blk),
        grid=(bsz, n_blk // ATT_QSTEP),
        in_specs=in_specs,
        out_specs=pl.BlockSpec((1, q_tok, w), lambda b, j: (b, j, 0)),
        out_shape=jax.ShapeDtypeStruct((bsz, n, w), BF16),
        scratch_shapes=[pltpu.VMEM((ATT_SLAB * ATT_QTOK, w), BF16), pltpu.VMEM((ATT_SLAB * ATT_QTOK, w), BF16)],
        compiler_params=_params("arbitrary", "arbitrary"),
        name="neighbourhood_attention",
    )(*args)


def _ctx_attn_kernel(q_ref, kc_ref, vc_ref, o_ref):
    w = q_ref.shape[2]
    lane = lax.broadcasted_iota(jnp.int32, (1, w), 1)
    q = q_ref[0].astype(F32) * (HEAD_DIM ** -0.5)
    kc = kc_ref[0].astype(BF16)
    vc = vc_ref[0].astype(BF16)
    acc = None
    for heads in _head_groups():
        s = lax.dot_general(_stack_heads(q, lane, heads), kc, _NT, preferred_element_type=F32)
        t = _softmax_pv([(s, vc)], lane, heads)
        acc = t if acc is None else acc + t
    o_ref[0] = acc.astype(o_ref.dtype)


def context_attention(zc, *, q_col, k_col, v_col):
    bsz, n_ctx, _ = zc.shape
    w = BRANCH_WIDTH
    return pl.pallas_call(
        _ctx_attn_kernel,
        grid=(bsz,),
        in_specs=[pl.BlockSpec((1, n_ctx, w), functools.partial(lambda b, c: (b, 0, c), c=c))
                  for c in (q_col, k_col, v_col)],
        out_specs=pl.BlockSpec((1, n_ctx, w), lambda b: (b, 0, 0)),
        out_shape=jax.ShapeDtypeStruct((bsz, n_ctx, w), BF16),
        compiler_params=_params("arbitrary"),
        name="context_attention",
    )(zc, zc, zc)


def _ln(v, g, b):
    mu = jnp.mean(v, axis=-1, keepdims=True)
    var = jnp.mean(jnp.square(v - mu), axis=-1, keepdims=True)
    return (v - mu) * lax.rsqrt(var + EPS) * g + b


def _gmlp_kernel(z_ref, lng_ref, lnb_ref, w_ref, bias_ref, o_ref):
    tm = z_ref.shape[1]
    z = jax.nn.gelu(z_ref[0])
    u = z[:, :BRANCH_WIDTH]
    v = _ln(z[:, BRANCH_WIDTH:], lng_ref[...], lnb_ref[...])
    group = lax.broadcasted_iota(jnp.int32, (1, BRANCH_WIDTH), 1) // HEAD_DIM
    wcat = w_ref[...]
    for c in range(tm // CHUNK):
        rows = slice(c * CHUNK, (c + 1) * CHUNK)
        vc = v[rows]
        vst = jnp.concatenate([jnp.where(group == g, vc, 0.0) for g in range(N_GROUPS)], axis=0).astype(BF16)
        mixed = jnp.dot(wcat, vst, preferred_element_type=F32) + bias_ref[...]
        o_ref[0, rows, :] = (u[rows] * mixed).astype(o_ref.dtype)


def chunk_gmlp(za, ln_g, ln_b, w_s, b_s, *, tm):
    bsz, n, w2 = za.shape
    w = w2 // 2
    wcat = jnp.transpose(w_s, (1, 0, 2)).reshape(CHUNK, N_GROUPS * CHUNK).astype(BF16)
    bias = jnp.repeat(b_s.T, HEAD_DIM, axis=1)
    return pl.pallas_call(
        _gmlp_kernel,
        grid=(bsz, n // tm),
        in_specs=[
            pl.BlockSpec((1, tm, w2), lambda b, i: (b, i, 0)),
            pl.BlockSpec((1, w), lambda b, i: (0, 0)),
            pl.BlockSpec((1, w), lambda b, i: (0, 0)),
            pl.BlockSpec((CHUNK, N_GROUPS * CHUNK), lambda b, i: (0, 0)),
            pl.BlockSpec((CHUNK, w), lambda b, i: (0, 0)),
        ],
        out_specs=pl.BlockSpec((1, tm, w), lambda b, i: (b, i, 0)),
        out_shape=jax.ShapeDtypeStruct((bsz, n, w), BF16),
        compiler_params=_params("arbitrary", "arbitrary"),
        name="chunk_gmlp",
    )(za, ln_g[None, :], ln_b[None, :], wcat, bias)


CONV_HALO = 16
CONV_SUB = 128
SUBLANES = 8
LANES = 128


def _conv_kernel(a_ref, g_ref, ap_ref, gp_ref, an_ref, gn_ref, w_ref, cb_ref, lng_ref, lnb_ref, o_ref, y_ref):
    i = pl.program_id(1)
    tm = a_ref.shape[1]

    def glu(a, g):
        return a * _sigmoid(g)

    y_ref[0:CONV_HALO, :] = jnp.where(i > 0, glu(ap_ref[0], gp_ref[0]), 0.0)
    y_ref[CONV_HALO:CONV_HALO + tm, :] = glu(a_ref[0], g_ref[0])
    y_ref[CONV_HALO + tm:, :] = jnp.where(i < pl.num_programs(1) - 1, glu(an_ref[0], gn_ref[0]), 0.0)
    first = CONV_HALO - CONV_WIDTH // 2
    nb = CONV_SUB + 2 * CONV_HALO
    for r in range(0, tm, CONV_SUB):
        blk = y_ref[r:r + nb, :]
        acc = None
        for res in range(SUBLANES):
            rot = blk if res == 0 else pltpu.roll(blk, nb - res, 0)
            for j in range(CONV_WIDTH):
                if (first + j) % SUBLANES == res:
                    a0 = first + j - res
                    t = rot[a0:a0 + CONV_SUB, :] * w_ref[j:j + 1, :]
                    acc = t if acc is None else acc + t
        y = _ln(acc + cb_ref[...], lng_ref[...], lnb_ref[...])
        o_ref[0, r:r + CONV_SUB, :] = _silu(y).astype(o_ref.dtype)


def conformer_conv(zcv, conv_w, conv_b, ln_g, ln_b, *, tm):
    bsz, n, w2 = zcv.shape
    w = w2 // 2
    hb = tm // CONV_HALO
    n_hb = n // CONV_HALO

    def main(col):
        return pl.BlockSpec((1, tm, w), lambda b, i: (b, i, col))

    def prev(col):
        return pl.BlockSpec((1, CONV_HALO, w), lambda b, i: (b, jnp.maximum(i * hb - 1, 0), col))

    def nxt(col):
        return pl.BlockSpec((1, CONV_HALO, w), lambda b, i: (b, jnp.minimum((i + 1) * hb, n_hb - 1), col))

    vec = pl.BlockSpec((1, w), lambda b, i: (0, 0))
    return pl.pallas_call(
        _conv_kernel,
        grid=(bsz, n // tm),
        in_specs=[main(0), main(1), prev(0), prev(1), nxt(0), nxt(1),
                  pl.BlockSpec((CONV_WIDTH, w), lambda b, i: (0, 0)), vec, vec, vec],
        out_specs=pl.BlockSpec((1, tm, w), lambda b, i: (b, i, 0)),
        out_shape=jax.ShapeDtypeStruct((bsz, n, w), BF16),
        scratch_shapes=[pltpu.VMEM((tm + 2 * CONV_HALO, w), F32)],
        compiler_params=_params("arbitrary", "arbitrary"),
        name="conformer_conv",
    )(zcv, zcv, zcv, zcv, zcv, zcv, conv_w, conv_b[None, :], ln_g[None, :], ln_b[None, :])


FOURIER_N2 = 128
FOURIER_UNROLL = 8
FOURIER_KB = 8


def _np_split(m):
    m = jnp.asarray(m, F32)
    hi = m.astype(BF16)
    return hi, (m - hi.astype(F32)).astype(BF16)


def _channel_dft_matrix():
    c = np.arange(HEAD_DIM)
    ang = 2.0 * np.pi * ((c[:, None] * c[None, :]) % HEAD_DIM) / HEAD_DIM
    eye = np.eye(N_GROUPS)
    return np.concatenate([np.kron(eye, np.cos(ang)), np.kron(eye, np.sin(ang))], axis=0)


def _fourier_stage1_kernel(xa_ref, xb_ref, mh_ref, ml_ref, o_ref, *, n1, n2):
    def one_fast_index(f, carry):
        rows = pl.ds(f, n1, stride=n2)
        x = jnp.concatenate([xa_ref[0, rows, :], xb_ref[0, rows, :]], axis=1)
        x_hi, x_lo = _split_bf16(x)
        res = _dot3(mh_ref[0, f], ml_ref[0, f], x_hi, x_lo)
        o_ref[0, 0, 0, rows, :] = res[:, :LANES]
        o_ref[0, 0, 1, rows, :] = res[:, LANES:]
        return carry

    lax.fori_loop(0, n2, one_fast_index, 0, unroll=FOURIER_UNROLL)


def _fourier_stage2_kernel(br_ref, bi_ref, m2h_ref, m2l_ref, mdh_ref, mdl_ref, o_ref, *, scale):
    for k in range(br_ref.shape[3]):
        b_re = jnp.concatenate([br_ref[0, 0, 0, k], br_ref[0, 0, 1, k]], axis=1)
        b_im = jnp.concatenate([bi_ref[0, 0, 0, k], bi_ref[0, 0, 1, k]], axis=1)
        b_hi, b_lo = _split_bf16(jnp.concatenate([b_re, b_im], axis=0))
        xs = _dot3(m2h_ref[...], m2l_ref[...], b_hi, b_lo)
        x_hi, x_lo = _split_bf16(jnp.concatenate([xs[:FOURIER_N2], xs[FOURIER_N2:]], axis=1))
        o_ref[0, :, k, :] = _dot3(x_hi, x_lo, mdh_ref[...], mdl_ref[...]) * scale


def _fourier_direct_kernel(x_ref, mph_ref, mpl_ref, mdh_ref, mdl_ref, o_ref, *, scale):
    n = x_ref.shape[1]
    x_hi, x_lo = _split_bf16(x_ref[0])
    p = _dot3(mph_ref[...], mpl_ref[...], x_hi, x_lo)
    p_hi, p_lo = _split_bf16(jnp.concatenate([p[:n], p[n:]], axis=1))
    o_ref[0] = (_dot3(p_hi, p_lo, mdh_ref[...], mdl_ref[...]) * scale).astype(o_ref.dtype)


def fourier_mix(zb):
    bsz, n, w = zb.shape
    scale = float(1.0 / np.sqrt(n * HEAD_DIM))
    mdh, mdl = _np_split(_channel_dft_matrix())
    md_spec2 = pl.BlockSpec((2 * w, w), lambda b, j: (0, 0))
    if n <= 2 * FOURIER_N2:
        t = np.arange(n)
        ang = 2.0 * np.pi * ((t[:, None] * t[None, :]) % n) / n
        mph, mpl = _np_split(np.concatenate([np.cos(ang), -np.sin(ang)], axis=0))
        return pl.pallas_call(
            functools.partial(_fourier_direct_kernel, scale=scale),
            grid=(bsz,),
            in_specs=[pl.BlockSpec((1, n, w), lambda b: (b, 0, 0)),
                      pl.BlockSpec((2 * n, n), lambda b: (0, 0)), pl.BlockSpec((2 * n, n), lambda b: (0, 0)),
                      pl.BlockSpec((2 * w, w), lambda b: (0, 0)), pl.BlockSpec((2 * w, w), lambda b: (0, 0))],
            out_specs=pl.BlockSpec((1, n, w), lambda b: (b, 0, 0)),
            out_shape=jax.ShapeDtypeStruct((bsz, n, w), BF16),
            compiler_params=_params("arbitrary"),
            name="fourier_direct",
        )(zb, mph, mpl, mdh, mdl)

    n1, n2 = n // FOURIER_N2, FOURIER_N2
    f, k1, s = np.arange(n2), np.arange(n1), np.arange(n1)
    ang1 = 2.0 * np.pi * ((k1[None, :, None] * (f[:, None, None] + n2 * s[None, None, :])) % n) / n
    m1h, m1l = _np_split(np.stack([np.cos(ang1), -np.sin(ang1)]))
    k2 = np.arange(n2)
    ang2 = 2.0 * np.pi * ((k2[:, None] * f[None, :]) % n2) / n2
    c2, s2 = np.cos(ang2), np.sin(ang2)
    m2h, m2l = _np_split(np.block([[c2, s2], [-s2, c2]]))

    m1_spec = pl.BlockSpec((1, n2, n1, n1), lambda b, p: (p, 0, 0, 0))
    n_half = w // LANES
    b_st = pl.pallas_call(
        functools.partial(_fourier_stage1_kernel, n1=n1, n2=n2),
        grid=(bsz, 2),
        in_specs=[pl.BlockSpec((1, n, LANES), lambda b, p: (b, 0, 0)),
                  pl.BlockSpec((1, n, LANES), lambda b, p: (b, 0, 1)), m1_spec, m1_spec],
        out_specs=pl.BlockSpec((1, 1, n_half, n, LANES), lambda b, p: (b, p, 0, 0, 0)),
        out_shape=jax.ShapeDtypeStruct((bsz, 2, n_half, n, LANES), F32),
        compiler_params=_params("arbitrary", "arbitrary"),
        name="fourier_stage1",
    )(zb, zb, m1h, m1l)

    kb = FOURIER_KB
    b_st = b_st.reshape(bsz, 2, n_half, n1, n2, LANES)
    out = pl.pallas_call(
        functools.partial(_fourier_stage2_kernel, scale=scale),
        grid=(bsz, n1 // kb),
        in_specs=[pl.BlockSpec((1, 1, n_half, kb, n2, LANES), lambda b, j: (b, 0, 0, j, 0, 0)),
                  pl.BlockSpec((1, 1, n_half, kb, n2, LANES), lambda b, j: (b, 1, 0, j, 0, 0)),
                  pl.BlockSpec((2 * n2, 2 * n2), lambda b, j: (0, 0)),
                  pl.BlockSpec((2 * n2, 2 * n2), lambda b, j: (0, 0)),
                  md_spec2, md_spec2],
        out_specs=pl.BlockSpec((1, n2, kb, w), lambda b, j: (b, 0, j, 0)),
        out_shape=jax.ShapeDtypeStruct((bsz, n2, n1, w), F32),
        compiler_params=_params("arbitrary", "arbitrary"),
        name="fourier_stage2",
    )(b_st, b_st, m2h, m2l, mdh, mdl)
    return out.reshape(bsz, n, w)


def expert_route(h, aff_t):
    n = h.shape[1]
    cap = EC_CAPACITY * n // N_EXPERTS
    slot, off = expert_topk(aff_t, cap)
    xe, gate = expert_gather(h, slot, aff_t, off, max(cap, ROUTE_WIN))
    return xe, gate, jnp.swapaxes(slot, 1, 2), off


def kernel(x, c, ctx, c_ctx, w_mod, b_mod, norm1_g, norm2_g, w_in, sgu_ln_g, sgu_ln_b, w_spatial, b_spatial,
           w_a_out, w_b_out, conv_w, conv_b, conv_ln_g, conv_ln_b, w_c_out, rpb, w_d_out, w_out, w_router,
           w_gate_e, w_up_e, w_down_e, final_norm_g):
    bsz = x.shape[0]
    xc = ctx
    assert bsz + 1 <= MOD_ROWS
    cond = jnp.concatenate([c, c_ctx[None, :], jnp.zeros((MOD_ROWS - bsz - 1, D_MODEL), F32)], axis=0)
    mod_all = modulation(cond, w_mod, b_mod)
    in_scale = jnp.concatenate([jnp.ones((V_END,), F32), jnp.full((IN_COLS - V_END,), GATE_HALF, F32)])
    w_in_b = (w_in * in_scale).astype(BF16)
    for l in range(DEPTH):
        last = l == DEPTH - 1
        sh1, sc1, gt1, sh2, sc2, gt2 = jnp.split(mod_all[l, :bsz, None, :], 6, axis=-1)
        cmod = jnp.broadcast_to(mod_all[l, bsz:bsz + 1, None, :], (bsz, 1, 6 * D_MODEL))
        csh1, csc1, cgt1, csh2, csc2, cgt2 = jnp.split(cmod, 6, axis=-1)

        wa, wb, wc, wd_ = ((w * GATE_HALF).astype(BF16) for w in (w_a_out[l], w_b_out[l], w_c_out[l], w_d_out[l]))
        wo = w_out[l].astype(BF16)
        g1 = norm1_g[l][None, :]

        tm_x, tm_c = _row_tile(x.shape[1], PROJ_TM), _row_tile(xc.shape[1], PROJ_TM)
        tmix_x, tmix_c = _row_tile(x.shape[1], MIXER_TM), _row_tile(xc.shape[1], MIXER_TM)
        za, zb, zcv, zqkv, zg = norm_inproj(x, g1, sc1, sh1, w_in_b, l, 0, IN_WIDTHS, IN_DTYPES, tm=tm_x)
        if last:
            (zqkv_c,) = norm_inproj(xc, g1, csc1, csh1, w_in_b, l, Q_END, (2 * BRANCH_WIDTH,), (BF16,), tm=tm_c)
            kc_col, vc_col = 0, 1
        else:
            cza, czb, czcv, zqkv_c, czg = norm_inproj(xc, g1, csc1, csh1, w_in_b, l, 0, IN_WIDTHS, IN_DTYPES,
                                                      tm=tm_c)
            kc_col, vc_col = 1, 2
        d_lat = neighbourhood_attention(zqkv, zqkv_c, rpb[l], q_col=0, k_col=1, v_col=2,
                                        kc_col=kc_col, vc_col=vc_col)
        a_lat = chunk_gmlp(za, sgu_ln_g[l], sgu_ln_b[l], w_spatial[l], b_spatial[l], tm=tmix_x)
        b_lat = fourier_mix(zb)
        c_lat = conformer_conv(zcv, conv_w[l], conv_b[l], conv_ln_g[l], conv_ln_b[l], tm=tmix_x)
        g2 = norm2_g[l][None, :]
        router_t = w_router[l].T
        x, h2, aff_t = merge_branches(a_lat, b_lat, c_lat, d_lat, zg, wa, wb, wc, wd_, wo, x, gt1,
                                      g2, sc2, sh2, router_t, tm=tm_x)
        if not last:
            a_c = chunk_gmlp(cza, sgu_ln_g[l], sgu_ln_b[l], w_spatial[l], b_spatial[l], tm=tmix_c)
            b_c = fourier_mix(czb)
            c_c = conformer_conv(czcv, conv_w[l], conv_b[l], conv_ln_g[l], conv_ln_b[l], tm=tmix_c)
            d_c = context_attention(zqkv_c, q_col=0, k_col=1, v_col=2)
            xc, hc2, aff_tc = merge_branches(a_c, b_c, c_c, d_c, czg, wa, wb, wc, wd_, wo, xc, cgt1,
                                             g2, csc2, csh2, router_t, tm=tm_c)

        xe, gate, slot_t, off = expert_route(h2, aff_t)
        if last:
            (y,) = expert_ffn([xe], [gate], w_gate_e, w_up_e, w_down_e, l)
        else:
            xe_c, gate_c, slot_tc, off_c = expert_route(hc2, aff_tc)
            y, y_c = expert_ffn([xe, xe_c], [gate, gate_c], w_gate_e, w_up_e, w_down_e, l)
            xc = expert_combine(y_c, slot_tc, off_c, xc, cgt2)
        x = expert_combine(y, slot_t, off, x, gt2, final_norm_g if last else None)
    return x
```

```python
import functools

import jax
import jax.numpy as jnp
import numpy as np
from jax import lax
from jax.experimental import pallas as pl
from jax.experimental.pallas import tpu as pltpu

D_MODEL = 1024
DEPTH = 2
GRID_W = 64
HEAD_DIM = 64
N_GROUPS = 4
BRANCH_WIDTH = N_GROUPS * HEAD_DIM
N_BRANCH = 4
CHUNK = 128
CONV_WIDTH = 31
WIN_ROWS = 8
WIN_COLS = 16
ROPE_BASE = 10000.0
N_EXPERTS = 16
EC_CAPACITY = 2
EPS = 1e-6
NEG_INF = -1e30
A_END = 2 * BRANCH_WIDTH
B_END = A_END + BRANCH_WIDTH
C_END = B_END + 2 * BRANCH_WIDTH
Q_END = C_END + BRANCH_WIDTH
K_END = Q_END + BRANCH_WIDTH
V_END = K_END + BRANCH_WIDTH
IN_COLS = V_END + N_BRANCH * D_MODEL
IN_WIDTHS = (A_END, B_END - A_END, C_END - B_END, V_END - C_END, IN_COLS - V_END)
IN_DTYPES = (jnp.float32, jnp.float32, jnp.float32, jnp.bfloat16, jnp.bfloat16)
GATE_HALF = 0.5

VMEM_LIMIT_BYTES = 56 * 1024 * 1024
F32 = jnp.float32
BF16 = jnp.bfloat16


PROJ_TM = 512
MIXER_TM = 1024


def _row_tile(n, target):
    tm = min(n, target)
    assert n % tm == 0
    return tm


def _sigmoid(v):
    return 0.5 * jnp.tanh(0.5 * v) + 0.5


def _silu(v):
    return v * _sigmoid(v)


def _params(*sem):
    return pltpu.CompilerParams(dimension_semantics=sem, vmem_limit_bytes=VMEM_LIMIT_BYTES)


MOD_ROWS = 8
MOD_TN = 1536


def _split_bf16(v):
    hi = v.astype(BF16)
    return hi, (v - hi.astype(F32)).astype(BF16)


def _dot3(a_hi, a_lo, b_hi, b_lo):
    return (jnp.dot(a_hi, b_hi, preferred_element_type=F32) + jnp.dot(a_hi, b_lo, preferred_element_type=F32)
            + jnp.dot(a_lo, b_hi, preferred_element_type=F32))


def _mod_kernel(c_ref, w_ref, b_ref, o_ref):
    s_hi, s_lo = _split_bf16(_silu(c_ref[...]))
    w_hi, w_lo = _split_bf16(w_ref[0])
    o_ref[0] = _dot3(s_hi, s_lo, w_hi, w_lo) + b_ref[0]


def modulation(cond, w_mod, b_mod):
    n_layers, d, cols = w_mod.shape
    return pl.pallas_call(
        _mod_kernel,
        grid=(n_layers, cols // MOD_TN),
        in_specs=[pl.BlockSpec((MOD_ROWS, d), lambda l, j: (0, 0)),
                  pl.BlockSpec((1, d, MOD_TN), lambda l, j: (l, 0, j)),
                  pl.BlockSpec((1, 1, MOD_TN), lambda l, j: (l, 0, j))],
        out_specs=pl.BlockSpec((1, MOD_ROWS, MOD_TN), lambda l, j: (l, 0, j)),
        out_shape=jax.ShapeDtypeStruct((n_layers, MOD_ROWS, cols), F32),
        compiler_params=_params("arbitrary", "arbitrary"),
        name="modulation",
    )(cond, w_mod, b_mod[:, None, :])


INPROJ_COL_CHUNK = 512


def _norm_inproj_kernel(x_ref, g_ref, sc_ref, sh_ref, w_ref, *o_refs):
    x = x_ref[0]
    y = x * lax.rsqrt(jnp.mean(x * x, axis=-1, keepdims=True) + EPS) * g_ref[...]
    h = (y * (1.0 + sc_ref[0]) + sh_ref[0]).astype(BF16)
    off = 0
    for o_ref in o_refs:
        width = o_ref.shape[2]
        for c0 in range(0, width, INPROJ_COL_CHUNK):
            cw = min(INPROJ_COL_CHUNK, width - c0)
            o_ref[0, :, c0:c0 + cw] = jnp.dot(h, w_ref[:, off + c0:off + c0 + cw],
                                              preferred_element_type=F32).astype(o_ref.dtype)
        off += width


def norm_inproj(x, g, sc, sh, w, l, col0, widths, dtypes, *, tm):
    bsz, n, d = x.shape
    cols = sum(widths)
    assert col0 % cols == 0
    return pl.pallas_call(
        _norm_inproj_kernel,
        grid=(bsz, n // tm),
        in_specs=[
            pl.BlockSpec((1, tm, d), lambda b, i: (b, i, 0)),
            pl.BlockSpec((1, d), lambda b, i: (0, 0)),
            pl.BlockSpec((1, 1, d), lambda b, i: (b, 0, 0)),
            pl.BlockSpec((1, 1, d), lambda b, i: (b, 0, 0)),
            pl.BlockSpec((None, d, cols), lambda b, i: (l, 0, col0 // cols), pipeline_mode=pl.Buffered(1)),
        ],
        out_specs=[pl.BlockSpec((1, tm, wd), lambda b, i: (b, i, 0)) for wd in widths],
        out_shape=[jax.ShapeDtypeStruct((bsz, n, wd), dt) for wd, dt in zip(widths, dtypes, strict=True)],
        compiler_params=_params("arbitrary", "arbitrary"),
        name="norm_inproj",
    )(x, g, sc, sh, w)


def _merge_kernel(a_ref, b_ref, c_ref, d_ref, ga_ref, gb_ref, gc_ref, gd_ref,
                  wa_ref, wb_ref, wc_ref, wd_ref, wo_ref, x_ref, gt_ref,
                  g2_ref, sc2_ref, sh2_ref, rt_ref, o_ref, h_ref, aff_ref):
    m = None
    for br, gz, w in ((a_ref, ga_ref, wa_ref), (b_ref, gb_ref, wb_ref),
                      (c_ref, gc_ref, wc_ref), (d_ref, gd_ref, wd_ref)):
        hp = jnp.dot(br[0].astype(BF16), w[...], preferred_element_type=F32)
        t = hp * jnp.tanh(gz[0].astype(F32)) + hp
        m = t if m is None else m + t
    mix = jnp.dot(m.astype(BF16), wo_ref[...], preferred_element_type=F32)
    x = x_ref[0] + gt_ref[0] * mix
    o_ref[0] = x
    y = x * lax.rsqrt(jnp.mean(x * x, axis=-1, keepdims=True) + EPS) * g2_ref[...]
    h = y * (1.0 + sc2_ref[0]) + sh2_ref[0]
    h_hi, h_lo = _split_bf16(h)
    r_hi, r_lo = _split_bf16(rt_ref[...])
    ne = rt_ref.shape[0]
    both = lax.dot_general(jnp.concatenate([r_hi, r_lo], axis=0), h_hi, _NT, preferred_element_type=F32)
    logits = both[:ne] + both[ne:] + lax.dot_general(r_hi, h_lo, _NT, preferred_element_type=F32)
    pr = jnp.exp(logits - jnp.max(logits, axis=0, keepdims=True))
    aff_ref[0] = pr / jnp.sum(pr, axis=0, keepdims=True)
    h_ref[0] = h_hi


def merge_branches(a, b, cc, d, z, w_a, w_b, w_c, w_d, w_o, x, gt, g2, sc2, sh2, router_t, *, tm):
    bsz, n, dm = x.shape
    w = a.shape[-1]
    ne = router_t.shape[0]
    br_spec = pl.BlockSpec((1, tm, w), lambda bi, i: (bi, i, 0))
    gate_specs = [pl.BlockSpec((1, tm, dm), functools.partial(lambda bi, i, k: (bi, i, k), k=k))
                  for k in range(N_BRANCH)]
    wbr_spec = pl.BlockSpec((w, dm), lambda bi, i: (0, 0))
    row_spec = pl.BlockSpec((1, tm, dm), lambda bi, i: (bi, i, 0))
    mod_spec = pl.BlockSpec((1, 1, dm), lambda bi, i: (bi, 0, 0))
    return pl.pallas_call(
        _merge_kernel,
        grid=(bsz, n // tm),
        in_specs=[br_spec] * 4 + gate_specs + [wbr_spec] * 4 + [
            pl.BlockSpec((dm, dm), lambda bi, i: (0, 0)), row_spec, mod_spec,
            pl.BlockSpec((1, dm), lambda bi, i: (0, 0)), mod_spec, mod_spec,
            pl.BlockSpec((ne, dm), lambda bi, i: (0, 0)),
        ],
        out_specs=[row_spec, row_spec, pl.BlockSpec((1, ne, tm), lambda bi, i: (bi, 0, i))],
        out_shape=[jax.ShapeDtypeStruct((bsz, n, dm), F32), jax.ShapeDtypeStruct((bsz, n, dm), BF16),
                   jax.ShapeDtypeStruct((bsz, ne, n), F32)],
        compiler_params=_params("arbitrary", "arbitrary"),
        name="merge_branches",
    )(a, b, cc, d, z, z, z, z, w_a, w_b, w_c, w_d, w_o, x, gt, g2, sc2, sh2, router_t)


ROUTE_TB = 256
ROUTE_WIN = 64
ROUTE_ALIGN = 16
GATHER_SUB = 8
COMBINE_SUB = 2
OFF_LANES = 128
TOPK_EXP_STEPS = (64, 32, 16, 8, 4, 2, 1)
TOPK_BISECT_STEPS = 32


def _topk_kernel(aff_ref, slot_ref, off_ref, *, cap):
    a = aff_ref[0]
    ne, n = a.shape
    capf = jnp.float32(cap)

    def count_ge(t):
        return jnp.sum(jnp.where(a >= t, 1.0, 0.0), axis=1, keepdims=True)

    hi = jnp.full((ne, 1), 2.0, F32)
    for s in TOPK_EXP_STEPS:
        cand = hi * (2.0 ** -s)
        hi = jnp.where(count_ge(cand) < capf, cand, hi)
    lo = jnp.where(hi <= 2.0 ** -126, 0.0, hi * 0.5)

    def bisect(_, carry):
        lo, hi = carry
        mid = 0.5 * (lo + hi)
        ok = count_ge(mid) >= capf
        return jnp.where(ok, mid, lo), jnp.where(ok, hi, mid)

    lo, hi = lax.fori_loop(0, TOPK_BISECT_STEPS, bisect, (lo, hi))
    need = capf - count_ge(hi)
    r_i = lax.broadcasted_iota(jnp.int32, (ROUTE_TB, ROUTE_TB), 0)
    c_i = lax.broadcasted_iota(jnp.int32, (ROUTE_TB, ROUTE_TB), 1)
    tri = jnp.where(r_i < c_i, 1.0, 0.0).astype(BF16)
    lane = lax.broadcasted_iota(jnp.int32, (ne, OFF_LANES), 1)
    run_eq = jnp.zeros((ne, 1), F32)
    run_sel = jnp.zeros((ne, 1), F32)
    offs = jnp.zeros((ne, OFF_LANES), F32)
    for c in range(n // ROUTE_TB):
        cols = slice(c * ROUTE_TB, (c + 1) * ROUTE_TB)
        a_c = a[:, cols]
        above = a_c >= hi
        eq_c = jnp.where(above, 0.0, jnp.where(a_c >= lo, 1.0, 0.0))
        rank = jnp.dot(eq_c.astype(BF16), tri, preferred_element_type=F32) + run_eq
        sel = jnp.where(above, 1.0, jnp.where(rank < need, eq_c, 0.0))
        pos = jnp.dot(sel.astype(BF16), tri, preferred_element_type=F32) + run_sel
        slot_ref[0, :, cols] = jnp.where(sel > 0.0, pos, -1.0).astype(jnp.int32)
        offs = jnp.where(lane == c, run_sel, offs)
        run_eq = run_eq + jnp.sum(eq_c, axis=1, keepdims=True)
        run_sel = run_sel + jnp.sum(sel, axis=1, keepdims=True)
    offs = jnp.where(lane == n // ROUTE_TB, run_sel, offs)
    off_ref[0] = offs.astype(jnp.int32)


def expert_topk(aff_t, cap):
    bsz, ne, n = aff_t.shape
    return pl.pallas_call(
        functools.partial(_topk_kernel, cap=cap),
        grid=(bsz,),
        in_specs=[pl.BlockSpec((1, ne, n), lambda b: (b, 0, 0))],
        out_specs=[pl.BlockSpec((1, ne, n), lambda b: (b, 0, 0)),
                   pl.BlockSpec((1, ne, OFF_LANES), lambda b: (b, 0, 0))],
        out_shape=[jax.ShapeDtypeStruct((bsz, ne, n), jnp.int32),
                   jax.ShapeDtypeStruct((bsz, ne, OFF_LANES), jnp.int32)],
        compiler_params=_params("arbitrary"),
        name="expert_topk",
    )(aff_t)


def _window_plan(off_ref, b, e, tb, w, rows_max):
    start = off_ref[b, e, tb]
    stop = off_ref[b, e, tb + 1]
    first = (start // ROUTE_ALIGN) * ROUTE_ALIGN + w * ROUTE_WIN
    active = first < stop
    return pl.multiple_of(jnp.minimum(first, rows_max), ROUTE_ALIGN), first, active


def _num_passes(off_ref, b, e_list, tb):
    n_pass = jnp.int32(0)
    for e in e_list:
        start = off_ref[b, e, tb]
        stop = off_ref[b, e, tb + 1]
        base = (start // ROUTE_ALIGN) * ROUTE_ALIGN
        n_pass = jnp.maximum(n_pass, (stop - base + ROUTE_WIN - 1) // ROUTE_WIN)
    return n_pass


def _gather_kernel(off_ref, h_ref, slot_ref, aff_ref, xe_ref, gate_ref, *, group):
    b, g, step = pl.program_id(0), pl.program_id(1), pl.program_id(2)
    rows_max = xe_ref.shape[2] - ROUTE_WIN
    n_sub = h_ref.shape[1] // ROUTE_TB

    sub = lax.broadcasted_iota(jnp.int32, (ROUTE_WIN, 1), 0)
    experts = [g * group + e for e in range(group)]

    for j in range(n_sub):
        tb = step * n_sub + j
        tok = slice(j * ROUTE_TB, (j + 1) * ROUTE_TB)

        def one_pass(w, carry, tb=tb, tok=tok):
            plans = [_window_plan(off_ref, b, ge, tb, w, rows_max) for ge in experts]
            hots = []
            for e, (rows, first, active) in enumerate(plans):
                ids = rows + sub
                want = jnp.where(active & (ids >= first), ids, -2)
                hots.append(slot_ref[0, e:e + 1, tok] == want)
            onehot = jnp.concatenate([jnp.where(hm, 1.0, 0.0) for hm in hots], axis=0).astype(BF16)
            res = jnp.dot(onehot, h_ref[0, tok, :], preferred_element_type=F32)
            for e, (rows, first, _) in enumerate(plans):
                win = pl.ds(rows, ROUTE_WIN)
                fresh = (rows + sub) >= jnp.maximum(off_ref[b, experts[e], tb], first)
                xe_ref[0, e, win, :] = jnp.where(fresh, res[e * ROUTE_WIN:(e + 1) * ROUTE_WIN].astype(BF16),
                                                 xe_ref[0, e, win, :])
                gsel = jnp.sum(jnp.where(hots[e], aff_ref[0, e:e + 1, tok], 0.0), axis=1, keepdims=True)
                gate_ref[0, e, win, :] = jnp.where(fresh, gsel, gate_ref[0, e, win, :])
            return carry

        lax.fori_loop(0, _num_passes(off_ref, b, experts, tb), one_pass, 0)


def expert_gather(h, slot, aff_t, off, cap_pad, *, group=8):
    bsz, n, d = h.shape
    ne = slot.shape[1]
    tok = ROUTE_TB * min(GATHER_SUB, n // ROUTE_TB)
    grid_spec = pltpu.PrefetchScalarGridSpec(
        num_scalar_prefetch=1,
        grid=(bsz, ne // group, n // tok),
        in_specs=[
            pl.BlockSpec((1, tok, d), lambda b, g, t, off: (b, t, 0)),
            pl.BlockSpec((1, group, tok), lambda b, g, t, off: (b, g, t)),
            pl.BlockSpec((1, group, tok), lambda b, g, t, off: (b, g, t)),
        ],
        out_specs=[pl.BlockSpec((1, group, cap_pad, d), lambda b, g, t, off: (b, g, 0, 0)),
                   pl.BlockSpec((1, group, cap_pad, 1), lambda b, g, t, off: (b, g, 0, 0))],
    )
    return pl.pallas_call(
        functools.partial(_gather_kernel, group=group),
        grid_spec=grid_spec,
        out_shape=[jax.ShapeDtypeStruct((bsz, ne, cap_pad, d), BF16),
                   jax.ShapeDtypeStruct((bsz, ne, cap_pad, 1), F32)],
        compiler_params=_params("arbitrary", "arbitrary", "arbitrary"),
        name="expert_gather",
    )(off, h, slot, aff_t)


def _combine_kernel(off_ref, y_ref, slot_t_ref, x_ref, gt_ref, fg_ref, o_ref, stage_ref, acc_ref):
    b, step = pl.program_id(0), pl.program_id(1)
    ne = y_ref.shape[1]
    rows_max = y_ref.shape[2] - ROUTE_WIN
    width = ne * ROUTE_WIN
    n_sub = x_ref.shape[1] // ROUTE_TB
    experts = list(range(ne))
    lane = lax.broadcasted_iota(jnp.int32, (1, width), 1)
    e_row = lax.broadcasted_iota(jnp.int32, (ne, width), 0)
    e_lane = lax.broadcasted_iota(jnp.int32, (ne, width), 1) // ROUTE_WIN
    expand = jnp.where(e_row == e_lane, 1.0, 0.0).astype(BF16)

    for j in range(n_sub):
        tb = step * n_sub + j
        tok = slice(j * ROUTE_TB, (j + 1) * ROUTE_TB)
        s1 = slot_t_ref[0, tok, :] + 1
        hi = (s1 // ROUTE_WIN).astype(F32).astype(BF16)
        lo = (s1 % ROUTE_WIN).astype(F32).astype(BF16)
        s1_wide = (float(ROUTE_WIN) * jnp.dot(hi, expand, preferred_element_type=F32)
                   + jnp.dot(lo, expand, preferred_element_type=F32))
        acc_ref[...] = jnp.zeros(acc_ref.shape, F32)

        def one_pass(w, carry, tb=tb, s1_wide=s1_wide):
            want = jnp.full((1, width), -1, jnp.int32)
            for e in experts:
                rows, first, active = _window_plan(off_ref, b, e, tb, w, rows_max)
                stage_ref[e * ROUTE_WIN:(e + 1) * ROUTE_WIN, :] = y_ref[0, e, pl.ds(rows, ROUTE_WIN), :]
                ids = rows + lane % ROUTE_WIN
                want = jnp.where((lane // ROUTE_WIN == e) & active & (ids >= first), ids + 1, want)
            onehot = jnp.where(s1_wide == want.astype(F32), 1.0, 0.0).astype(BF16)
            acc_ref[...] += jnp.dot(onehot, stage_ref[...], preferred_element_type=F32)
            return carry

        lax.fori_loop(0, _num_passes(off_ref, b, experts, tb), one_pass, 0)
        out = x_ref[0, tok, :] + gt_ref[0] * acc_ref[...]
        if fg_ref is not None:
            out = out * lax.rsqrt(jnp.mean(out * out, axis=-1, keepdims=True) + EPS) * fg_ref[...]
        o_ref[0, tok, :] = out


def _combine_kernel_plain(off_ref, y_ref, slot_t_ref, x_ref, gt_ref, o_ref, stage_ref, acc_ref):
    _combine_kernel(off_ref, y_ref, slot_t_ref, x_ref, gt_ref, None, o_ref, stage_ref, acc_ref)


def expert_combine(y, slot_t, off, x, gt, final_g=None):
    bsz, n, d = x.shape
    ne, cap_pad = y.shape[1], y.shape[2]
    tok = ROUTE_TB * min(COMBINE_SUB, n // ROUTE_TB)
    grid_spec = pltpu.PrefetchScalarGridSpec(
        num_scalar_prefetch=1,
        grid=(bsz, n // tok),
        in_specs=[
            pl.BlockSpec((1, ne, cap_pad, d), lambda b, t, off: (b, 0, 0, 0), pipeline_mode=pl.Buffered(1)),
            pl.BlockSpec((1, tok, ne), lambda b, t, off: (b, t, 0)),
            pl.BlockSpec((1, tok, d), lambda b, t, off: (b, t, 0)),
            pl.BlockSpec((1, 1, d), lambda b, t, off: (b, 0, 0)),
        ] + ([] if final_g is None else [pl.BlockSpec((1, d), lambda b, t, off: (0, 0))]),
        out_specs=pl.BlockSpec((1, tok, d), lambda b, t, off: (b, t, 0)),
        scratch_shapes=[pltpu.VMEM((ne * ROUTE_WIN, d), BF16), pltpu.VMEM((ROUTE_TB, d), F32)],
    )
    args = (off, y, slot_t, x, gt) + (() if final_g is None else (final_g[None, :],))
    return pl.pallas_call(
        _combine_kernel_plain if final_g is None else _combine_kernel,
        grid_spec=grid_spec,
        out_shape=jax.ShapeDtypeStruct((bsz, n, d), F32),
        compiler_params=_params("arbitrary", "arbitrary"),
        name="expert_combine",
    )(*args)


FFN_FBLK = 512
FFN_MIN_ROWS = 256


def _expert_ffn_kernel(*refs, n_sets, n_f):
    xe_refs, gate_refs = refs[:n_sets], refs[n_sets:2 * n_sets]
    wg_ref, wu_ref, wd_ref = refs[2 * n_sets:2 * n_sets + 3]
    o_refs = refs[2 * n_sets + 3:3 * n_sets + 3]
    acc_refs = refs[3 * n_sets + 3:]
    f = pl.program_id(1)
    blocks = [(s, b) for s in range(n_sets) for b in range(xe_refs[s].shape[0])]
    big = [sb for sb in blocks if xe_refs[sb[0]].shape[2] >= FFN_MIN_ROWS]
    small = [sb for sb in blocks if sb not in big]
    groups = [[sb] for sb in big[:-1]] + [big[-1:] + small] if big else [small]

    def hidden_block(first, last):
        wg = wg_ref[0].astype(BF16)
        wu = wu_ref[0].astype(BF16)
        wd = wd_ref[0].astype(BF16)
        for group in groups:
            xs = [xe_refs[s][b, 0] for s, b in group]
            xe = xs[0] if len(xs) == 1 else jnp.concatenate(xs, axis=0)
            a = jnp.dot(xe, wg, preferred_element_type=F32)
            u = jnp.dot(xe, wu, preferred_element_type=F32)
            t = jnp.dot((_silu(a) * u).astype(BF16), wd, preferred_element_type=F32)
            r0 = 0
            for s, b in group:
                rows = xe_refs[s].shape[2]
                part = t[r0:r0 + rows] if first else acc_refs[s][b] + t[r0:r0 + rows]
                if last:
                    o_refs[s][b, 0] = (part * gate_refs[s][b, 0]).astype(o_refs[s].dtype)
                else:
                    acc_refs[s][b] = part
                r0 += rows

    if n_f == 1:
        hidden_block(True, True)
        return
    pl.when(f == 0)(lambda: hidden_block(True, False))
    if n_f > 2:
        pl.when((f > 0) & (f < n_f - 1))(lambda: hidden_block(False, False))
    pl.when(f == n_f - 1)(lambda: hidden_block(False, True))


def expert_ffn(xes, gates, wg, wu, wd, l):
    n_sets = len(xes)
    _, ne, d, ff = wg.shape
    xe_specs = [pl.BlockSpec((xe.shape[0], 1) + xe.shape[2:], lambda e, f: (0, e, 0, 0)) for xe in xes]
    gate_specs = [pl.BlockSpec((g.shape[0], 1) + g.shape[2:], lambda e, f: (0, e, 0, 0)) for g in gates]
    return pl.pallas_call(
        functools.partial(_expert_ffn_kernel, n_sets=n_sets, n_f=ff // FFN_FBLK),
        grid=(ne, ff // FFN_FBLK),
        in_specs=xe_specs + gate_specs + [
            pl.BlockSpec((None, 1, d, FFN_FBLK), lambda e, f: (l, e, 0, f)),
            pl.BlockSpec((None, 1, d, FFN_FBLK), lambda e, f: (l, e, 0, f)),
            pl.BlockSpec((None, 1, FFN_FBLK, d), lambda e, f: (l, e, f, 0)),
        ],
        out_specs=xe_specs,
        out_shape=[jax.ShapeDtypeStruct(xe.shape, BF16) for xe in xes],
        scratch_shapes=[pltpu.VMEM((xe.shape[0],) + xe.shape[2:], F32) for xe in xes],
        compiler_params=_params("arbitrary", "arbitrary"),
        name="expert_ffn",
    )(*xes, *gates, wg, wu, wd)


ATT_QROWS = 4
ATT_QTOK = ATT_QROWS * GRID_W
ATT_KROWS = ATT_QROWS + WIN_ROWS
ATT_KBLK = ATT_KROWS * GRID_W // ATT_QTOK
_NT = (((1,), (1,)), ((), ()))

def _rope_tables(n):
    quarter = HEAD_DIM // 4
    inv = ROPE_BASE ** (-np.arange(quarter, dtype=np.float64) / quarter)
    t = np.arange(n)
    ang_r = (t // GRID_W)[:, None] * inv[None, :]
    ang_c = (t % GRID_W)[:, None] * inv[None, :]
    cos = np.concatenate([np.cos(ang_r)] * 2 + [np.cos(ang_c)] * 2, axis=1)
    sin = np.concatenate([-np.sin(ang_r), np.sin(ang_r), -np.sin(ang_c), np.sin(ang_c)], axis=1)
    return (jnp.asarray(np.concatenate([cos, cos], axis=1), F32),
            jnp.asarray(np.concatenate([sin, sin], axis=1), F32))


def _attn_bias_tables(rpb, n_rows):
    n_blk = n_rows // ATT_QROWS
    n_dr = 2 * WIN_ROWS - 1
    qc = np.arange(GRID_W)
    kc = np.arange(GRID_W)
    c_start = np.clip(qc - WIN_COLS // 2, 0, GRID_W - WIN_COLS)
    col_ok = (kc[None, :] >= c_start[:, None]) & (kc[None, :] < c_start[:, None] + WIN_COLS)
    assert np.all(np.abs(kc[None, :] - qc[:, None])[col_ok] < WIN_COLS)
    row_sel, oks = [], []
    for j in (0, 1, n_blk - 1):
        ks = min(max(ATT_QROWS * j - WIN_ROWS // 2, 0), n_rows - ATT_KROWS)
        r = ATT_QROWS * j + np.arange(ATT_QROWS)
        kr0 = np.clip(r - WIN_ROWS // 2, 0, n_rows - WIN_ROWS)
        krow = ks + np.arange(ATT_KROWS)
        row_ok = (krow[None, :] >= kr0[:, None]) & (krow[None, :] < kr0[:, None] + WIN_ROWS)
        d_row = np.clip(krow[None, :] - r[:, None] + (WIN_ROWS - 1), 0, n_dr - 1)
        row_sel.append((d_row.reshape(-1)[:, None] == np.arange(n_dr)[None, :]).astype(np.float32))
        oks.append(row_ok[:, None, :, None] & col_ok[None, :, None, :])
    a = jnp.einsum('cpr,hrd->chpd', jnp.asarray(np.stack(row_sel)), rpb.astype(F32), precision=lax.Precision.HIGHEST)
    a = a.reshape(3, N_GROUPS, ATT_QROWS, ATT_KROWS, 2 * WIN_COLS - 1)
    pad = jnp.zeros(a.shape[:-1] + (GRID_W - (2 * WIN_COLS - 1),), F32)
    v = jnp.concatenate([a[..., WIN_COLS - 1:], pad, jnp.roll(a, -1, axis=3)[..., :WIN_COLS - 1]], axis=-1)
    v = v.reshape(3, N_GROUPS, ATT_QROWS, ATT_KROWS * GRID_W)
    ok = jnp.asarray(np.stack(oks).reshape(3, ATT_QTOK, ATT_KROWS * GRID_W).astype(np.float32))
    return pl.pallas_call(
        _attn_bias_kernel,
        grid=(3, N_GROUPS),
        in_specs=[pl.BlockSpec((1, 1, ATT_QROWS, ATT_KROWS * GRID_W), lambda c, h: (c, h, 0, 0)),
                  pl.BlockSpec((1, ATT_QTOK, ATT_KROWS * GRID_W), lambda c, h: (c, 0, 0))],
        out_specs=pl.BlockSpec((1, 1, ATT_QTOK, ATT_KROWS * GRID_W), lambda c, h: (c, h, 0, 0)),
        out_shape=jax.ShapeDtypeStruct((3, N_GROUPS, ATT_QTOK, ATT_KROWS * GRID_W), F32),
        compiler_params=_params("arbitrary", "arbitrary"),
        name="attn_bias",
    )(v, ok)


def _attn_bias_kernel(v_ref, ok_ref, o_ref):
    width = v_ref.shape[3]
    for qr in range(ATT_QROWS):
        rows = slice(qr * GRID_W, (qr + 1) * GRID_W)
        base = jnp.broadcast_to(v_ref[0, 0, qr:qr + 1, :], (GRID_W, width))
        slab = pltpu.roll(base, 0, 1, stride=1, stride_axis=0)
        o_ref[0, 0, rows, :] = jnp.where(ok_ref[0, rows, :] > 0.0, slab, NEG_INF)


def _rope(x, cos_ref, sin_ref, first16):
    c = cos_ref[...]
    s = sin_ref[...]
    c2 = jnp.concatenate([c, c], axis=1)
    s2 = jnp.concatenate([s, s], axis=1)
    w = x.shape[1]
    partner = jnp.where(first16, pltpu.roll(x, w - HEAD_DIM // 4, 1), pltpu.roll(x, HEAD_DIM // 4, 1))
    return x * c2 + partner * s2


ATT_HEAD_STACK = 1


def _stack_heads(x, lane, heads):
    return jnp.concatenate([jnp.where((lane // HEAD_DIM) == h, x, 0.0) for h in heads], axis=0).astype(BF16)


def _softmax_pv(scores_values, lane, heads):
    m = None
    for s, _ in scores_values:
        mx = jnp.max(s, axis=1, keepdims=True)
        m = mx if m is None else jnp.maximum(m, mx)
    den = None
    o = None
    for s, v in scores_values:
        p = jnp.exp(s - m)
        sm = jnp.sum(p, axis=1, keepdims=True)
        den = sm if den is None else den + sm
        t = jnp.dot(p.astype(BF16), v, preferred_element_type=F32)
        o = t if o is None else o + t
    o = o * (1.0 / den)
    rows = o.shape[0] // len(heads)
    out = None
    for i, h in enumerate(heads):
        t = jnp.where((lane // HEAD_DIM) == h, o[i * rows:(i + 1) * rows], 0.0)
        out = t if out is None else out + t
    return out


def _head_groups():
    return [tuple(range(h, h + ATT_HEAD_STACK)) for h in range(0, N_GROUPS, ATT_HEAD_STACK)]


ATT_QSTEP = 4
ATT_SLAB = ATT_KBLK + ATT_QSTEP - 1


def _att_key_block(jq, n_blk):
    return jnp.clip(jq - 1, 0, n_blk - ATT_KBLK)


def _nattn_kernel(*refs, n_blk):
    q_ref = refs[0]
    k_refs, v_refs = refs[1:1 + ATT_SLAB], refs[1 + ATT_SLAB:1 + 2 * ATT_SLAB]
    cq_ref, sq_ref = refs[1 + 2 * ATT_SLAB:3 + 2 * ATT_SLAB]
    ck_refs = refs[3 + 2 * ATT_SLAB:3 + 3 * ATT_SLAB]
    sk_refs = refs[3 + 3 * ATT_SLAB:3 + 4 * ATT_SLAB]
    kc_ref, vc_ref = refs[3 + 4 * ATT_SLAB:5 + 4 * ATT_SLAB]
    bias_refs = refs[5 + 4 * ATT_SLAB:5 + 4 * ATT_SLAB + ATT_QSTEP]
    o_ref, k_scr, v_scr = refs[5 + 4 * ATT_SLAB + ATT_QSTEP:]
    j = pl.program_id(1)
    w = q_ref.shape[2]
    lane = lax.broadcasted_iota(jnp.int32, (1, w), 1)
    first16 = (lane % (HEAD_DIM // 2)) < (HEAD_DIM // 4)
    q_all = q_ref[0].astype(F32) * (HEAD_DIM ** -0.5)
    q_rot_all = _rope(q_all, cq_ref, sq_ref, first16)
    for i in range(ATT_SLAB):
        rows = slice(i * ATT_QTOK, (i + 1) * ATT_QTOK)
        k_scr[rows, :] = _rope(k_refs[i][0].astype(F32), ck_refs[i], sk_refs[i], first16).astype(BF16)
        v_scr[rows, :] = v_refs[i][0].astype(BF16)
    kc = kc_ref[0].astype(BF16)
    vc = vc_ref[0].astype(BF16)
    slab0 = _att_key_block(j * ATT_QSTEP, n_blk - ATT_QSTEP + 1)
    for i in range(ATT_QSTEP):
        rows = slice(i * ATT_QTOK, (i + 1) * ATT_QTOK)
        q, q_rot = q_all[rows], q_rot_all[rows]
        start = pl.multiple_of((_att_key_block(j * ATT_QSTEP + i, n_blk) - slab0) * ATT_QTOK, ATT_QTOK)
        keys = pl.ds(start, ATT_KBLK * ATT_QTOK)
        k_rot, v = k_scr[keys, :], v_scr[keys, :]
        acc = None
        for heads in _head_groups():
            bias = jnp.concatenate([bias_refs[i][0, h] for h in heads], axis=0)
            s_win = lax.dot_general(_stack_heads(q_rot, lane, heads), k_rot, _NT, preferred_element_type=F32) + bias
            s_ctx = lax.dot_general(_stack_heads(q, lane, heads), kc, _NT, preferred_element_type=F32)
            t = _softmax_pv([(s_win, v), (s_ctx, vc)], lane, heads)
            acc = t if acc is None else acc + t
        o_ref[0, rows, :] = acc.astype(o_ref.dtype)


def neighbourhood_attention(z, zc, rpb, *, q_col, k_col, v_col, kc_col, vc_col):
    bsz, n, _ = z.shape
    n_ctx = zc.shape[1]
    w = BRANCH_WIDTH
    n_blk = n // ATT_QTOK
    assert n_blk % ATT_QSTEP == 0 and n_blk >= ATT_SLAB
    cos_t, sin_t = _rope_tables(n)
    bias = _attn_bias_tables(rpb, n // GRID_W)
    q_tok = ATT_QSTEP * ATT_QTOK

    def slab(j):
        return _att_key_block(j * ATT_QSTEP, n_blk - ATT_QSTEP + 1)

    def kspec(col, off):
        return pl.BlockSpec((1, ATT_QTOK, w), lambda b, j: (b, slab(j) + off, col))

    def tspec(off):
        return pl.BlockSpec((ATT_QTOK, 2 * HEAD_DIM), lambda b, j: (slab(j) + off, 0))

    def bias_spec(i):
        def cfg(b, j):
            jq = j * ATT_QSTEP + i
            return (jnp.minimum(jq, 1) + jq // (n_blk - 1), 0, 0, 0)
        return pl.BlockSpec((1, N_GROUPS, ATT_QTOK, ATT_KROWS * GRID_W), cfg)

    qt_spec = pl.BlockSpec((q_tok, 2 * HEAD_DIM), lambda b, j: (j, 0))
    in_specs = ([pl.BlockSpec((1, q_tok, w), lambda b, j: (b, j, q_col))]
                + [kspec(k_col, i) for i in range(ATT_SLAB)] + [kspec(v_col, i) for i in range(ATT_SLAB)]
                + [qt_spec, qt_spec] + [tspec(i) for i in range(ATT_SLAB)] * 2
                + [pl.BlockSpec((1, n_ctx, w), lambda b, j: (b, 0, kc_col)),
                   pl.BlockSpec((1, n_ctx, w), lambda b, j: (b, 0, vc_col))]
                + [bias_spec(i) for i in range(ATT_QSTEP)])
    args = ([z] * (1 + 2 * ATT_SLAB) + [cos_t, sin_t] + [cos_t] * ATT_SLAB + [sin_t] * ATT_SLAB + [zc, zc]
            + [bias] * ATT_QSTEP)
    return pl.pallas_call(
        functools.partial(_nattn_kernel, n_blk=n_blk),
        grid=(bsz, n_blk // ATT_QSTEP),
        in_specs=in_specs,
        out_specs=pl.BlockSpec((1, q_tok, w), lambda b, j: (b, j, 0)),
        out_shape=jax.ShapeDtypeStruct((bsz, n, w), BF16),
        scratch_shapes=[pltpu.VMEM((ATT_SLAB * ATT_QTOK, w), BF16), pltpu.VMEM((ATT_SLAB * ATT_QTOK, w), BF16)],
        compiler_params=_params("arbitrary", "arbitrary"),
        name="neighbourhood_attention",
    )(*args)


def _ctx_attn_kernel(q_ref, kc_ref, vc_ref, o_ref):
    w = q_ref.shape[2]
    lane = lax.broadcasted_iota(jnp.int32, (1, w), 1)
    q = q_ref[0].astype(F32) * (HEAD_DIM ** -0.5)
    kc = kc_ref[0].astype(BF16)
    vc = vc_ref[0].astype(BF16)
    acc = None
    for heads in _head_groups():
        s = lax.dot_general(_stack_heads(q, lane, heads), kc, _NT, preferred_element_type=F32)
        t = _softmax_pv([(s, vc)], lane, heads)
        acc = t if acc is None else acc + t
    o_ref[0] = acc.astype(o_ref.dtype)


def context_attention(zc, *, q_col, k_col, v_col):
    bsz, n_ctx, _ = zc.shape
    w = BRANCH_WIDTH
    return pl.pallas_call(
        _ctx_attn_kernel,
        grid=(bsz,),
        in_specs=[pl.BlockSpec((1, n_ctx, w), functools.partial(lambda b, c: (b, 0, c), c=c))
                  for c in (q_col, k_col, v_col)],
        out_specs=pl.BlockSpec((1, n_ctx, w), lambda b: (b, 0, 0)),
        out_shape=jax.ShapeDtypeStruct((bsz, n_ctx, w), BF16),
        compiler_params=_params("arbitrary"),
        name="context_attention",
    )(zc, zc, zc)


def _ln(v, g, b):
    mu = jnp.mean(v, axis=-1, keepdims=True)
    var = jnp.mean(jnp.square(v - mu), axis=-1, keepdims=True)
    return (v - mu) * lax.rsqrt(var + EPS) * g + b


def _gmlp_kernel(z_ref, lng_ref, lnb_ref, w_ref, bias_ref, o_ref):
    tm = z_ref.shape[1]
    z = jax.nn.gelu(z_ref[0])
    u = z[:, :BRANCH_WIDTH]
    v = _ln(z[:, BRANCH_WIDTH:], lng_ref[...], lnb_ref[...])
    group = lax.broadcasted_iota(jnp.int32, (1, BRANCH_WIDTH), 1) // HEAD_DIM
    wcat = w_ref[...]
    for c in range(tm // CHUNK):
        rows = slice(c * CHUNK, (c + 1) * CHUNK)
        vc = v[rows]
        vst = jnp.concatenate([jnp.where(group == g, vc, 0.0) for g in range(N_GROUPS)], axis=0).astype(BF16)
        mixed = jnp.dot(wcat, vst, preferred_element_type=F32) + bias_ref[...]
        o_ref[0, rows, :] = (u[rows] * mixed).astype(o_ref.dtype)


def chunk_gmlp(za, ln_g, ln_b, w_s, b_s, *, tm):
    bsz, n, w2 = za.shape
    w = w2 // 2
    wcat = jnp.transpose(w_s, (1, 0, 2)).reshape(CHUNK, N_GROUPS * CHUNK).astype(BF16)
    bias = jnp.repeat(b_s.T, HEAD_DIM, axis=1)
    return pl.pallas_call(
        _gmlp_kernel,
        grid=(bsz, n // tm),
        in_specs=[
            pl.BlockSpec((1, tm, w2), lambda b, i: (b, i, 0)),
            pl.BlockSpec((1, w), lambda b, i: (0, 0)),
            pl.BlockSpec((1, w), lambda b, i: (0, 0)),
            pl.BlockSpec((CHUNK, N_GROUPS * CHUNK), lambda b, i: (0, 0)),
            pl.BlockSpec((CHUNK, w), lambda b, i: (0, 0)),
        ],
        out_specs=pl.BlockSpec((1, tm, w), lambda b, i: (b, i, 0)),
        out_shape=jax.ShapeDtypeStruct((bsz, n, w), BF16),
        compiler_params=_params("arbitrary", "arbitrary"),
        name="chunk_gmlp",
    )(za, ln_g[None, :], ln_b[None, :], wcat, bias)


CONV_HALO = 16
CONV_SUB = 128
SUBLANES = 8
LANES = 128


def _conv_kernel(a_ref, g_ref, ap_ref, gp_ref, an_ref, gn_ref, w_ref, cb_ref, lng_ref, lnb_ref, o_ref, y_ref):
    i = pl.program_id(1)
    tm = a_ref.shape[1]

    def glu(a, g):
        return a * _sigmoid(g)

    y_ref[0:CONV_HALO, :] = jnp.where(i > 0, glu(ap_ref[0], gp_ref[0]), 0.0)
    y_ref[CONV_HALO:CONV_HALO + tm, :] = glu(a_ref[0], g_ref[0])
    y_ref[CONV_HALO + tm:, :] = jnp.where(i < pl.num_programs(1) - 1, glu(an_ref[0], gn_ref[0]), 0.0)
    first = CONV_HALO - CONV_WIDTH // 2
    nb = CONV_SUB + 2 * CONV_HALO
    for r in range(0, tm, CONV_SUB):
        blk = y_ref[r:r + nb, :]
        acc = None
        for res in range(SUBLANES):
            rot = blk if res == 0 else pltpu.roll(blk, nb - res, 0)
            for j in range(CONV_WIDTH):
                if (first + j) % SUBLANES == res:
                    a0 = first + j - res
                    t = rot[a0:a0 + CONV_SUB, :] * w_ref[j:j + 1, :]
                    acc = t if acc is None else acc + t
        y = _ln(acc + cb_ref[...], lng_ref[...], lnb_ref[...])
        o_ref[0, r:r + CONV_SUB, :] = _silu(y).astype(o_ref.dtype)


def conformer_conv(zcv, conv_w, conv_b, ln_g, ln_b, *, tm):
    bsz, n, w2 = zcv.shape
    w = w2 // 2
    hb = tm // CONV_HALO
    n_hb = n // CONV_HALO

    def main(col):
        return pl.BlockSpec((1, tm, w), lambda b, i: (b, i, col))

    def prev(col):
        return pl.BlockSpec((1, CONV_HALO, w), lambda b, i: (b, jnp.maximum(i * hb - 1, 0), col))

    def nxt(col):
        return pl.BlockSpec((1, CONV_HALO, w), lambda b, i: (b, jnp.minimum((i + 1) * hb, n_hb - 1), col))

    vec = pl.BlockSpec((1, w), lambda b, i: (0, 0))
    return pl.pallas_call(
        _conv_kernel,
        grid=(bsz, n // tm),
        in_specs=[main(0), main(1), prev(0), prev(1), nxt(0), nxt(1),
                  pl.BlockSpec((CONV_WIDTH, w), lambda b, i: (0, 0)), vec, vec, vec],
        out_specs=pl.BlockSpec((1, tm, w), lambda b, i: (b, i, 0)),
        out_shape=jax.ShapeDtypeStruct((bsz, n, w), BF16),
        scratch_shapes=[pltpu.VMEM((tm + 2 * CONV_HALO, w), F32)],
        compiler_params=_params("arbitrary", "arbitrary"),
        name="conformer_conv",
    )(zcv, zcv, zcv, zcv, zcv, zcv, conv_w, conv_b[None, :], ln_g[None, :], ln_b[None, :])


FOURIER_N2 = 128
FOURIER_UNROLL = 8
FOURIER_KB = 8


def _np_split(m):
    m = jnp.asarray(m, F32)
    hi = m.astype(BF16)
    return hi, (m - hi.astype(F32)).astype(BF16)


def _channel_dft_matrix():
    c = np.arange(HEAD_DIM)
    ang = 2.0 * np.pi * ((c[:, None] * c[None, :]) % HEAD_DIM) / HEAD_DIM
    eye = np.eye(N_GROUPS)
    return np.concatenate([np.kron(eye, np.cos(ang)), np.kron(eye, np.sin(ang))], axis=0)


def _fourier_stage1_kernel(xa_ref, xb_ref, mh_ref, ml_ref, o_ref, *, n1, n2):
    def one_fast_index(f, carry):
        rows = pl.ds(f, n1, stride=n2)
        x = jnp.concatenate([xa_ref[0, rows, :], xb_ref[0, rows, :]], axis=1)
        x_hi, x_lo = _split_bf16(x)
        res = _dot3(mh_ref[0, f], ml_ref[0, f], x_hi, x_lo)
        o_ref[0, 0, 0, rows, :] = res[:, :LANES]
        o_ref[0, 0, 1, rows, :] = res[:, LANES:]
        return carry

    lax.fori_loop(0, n2, one_fast_index, 0, unroll=FOURIER_UNROLL)


def _fourier_stage2_kernel(br_ref, bi_ref, m2h_ref, m2l_ref, mdh_ref, mdl_ref, o_ref, *, scale):
    for k in range(br_ref.shape[3]):
        b_re = jnp.concatenate([br_ref[0, 0, 0, k], br_ref[0, 0, 1, k]], axis=1)
        b_im = jnp.concatenate([bi_ref[0, 0, 0, k], bi_ref[0, 0, 1, k]], axis=1)
        b_hi, b_lo = _split_bf16(jnp.concatenate([b_re, b_im], axis=0))
        xs = _dot3(m2h_ref[...], m2l_ref[...], b_hi, b_lo)
        x_hi, x_lo = _split_bf16(jnp.concatenate([xs[:FOURIER_N2], xs[FOURIER_N2:]], axis=1))
        o_ref[0, :, k, :] = _dot3(x_hi, x_lo, mdh_ref[...], mdl_ref[...]) * scale


def _fourier_direct_kernel(x_ref, mph_ref, mpl_ref, mdh_ref, mdl_ref, o_ref, *, scale):
    n = x_ref.shape[1]
    x_hi, x_lo = _split_bf16(x_ref[0])
    p = _dot3(mph_ref[...], mpl_ref[...], x_hi, x_lo)
    p_hi, p_lo = _split_bf16(jnp.concatenate([p[:n], p[n:]], axis=1))
    o_ref[0] = (_dot3(p_hi, p_lo, mdh_ref[...], mdl_ref[...]) * scale).astype(o_ref.dtype)


def fourier_mix(zb):
    bsz, n, w = zb.shape
    scale = float(1.0 / np.sqrt(n * HEAD_DIM))
    mdh, mdl = _np_split(_channel_dft_matrix())
    md_spec2 = pl.BlockSpec((2 * w, w), lambda b, j: (0, 0))
    if n <= 2 * FOURIER_N2:
        t = np.arange(n)
        ang = 2.0 * np.pi * ((t[:, None] * t[None, :]) % n) / n
        mph, mpl = _np_split(np.concatenate([np.cos(ang), -np.sin(ang)], axis=0))
        return pl.pallas_call(
            functools.partial(_fourier_direct_kernel, scale=scale),
            grid=(bsz,),
            in_specs=[pl.BlockSpec((1, n, w), lambda b: (b, 0, 0)),
                      pl.BlockSpec((2 * n, n), lambda b: (0, 0)), pl.BlockSpec((2 * n, n), lambda b: (0, 0)),
                      pl.BlockSpec((2 * w, w), lambda b: (0, 0)), pl.BlockSpec((2 * w, w), lambda b: (0, 0))],
            out_specs=pl.BlockSpec((1, n, w), lambda b: (b, 0, 0)),
            out_shape=jax.ShapeDtypeStruct((bsz, n, w), BF16),
            compiler_params=_params("arbitrary"),
            name="fourier_direct",
        )(zb, mph, mpl, mdh, mdl)

    n1, n2 = n // FOURIER_N2, FOURIER_N2
    f, k1, s = np.arange(n2), np.arange(n1), np.arange(n1)
    ang1 = 2.0 * np.pi * ((k1[None, :, None] * (f[:, None, None] + n2 * s[None, None, :])) % n) / n
    m1h, m1l = _np_split(np.stack([np.cos(ang1), -np.sin(ang1)]))
    k2 = np.arange(n2)
    ang2 = 2.0 * np.pi * ((k2[:, None] * f[None, :]) % n2) / n2
    c2, s2 = np.cos(ang2), np.sin(ang2)
    m2h, m2l = _np_split(np.block([[c2, s2], [-s2, c2]]))

    m1_spec = pl.BlockSpec((1, n2, n1, n1), lambda b, p: (p, 0, 0, 0))
    n_half = w // LANES
    b_st = pl.pallas_call(
        functools.partial(_fourier_stage1_kernel, n1=n1, n2=n2),
        grid=(bsz, 2),
        in_specs=[pl.BlockSpec((1, n, LANES), lambda b, p: (b, 0, 0)),
                  pl.BlockSpec((1, n, LANES), lambda b, p: (b, 0, 1)), m1_spec, m1_spec],
        out_specs=pl.BlockSpec((1, 1, n_half, n, LANES), lambda b, p: (b, p, 0, 0, 0)),
        out_shape=jax.ShapeDtypeStruct((bsz, 2, n_half, n, LANES), F32),
        compiler_params=_params("arbitrary", "arbitrary"),
        name="fourier_stage1",
    )(zb, zb, m1h, m1l)

    kb = FOURIER_KB
    b_st = b_st.reshape(bsz, 2, n_half, n1, n2, LANES)
    out = pl.pallas_call(
        functools.partial(_fourier_stage2_kernel, scale=scale),
        grid=(bsz, n1 // kb),
        in_specs=[pl.BlockSpec((1, 1, n_half, kb, n2, LANES), lambda b, j: (b, 0, 0, j, 0, 0)),
                  pl.BlockSpec((1, 1, n_half, kb, n2, LANES), lambda b, j: (b, 1, 0, j, 0, 0)),
                  pl.BlockSpec((2 * n2, 2 * n2), lambda b, j: (0, 0)),
                  pl.BlockSpec((2 * n2, 2 * n2), lambda b, j: (0, 0)),
                  md_spec2, md_spec2],
        out_specs=pl.BlockSpec((1, n2, kb, w), lambda b, j: (b, 0, j, 0)),
        out_shape=jax.ShapeDtypeStruct((bsz, n2, n1, w), F32),
        compiler_params=_params("arbitrary", "arbitrary"),
        name="fourier_stage2",
    )(b_st, b_st, m2h, m2l, mdh, mdl)
    return out.reshape(bsz, n, w)


def expert_route(h, aff_t):
    n = h.shape[1]
    cap = EC_CAPACITY * n // N_EXPERTS
    slot, off = expert_topk(aff_t, cap)
    xe, gate = expert_gather(h, slot, aff_t, off, max(cap, ROUTE_WIN))
    return xe, gate, jnp.swapaxes(slot, 1, 2), off


def kernel(x, c, ctx, c_ctx, w_mod, b_mod, norm1_g, norm2_g, w_in, sgu_ln_g, sgu_ln_b, w_spatial, b_spatial,
           w_a_out, w_b_out, conv_w, conv_b, conv_ln_g, conv_ln_b, w_c_out, rpb, w_d_out, w_out, w_router,
           w_gate_e, w_up_e, w_down_e, final_norm_g):
    bsz = x.shape[0]
    xc = ctx
    assert bsz + 1 <= MOD_ROWS
    cond = jnp.concatenate([c, c_ctx[None, :], jnp.zeros((MOD_ROWS - bsz - 1, D_MODEL), F32)], axis=0)
    mod_all = modulation(cond, w_mod, b_mod)
    in_scale = jnp.concatenate([jnp.ones((V_END,), F32), jnp.full((IN_COLS - V_END,), GATE_HALF, F32)])
    w_in_b = (w_in * in_scale).astype(BF16)
    for l in range(DEPTH):
        last = l == DEPTH - 1
        sh1, sc1, gt1, sh2, sc2, gt2 = jnp.split(mod_all[l, :bsz, None, :], 6, axis=-1)
        cmod = jnp.broadcast_to(mod_all[l, bsz:bsz + 1, None, :], (bsz, 1, 6 * D_MODEL))
        csh1, csc1, cgt1, csh2, csc2, cgt2 = jnp.split(cmod, 6, axis=-1)

        wa, wb, wc, wd_ = ((w * GATE_HALF).astype(BF16) for w in (w_a_out[l], w_b_out[l], w_c_out[l], w_d_out[l]))
        wo = w_out[l].astype(BF16)
        g1 = norm1_g[l][None, :]

        tm_x, tm_c = _row_tile(x.shape[1], PROJ_TM), _row_tile(xc.shape[1], PROJ_TM)
        tmix_x, tmix_c = _row_tile(x.shape[1], MIXER_TM), _row_tile(xc.shape[1], MIXER_TM)
        za, zb, zcv, zqkv, zg = norm_inproj(x, g1, sc1, sh1, w_in_b, l, 0, IN_WIDTHS, IN_DTYPES, tm=tm_x)
        if last:
            (zqkv_c,) = norm_inproj(xc, g1, csc1, csh1, w_in_b, l, Q_END, (2 * BRANCH_WIDTH,), (BF16,), tm=tm_c)
            kc_col, vc_col = 0, 1
        else:
            cza, czb, czcv, zqkv_c, czg = norm_inproj(xc, g1, csc1, csh1, w_in_b, l, 0, IN_WIDTHS, IN_DTYPES,
                                                      tm=tm_c)
            kc_col, vc_col = 1, 2
        d_lat = neighbourhood_attention(zqkv, zqkv_c, rpb[l], q_col=0, k_col=1, v_col=2,
                                        kc_col=kc_col, vc_col=vc_col)
        a_lat = chunk_gmlp(za, sgu_ln_g[l], sgu_ln_b[l], w_spatial[l], b_spatial[l], tm=tmix_x)
        b_lat = fourier_mix(zb)
        c_lat = conformer_conv(zcv, conv_w[l], conv_b[l], conv_ln_g[l], conv_ln_b[l], tm=tmix_x)
        g2 = norm2_g[l][None, :]
        router_t = w_router[l].T
        x, h2, aff_t = merge_branches(a_lat, b_lat, c_lat, d_lat, zg, wa, wb, wc, wd_, wo, x, gt1,
                                      g2, sc2, sh2, router_t, tm=tm_x)
        if not last:
            a_c = chunk_gmlp(cza, sgu_ln_g[l], sgu_ln_b[l], w_spatial[l], b_spatial[l], tm=tmix_c)
            b_c = fourier_mix(czb)
            c_c = conformer_conv(czcv, conv_w[l], conv_b[l], conv_ln_g[l], conv_ln_b[l], tm=tmix_c)
            d_c = context_attention(zqkv_c, q_col=0, k_col=1, v_col=2)
            xc, hc2, aff_tc = merge_branches(a_c, b_c, c_c, d_c, czg, wa, wb, wc, wd_, wo, xc, cgt1,
                                             g2, csc2, csh2, router_t, tm=tm_c)

        xe, gate, slot_t, off = expert_route(h2, aff_t)
        if last:
            (y,) = expert_ffn([xe], [gate], w_gate_e, w_up_e, w_down_e, l)
        else:
            xe_c, gate_c, slot_tc, off_c = expert_route(hc2, aff_tc)
            y, y_c = expert_ffn([xe, xe_c], [gate, gate_c], w_gate_e, w_up_e, w_down_e, l)
            xc = expert_combine(y_c, slot_tc, off_c, xc, cgt2)
        x = expert_combine(y, slot_t, off, x, gt2, final_norm_g if last else None)
    return x
```

```python
import functools

import jax
import jax.numpy as jnp
import numpy as np
from jax import lax
from jax.experimental import pallas as pl
from jax.experimental.pallas import tpu as pltpu

D_MODEL = 1024
DEPTH = 2
GRID_W = 64
HEAD_DIM = 64
N_GROUPS = 4
BRANCH_WIDTH = N_GROUPS * HEAD_DIM
N_BRANCH = 4
CHUNK = 128
CONV_WIDTH = 31
WIN_ROWS = 8
WIN_COLS = 16
ROPE_BASE = 10000.0
N_EXPERTS = 16
EC_CAPACITY = 2
EPS = 1e-6
NEG_INF = -1e30
A_END = 2 * BRANCH_WIDTH
B_END = A_END + BRANCH_WIDTH
C_END = B_END + 2 * BRANCH_WIDTH
Q_END = C_END + BRANCH_WIDTH
K_END = Q_END + BRANCH_WIDTH
V_END = K_END + BRANCH_WIDTH
IN_COLS = V_END + N_BRANCH * D_MODEL
IN_WIDTHS = (A_END, B_END - A_END, C_END - B_END, V_END - C_END, IN_COLS - V_END)
IN_DTYPES = (jnp.float32, jnp.float32, jnp.float32, jnp.bfloat16, jnp.bfloat16)
GATE_HALF = 0.5

VMEM_LIMIT_BYTES = 56 * 1024 * 1024
F32 = jnp.float32
BF16 = jnp.bfloat16


PROJ_TM = 512
MIXER_TM = 1024


def _row_tile(n, target):
    tm = min(n, target)
    assert n % tm == 0
    return tm


def _sigmoid(v):
    return 0.5 * jnp.tanh(0.5 * v) + 0.5


def _silu(v):
    return v * _sigmoid(v)


def _params(*sem):
    return pltpu.CompilerParams(dimension_semantics=sem, vmem_limit_bytes=VMEM_LIMIT_BYTES)


MOD_ROWS = 8
MOD_TN = 1536


def _split_bf16(v):
    hi = v.astype(BF16)
    return hi, (v - hi.astype(F32)).astype(BF16)


def _dot3(a_hi, a_lo, b_hi, b_lo):
    return (jnp.dot(a_hi, b_hi, preferred_element_type=F32) + jnp.dot(a_hi, b_lo, preferred_element_type=F32)
            + jnp.dot(a_lo, b_hi, preferred_element_type=F32))


def _mod_kernel(c_ref, w_ref, b_ref, o_ref):
    s_hi, s_lo = _split_bf16(_silu(c_ref[...]))
    w_hi, w_lo = _split_bf16(w_ref[0])
    o_ref[0] = _dot3(s_hi, s_lo, w_hi, w_lo) + b_ref[0]


def modulation(cond, w_mod, b_mod):
    n_layers, d, cols = w_mod.shape
    return pl.pallas_call(
        _mod_kernel,
        grid=(n_layers, cols // MOD_TN),
        in_specs=[pl.BlockSpec((MOD_ROWS, d), lambda l, j: (0, 0)),
                  pl.BlockSpec((1, d, MOD_TN), lambda l, j: (l, 0, j)),
                  pl.BlockSpec((1, 1, MOD_TN), lambda l, j: (l, 0, j))],
        out_specs=pl.BlockSpec((1, MOD_ROWS, MOD_TN), lambda l, j: (l, 0, j)),
        out_shape=jax.ShapeDtypeStruct((n_layers, MOD_ROWS, cols), F32),
        compiler_params=_params("arbitrary", "arbitrary"),
        name="modulation",
    )(cond, w_mod, b_mod[:, None, :])


INPROJ_COL_CHUNK = 512


def _norm_inproj_kernel(x_ref, g_ref, sc_ref, sh_ref, w_ref, *o_refs):
    x = x_ref[0]
    y = x * lax.rsqrt(jnp.mean(x * x, axis=-1, keepdims=True) + EPS) * g_ref[...]
    h = (y * (1.0 + sc_ref[0]) + sh_ref[0]).astype(BF16)
    off = 0
    for o_ref in o_refs:
        width = o_ref.shape[2]
        for c0 in range(0, width, INPROJ_COL_CHUNK):
            cw = min(INPROJ_COL_CHUNK, width - c0)
            o_ref[0, :, c0:c0 + cw] = jnp.dot(h, w_ref[:, off + c0:off + c0 + cw],
                                              preferred_element_type=F32).astype(o_ref.dtype)
        off += width


def norm_inproj(x, g, sc, sh, w, l, col0, widths, dtypes, *, tm):
    bsz, n, d = x.shape
    cols = sum(widths)
    assert col0 % cols == 0
    return pl.pallas_call(
        _norm_inproj_kernel,
        grid=(bsz, n // tm),
        in_specs=[
            pl.BlockSpec((1, tm, d), lambda b, i: (b, i, 0)),
            pl.BlockSpec((1, d), lambda b, i: (0, 0)),
            pl.BlockSpec((1, 1, d), lambda b, i: (b, 0, 0)),
            pl.BlockSpec((1, 1, d), lambda b, i: (b, 0, 0)),
            pl.BlockSpec((None, d, cols), lambda b, i: (l, 0, col0 // cols), pipeline_mode=pl.Buffered(1)),
        ],
        out_specs=[pl.BlockSpec((1, tm, wd), lambda b, i: (b, i, 0)) for wd in widths],
        out_shape=[jax.ShapeDtypeStruct((bsz, n, wd), dt) for wd, dt in zip(widths, dtypes, strict=True)],
        compiler_params=_params("arbitrary", "arbitrary"),
        name="norm_inproj",
    )(x, g, sc, sh, w)


def _merge_kernel(a_ref, b_ref, c_ref, d_ref, ga_ref, gb_ref, gc_ref, gd_ref,
                  wa_ref, wb_ref, wc_ref, wd_ref, wo_ref, x_ref, gt_ref,
                  g2_ref, sc2_ref, sh2_ref, rt_ref, o_ref, h_ref, aff_ref):
    m = None
    for br, gz, w in ((a_ref, ga_ref, wa_ref), (b_ref, gb_ref, wb_ref),
                      (c_ref, gc_ref, wc_ref), (d_ref, gd_ref, wd_ref)):
        hp = jnp.dot(br[0].astype(BF16), w[...], preferred_element_type=F32)
        t = hp * jnp.tanh(gz[0].astype(F32)) + hp
        m = t if m is None else m + t
    mix = jnp.dot(m.astype(BF16), wo_ref[...], preferred_element_type=F32)
    x = x_ref[0] + gt_ref[0] * mix
    o_ref[0] = x
    y = x * lax.rsqrt(jnp.mean(x * x, axis=-1, keepdims=True) + EPS) * g2_ref[...]
    h = y * (1.0 + sc2_ref[0]) + sh2_ref[0]
    h_hi, h_lo = _split_bf16(h)
    r_hi, r_lo = _split_bf16(rt_ref[...])
    ne = rt_ref.shape[0]
    both = lax.dot_general(jnp.concatenate([r_hi, r_lo], axis=0), h_hi, _NT, preferred_element_type=F32)
    logits = both[:ne] + both[ne:] + lax.dot_general(r_hi, h_lo, _NT, preferred_element_type=F32)
    pr = jnp.exp(logits - jnp.max(logits, axis=0, keepdims=True))
    aff_ref[0] = pr / jnp.sum(pr, axis=0, keepdims=True)
    h_ref[0] = h_hi


def merge_branches(a, b, cc, d, z, w_a, w_b, w_c, w_d, w_o, x, gt, g2, sc2, sh2, router_t, *, tm):
    bsz, n, dm = x.shape
    w = a.shape[-1]
    ne = router_t.shape[0]
    br_spec = pl.BlockSpec((1, tm, w), lambda bi, i: (bi, i, 0))
    gate_specs = [pl.BlockSpec((1, tm, dm), functools.partial(lambda bi, i, k: (bi, i, k), k=k))
                  for k in range(N_BRANCH)]
    wbr_spec = pl.BlockSpec((w, dm), lambda bi, i: (0, 0))
    row_spec = pl.BlockSpec((1, tm, dm), lambda bi, i: (bi, i, 0))
    mod_spec = pl.BlockSpec((1, 1, dm), lambda bi, i: (bi, 0, 0))
    return pl.pallas_call(
        _merge_kernel,
        grid=(bsz, n // tm),
        in_specs=[br_spec] * 4 + gate_specs + [wbr_spec] * 4 + [
            pl.BlockSpec((dm, dm), lambda bi, i: (0, 0)), row_spec, mod_spec,
            pl.BlockSpec((1, dm), lambda bi, i: (0, 0)), mod_spec, mod_spec,
            pl.BlockSpec((ne, dm), lambda bi, i: (0, 0)),
        ],
        out_specs=[row_spec, row_spec, pl.BlockSpec((1, ne, tm), lambda bi, i: (bi, 0, i))],
        out_shape=[jax.ShapeDtypeStruct((bsz, n, dm), F32), jax.ShapeDtypeStruct((bsz, n, dm), BF16),
                   jax.ShapeDtypeStruct((bsz, ne, n), F32)],
        compiler_params=_params("arbitrary", "arbitrary"),
        name="merge_branches",
    )(a, b, cc, d, z, z, z, z, w_a, w_b, w_c, w_d, w_o, x, gt, g2, sc2, sh2, router_t)


ROUTE_TB = 256
ROUTE_WIN = 64
ROUTE_ALIGN = 16
GATHER_SUB = 8
COMBINE_SUB = 2
OFF_LANES = 128
TOPK_EXP_STEPS = (64, 32, 16, 8, 4, 2, 1)
TOPK_BISECT_STEPS = 32


def _topk_kernel(aff_ref, slot_ref, off_ref, *, cap):
    a = aff_ref[0]
    ne, n = a.shape
    capf = jnp.float32(cap)

    def count_ge(t):
        return jnp.sum(jnp.where(a >= t, 1.0, 0.0), axis=1, keepdims=True)

    hi = jnp.full((ne, 1), 2.0, F32)
    for s in TOPK_EXP_STEPS:
        cand = hi * (2.0 ** -s)
        hi = jnp.where(count_ge(cand) < capf, cand, hi)
    lo = jnp.where(hi <= 2.0 ** -126, 0.0, hi * 0.5)

    def bisect(_, carry):
        lo, hi = carry
        mid = 0.5 * (lo + hi)
        ok = count_ge(mid) >= capf
        return jnp.where(ok, mid, lo), jnp.where(ok, hi, mid)

    lo, hi = lax.fori_loop(0, TOPK_BISECT_STEPS, bisect, (lo, hi))
    need = capf - count_ge(hi)
    r_i = lax.broadcasted_iota(jnp.int32, (ROUTE_TB, ROUTE_TB), 0)
    c_i = lax.broadcasted_iota(jnp.int32, (ROUTE_TB, ROUTE_TB), 1)
    tri = jnp.where(r_i < c_i, 1.0, 0.0).astype(BF16)
    lane = lax.broadcasted_iota(jnp.int32, (ne, OFF_LANES), 1)
    run_eq = jnp.zeros((ne, 1), F32)
    run_sel = jnp.zeros((ne, 1), F32)
    offs = jnp.zeros((ne, OFF_LANES), F32)
    for c in range(n // ROUTE_TB):
        cols = slice(c * ROUTE_TB, (c + 1) * ROUTE_TB)
        a_c = a[:, cols]
        above = a_c >= hi
        eq_c = jnp.where(above, 0.0, jnp.where(a_c >= lo, 1.0, 0.0))
        rank = jnp.dot(eq_c.astype(BF16), tri, preferred_element_type=F32) + run_eq
        sel = jnp.where(above, 1.0, jnp.where(rank < need, eq_c, 0.0))
        pos = jnp.dot(sel.astype(BF16), tri, preferred_element_type=F32) + run_sel
        slot_ref[0, :, cols] = jnp.where(sel > 0.0, pos, -1.0).astype(jnp.int32)
        offs = jnp.where(lane == c, run_sel, offs)
        run_eq = run_eq + jnp.sum(eq_c, axis=1, keepdims=True)
        run_sel = run_sel + jnp.sum(sel, axis=1, keepdims=True)
    offs = jnp.where(lane == n // ROUTE_TB, run_sel, offs)
    off_ref[0] = offs.astype(jnp.int32)


def expert_topk(aff_t, cap):
    bsz, ne, n = aff_t.shape
    return pl.pallas_call(
        functools.partial(_topk_kernel, cap=cap),
        grid=(bsz,),
        in_specs=[pl.BlockSpec((1, ne, n), lambda b: (b, 0, 0))],
        out_specs=[pl.BlockSpec((1, ne, n), lambda b: (b, 0, 0)),
                   pl.BlockSpec((1, ne, OFF_LANES), lambda b: (b, 0, 0))],
        out_shape=[jax.ShapeDtypeStruct((bsz, ne, n), jnp.int32),
                   jax.ShapeDtypeStruct((bsz, ne, OFF_LANES), jnp.int32)],
        compiler_params=_params("arbitrary"),
        name="expert_topk",
    )(aff_t)


def _window_plan(off_ref, b, e, tb, w, rows_max):
    start = off_ref[b, e, tb]
    stop = off_ref[b, e, tb + 1]
    first = (start // ROUTE_ALIGN) * ROUTE_ALIGN + w * ROUTE_WIN
    active = first < stop
    return pl.multiple_of(jnp.minimum(first, rows_max), ROUTE_ALIGN), first, active


def _num_passes(off_ref, b, e_list, tb):
    n_pass = jnp.int32(0)
    for e in e_list:
        start = off_ref[b, e, tb]
        stop = off_ref[b, e, tb + 1]
        base = (start // ROUTE_ALIGN) * ROUTE_ALIGN
        n_pass = jnp.maximum(n_pass, (stop - base + ROUTE_WIN - 1) // ROUTE_WIN)
    return n_pass


def _gather_kernel(off_ref, h_ref, slot_ref, aff_ref, xe_ref, gate_ref, *, group):
    b, g, step = pl.program_id(0), pl.program_id(1), pl.program_id(2)
    rows_max = xe_ref.shape[2] - ROUTE_WIN
    n_sub = h_ref.shape[1] // ROUTE_TB

    @pl.when(step == 0)
    def _():
        xe_ref[...] = jnp.zeros(xe_ref.shape, xe_ref.dtype)
        gate_ref[...] = jnp.zeros(gate_ref.shape, gate_ref.dtype)

    sub = lax.broadcasted_iota(jnp.int32, (ROUTE_WIN, 1), 0)
    experts = [g * group + e for e in range(group)]

    for j in range(n_sub):
        tb = step * n_sub + j
        tok = slice(j * ROUTE_TB, (j + 1) * ROUTE_TB)

        def one_pass(w, carry, tb=tb, tok=tok):
            plans = [_window_plan(off_ref, b, ge, tb, w, rows_max) for ge in experts]
            hots = []
            for e, (rows, first, active) in enumerate(plans):
                ids = rows + sub
                want = jnp.where(active & (ids >= first), ids, -2)
                hots.append(slot_ref[0, e:e + 1, tok] == want)
            onehot = jnp.concatenate([jnp.where(hm, 1.0, 0.0) for hm in hots], axis=0).astype(BF16)
            res = jnp.dot(onehot, h_ref[0, tok, :], preferred_element_type=F32)
            for e, (rows, _, _) in enumerate(plans):
                win = pl.ds(rows, ROUTE_WIN)
                xe_ref[0, e, win, :] = xe_ref[0, e, win, :] + res[e * ROUTE_WIN:(e + 1) * ROUTE_WIN].astype(BF16)
                gsel = jnp.sum(jnp.where(hots[e], aff_ref[0, e:e + 1, tok], 0.0), axis=1, keepdims=True)
                gate_ref[0, e, win, :] = gate_ref[0, e, win, :] + gsel
            return carry

        lax.fori_loop(0, _num_passes(off_ref, b, experts, tb), one_pass, 0)


def expert_gather(h, slot, aff_t, off, cap_pad, *, group=8):
    bsz, n, d = h.shape
    ne = slot.shape[1]
    tok = ROUTE_TB * min(GATHER_SUB, n // ROUTE_TB)
    grid_spec = pltpu.PrefetchScalarGridSpec(
        num_scalar_prefetch=1,
        grid=(bsz, ne // group, n // tok),
        in_specs=[
            pl.BlockSpec((1, tok, d), lambda b, g, t, off: (b, t, 0)),
            pl.BlockSpec((1, group, tok), lambda b, g, t, off: (b, g, t)),
            pl.BlockSpec((1, group, tok), lambda b, g, t, off: (b, g, t)),
        ],
        out_specs=[pl.BlockSpec((1, group, cap_pad, d), lambda b, g, t, off: (b, g, 0, 0)),
                   pl.BlockSpec((1, group, cap_pad, 1), lambda b, g, t, off: (b, g, 0, 0))],
    )
    return pl.pallas_call(
        functools.partial(_gather_kernel, group=group),
        grid_spec=grid_spec,
        out_shape=[jax.ShapeDtypeStruct((bsz, ne, cap_pad, d), BF16),
                   jax.ShapeDtypeStruct((bsz, ne, cap_pad, 1), F32)],
        compiler_params=_params("arbitrary", "arbitrary", "arbitrary"),
        name="expert_gather",
    )(off, h, slot, aff_t)


def _combine_kernel(off_ref, y_ref, slot_t_ref, x_ref, gt_ref, fg_ref, o_ref, stage_ref, acc_ref):
    b, step = pl.program_id(0), pl.program_id(1)
    ne = y_ref.shape[1]
    rows_max = y_ref.shape[2] - ROUTE_WIN
    width = ne * ROUTE_WIN
    n_sub = x_ref.shape[1] // ROUTE_TB
    experts = list(range(ne))
    lane = lax.broadcasted_iota(jnp.int32, (1, width), 1)
    e_row = lax.broadcasted_iota(jnp.int32, (ne, width), 0)
    e_lane = lax.broadcasted_iota(jnp.int32, (ne, width), 1) // ROUTE_WIN
    expand = jnp.where(e_row == e_lane, 1.0, 0.0).astype(BF16)

    for j in range(n_sub):
        tb = step * n_sub + j
        tok = slice(j * ROUTE_TB, (j + 1) * ROUTE_TB)
        s1 = slot_t_ref[0, tok, :] + 1
        hi = (s1 // ROUTE_WIN).astype(F32).astype(BF16)
        lo = (s1 % ROUTE_WIN).astype(F32).astype(BF16)
        s1_wide = (float(ROUTE_WIN) * jnp.dot(hi, expand, preferred_element_type=F32)
                   + jnp.dot(lo, expand, preferred_element_type=F32))
        acc_ref[...] = jnp.zeros(acc_ref.shape, F32)

        def one_pass(w, carry, tb=tb, s1_wide=s1_wide):
            want = jnp.full((1, width), -1, jnp.int32)
            for e in experts:
                rows, first, active = _window_plan(off_ref, b, e, tb, w, rows_max)
                stage_ref[e * ROUTE_WIN:(e + 1) * ROUTE_WIN, :] = y_ref[0, e, pl.ds(rows, ROUTE_WIN), :]
                ids = rows + lane % ROUTE_WIN
                want = jnp.where((lane // ROUTE_WIN == e) & active & (ids >= first), ids + 1, want)
            onehot = jnp.where(s1_wide == want.astype(F32), 1.0, 0.0).astype(BF16)
            acc_ref[...] += jnp.dot(onehot, stage_ref[...], preferred_element_type=F32)
            return carry

        lax.fori_loop(0, _num_passes(off_ref, b, experts, tb), one_pass, 0)
        out = x_ref[0, tok, :] + gt_ref[0] * acc_ref[...]
        if fg_ref is not None:
            out = out * lax.rsqrt(jnp.mean(out * out, axis=-1, keepdims=True) + EPS) * fg_ref[...]
        o_ref[0, tok, :] = out


def _combine_kernel_plain(off_ref, y_ref, slot_t_ref, x_ref, gt_ref, o_ref, stage_ref, acc_ref):
    _combine_kernel(off_ref, y_ref, slot_t_ref, x_ref, gt_ref, None, o_ref, stage_ref, acc_ref)


def expert_combine(y, slot_t, off, x, gt, final_g=None):
    bsz, n, d = x.shape
    ne, cap_pad = y.shape[1], y.shape[2]
    tok = ROUTE_TB * min(COMBINE_SUB, n // ROUTE_TB)
    grid_spec = pltpu.PrefetchScalarGridSpec(
        num_scalar_prefetch=1,
        grid=(bsz, n // tok),
        in_specs=[
            pl.BlockSpec((1, ne, cap_pad, d), lambda b, t, off: (b, 0, 0, 0), pipeline_mode=pl.Buffered(1)),
            pl.BlockSpec((1, tok, ne), lambda b, t, off: (b, t, 0)),
            pl.BlockSpec((1, tok, d), lambda b, t, off: (b, t, 0)),
            pl.BlockSpec((1, 1, d), lambda b, t, off: (b, 0, 0)),
        ] + ([] if final_g is None else [pl.BlockSpec((1, d), lambda b, t, off: (0, 0))]),
        out_specs=pl.BlockSpec((1, tok, d), lambda b, t, off: (b, t, 0)),
        scratch_shapes=[pltpu.VMEM((ne * ROUTE_WIN, d), BF16), pltpu.VMEM((ROUTE_TB, d), F32)],
    )
    args = (off, y, slot_t, x, gt) + (() if final_g is None else (final_g[None, :],))
    return pl.pallas_call(
        _combine_kernel_plain if final_g is None else _combine_kernel,
        grid_spec=grid_spec,
        out_shape=jax.ShapeDtypeStruct((bsz, n, d), F32),
        compiler_params=_params("arbitrary", "arbitrary"),
        name="expert_combine",
    )(*args)


FFN_FBLK = 512
FFN_MIN_ROWS = 256


def _expert_ffn_kernel(*refs, n_sets, n_f):
    xe_refs, gate_refs = refs[:n_sets], refs[n_sets:2 * n_sets]
    wg_ref, wu_ref, wd_ref = refs[2 * n_sets:2 * n_sets + 3]
    o_refs = refs[2 * n_sets + 3:3 * n_sets + 3]
    acc_refs = refs[3 * n_sets + 3:]
    f = pl.program_id(1)
    blocks = [(s, b) for s in range(n_sets) for b in range(xe_refs[s].shape[0])]
    big = [sb for sb in blocks if xe_refs[sb[0]].shape[2] >= FFN_MIN_ROWS]
    small = [sb for sb in blocks if sb not in big]
    groups = [[sb] for sb in big[:-1]] + [big[-1:] + small] if big else [small]

    def hidden_block(first, last):
        wg = wg_ref[0].astype(BF16)
        wu = wu_ref[0].astype(BF16)
        wd = wd_ref[0].astype(BF16)
        for group in groups:
            xs = [xe_refs[s][b, 0] for s, b in group]
            xe = xs[0] if len(xs) == 1 else jnp.concatenate(xs, axis=0)
            a = jnp.dot(xe, wg, preferred_element_type=F32)
            u = jnp.dot(xe, wu, preferred_element_type=F32)
            t = jnp.dot((_silu(a) * u).astype(BF16), wd, preferred_element_type=F32)
            r0 = 0
            for s, b in group:
                rows = xe_refs[s].shape[2]
                part = t[r0:r0 + rows] if first else acc_refs[s][b] + t[r0:r0 + rows]
                if last:
                    o_refs[s][b, 0] = (part * gate_refs[s][b, 0]).astype(o_refs[s].dtype)
                else:
                    acc_refs[s][b] = part
                r0 += rows

    if n_f == 1:
        hidden_block(True, True)
        return
    pl.when(f == 0)(lambda: hidden_block(True, False))
    if n_f > 2:
        pl.when((f > 0) & (f < n_f - 1))(lambda: hidden_block(False, False))
    pl.when(f == n_f - 1)(lambda: hidden_block(False, True))


def expert_ffn(xes, gates, wg, wu, wd, l):
    n_sets = len(xes)
    _, ne, d, ff = wg.shape
    xe_specs = [pl.BlockSpec((xe.shape[0], 1) + xe.shape[2:], lambda e, f: (0, e, 0, 0)) for xe in xes]
    gate_specs = [pl.BlockSpec((g.shape[0], 1) + g.shape[2:], lambda e, f: (0, e, 0, 0)) for g in gates]
    return pl.pallas_call(
        functools.partial(_expert_ffn_kernel, n_sets=n_sets, n_f=ff // FFN_FBLK),
        grid=(ne, ff // FFN_FBLK),
        in_specs=xe_specs + gate_specs + [
            pl.BlockSpec((None, 1, d, FFN_FBLK), lambda e, f: (l, e, 0, f)),
            pl.BlockSpec((None, 1, d, FFN_FBLK), lambda e, f: (l, e, 0, f)),
            pl.BlockSpec((None, 1, FFN_FBLK, d), lambda e, f: (l, e, f, 0)),
        ],
        out_specs=xe_specs,
        out_shape=[jax.ShapeDtypeStruct(xe.shape, BF16) for xe in xes],
        scratch_shapes=[pltpu.VMEM((xe.shape[0],) + xe.shape[2:], F32) for xe in xes],
        compiler_params=_params("arbitrary", "arbitrary"),
        name="expert_ffn",
    )(*xes, *gates, wg, wu, wd)


ATT_QROWS = 4
ATT_QTOK = ATT_QROWS * GRID_W
ATT_KROWS = ATT_QROWS + WIN_ROWS
ATT_KBLK = ATT_KROWS * GRID_W // ATT_QTOK
_NT = (((1,), (1,)), ((), ()))

def _rope_tables(n):
    quarter = HEAD_DIM // 4
    inv = ROPE_BASE ** (-np.arange(quarter, dtype=np.float64) / quarter)
    t = np.arange(n)
    ang_r = (t // GRID_W)[:, None] * inv[None, :]
    ang_c = (t % GRID_W)[:, None] * inv[None, :]
    cos = np.concatenate([np.cos(ang_r)] * 2 + [np.cos(ang_c)] * 2, axis=1)
    sin = np.concatenate([-np.sin(ang_r), np.sin(ang_r), -np.sin(ang_c), np.sin(ang_c)], axis=1)
    return (jnp.asarray(np.concatenate([cos, cos], axis=1), F32),
            jnp.asarray(np.concatenate([sin, sin], axis=1), F32))


def _attn_bias_tables(rpb, n_rows):
    n_blk = n_rows // ATT_QROWS
    n_dr = 2 * WIN_ROWS - 1
    qc = np.arange(GRID_W)
    kc = np.arange(GRID_W)
    c_start = np.clip(qc - WIN_COLS // 2, 0, GRID_W - WIN_COLS)
    col_ok = (kc[None, :] >= c_start[:, None]) & (kc[None, :] < c_start[:, None] + WIN_COLS)
    assert np.all(np.abs(kc[None, :] - qc[:, None])[col_ok] < WIN_COLS)
    row_sel, oks = [], []
    for j in (0, 1, n_blk - 1):
        ks = min(max(ATT_QROWS * j - WIN_ROWS // 2, 0), n_rows - ATT_KROWS)
        r = ATT_QROWS * j + np.arange(ATT_QROWS)
        kr0 = np.clip(r - WIN_ROWS // 2, 0, n_rows - WIN_ROWS)
        krow = ks + np.arange(ATT_KROWS)
        row_ok = (krow[None, :] >= kr0[:, None]) & (krow[None, :] < kr0[:, None] + WIN_ROWS)
        d_row = np.clip(krow[None, :] - r[:, None] + (WIN_ROWS - 1), 0, n_dr - 1)
        row_sel.append((d_row.reshape(-1)[:, None] == np.arange(n_dr)[None, :]).astype(np.float32))
        oks.append(row_ok[:, None, :, None] & col_ok[None, :, None, :])
    a = jnp.einsum('cpr,hrd->chpd', jnp.asarray(np.stack(row_sel)), rpb.astype(F32), precision=lax.Precision.HIGHEST)
    a = a.reshape(3, N_GROUPS, ATT_QROWS, ATT_KROWS, 2 * WIN_COLS - 1)
    pad = jnp.zeros(a.shape[:-1] + (GRID_W - (2 * WIN_COLS - 1),), F32)
    v = jnp.concatenate([a[..., WIN_COLS - 1:], pad, jnp.roll(a, -1, axis=3)[..., :WIN_COLS - 1]], axis=-1)
    v = v.reshape(3, N_GROUPS, ATT_QROWS, ATT_KROWS * GRID_W)
    ok = jnp.asarray(np.stack(oks).reshape(3, ATT_QTOK, ATT_KROWS * GRID_W).astype(np.float32))
    return pl.pallas_call(
        _attn_bias_kernel,
        grid=(3, N_GROUPS),
        in_specs=[pl.BlockSpec((1, 1, ATT_QROWS, ATT_KROWS * GRID_W), lambda c, h: (c, h, 0, 0)),
                  pl.BlockSpec((1, ATT_QTOK, ATT_KROWS * GRID_W), lambda c, h: (c, 0, 0))],
        out_specs=pl.BlockSpec((1, 1, ATT_QTOK, ATT_KROWS * GRID_W), lambda c, h: (c, h, 0, 0)),
        out_shape=jax.ShapeDtypeStruct((3, N_GROUPS, ATT_QTOK, ATT_KROWS * GRID_W), F32),
        compiler_params=_params("arbitrary", "arbitrary"),
        name="attn_bias",
    )(v, ok)


def _attn_bias_kernel(v_ref, ok_ref, o_ref):
    width = v_ref.shape[3]
    for qr in range(ATT_QROWS):
        rows = slice(qr * GRID_W, (qr + 1) * GRID_W)
        base = jnp.broadcast_to(v_ref[0, 0, qr:qr + 1, :], (GRID_W, width))
        slab = pltpu.roll(base, 0, 1, stride=1, stride_axis=0)
        o_ref[0, 0, rows, :] = jnp.where(ok_ref[0, rows, :] > 0.0, slab, NEG_INF)


def _rope(x, cos_ref, sin_ref, first16):
    c = cos_ref[...]
    s = sin_ref[...]
    c2 = jnp.concatenate([c, c], axis=1)
    s2 = jnp.concatenate([s, s], axis=1)
    w = x.shape[1]
    partner = jnp.where(first16, pltpu.roll(x, w - HEAD_DIM // 4, 1), pltpu.roll(x, HEAD_DIM // 4, 1))
    return x * c2 + partner * s2


ATT_HEAD_STACK = 1


def _stack_heads(x, lane, heads):
    return jnp.concatenate([jnp.where((lane // HEAD_DIM) == h, x, 0.0) for h in heads], axis=0).astype(BF16)


def _softmax_pv(scores_values, lane, heads):
    m = None
    for s, _ in scores_values:
        mx = jnp.max(s, axis=1, keepdims=True)
        m = mx if m is None else jnp.maximum(m, mx)
    den = None
    o = None
    for s, v in scores_values:
        p = jnp.exp(s - m)
        sm = jnp.sum(p, axis=1, keepdims=True)
        den = sm if den is None else den + sm
        t = jnp.dot(p.astype(BF16), v, preferred_element_type=F32)
        o = t if o is None else o + t
    o = o * (1.0 / den)
    rows = o.shape[0] // len(heads)
    out = None
    for i, h in enumerate(heads):
        t = jnp.where((lane // HEAD_DIM) == h, o[i * rows:(i + 1) * rows], 0.0)
        out = t if out is None else out + t
    return out


def _head_groups():
    return [tuple(range(h, h + ATT_HEAD_STACK)) for h in range(0, N_GROUPS, ATT_HEAD_STACK)]


ATT_QSTEP = 4
ATT_SLAB = ATT_KBLK + ATT_QSTEP - 1


def _att_key_block(jq, n_blk):
    return jnp.clip(jq - 1, 0, n_blk - ATT_KBLK)


def _nattn_kernel(*refs, n_blk):
    q_ref = refs[0]
    k_refs, v_refs = refs[1:1 + ATT_SLAB], refs[1 + ATT_SLAB:1 + 2 * ATT_SLAB]
    cq_ref, sq_ref = refs[1 + 2 * ATT_SLAB:3 + 2 * ATT_SLAB]
    ck_refs = refs[3 + 2 * ATT_SLAB:3 + 3 * ATT_SLAB]
    sk_refs = refs[3 + 3 * ATT_SLAB:3 + 4 * ATT_SLAB]
    kc_ref, vc_ref = refs[3 + 4 * ATT_SLAB:5 + 4 * ATT_SLAB]
    bias_refs = refs[5 + 4 * ATT_SLAB:5 + 4 * ATT_SLAB + ATT_QSTEP]
    o_ref, k_scr, v_scr = refs[5 + 4 * ATT_SLAB + ATT_QSTEP:]
    j = pl.program_id(1)
    w = q_ref.shape[2]
    lane = lax.broadcasted_iota(jnp.int32, (1, w), 1)
    first16 = (lane % (HEAD_DIM // 2)) < (HEAD_DIM // 4)
    q_all = q_ref[0].astype(F32) * (HEAD_DIM ** -0.5)
    q_rot_all = _rope(q_all, cq_ref, sq_ref, first16)
    for i in range(ATT_SLAB):
        rows = slice(i * ATT_QTOK, (i + 1) * ATT_QTOK)
        k_scr[rows, :] = _rope(k_refs[i][0].astype(F32), ck_refs[i], sk_refs[i], first16).astype(BF16)
        v_scr[rows, :] = v_refs[i][0].astype(BF16)
    kc = kc_ref[0].astype(BF16)
    vc = vc_ref[0].astype(BF16)
    slab0 = _att_key_block(j * ATT_QSTEP, n_blk - ATT_QSTEP + 1)
    for i in range(ATT_QSTEP):
        rows = slice(i * ATT_QTOK, (i + 1) * ATT_QTOK)
        q, q_rot = q_all[rows], q_rot_all[rows]
        start = pl.multiple_of((_att_key_block(j * ATT_QSTEP + i, n_blk) - slab0) * ATT_QTOK, ATT_QTOK)
        keys = pl.ds(start, ATT_KBLK * ATT_QTOK)
        k_rot, v = k_scr[keys, :], v_scr[keys, :]
        acc = None
        for heads in _head_groups():
            bias = jnp.concatenate([bias_refs[i][0, h] for h in heads], axis=0)
            s_win = lax.dot_general(_stack_heads(q_rot, lane, heads), k_rot, _NT, preferred_element_type=F32) + bias
            s_ctx = lax.dot_general(_stack_heads(q, lane, heads), kc, _NT, preferred_element_type=F32)
            t = _softmax_pv([(s_win, v), (s_ctx, vc)], lane, heads)
            acc = t if acc is None else acc + t
        o_ref[0, rows, :] = acc.astype(o_ref.dtype)


def neighbourhood_attention(z, zc, rpb, *, q_col, k_col, v_col, kc_col, vc_col):
    bsz, n, _ = z.shape
    n_ctx = zc.shape[1]
    w = BRANCH_WIDTH
    n_blk = n // ATT_QTOK
    assert n_blk % ATT_QSTEP == 0 and n_blk >= ATT_SLAB
    cos_t, sin_t = _rope_tables(n)
    bias = _attn_bias_tables(rpb, n // GRID_W)
    q_tok = ATT_QSTEP * ATT_QTOK

    def slab(j):
        return _att_key_block(j * ATT_QSTEP, n_blk - ATT_QSTEP + 1)

    def kspec(col, off):
        return pl.BlockSpec((1, ATT_QTOK, w), lambda b, j: (b, slab(j) + off, col))

    def tspec(off):
        return pl.BlockSpec((ATT_QTOK, 2 * HEAD_DIM), lambda b, j: (slab(j) + off, 0))

    def bias_spec(i):
        def cfg(b, j):
            jq = j * ATT_QSTEP + i
            return (jnp.minimum(jq, 1) + jq // (n_blk - 1), 0, 0, 0)
        return pl.BlockSpec((1, N_GROUPS, ATT_QTOK, ATT_KROWS * GRID_W), cfg)

    qt_spec = pl.BlockSpec((q_tok, 2 * HEAD_DIM), lambda b, j: (j, 0))
    in_specs = ([pl.BlockSpec((1, q_tok, w), lambda b, j: (b, j, q_col))]
                + [kspec(k_col, i) for i in range(ATT_SLAB)] + [kspec(v_col, i) for i in range(ATT_SLAB)]
                + [qt_spec, qt_spec] + [tspec(i) for i in range(ATT_SLAB)] * 2
                + [pl.BlockSpec((1, n_ctx, w), lambda b, j: (b, 0, kc_col)),
                   pl.BlockSpec((1, n_ctx, w), lambda b, j: (b, 0, vc_col))]
                + [bias_spec(i) for i in range(ATT_QSTEP)])
    args = ([z] * (1 + 2 * ATT_SLAB) + [cos_t, sin_t] + [cos_t] * ATT_SLAB + [sin_t] * ATT_SLAB + [zc, zc]
            + [bias] * ATT_QSTEP)
    return pl.pallas_call(
        functools.partial(_nattn_kernel, n_blk=n_blk),
        grid=(bsz, n_blk // ATT_QSTEP),
        in_specs=in_specs,
        out_specs=pl.BlockSpec((1, q_tok, w), lambda b, j: (b, j, 0)),
        out_shape=jax.ShapeDtypeStruct((bsz, n, w), BF16),
        scratch_shapes=[pltpu.VMEM((ATT_SLAB * ATT_QTOK, w), BF16), pltpu.VMEM((ATT_SLAB * ATT_QTOK, w), BF16)],
        compiler_params=_params("arbitrary", "arbitrary"),
        name="neighbourhood_attention",
    )(*args)


def _ctx_attn_kernel(q_ref, kc_ref, vc_ref, o_ref):
    w = q_ref.shape[2]
    lane = lax.broadcasted_iota(jnp.int32, (1, w), 1)
    q = q_ref[0].astype(F32) * (HEAD_DIM ** -0.5)
    kc = kc_ref[0].astype(BF16)
    vc = vc_ref[0].astype(BF16)
    acc = None
    for heads in _head_groups():
        s = lax.dot_general(_stack_heads(q, lane, heads), kc, _NT, preferred_element_type=F32)
        t = _softmax_pv([(s, vc)], lane, heads)
        acc = t if acc is None else acc + t
    o_ref[0] = acc.astype(o_ref.dtype)


def context_attention(zc, *, q_col, k_col, v_col):
    bsz, n_ctx, _ = zc.shape
    w = BRANCH_WIDTH
    return pl.pallas_call(
        _ctx_attn_kernel,
        grid=(bsz,),
        in_specs=[pl.BlockSpec((1, n_ctx, w), functools.partial(lambda b, c: (b, 0, c), c=c))
                  for c in (q_col, k_col, v_col)],
        out_specs=pl.BlockSpec((1, n_ctx, w), lambda b: (b, 0, 0)),
        out_shape=jax.ShapeDtypeStruct((bsz, n_ctx, w), BF16),
        compiler_params=_params("arbitrary"),
        name="context_attention",
    )(zc, zc, zc)


def _ln(v, g, b):
    mu = jnp.mean(v, axis=-1, keepdims=True)
    var = jnp.mean(jnp.square(v - mu), axis=-1, keepdims=True)
    return (v - mu) * lax.rsqrt(var + EPS) * g + b


def _gmlp_kernel(z_ref, lng_ref, lnb_ref, w_ref, bias_ref, o_ref):
    tm = z_ref.shape[1]
    z = jax.nn.gelu(z_ref[0])
    u = z[:, :BRANCH_WIDTH]
    v = _ln(z[:, BRANCH_WIDTH:], lng_ref[...], lnb_ref[...])
    group = lax.broadcasted_iota(jnp.int32, (1, BRANCH_WIDTH), 1) // HEAD_DIM
    wcat = w_ref[...]
    for c in range(tm // CHUNK):
        rows = slice(c * CHUNK, (c + 1) * CHUNK)
        vc = v[rows]
        vst = jnp.concatenate([jnp.where(group == g, vc, 0.0) for g in range(N_GROUPS)], axis=0).astype(BF16)
        mixed = jnp.dot(wcat, vst, preferred_element_type=F32) + bias_ref[...]
        o_ref[0, rows, :] = (u[rows] * mixed).astype(o_ref.dtype)


def chunk_gmlp(za, ln_g, ln_b, w_s, b_s, *, tm):
    bsz, n, w2 = za.shape
    w = w2 // 2
    wcat = jnp.transpose(w_s, (1, 0, 2)).reshape(CHUNK, N_GROUPS * CHUNK).astype(BF16)
    bias = jnp.repeat(b_s.T, HEAD_DIM, axis=1)
    return pl.pallas_call(
        _gmlp_kernel,
        grid=(bsz, n // tm),
        in_specs=[
            pl.BlockSpec((1, tm, w2), lambda b, i: (b, i, 0)),
            pl.BlockSpec((1, w), lambda b, i: (0, 0)),
            pl.BlockSpec((1, w), lambda b, i: (0, 0)),
            pl.BlockSpec((CHUNK, N_GROUPS * CHUNK), lambda b, i: (0, 0)),
            pl.BlockSpec((CHUNK, w), lambda b, i: (0, 0)),
        ],
        out_specs=pl.BlockSpec((1, tm, w), lambda b, i: (b, i, 0)),
        out_shape=jax.ShapeDtypeStruct((bsz, n, w), BF16),
        compiler_params=_params("arbitrary", "arbitrary"),
        name="chunk_gmlp",
    )(za, ln_g[None, :], ln_b[None, :], wcat, bias)


CONV_HALO = 16
CONV_SUB = 128
SUBLANES = 8
LANES = 128


def _conv_kernel(a_ref, g_ref, ap_ref, gp_ref, an_ref, gn_ref, w_ref, cb_ref, lng_ref, lnb_ref, o_ref, y_ref):
    i = pl.program_id(1)
    tm = a_ref.shape[1]

    def glu(a, g):
        return a * _sigmoid(g)

    y_ref[0:CONV_HALO, :] = jnp.where(i > 0, glu(ap_ref[0], gp_ref[0]), 0.0)
    y_ref[CONV_HALO:CONV_HALO + tm, :] = glu(a_ref[0], g_ref[0])
    y_ref[CONV_HALO + tm:, :] = jnp.where(i < pl.num_programs(1) - 1, glu(an_ref[0], gn_ref[0]), 0.0)
    first = CONV_HALO - CONV_WIDTH // 2
    nb = CONV_SUB + 2 * CONV_HALO
    for r in range(0, tm, CONV_SUB):
        blk = y_ref[r:r + nb, :]
        acc = None
        for res in range(SUBLANES):
            rot = blk if res == 0 else pltpu.roll(blk, nb - res, 0)
            for j in range(CONV_WIDTH):
                if (first + j) % SUBLANES == res:
                    a0 = first + j - res
                    t = rot[a0:a0 + CONV_SUB, :] * w_ref[j:j + 1, :]
                    acc = t if acc is None else acc + t
        y = _ln(acc + cb_ref[...], lng_ref[...], lnb_ref[...])
        o_ref[0, r:r + CONV_SUB, :] = _silu(y).astype(o_ref.dtype)


def conformer_conv(zcv, conv_w, conv_b, ln_g, ln_b, *, tm):
    bsz, n, w2 = zcv.shape
    w = w2 // 2
    hb = tm // CONV_HALO
    n_hb = n // CONV_HALO

    def main(col):
        return pl.BlockSpec((1, tm, w), lambda b, i: (b, i, col))

    def prev(col):
        return pl.BlockSpec((1, CONV_HALO, w), lambda b, i: (b, jnp.maximum(i * hb - 1, 0), col))

    def nxt(col):
        return pl.BlockSpec((1, CONV_HALO, w), lambda b, i: (b, jnp.minimum((i + 1) * hb, n_hb - 1), col))

    vec = pl.BlockSpec((1, w), lambda b, i: (0, 0))
    return pl.pallas_call(
        _conv_kernel,
        grid=(bsz, n // tm),
        in_specs=[main(0), main(1), prev(0), prev(1), nxt(0), nxt(1),
                  pl.BlockSpec((CONV_WIDTH, w), lambda b, i: (0, 0)), vec, vec, vec],
        out_specs=pl.BlockSpec((1, tm, w), lambda b, i: (b, i, 0)),
        out_shape=jax.ShapeDtypeStruct((bsz, n, w), BF16),
        scratch_shapes=[pltpu.VMEM((tm + 2 * CONV_HALO, w), F32)],
        compiler_params=_params("arbitrary", "arbitrary"),
        name="conformer_conv",
    )(zcv, zcv, zcv, zcv, zcv, zcv, conv_w, conv_b[None, :], ln_g[None, :], ln_b[None, :])


def _gmlp_conv_kernel(*refs):
    _gmlp_kernel(*refs[:5], refs[15])
    _conv_kernel(*refs[5:15], refs[16], refs[17])


def gmlp_and_conv(za, sgu_g, sgu_b, w_s, b_s, zcv, conv_w, conv_b, cln_g, cln_b, *, tm):
    bsz, n, w2 = za.shape
    w = w2 // 2
    hb = tm // CONV_HALO
    n_hb = n // CONV_HALO
    wcat = jnp.transpose(w_s, (1, 0, 2)).reshape(CHUNK, N_GROUPS * CHUNK).astype(BF16)
    bias = jnp.repeat(b_s.T, HEAD_DIM, axis=1)
    vec = pl.BlockSpec((1, w), lambda b, i: (0, 0))
    row = pl.BlockSpec((1, tm, w), lambda b, i: (b, i, 0))

    def main(col):
        return pl.BlockSpec((1, tm, w), lambda b, i: (b, i, col))

    def prev(col):
        return pl.BlockSpec((1, CONV_HALO, w), lambda b, i: (b, jnp.maximum(i * hb - 1, 0), col))

    def nxt(col):
        return pl.BlockSpec((1, CONV_HALO, w), lambda b, i: (b, jnp.minimum((i + 1) * hb, n_hb - 1), col))

    return pl.pallas_call(
        _gmlp_conv_kernel,
        grid=(bsz, n // tm),
        in_specs=[pl.BlockSpec((1, tm, w2), lambda b, i: (b, i, 0)), vec, vec,
                  pl.BlockSpec((CHUNK, N_GROUPS * CHUNK), lambda b, i: (0, 0)),
                  pl.BlockSpec((CHUNK, w), lambda b, i: (0, 0)),
                  main(0), main(1), prev(0), prev(1), nxt(0), nxt(1),
                  pl.BlockSpec((CONV_WIDTH, w), lambda b, i: (0, 0)), vec, vec, vec],
        out_specs=[row, row],
        out_shape=[jax.ShapeDtypeStruct((bsz, n, w), BF16)] * 2,
        scratch_shapes=[pltpu.VMEM((tm + 2 * CONV_HALO, w), F32)],
        compiler_params=_params("arbitrary", "arbitrary"),
        name="gmlp_and_conv",
    )(za, sgu_g[None, :], sgu_b[None, :], wcat, bias,
      zcv, zcv, zcv, zcv, zcv, zcv, conv_w, conv_b[None, :], cln_g[None, :], cln_b[None, :])


FOURIER_N2 = 128
FOURIER_UNROLL = 8
FOURIER_KB = 8


def _np_split(m):
    m = jnp.asarray(m, F32)
    hi = m.astype(BF16)
    return hi, (m - hi.astype(F32)).astype(BF16)


def _channel_dft_matrix():
    c = np.arange(HEAD_DIM)
    ang = 2.0 * np.pi * ((c[:, None] * c[None, :]) % HEAD_DIM) / HEAD_DIM
    eye = np.eye(N_GROUPS)
    return np.concatenate([np.kron(eye, np.cos(ang)), np.kron(eye, np.sin(ang))], axis=0)


def _fourier_stage1_kernel(xa_ref, xb_ref, mh_ref, ml_ref, o_ref, *, n1, n2):
    def one_fast_index(f, carry):
        rows = pl.ds(f, n1, stride=n2)
        x = jnp.concatenate([xa_ref[0, rows, :], xb_ref[0, rows, :]], axis=1)
        x_hi, x_lo = _split_bf16(x)
        res = _dot3(mh_ref[0, f], ml_ref[0, f], x_hi, x_lo)
        o_ref[0, 0, 0, rows, :] = res[:, :LANES]
        o_ref[0, 0, 1, rows, :] = res[:, LANES:]
        return carry

    lax.fori_loop(0, n2, one_fast_index, 0, unroll=FOURIER_UNROLL)


def _fourier_stage2_kernel(br_ref, bi_ref, m2h_ref, m2l_ref, mdh_ref, mdl_ref, o_ref, *, scale):
    for k in range(br_ref.shape[3]):
        b_re = jnp.concatenate([br_ref[0, 0, 0, k], br_ref[0, 0, 1, k]], axis=1)
        b_im = jnp.concatenate([bi_ref[0, 0, 0, k], bi_ref[0, 0, 1, k]], axis=1)
        b_hi, b_lo = _split_bf16(jnp.concatenate([b_re, b_im], axis=0))
        xs = _dot3(m2h_ref[...], m2l_ref[...], b_hi, b_lo)
        x_hi, x_lo = _split_bf16(jnp.concatenate([xs[:FOURIER_N2], xs[FOURIER_N2:]], axis=1))
        o_ref[0, :, k, :] = _dot3(x_hi, x_lo, mdh_ref[...], mdl_ref[...]) * scale


def _fourier_direct_kernel(x_ref, mph_ref, mpl_ref, mdh_ref, mdl_ref, o_ref, *, scale):
    n = x_ref.shape[1]
    x_hi, x_lo = _split_bf16(x_ref[0])
    p = _dot3(mph_ref[...], mpl_ref[...], x_hi, x_lo)
    p_hi, p_lo = _split_bf16(jnp.concatenate([p[:n], p[n:]], axis=1))
    o_ref[0] = (_dot3(p_hi, p_lo, mdh_ref[...], mdl_ref[...]) * scale).astype(o_ref.dtype)


def fourier_mix(zb):
    bsz, n, w = zb.shape
    scale = float(1.0 / np.sqrt(n * HEAD_DIM))
    mdh, mdl = _np_split(_channel_dft_matrix())
    md_spec2 = pl.BlockSpec((2 * w, w), lambda b, j: (0, 0))
    if n <= 2 * FOURIER_N2:
        t = np.arange(n)
        ang = 2.0 * np.pi * ((t[:, None] * t[None, :]) % n) / n
        mph, mpl = _np_split(np.concatenate([np.cos(ang), -np.sin(ang)], axis=0))
        return pl.pallas_call(
            functools.partial(_fourier_direct_kernel, scale=scale),
            grid=(bsz,),
            in_specs=[pl.BlockSpec((1, n, w), lambda b: (b, 0, 0)),
                      pl.BlockSpec((2 * n, n), lambda b: (0, 0)), pl.BlockSpec((2 * n, n), lambda b: (0, 0)),
                      pl.BlockSpec((2 * w, w), lambda b: (0, 0)), pl.BlockSpec((2 * w, w), lambda b: (0, 0))],
            out_specs=pl.BlockSpec((1, n, w), lambda b: (b, 0, 0)),
            out_shape=jax.ShapeDtypeStruct((bsz, n, w), BF16),
            compiler_params=_params("arbitrary"),
            name="fourier_direct",
        )(zb, mph, mpl, mdh, mdl)

    n1, n2 = n // FOURIER_N2, FOURIER_N2
    f, k1, s = np.arange(n2), np.arange(n1), np.arange(n1)
    ang1 = 2.0 * np.pi * ((k1[None, :, None] * (f[:, None, None] + n2 * s[None, None, :])) % n) / n
    m1h, m1l = _np_split(np.stack([np.cos(ang1), -np.sin(ang1)]))
    k2 = np.arange(n2)
    ang2 = 2.0 * np.pi * ((k2[:, None] * f[None, :]) % n2) / n2
    c2, s2 = np.cos(ang2), np.sin(ang2)
    m2h, m2l = _np_split(np.block([[c2, s2], [-s2, c2]]))

    m1_spec = pl.BlockSpec((1, n2, n1, n1), lambda b, p: (p, 0, 0, 0))
    n_half = w // LANES
    b_st = pl.pallas_call(
        functools.partial(_fourier_stage1_kernel, n1=n1, n2=n2),
        grid=(bsz, 2),
        in_specs=[pl.BlockSpec((1, n, LANES), lambda b, p: (b, 0, 0)),
                  pl.BlockSpec((1, n, LANES), lambda b, p: (b, 0, 1)), m1_spec, m1_spec],
        out_specs=pl.BlockSpec((1, 1, n_half, n, LANES), lambda b, p: (b, p, 0, 0, 0)),
        out_shape=jax.ShapeDtypeStruct((bsz, 2, n_half, n, LANES), F32),
        compiler_params=_params("arbitrary", "arbitrary"),
        name="fourier_stage1",
    )(zb, zb, m1h, m1l)

    kb = FOURIER_KB
    b_st = b_st.reshape(bsz, 2, n_half, n1, n2, LANES)
    out = pl.pallas_call(
        functools.partial(_fourier_stage2_kernel, scale=scale),
        grid=(bsz, n1 // kb),
        in_specs=[pl.BlockSpec((1, 1, n_half, kb, n2, LANES), lambda b, j: (b, 0, 0, j, 0, 0)),
                  pl.BlockSpec((1, 1, n_half, kb, n2, LANES), lambda b, j: (b, 1, 0, j, 0, 0)),
                  pl.BlockSpec((2 * n2, 2 * n2), lambda b, j: (0, 0)),
                  pl.BlockSpec((2 * n2, 2 * n2), lambda b, j: (0, 0)),
                  md_spec2, md_spec2],
        out_specs=pl.BlockSpec((1, n2, kb, w), lambda b, j: (b, 0, j, 0)),
        out_shape=jax.ShapeDtypeStruct((bsz, n2, n1, w), F32),
        compiler_params=_params("arbitrary", "arbitrary"),
        name="fourier_stage2",
    )(b_st, b_st, m2h, m2l, mdh, mdl)
    return out.reshape(bsz, n, w)


def expert_route(h, aff_t):
    n = h.shape[1]
    cap = EC_CAPACITY * n // N_EXPERTS
    slot, off = expert_topk(aff_t, cap)
    xe, gate = expert_gather(h, slot, aff_t, off, max(cap, ROUTE_WIN))
    return xe, gate, jnp.swapaxes(slot, 1, 2), off


def kernel(x, c, ctx, c_ctx, w_mod, b_mod, norm1_g, norm2_g, w_in, sgu_ln_g, sgu_ln_b, w_spatial, b_spatial,
           w_a_out, w_b_out, conv_w, conv_b, conv_ln_g, conv_ln_b, w_c_out, rpb, w_d_out, w_out, w_router,
           w_gate_e, w_up_e, w_down_e, final_norm_g):
    bsz = x.shape[0]
    xc = ctx
    assert bsz + 1 <= MOD_ROWS
    cond = jnp.concatenate([c, c_ctx[None, :], jnp.zeros((MOD_ROWS - bsz - 1, D_MODEL), F32)], axis=0)
    mod_all = modulation(cond, w_mod, b_mod)
    in_scale = jnp.concatenate([jnp.ones((V_END,), F32), jnp.full((IN_COLS - V_END,), GATE_HALF, F32)])
    w_in_b = (w_in * in_scale).astype(BF16)
    for l in range(DEPTH):
        last = l == DEPTH - 1
        sh1, sc1, gt1, sh2, sc2, gt2 = jnp.split(mod_all[l, :bsz, None, :], 6, axis=-1)
        cmod = jnp.broadcast_to(mod_all[l, bsz:bsz + 1, None, :], (bsz, 1, 6 * D_MODEL))
        csh1, csc1, cgt1, csh2, csc2, cgt2 = jnp.split(cmod, 6, axis=-1)

        wa, wb, wc, wd_ = ((w * GATE_HALF).astype(BF16) for w in (w_a_out[l], w_b_out[l], w_c_out[l], w_d_out[l]))
        wo = w_out[l].astype(BF16)
        g1 = norm1_g[l][None, :]

        tm_x, tm_c = _row_tile(x.shape[1], PROJ_TM), _row_tile(xc.shape[1], PROJ_TM)
        tmix_x, tmix_c = _row_tile(x.shape[1], MIXER_TM), _row_tile(xc.shape[1], MIXER_TM)
        za, zb, zcv, zqkv, zg = norm_inproj(x, g1, sc1, sh1, w_in_b, l, 0, IN_WIDTHS, IN_DTYPES, tm=tm_x)
        if last:
            (zqkv_c,) = norm_inproj(xc, g1, csc1, csh1, w_in_b, l, Q_END, (2 * BRANCH_WIDTH,), (BF16,), tm=tm_c)
            kc_col, vc_col = 0, 1
        else:
            cza, czb, czcv, zqkv_c, czg = norm_inproj(xc, g1, csc1, csh1, w_in_b, l, 0, IN_WIDTHS, IN_DTYPES,
                                                      tm=tm_c)
            kc_col, vc_col = 1, 2
        d_lat = neighbourhood_attention(zqkv, zqkv_c, rpb[l], q_col=0, k_col=1, v_col=2,
                                        kc_col=kc_col, vc_col=vc_col)
        a_lat, c_lat = gmlp_and_conv(za, sgu_ln_g[l], sgu_ln_b[l], w_spatial[l], b_spatial[l],
                                     zcv, conv_w[l], conv_b[l], conv_ln_g[l], conv_ln_b[l], tm=tmix_x)
        b_lat = fourier_mix(zb)
        g2 = norm2_g[l][None, :]
        router_t = w_router[l].T
        x, h2, aff_t = merge_branches(a_lat, b_lat, c_lat, d_lat, zg, wa, wb, wc, wd_, wo, x, gt1,
                                      g2, sc2, sh2, router_t, tm=tm_x)
        if not last:
            a_c = chunk_gmlp(cza, sgu_ln_g[l], sgu_ln_b[l], w_spatial[l], b_spatial[l], tm=tmix_c)
            b_c = fourier_mix(czb)
            c_c = conformer_conv(czcv, conv_w[l], conv_b[l], conv_ln_g[l], conv_ln_b[l], tm=tmix_c)
            d_c = context_attention(zqkv_c, q_col=0, k_col=1, v_col=2)
            xc, hc2, aff_tc = merge_branches(a_c, b_c, c_c, d_c, czg, wa, wb, wc, wd_, wo, xc, cgt1,
                                             g2, csc2, csh2, router_t, tm=tm_c)

        xe, gate, slot_t, off = expert_route(h2, aff_t)
        if last:
            (y,) = expert_ffn([xe], [gate], w_gate_e, w_up_e, w_down_e, l)
        else:
            xe_c, gate_c, slot_tc, off_c = expert_route(hc2, aff_tc)
            y, y_c = expert_ffn([xe, xe_c], [gate, gate_c], w_gate_e, w_up_e, w_down_e, l)
            xc = expert_combine(y_c, slot_tc, off_c, xc, cgt2)
        x = expert_combine(y, slot_t, off, x, gt2, final_norm_g if last else None)
    return x
```
